```python
import jax, jax.numpy as jnp
from jax import lax
import numpy as np

D_MODEL = 2048
BATCH = 8
SEQ = 2048
DEPTH = 2

N_MIXERS = 2
N_MLA = (DEPTH + 1) // 2
N_CONV = DEPTH // 2
N_HEADS = 16
Q_LORA = 512
KV_LORA = 512
D_NOPE = 128
D_ROPE = 64
D_V = 128
QK_DIM = D_NOPE + D_ROPE
ROPE_THETA = 10000.0
Q_BLOCK = 128
CONV_CH = D_MODEL
CONV_WIDTH = 31
D_FF = 5632
FFN_RESIDUAL_WEIGHT = 0.5
D_PLE = 256
EPS = 1e-6

kernel_name = "mla_conformer_macaron_hybrid"


def rms_norm(x, g):
    xf = x.astype(jnp.float32)
    y = xf * lax.rsqrt(jnp.mean(xf * xf, axis=-1, keepdims=True) + EPS)
    return (y * g.astype(jnp.float32)).astype(x.dtype)


def layer_norm(x, g, b):
    xf = x.astype(jnp.float32)
    mu = jnp.mean(xf, axis=-1, keepdims=True)
    xc = xf - mu
    y = xc * lax.rsqrt(jnp.mean(xc * xc, axis=-1, keepdims=True) + EPS)
    return (y * g.astype(jnp.float32) + b.astype(jnp.float32)).astype(x.dtype)


def swiglu_ffn(h, w_in, w_out):
    g, u = jnp.split(h @ w_in, 2, axis=-1)
    return (jax.nn.silu(g) * u) @ w_out


def rope_tables(positions):
    inv_freq = ROPE_THETA ** (-jnp.arange(0, D_ROPE, 2, dtype=jnp.float32) / D_ROPE)
    ang = positions.astype(jnp.float32)[..., None] * inv_freq
    return jnp.cos(ang)[:, :, None, :], jnp.sin(ang)[:, :, None, :]


def apply_rope_tail(x, cos, sin):
    x_nope, x_rope = x[..., :D_NOPE], x[..., D_NOPE:]
    xr = x_rope.astype(jnp.float32)
    x1, x2 = xr[..., : D_ROPE // 2], xr[..., D_ROPE // 2:]
    rot = jnp.concatenate([x1 * cos - x2 * sin, x2 * cos + x1 * sin], axis=-1)
    return jnp.concatenate([x_nope, rot.astype(x.dtype)], axis=-1)


def causal_blocked_attention(q, k, v):
    S = q.shape[1]
    scale = QK_DIM ** -0.5
    outs = []
    for blk in range(S // Q_BLOCK):
        s0 = blk * Q_BLOCK
        n_keys = s0 + Q_BLOCK
        q_b = q[:, s0:s0 + Q_BLOCK]
        k_b = k[:, :n_keys]
        v_b = v[:, :n_keys]
        scores = jnp.einsum('bqhe,bkhe->bhqk', q_b, k_b).astype(jnp.float32) * scale
        q_idx = s0 + jnp.arange(Q_BLOCK)[:, None]
        k_idx = jnp.arange(n_keys)[None, :]
        scores = jnp.where(k_idx <= q_idx, scores, -jnp.inf)
        probs = jax.nn.softmax(scores, axis=-1).astype(v.dtype)
        outs.append(jnp.einsum('bhqk,bkhd->bqhd', probs, v_b))
    return jnp.concatenate(outs, axis=1)


def mla_mixer(h, positions, w_in, q_lat_norm, kv_lat_norm, w_uq, w_ukv, q_gain, k_gain, w_o):
    B, S, _ = h.shape
    lat = h @ w_in
    c_q, c_kv, k_rope = jnp.split(lat, [Q_LORA, Q_LORA + KV_LORA], axis=-1)
    c_q = rms_norm(c_q, q_lat_norm)
    c_kv = rms_norm(c_kv, kv_lat_norm)
    q = (c_q @ w_uq).reshape(B, S, N_HEADS, QK_DIM)
    kv = (c_kv @ w_ukv).reshape(B, S, N_HEADS, D_NOPE + D_V)
    k_nope, v = jnp.split(kv, [D_NOPE], axis=-1)
    k_rope = jnp.broadcast_to(k_rope[:, :, None, :], (B, S, N_HEADS, D_ROPE))
    k = jnp.concatenate([k_nope, k_rope], axis=-1)
    q = rms_norm(q, q_gain)
    k = rms_norm(k, k_gain)
    cos, sin = rope_tables(positions)
    q = apply_rope_tail(q, cos, sin)
    k = apply_rope_tail(k, cos, sin)
    o = causal_blocked_attention(q, k, v)
    return o.reshape(B, S, N_HEADS * D_V) @ w_o


def conv_mixer(h, w_pw1, b_pw1, w_dw, b_dw, ln_g, ln_b, w_pw2):
    a, g = jnp.split(h @ w_pw1 + b_pw1, 2, axis=-1)
    u = a * jax.nn.sigmoid(g)
    u = lax.conv_general_dilated(
        u, w_dw[:, None, :], window_strides=(1,), padding=[(CONV_WIDTH - 1, 0)],
        dimension_numbers=('NWC', 'WIO', 'NWC'), feature_group_count=CONV_CH) + b_dw
    u = jax.nn.silu(layer_norm(u, ln_g, ln_b))
    return u @ w_pw2


def _fwd_setup_inputs(seed: int = 0) -> dict:
    key = jax.random.key(seed)
    ks = iter(jax.random.split(key, 40))

    def w(shape, fan_in):
        return jax.random.normal(next(ks), shape, jnp.float32) * (fan_in ** -0.5)

    def gain(shape):
        return 1.0 + 0.02 * jax.random.normal(next(ks), shape, jnp.float32)

    def bias(shape):
        return 0.01 * jax.random.normal(next(ks), shape, jnp.float32)

    x = jax.random.normal(next(ks), (BATCH, SEQ, D_MODEL), jnp.float32)
    p = jax.random.normal(next(ks), (DEPTH, BATCH, SEQ, D_PLE), jnp.float32)
    positions = jnp.broadcast_to(jnp.arange(SEQ, dtype=jnp.int32)[None, :], (BATCH, SEQ))
    return {
        "x": x, "p": p, "positions": positions,
        "ffn_a_norm": gain((DEPTH, D_MODEL)),
        "ffn_a_w_in": w((DEPTH, D_MODEL, 2 * D_FF), D_MODEL),
        "ffn_a_w_out": w((DEPTH, D_FF, D_MODEL), D_FF),
        "ffn_b_norm": gain((DEPTH, D_MODEL)),
        "ffn_b_w_in": w((DEPTH, D_MODEL, 2 * D_FF), D_MODEL),
        "ffn_b_w_out": w((DEPTH, D_FF, D_MODEL), D_FF),
        "mix_norm": gain((DEPTH, D_MODEL)),
        "mla_w_in": w((N_MLA, D_MODEL, Q_LORA + KV_LORA + D_ROPE), D_MODEL),
        "mla_q_lat_norm": gain((N_MLA, Q_LORA)),
        "mla_kv_lat_norm": gain((N_MLA, KV_LORA)),
        "mla_w_uq": w((N_MLA, Q_LORA, N_HEADS * QK_DIM), Q_LORA),
        "mla_w_ukv": w((N_MLA, KV_LORA, N_HEADS * (D_NOPE + D_V)), KV_LORA),
        "mla_q_gain": gain((N_MLA, QK_DIM)),
        "mla_k_gain": gain((N_MLA, QK_DIM)),
        "mla_w_o": w((N_MLA, N_HEADS * D_V, D_MODEL), N_HEADS * D_V),
        "conv_w_pw1": w((N_CONV, D_MODEL, 2 * CONV_CH), D_MODEL),
        "conv_b_pw1": bias((N_CONV, 2 * CONV_CH)),
        "conv_w_dw": w((N_CONV, CONV_WIDTH, CONV_CH), CONV_WIDTH),
        "conv_b_dw": bias((N_CONV, CONV_CH)),
        "conv_ln_g": gain((N_CONV, CONV_CH)),
        "conv_ln_b": bias((N_CONV, CONV_CH)),
        "conv_w_pw2": w((N_CONV, CONV_CH, D_MODEL), CONV_CH),
        "ple_w_proj": w((DEPTH, D_PLE, D_MODEL), D_PLE),
        "ple_norm": gain((DEPTH, D_MODEL)),
        "ple_gate_norm": gain((DEPTH, D_MODEL)),
        "ple_w_gate": w((DEPTH, D_MODEL, D_MODEL), D_MODEL),
    }


def _fwd_reference(x, p, positions,
              ffn_a_norm, ffn_a_w_in, ffn_a_w_out,
              ffn_b_norm, ffn_b_w_in, ffn_b_w_out,
              mix_norm,
              mla_w_in, mla_q_lat_norm, mla_kv_lat_norm, mla_w_uq, mla_w_ukv,
              mla_q_gain, mla_k_gain, mla_w_o,
              conv_w_pw1, conv_b_pw1, conv_w_dw, conv_b_dw, conv_ln_g, conv_ln_b, conv_w_pw2,
              ple_w_proj, ple_norm, ple_gate_norm, ple_w_gate):
    h = x
    for i in range(DEPTH):
        h = h + FFN_RESIDUAL_WEIGHT * swiglu_ffn(rms_norm(h, ffn_a_norm[i]), ffn_a_w_in[i], ffn_a_w_out[i])
        hn = rms_norm(h, mix_norm[i])
        j = i // N_MIXERS
        if i % N_MIXERS == 0:
            h = h + mla_mixer(hn, positions, mla_w_in[j], mla_q_lat_norm[j], mla_kv_lat_norm[j],
                              mla_w_uq[j], mla_w_ukv[j], mla_q_gain[j], mla_k_gain[j], mla_w_o[j])
        else:
            h = h + conv_mixer(hn, conv_w_pw1[j], conv_b_pw1[j], conv_w_dw[j], conv_b_dw[j],
                               conv_ln_g[j], conv_ln_b[j], conv_w_pw2[j])
        h = h + FFN_RESIDUAL_WEIGHT * swiglu_ffn(rms_norm(h, ffn_b_norm[i]), ffn_b_w_in[i], ffn_b_w_out[i])
        e = rms_norm(p[i] @ ple_w_proj[i], ple_norm[i])
        gate = jax.nn.sigmoid(rms_norm(h, ple_gate_norm[i]) @ ple_w_gate[i])
        h = h + e * gate
    return h


import jax as _jax
import jax.numpy as _jnp

TWIN_FORMAT = 'train_step'
FWD_PARAMS = ['x', 'p', 'positions', 'ffn_a_norm', 'ffn_a_w_in', 'ffn_a_w_out', 'ffn_b_norm', 'ffn_b_w_in', 'ffn_b_w_out', 'mix_norm', 'mla_w_in', 'mla_q_lat_norm', 'mla_kv_lat_norm', 'mla_w_uq', 'mla_w_ukv', 'mla_q_gain', 'mla_k_gain', 'mla_w_o', 'conv_w_pw1', 'conv_b_pw1', 'conv_w_dw', 'conv_b_dw', 'conv_ln_g', 'conv_ln_b', 'conv_w_pw2', 'ple_w_proj', 'ple_norm', 'ple_gate_norm', 'ple_w_gate']
TWIN_WEIGHTS = ['ffn_a_norm', 'ffn_a_w_in', 'ffn_a_w_out', 'ffn_b_norm', 'ffn_b_w_in', 'ffn_b_w_out', 'mix_norm', 'mla_w_in', 'mla_q_lat_norm', 'mla_kv_lat_norm', 'mla_w_uq', 'mla_w_ukv', 'mla_q_gain', 'mla_k_gain', 'mla_w_o', 'conv_w_pw1', 'conv_b_pw1', 'conv_w_dw', 'conv_b_dw', 'conv_ln_g', 'conv_ln_b', 'conv_w_pw2', 'ple_w_proj', 'ple_norm', 'ple_gate_norm', 'ple_w_gate']
TWIN_DIFF_INPUT = 'x'
TWIN_INPUTS = ['x', 'p', 'positions', 'ffn_a_norm', 'ffn_a_w_in', 'ffn_a_w_out', 'ffn_b_norm', 'ffn_b_w_in', 'ffn_b_w_out', 'mix_norm', 'mla_w_in', 'mla_q_lat_norm', 'mla_kv_lat_norm', 'mla_w_uq', 'mla_w_ukv', 'mla_q_gain', 'mla_k_gain', 'mla_w_o', 'conv_w_pw1', 'conv_b_pw1', 'conv_w_dw', 'conv_b_dw', 'conv_ln_g', 'conv_ln_b', 'conv_w_pw2', 'ple_w_proj', 'ple_norm', 'ple_gate_norm', 'ple_w_gate', 'loss_target', 'm_ffn_a_norm', 'm_ffn_a_w_in', 'm_ffn_a_w_out', 'm_ffn_b_norm', 'm_ffn_b_w_in', 'm_ffn_b_w_out', 'm_mix_norm', 'm_mla_w_in', 'm_mla_q_lat_norm', 'm_mla_kv_lat_norm', 'm_mla_w_uq', 'm_mla_w_ukv', 'm_mla_q_gain', 'm_mla_k_gain', 'm_mla_w_o', 'm_conv_w_pw1', 'm_conv_b_pw1', 'm_conv_w_dw', 'm_conv_b_dw', 'm_conv_ln_g', 'm_conv_ln_b', 'm_conv_w_pw2', 'm_ple_w_proj', 'm_ple_norm', 'm_ple_gate_norm', 'm_ple_w_gate', 'v_ffn_a_norm', 'v_ffn_a_w_in', 'v_ffn_a_w_out', 'v_ffn_b_norm', 'v_ffn_b_w_in', 'v_ffn_b_w_out', 'v_mix_norm', 'v_mla_w_in', 'v_mla_q_lat_norm', 'v_mla_kv_lat_norm', 'v_mla_w_uq', 'v_mla_w_ukv', 'v_mla_q_gain', 'v_mla_k_gain', 'v_mla_w_o', 'v_conv_w_pw1', 'v_conv_b_pw1', 'v_conv_w_dw', 'v_conv_b_dw', 'v_conv_ln_g', 'v_conv_ln_b', 'v_conv_w_pw2', 'v_ple_w_proj', 'v_ple_norm', 'v_ple_gate_norm', 'v_ple_w_gate']
TWIN_OUTPUTS = ['loss', 'grad_x', 'grad_ffn_a_norm', 'grad_ffn_a_w_in', 'grad_ffn_a_w_out', 'grad_ffn_b_norm', 'grad_ffn_b_w_in', 'grad_ffn_b_w_out', 'grad_mix_norm', 'grad_mla_w_in', 'grad_mla_q_lat_norm', 'grad_mla_kv_lat_norm', 'grad_mla_w_uq', 'grad_mla_w_ukv', 'grad_mla_q_gain', 'grad_mla_k_gain', 'grad_mla_w_o', 'grad_conv_w_pw1', 'grad_conv_b_pw1', 'grad_conv_w_dw', 'grad_conv_b_dw', 'grad_conv_ln_g', 'grad_conv_ln_b', 'grad_conv_w_pw2', 'grad_ple_w_proj', 'grad_ple_norm', 'grad_ple_gate_norm', 'grad_ple_w_gate', 'delta_ffn_a_norm', 'delta_ffn_a_w_in', 'delta_ffn_a_w_out', 'delta_ffn_b_norm', 'delta_ffn_b_w_in', 'delta_ffn_b_w_out', 'delta_mix_norm', 'delta_mla_w_in', 'delta_mla_q_lat_norm', 'delta_mla_kv_lat_norm', 'delta_mla_w_uq', 'delta_mla_w_ukv', 'delta_mla_q_gain', 'delta_mla_k_gain', 'delta_mla_w_o', 'delta_conv_w_pw1', 'delta_conv_b_pw1', 'delta_conv_w_dw', 'delta_conv_b_dw', 'delta_conv_ln_g', 'delta_conv_ln_b', 'delta_conv_w_pw2', 'delta_ple_w_proj', 'delta_ple_norm', 'delta_ple_gate_norm', 'delta_ple_w_gate', 'new_m_ffn_a_norm', 'new_m_ffn_a_w_in', 'new_m_ffn_a_w_out', 'new_m_ffn_b_norm', 'new_m_ffn_b_w_in', 'new_m_ffn_b_w_out', 'new_m_mix_norm', 'new_m_mla_w_in', 'new_m_mla_q_lat_norm', 'new_m_mla_kv_lat_norm', 'new_m_mla_w_uq', 'new_m_mla_w_ukv', 'new_m_mla_q_gain', 'new_m_mla_k_gain', 'new_m_mla_w_o', 'new_m_conv_w_pw1', 'new_m_conv_b_pw1', 'new_m_conv_w_dw', 'new_m_conv_b_dw', 'new_m_conv_ln_g', 'new_m_conv_ln_b', 'new_m_conv_w_pw2', 'new_m_ple_w_proj', 'new_m_ple_norm', 'new_m_ple_gate_norm', 'new_m_ple_w_gate', 'new_v_ffn_a_norm', 'new_v_ffn_a_w_in', 'new_v_ffn_a_w_out', 'new_v_ffn_b_norm', 'new_v_ffn_b_w_in', 'new_v_ffn_b_w_out', 'new_v_mix_norm', 'new_v_mla_w_in', 'new_v_mla_q_lat_norm', 'new_v_mla_kv_lat_norm', 'new_v_mla_w_uq', 'new_v_mla_w_ukv', 'new_v_mla_q_gain', 'new_v_mla_k_gain', 'new_v_mla_w_o', 'new_v_conv_w_pw1', 'new_v_conv_b_pw1', 'new_v_conv_w_dw', 'new_v_conv_b_dw', 'new_v_conv_ln_g', 'new_v_conv_ln_b', 'new_v_conv_w_pw2', 'new_v_ple_w_proj', 'new_v_ple_norm', 'new_v_ple_gate_norm', 'new_v_ple_w_gate']
TWIN_LEAF_KINDS = {'loss': 'loss', 'grad_x': 'grad_x', 'grad_ffn_a_norm': 'grad_w', 'grad_ffn_a_w_in': 'grad_w', 'grad_ffn_a_w_out': 'grad_w', 'grad_ffn_b_norm': 'grad_w', 'grad_ffn_b_w_in': 'grad_w', 'grad_ffn_b_w_out': 'grad_w', 'grad_mix_norm': 'grad_w', 'grad_mla_w_in': 'grad_w', 'grad_mla_q_lat_norm': 'grad_w', 'grad_mla_kv_lat_norm': 'grad_w', 'grad_mla_w_uq': 'grad_w', 'grad_mla_w_ukv': 'grad_w', 'grad_mla_q_gain': 'grad_w', 'grad_mla_k_gain': 'grad_w', 'grad_mla_w_o': 'grad_w', 'grad_conv_w_pw1': 'grad_w', 'grad_conv_b_pw1': 'grad_w', 'grad_conv_w_dw': 'grad_w', 'grad_conv_b_dw': 'grad_w', 'grad_conv_ln_g': 'grad_w', 'grad_conv_ln_b': 'grad_w', 'grad_conv_w_pw2': 'grad_w', 'grad_ple_w_proj': 'grad_w', 'grad_ple_norm': 'grad_w', 'grad_ple_gate_norm': 'grad_w', 'grad_ple_w_gate': 'grad_w', 'delta_ffn_a_norm': 'delta_w', 'delta_ffn_a_w_in': 'delta_w', 'delta_ffn_a_w_out': 'delta_w', 'delta_ffn_b_norm': 'delta_w', 'delta_ffn_b_w_in': 'delta_w', 'delta_ffn_b_w_out': 'delta_w', 'delta_mix_norm': 'delta_w', 'delta_mla_w_in': 'delta_w', 'delta_mla_q_lat_norm': 'delta_w', 'delta_mla_kv_lat_norm': 'delta_w', 'delta_mla_w_uq': 'delta_w', 'delta_mla_w_ukv': 'delta_w', 'delta_mla_q_gain': 'delta_w', 'delta_mla_k_gain': 'delta_w', 'delta_mla_w_o': 'delta_w', 'delta_conv_w_pw1': 'delta_w', 'delta_conv_b_pw1': 'delta_w', 'delta_conv_w_dw': 'delta_w', 'delta_conv_b_dw': 'delta_w', 'delta_conv_ln_g': 'delta_w', 'delta_conv_ln_b': 'delta_w', 'delta_conv_w_pw2': 'delta_w', 'delta_ple_w_proj': 'delta_w', 'delta_ple_norm': 'delta_w', 'delta_ple_gate_norm': 'delta_w', 'delta_ple_w_gate': 'delta_w', 'new_m_ffn_a_norm': 'new_m', 'new_m_ffn_a_w_in': 'new_m', 'new_m_ffn_a_w_out': 'new_m', 'new_m_ffn_b_norm': 'new_m', 'new_m_ffn_b_w_in': 'new_m', 'new_m_ffn_b_w_out': 'new_m', 'new_m_mix_norm': 'new_m', 'new_m_mla_w_in': 'new_m', 'new_m_mla_q_lat_norm': 'new_m', 'new_m_mla_kv_lat_norm': 'new_m', 'new_m_mla_w_uq': 'new_m', 'new_m_mla_w_ukv': 'new_m', 'new_m_mla_q_gain': 'new_m', 'new_m_mla_k_gain': 'new_m', 'new_m_mla_w_o': 'new_m', 'new_m_conv_w_pw1': 'new_m', 'new_m_conv_b_pw1': 'new_m', 'new_m_conv_w_dw': 'new_m', 'new_m_conv_b_dw': 'new_m', 'new_m_conv_ln_g': 'new_m', 'new_m_conv_ln_b': 'new_m', 'new_m_conv_w_pw2': 'new_m', 'new_m_ple_w_proj': 'new_m', 'new_m_ple_norm': 'new_m', 'new_m_ple_gate_norm': 'new_m', 'new_m_ple_w_gate': 'new_m', 'new_v_ffn_a_norm': 'new_v', 'new_v_ffn_a_w_in': 'new_v', 'new_v_ffn_a_w_out': 'new_v', 'new_v_ffn_b_norm': 'new_v', 'new_v_ffn_b_w_in': 'new_v', 'new_v_ffn_b_w_out': 'new_v', 'new_v_mix_norm': 'new_v', 'new_v_mla_w_in': 'new_v', 'new_v_mla_q_lat_norm': 'new_v', 'new_v_mla_kv_lat_norm': 'new_v', 'new_v_mla_w_uq': 'new_v', 'new_v_mla_w_ukv': 'new_v', 'new_v_mla_q_gain': 'new_v', 'new_v_mla_k_gain': 'new_v', 'new_v_mla_w_o': 'new_v', 'new_v_conv_w_pw1': 'new_v', 'new_v_conv_b_pw1': 'new_v', 'new_v_conv_w_dw': 'new_v', 'new_v_conv_b_dw': 'new_v', 'new_v_conv_ln_g': 'new_v', 'new_v_conv_ln_b': 'new_v', 'new_v_conv_w_pw2': 'new_v', 'new_v_ple_w_proj': 'new_v', 'new_v_ple_norm': 'new_v', 'new_v_ple_gate_norm': 'new_v', 'new_v_ple_w_gate': 'new_v'}


def _forward(args):
    return _fwd_reference(*[args[k] for k in FWD_PARAMS])


def _output_shape():
    out = _jax.eval_shape(lambda: _forward(_fwd_setup_inputs(0)))
    return out.shape, out.dtype

N_MICROBATCH = 1
ADAM_LR = 0.001
ADAM_B1 = 0.9
ADAM_B2 = 0.999
ADAM_EPS = 1e-08
ADAM_WD = 0.01
ADAM_STEP = 10
PER_EXAMPLE_BATCH_AXIS = {'x': 0, 'p': 1, 'positions': 0, 'loss_target': 0}
SHARED_INPUTS = []
_WEIGHT_DTYPES = {'ffn_a_norm': _jnp.float32, 'ffn_a_w_in': _jnp.float32, 'ffn_a_w_out': _jnp.float32, 'ffn_b_norm': _jnp.float32, 'ffn_b_w_in': _jnp.float32, 'ffn_b_w_out': _jnp.float32, 'mix_norm': _jnp.float32, 'mla_w_in': _jnp.float32, 'mla_q_lat_norm': _jnp.float32, 'mla_kv_lat_norm': _jnp.float32, 'mla_w_uq': _jnp.float32, 'mla_w_ukv': _jnp.float32, 'mla_q_gain': _jnp.float32, 'mla_k_gain': _jnp.float32, 'mla_w_o': _jnp.float32, 'conv_w_pw1': _jnp.float32, 'conv_b_pw1': _jnp.float32, 'conv_w_dw': _jnp.float32, 'conv_b_dw': _jnp.float32, 'conv_ln_g': _jnp.float32, 'conv_ln_b': _jnp.float32, 'conv_w_pw2': _jnp.float32, 'ple_w_proj': _jnp.float32, 'ple_norm': _jnp.float32, 'ple_gate_norm': _jnp.float32, 'ple_w_gate': _jnp.float32}
MOMENT_SCALE = {'ffn_a_norm': 1.534345e+00, 'ffn_a_w_in': 2.979402e-02, 'ffn_a_w_out': 4.857252e-02, 'ffn_b_norm': 1.539108e+00, 'ffn_b_w_in': 3.018370e-02, 'ffn_b_w_out': 4.732438e-02, 'mix_norm': 7.943688e-02, 'mla_w_in': 9.725160e-02, 'mla_q_lat_norm': 6.616151e-02, 'mla_kv_lat_norm': 4.026121e-01, 'mla_w_uq': 2.724683e-02, 'mla_w_ukv': 4.033634e-02, 'mla_q_gain': 5.746448e-01, 'mla_k_gain': 5.754081e-01, 'mla_w_o': 4.804580e-02, 'conv_w_pw1': 6.139351e-02, 'conv_b_pw1': 6.388013e-01, 'conv_w_dw': 1.163988e-01, 'conv_b_dw': 1.674899e+00, 'conv_ln_g': 3.531458e+00, 'conv_ln_b': 2.402889e+00, 'conv_w_pw2': 3.476998e-01, 'ple_w_proj': 7.179636e-02, 'ple_norm': 2.351644e+00, 'ple_gate_norm': 2.534619e-01, 'ple_w_gate': 7.631450e-02}


def _to_microbatches(a, axis):
    t = _jnp.moveaxis(a, axis, 0)
    t = t.reshape((N_MICROBATCH, t.shape[0] // N_MICROBATCH) + t.shape[1:])
    return _jnp.moveaxis(t, 1, axis + 1)


def setup_inputs(seed: int = 0) -> dict:
    inp = _fwd_setup_inputs(seed)
    key = _jax.random.fold_in(_jax.random.key(seed), 7919)
    shape, _ = _output_shape()
    out = dict(inp)
    out["loss_target"] = _jax.random.normal(_jax.random.fold_in(key, 0), shape, _jnp.float32)
    for i, name in enumerate(TWIN_WEIGHTS):
        w = inp[name].astype(_jnp.float32)
        if MOMENT_SCALE is None:
            s = _jnp.sqrt(_jnp.mean(_jnp.square(w)) + 1e-30)
        else:
            s = MOMENT_SCALE[name]
        km, kv = _jax.random.split(_jax.random.fold_in(key, i + 1))
        out[name] = w
        out["m_" + name] = s * _jax.random.normal(km, w.shape, _jnp.float32)
        out["v_" + name] = (s * s) * _jax.random.uniform(kv, w.shape, _jnp.float32, 0.5, 1.5)
    if N_MICROBATCH > 1:
        for name, axis in PER_EXAMPLE_BATCH_AXIS.items():
            out[name] = _to_microbatches(out[name], axis)
    return {'x': out['x'], 'p': out['p'], 'positions': out['positions'], 'ffn_a_norm': out['ffn_a_norm'], 'ffn_a_w_in': out['ffn_a_w_in'], 'ffn_a_w_out': out['ffn_a_w_out'], 'ffn_b_norm': out['ffn_b_norm'], 'ffn_b_w_in': out['ffn_b_w_in'], 'ffn_b_w_out': out['ffn_b_w_out'], 'mix_norm': out['mix_norm'], 'mla_w_in': out['mla_w_in'], 'mla_q_lat_norm': out['mla_q_lat_norm'], 'mla_kv_lat_norm': out['mla_kv_lat_norm'], 'mla_w_uq': out['mla_w_uq'], 'mla_w_ukv': out['mla_w_ukv'], 'mla_q_gain': out['mla_q_gain'], 'mla_k_gain': out['mla_k_gain'], 'mla_w_o': out['mla_w_o'], 'conv_w_pw1': out['conv_w_pw1'], 'conv_b_pw1': out['conv_b_pw1'], 'conv_w_dw': out['conv_w_dw'], 'conv_b_dw': out['conv_b_dw'], 'conv_ln_g': out['conv_ln_g'], 'conv_ln_b': out['conv_ln_b'], 'conv_w_pw2': out['conv_w_pw2'], 'ple_w_proj': out['ple_w_proj'], 'ple_norm': out['ple_norm'], 'ple_gate_norm': out['ple_gate_norm'], 'ple_w_gate': out['ple_w_gate'], 'loss_target': out['loss_target'], 'm_ffn_a_norm': out['m_ffn_a_norm'], 'm_ffn_a_w_in': out['m_ffn_a_w_in'], 'm_ffn_a_w_out': out['m_ffn_a_w_out'], 'm_ffn_b_norm': out['m_ffn_b_norm'], 'm_ffn_b_w_in': out['m_ffn_b_w_in'], 'm_ffn_b_w_out': out['m_ffn_b_w_out'], 'm_mix_norm': out['m_mix_norm'], 'm_mla_w_in': out['m_mla_w_in'], 'm_mla_q_lat_norm': out['m_mla_q_lat_norm'], 'm_mla_kv_lat_norm': out['m_mla_kv_lat_norm'], 'm_mla_w_uq': out['m_mla_w_uq'], 'm_mla_w_ukv': out['m_mla_w_ukv'], 'm_mla_q_gain': out['m_mla_q_gain'], 'm_mla_k_gain': out['m_mla_k_gain'], 'm_mla_w_o': out['m_mla_w_o'], 'm_conv_w_pw1': out['m_conv_w_pw1'], 'm_conv_b_pw1': out['m_conv_b_pw1'], 'm_conv_w_dw': out['m_conv_w_dw'], 'm_conv_b_dw': out['m_conv_b_dw'], 'm_conv_ln_g': out['m_conv_ln_g'], 'm_conv_ln_b': out['m_conv_ln_b'], 'm_conv_w_pw2': out['m_conv_w_pw2'], 'm_ple_w_proj': out['m_ple_w_proj'], 'm_ple_norm': out['m_ple_norm'], 'm_ple_gate_norm': out['m_ple_gate_norm'], 'm_ple_w_gate': out['m_ple_w_gate'], 'v_ffn_a_norm': out['v_ffn_a_norm'], 'v_ffn_a_w_in': out['v_ffn_a_w_in'], 'v_ffn_a_w_out': out['v_ffn_a_w_out'], 'v_ffn_b_norm': out['v_ffn_b_norm'], 'v_ffn_b_w_in': out['v_ffn_b_w_in'], 'v_ffn_b_w_out': out['v_ffn_b_w_out'], 'v_mix_norm': out['v_mix_norm'], 'v_mla_w_in': out['v_mla_w_in'], 'v_mla_q_lat_norm': out['v_mla_q_lat_norm'], 'v_mla_kv_lat_norm': out['v_mla_kv_lat_norm'], 'v_mla_w_uq': out['v_mla_w_uq'], 'v_mla_w_ukv': out['v_mla_w_ukv'], 'v_mla_q_gain': out['v_mla_q_gain'], 'v_mla_k_gain': out['v_mla_k_gain'], 'v_mla_w_o': out['v_mla_w_o'], 'v_conv_w_pw1': out['v_conv_w_pw1'], 'v_conv_b_pw1': out['v_conv_b_pw1'], 'v_conv_w_dw': out['v_conv_w_dw'], 'v_conv_b_dw': out['v_conv_b_dw'], 'v_conv_ln_g': out['v_conv_ln_g'], 'v_conv_ln_b': out['v_conv_ln_b'], 'v_conv_w_pw2': out['v_conv_w_pw2'], 'v_ple_w_proj': out['v_ple_w_proj'], 'v_ple_norm': out['v_ple_norm'], 'v_ple_gate_norm': out['v_ple_gate_norm'], 'v_ple_w_gate': out['v_ple_w_gate']}


def _loss(weights, diff, rest, loss_target):
    with _jax.named_scope("forward"):
        args = {**rest, TWIN_DIFF_INPUT: diff, **{k: w.astype(_WEIGHT_DTYPES[k]) for k, w in weights.items()}}
        y = _forward(args)
    with _jax.named_scope("loss_head"):
        err = _jnp.square(y.astype(_jnp.float32) - loss_target)
        return 0.5 * _jnp.sum(_jnp.mean(err, axis=-1)) if err.ndim else 0.5 * err


def _adamw(w, g, m, v):
    m = ADAM_B1 * m + (1.0 - ADAM_B1) * g
    v = ADAM_B2 * v + (1.0 - ADAM_B2) * _jnp.square(g)
    m_hat = m / (1.0 - ADAM_B1 ** ADAM_STEP)
    v_hat = v / (1.0 - ADAM_B2 ** ADAM_STEP)
    delta = -ADAM_LR * (m_hat / (_jnp.sqrt(v_hat) + ADAM_EPS) + ADAM_WD * w)
    return delta, m, v


def reference(x, p, positions, ffn_a_norm, ffn_a_w_in, ffn_a_w_out, ffn_b_norm, ffn_b_w_in, ffn_b_w_out, mix_norm, mla_w_in, mla_q_lat_norm, mla_kv_lat_norm, mla_w_uq, mla_w_ukv, mla_q_gain, mla_k_gain, mla_w_o, conv_w_pw1, conv_b_pw1, conv_w_dw, conv_b_dw, conv_ln_g, conv_ln_b, conv_w_pw2, ple_w_proj, ple_norm, ple_gate_norm, ple_w_gate, loss_target, m_ffn_a_norm, m_ffn_a_w_in, m_ffn_a_w_out, m_ffn_b_norm, m_ffn_b_w_in, m_ffn_b_w_out, m_mix_norm, m_mla_w_in, m_mla_q_lat_norm, m_mla_kv_lat_norm, m_mla_w_uq, m_mla_w_ukv, m_mla_q_gain, m_mla_k_gain, m_mla_w_o, m_conv_w_pw1, m_conv_b_pw1, m_conv_w_dw, m_conv_b_dw, m_conv_ln_g, m_conv_ln_b, m_conv_w_pw2, m_ple_w_proj, m_ple_norm, m_ple_gate_norm, m_ple_w_gate, v_ffn_a_norm, v_ffn_a_w_in, v_ffn_a_w_out, v_ffn_b_norm, v_ffn_b_w_in, v_ffn_b_w_out, v_mix_norm, v_mla_w_in, v_mla_q_lat_norm, v_mla_kv_lat_norm, v_mla_w_uq, v_mla_w_ukv, v_mla_q_gain, v_mla_k_gain, v_mla_w_o, v_conv_w_pw1, v_conv_b_pw1, v_conv_w_dw, v_conv_b_dw, v_conv_ln_g, v_conv_ln_b, v_conv_w_pw2, v_ple_w_proj, v_ple_norm, v_ple_gate_norm, v_ple_w_gate):
    given = dict(x=x, p=p, positions=positions, ffn_a_norm=ffn_a_norm, ffn_a_w_in=ffn_a_w_in, ffn_a_w_out=ffn_a_w_out, ffn_b_norm=ffn_b_norm, ffn_b_w_in=ffn_b_w_in, ffn_b_w_out=ffn_b_w_out, mix_norm=mix_norm, mla_w_in=mla_w_in, mla_q_lat_norm=mla_q_lat_norm, mla_kv_lat_norm=mla_kv_lat_norm, mla_w_uq=mla_w_uq, mla_w_ukv=mla_w_ukv, mla_q_gain=mla_q_gain, mla_k_gain=mla_k_gain, mla_w_o=mla_w_o, conv_w_pw1=conv_w_pw1, conv_b_pw1=conv_b_pw1, conv_w_dw=conv_w_dw, conv_b_dw=conv_b_dw, conv_ln_g=conv_ln_g, conv_ln_b=conv_ln_b, conv_w_pw2=conv_w_pw2, ple_w_proj=ple_w_proj, ple_norm=ple_norm, ple_gate_norm=ple_gate_norm, ple_w_gate=ple_w_gate, loss_target=loss_target, m_ffn_a_norm=m_ffn_a_norm, m_ffn_a_w_in=m_ffn_a_w_in, m_ffn_a_w_out=m_ffn_a_w_out, m_ffn_b_norm=m_ffn_b_norm, m_ffn_b_w_in=m_ffn_b_w_in, m_ffn_b_w_out=m_ffn_b_w_out, m_mix_norm=m_mix_norm, m_mla_w_in=m_mla_w_in, m_mla_q_lat_norm=m_mla_q_lat_norm, m_mla_kv_lat_norm=m_mla_kv_lat_norm, m_mla_w_uq=m_mla_w_uq, m_mla_w_ukv=m_mla_w_ukv, m_mla_q_gain=m_mla_q_gain, m_mla_k_gain=m_mla_k_gain, m_mla_w_o=m_mla_w_o, m_conv_w_pw1=m_conv_w_pw1, m_conv_b_pw1=m_conv_b_pw1, m_conv_w_dw=m_conv_w_dw, m_conv_b_dw=m_conv_b_dw, m_conv_ln_g=m_conv_ln_g, m_conv_ln_b=m_conv_ln_b, m_conv_w_pw2=m_conv_w_pw2, m_ple_w_proj=m_ple_w_proj, m_ple_norm=m_ple_norm, m_ple_gate_norm=m_ple_gate_norm, m_ple_w_gate=m_ple_w_gate, v_ffn_a_norm=v_ffn_a_norm, v_ffn_a_w_in=v_ffn_a_w_in, v_ffn_a_w_out=v_ffn_a_w_out, v_ffn_b_norm=v_ffn_b_norm, v_ffn_b_w_in=v_ffn_b_w_in, v_ffn_b_w_out=v_ffn_b_w_out, v_mix_norm=v_mix_norm, v_mla_w_in=v_mla_w_in, v_mla_q_lat_norm=v_mla_q_lat_norm, v_mla_kv_lat_norm=v_mla_kv_lat_norm, v_mla_w_uq=v_mla_w_uq, v_mla_w_ukv=v_mla_w_ukv, v_mla_q_gain=v_mla_q_gain, v_mla_k_gain=v_mla_k_gain, v_mla_w_o=v_mla_w_o, v_conv_w_pw1=v_conv_w_pw1, v_conv_b_pw1=v_conv_b_pw1, v_conv_w_dw=v_conv_w_dw, v_conv_b_dw=v_conv_b_dw, v_conv_ln_g=v_conv_ln_g, v_conv_ln_b=v_conv_ln_b, v_conv_w_pw2=v_conv_w_pw2, v_ple_w_proj=v_ple_w_proj, v_ple_norm=v_ple_norm, v_ple_gate_norm=v_ple_gate_norm, v_ple_w_gate=v_ple_w_gate)
    weights = {n: given[n] for n in TWIN_WEIGHTS}
    shared = {n: given[n] for n in SHARED_INPUTS}
    per_example = {n: given[n] for n in ['x', 'p', 'positions']}
    grad_fn = _jax.value_and_grad(_loss, argnums=(0, 1))

    def one_microbatch(ex, loss_target):
        ex = dict(ex)
        diff = ex.pop(TWIN_DIFF_INPUT)
        return grad_fn(weights, diff, {**shared, **ex}, loss_target)

    if N_MICROBATCH == 1:
        loss, (grad_w, grad_x) = one_microbatch(per_example, given["loss_target"])
    else:
        def body(carry, xs):
            loss_sum, grad_sum = carry
            l_k, (gw_k, gx_k) = one_microbatch(xs[0], xs[1])
            with _jax.named_scope("update"):
                return (loss_sum + l_k, _jax.tree.map(_jnp.add, grad_sum, gw_k)), gx_k

        init = (_jnp.zeros((), _jnp.float32), _jax.tree.map(_jnp.zeros_like, weights))
        (loss, grad_w), grad_x = _jax.lax.scan(body, init, (per_example, given["loss_target"]))
    with _jax.named_scope("update"):
        delta_w, new_m, new_v = {}, {}, {}
        for n in TWIN_WEIGHTS:
            delta_w[n], new_m[n], new_v[n] = _adamw(weights[n], grad_w[n], given["m_" + n], given["v_" + n])
    return (loss, grad_x, *[grad_w[n] for n in TWIN_WEIGHTS], *[delta_w[n] for n in TWIN_WEIGHTS],
            *[new_m[n] for n in TWIN_WEIGHTS], *[new_v[n] for n in TWIN_WEIGHTS])
```

```python
import functools

import jax
import jax.numpy as jnp
from jax import lax
from jax.experimental import pallas as pl
from jax.experimental.pallas import tpu as pltpu

F32 = jnp.float32
BF16 = jnp.bfloat16
MESH = pl.DeviceIdType.MESH
ANY = pl.BlockSpec(memory_space=pl.ANY)

N_DEV = 8
N_HEADS = 16
D_NOPE = 128
D_ROPE = 64
D_V = 128
QK_DIM = D_NOPE + D_ROPE
HEAD_PAD = 256
ROPE_THETA = 10000.0
CONV_WIDTH = 31
CONV_PAD = 32
FFN_RES = 0.5
EPS = 1e-6
ADAM_LR = 0.001
ADAM_B1 = 0.9
ADAM_B2 = 0.999
ADAM_EPS = 1e-08
ADAM_WD = 0.01
ADAM_STEP = 10
VMEM_LIMIT = 56 * 1024 * 1024


def _sds(shape, dtype):
    return jax.ShapeDtypeStruct(tuple(int(s) for s in shape), dtype)


def _tile(n, pref, mult=128):
    if n <= pref:
        return n
    t = (pref // mult) * mult
    while t >= mult:
        if n % t == 0:
            return t
        t -= mult
    return n


def _params():
    return pltpu.CompilerParams(vmem_limit_bytes=VMEM_LIMIT)


def _matmul(name, grid, ops, terms, dims, acc_shapes, outs, epilogue, extras=()):
    nk = grid[2]
    n_ops, n_ex, n_out, n_acc = len(ops), len(extras), len(outs), len(acc_shapes)

    def body(*refs):
        op_refs = refs[:n_ops]
        ex_refs = refs[n_ops:n_ops + n_ex]
        out_refs = refs[n_ops + n_ex:n_ops + n_ex + n_out]
        acc_refs = refs[n_ops + n_ex + n_out:]
        vals = {}

        def opval(i):
            if i not in vals:
                v = op_refs[i][...]
                vals[i] = v if v.dtype == BF16 else v.astype(BF16)
            return vals[i]

        parts = [None] * n_acc
        for ai, li, ri in terms:
            d = lax.dot_general(opval(li), opval(ri), (dims, ((), ())), preferred_element_type=F32)
            parts[ai] = d if parts[ai] is None else parts[ai] + d
        if nk == 1:
            epilogue(parts, ex_refs, out_refs)
            return
        k = pl.program_id(2)

        @pl.when(k == 0)
        def _():
            for a_ref, p in zip(acc_refs, parts):
                a_ref[...] = p

        @pl.when(k > 0)
        def _():
            for a_ref, p in zip(acc_refs, parts):
                a_ref[...] += p

        @pl.when(k == nk - 1)
        def _():
            epilogue([a[...] for a in acc_refs], ex_refs, out_refs)

    scratch = [pltpu.VMEM(s, F32) for s in acc_shapes] if nk > 1 else []
    res = pl.pallas_call(
        body, name=name, grid=grid,
        in_specs=[s for _, s in ops] + [s for _, s in extras],
        out_specs=[s for _, s in outs],
        out_shape=[o for o, _ in outs],
        scratch_shapes=scratch,
        compiler_params=_params(),
    )(*[a for a, _ in ops], *[a for a, _ in extras])
    return res


NN = ((1,), (0,))
NT = ((1,), (1,))
TN = ((0,), (0,))


def _store(i=0):
    def ep(accs, ex, outs):
        outs[0][...] = accs[0].astype(outs[0].dtype)
    return ep


def _mm_nn(name, a, b, out_dtype, tm=1024, tn=1024, tk=512, epilogue=None, extras=(), extra_outs=()):
    M, K = a.shape
    N = b.shape[1]
    tm, tn, tk = _tile(M, tm, 16), _tile(N, tn), _tile(K, tk)
    outs = [(_sds((M, N), out_dtype), pl.BlockSpec((tm, tn), lambda i, j, k: (i, j)))] + list(extra_outs)
    return _matmul(name, (M // tm, N // tn, K // tk),
                   [(a, pl.BlockSpec((tm, tk), lambda i, j, k: (i, k))),
                    (b, pl.BlockSpec((tk, tn), lambda i, j, k: (k, j)))],
                   [(0, 0, 1)], NN, [(tm, tn)], outs, epilogue or _store(), extras)


def _mm_nt(name, a, b, out_dtype, tm=1024, tn=1024, tk=512, epilogue=None, extras=()):
    M, K = a.shape
    N = b.shape[0]
    tm, tn, tk = _tile(M, tm, 16), _tile(N, tn), _tile(K, tk)
    outs = [(_sds((M, N), out_dtype), pl.BlockSpec((tm, tn), lambda i, j, k: (i, j)))]
    return _matmul(name, (M // tm, N // tn, K // tk),
                   [(a, pl.BlockSpec((tm, tk), lambda i, j, k: (i, k))),
                    (b, pl.BlockSpec((tn, tk), lambda i, j, k: (j, k)))],
                   [(0, 0, 1)], NT, [(tm, tn)], outs, epilogue or _store(), extras)


def _mm_tn(name, a, b, out_dtype, tm=512, tn=1024, tk=1024, scale=None):
    T, M = a.shape
    N = b.shape[1]
    tm, tn, tk = _tile(M, tm), _tile(N, tn), _tile(T, tk)

    def ep(accs, ex, outs):
        v = accs[0] if scale is None else accs[0] * scale
        outs[0][...] = v.astype(outs[0].dtype)

    outs = [(_sds((M, N), out_dtype), pl.BlockSpec((tm, tn), lambda i, j, k: (i, j)))]
    return _matmul(name, (M // tm, N // tn, T // tk),
                   [(a, pl.BlockSpec((tk, tm), lambda i, j, k: (k, i))),
                    (b, pl.BlockSpec((tk, tn), lambda i, j, k: (k, j)))],
                   [(0, 0, 1)], TN, [(tm, tn)], outs, ep)[0]


def _mm_nn_sm(name, a, w, out_dtype, tm=1024, tk=512, epilogue=None, extras=()):
    M, K = a.shape
    S, _, Ns = w.shape
    tm, tk = _tile(M, tm, 16), _tile(K, tk)
    outs = [(_sds((M, S * Ns), out_dtype), pl.BlockSpec((tm, Ns), lambda j, i, k: (i, j)))]
    return _matmul(name, (S, M // tm, K // tk),
                   [(a, pl.BlockSpec((tm, tk), lambda j, i, k: (i, k))),
                    (w, pl.BlockSpec((None, tk, Ns), lambda j, i, k: (j, k, 0)))],
                   [(0, 0, 1)], NN, [(tm, Ns)], outs, epilogue or _store(), extras)[0]


def _mm_nt_sm(name, a, w, out_dtype, tm=1024, tn=1024):
    M = a.shape[0]
    S, K, Ns = w.shape
    tm, tn = _tile(M, tm, 16), _tile(K, tn)
    outs = [(_sds((M, K), out_dtype), pl.BlockSpec((tm, tn), lambda i, n, j: (i, n)))]
    return _matmul(name, (M // tm, K // tn, S),
                   [(a, pl.BlockSpec((tm, Ns), lambda i, n, j: (i, j))),
                    (w, pl.BlockSpec((None, tn, Ns), lambda i, n, j: (j, n, 0)))],
                   [(0, 0, 1)], NT, [(tm, tn)], outs, _store())[0]


def _mm_tn_sm(name, a, b, S, out_dtype, tm=1024, tk=1024):
    T, M = a.shape
    Ns = b.shape[1] // S
    tm, tk = _tile(M, tm), _tile(T, tk)
    outs = [(_sds((S, M, Ns), out_dtype), pl.BlockSpec((None, tm, Ns), lambda j, i, k: (j, i, 0)))]
    return _matmul(name, (S, M // tm, T // tk),
                   [(a, pl.BlockSpec((tk, tm), lambda j, i, k: (k, i))),
                    (b, pl.BlockSpec((tk, Ns), lambda j, i, k: (k, j)))],
                   [(0, 0, 1)], TN, [(tm, Ns)], outs, _store())[0]


def _row_spec(shape, axis, tm):
    block = tuple(tm if d == axis else s for d, s in enumerate(shape))
    nd = len(shape)

    def imap(i):
        return tuple(i if d == axis else 0 for d in range(nd))
    return pl.BlockSpec(block, imap)


def _full_spec(shape):
    nd = len(shape)
    return pl.BlockSpec(tuple(shape), lambda i: (0,) * nd)


def _rowwise(name, fn, T, tm, rows, consts, outs, accs=()):
    tm = _tile(T, tm, 16)
    n_in = len(rows) + len(consts)
    n_out = len(outs)

    def body(*refs):
        in_refs = refs[:n_in]
        out_refs = refs[n_in:n_in + n_out]
        acc_refs = refs[n_in + n_out:]
        i = pl.program_id(0)

        def acc_add(ai, val):
            @pl.when(i == 0)
            def _():
                acc_refs[ai][...] = val

            @pl.when(i > 0)
            def _():
                acc_refs[ai][...] += val

        fn(in_refs, out_refs, acc_add)

    res = pl.pallas_call(
        body, name=name, grid=(T // tm,),
        in_specs=[_row_spec(a.shape, ax, tm) for a, ax in rows] + [_full_spec(c.shape) for c in consts],
        out_specs=[_row_spec(s, ax, tm) for s, _, ax in outs] + [_full_spec(s) for s in accs],
        out_shape=[_sds(s, d) for s, d, _ in outs] + [_sds(s, F32) for s in accs],
        compiler_params=_params(),
    )(*[a for a, _ in rows], *consts)
    return res


def _rms(x, g, n=None):
    n = x.shape[-1] if n is None else n
    return x * lax.rsqrt(jnp.sum(x * x, axis=-1, keepdims=True) * (1.0 / n) + EPS) * g


def _norm_fwd(name, h, gain):
    T, D = h.shape

    def fn(ins, outs, acc):
        outs[0][...] = _rms(ins[0][...], ins[1][...]).astype(BF16)

    return _rowwise(name, fn, T, 256, [(h, 0)], [gain], [((T, D), BF16, 0)])[0]


def _norm_bwd(name, h, gain, dhn, dh_res):
    T, D = h.shape

    def fn(ins, outs, acc):
        _, vjp = jax.vjp(_rms, ins[0][...], ins[3][...])
        dh, dg = vjp(ins[1][...])
        dh = dh + ins[2][...]
        outs[0][...] = dh
        outs[1][...] = dh.astype(BF16)
        acc(0, dg)

    return _rowwise(name, fn, T, 256, [(h, 0), (dhn, 0), (dh_res, 0)], [gain],
                    [((T, D), F32, 0), ((T, D), BF16, 0)], [(1, D)])


def _ffn_fwd(tag, h, gain, w_in, w_out):
    T, D = h.shape
    S, _, Ns = w_in.shape
    half = S // 2
    F = half * Ns
    hn = _norm_fwd(tag + "_norm", h, gain)
    tm, tk = _tile(T, 1024, 16), _tile(D, 512)

    def ep(accs, ex, outs):
        g, u = accs
        act = g * jax.nn.sigmoid(g) * u
        outs[0][0] = g.astype(BF16)
        outs[0][1] = u.astype(BF16)
        outs[1][...] = act.astype(BF16)

    gu, act = _matmul(
        tag + "_in", (half, T // tm, D // tk),
        [(hn, pl.BlockSpec((tm, tk), lambda j, i, k: (i, k))),
         (w_in, pl.BlockSpec((None, tk, Ns), lambda j, i, k: (j, k, 0))),
         (w_in, pl.BlockSpec((None, tk, Ns), lambda j, i, k: (j + half, k, 0)))],
        [(0, 0, 1), (1, 0, 2)], NN, [(tm, Ns), (tm, Ns)],
        [(_sds((2, T, F), BF16), pl.BlockSpec((2, tm, Ns), lambda j, i, k: (0, i, j))),
         (_sds((T, F), BF16), pl.BlockSpec((tm, Ns), lambda j, i, k: (i, j)))],
        ep)

    def ep_out(accs, ex, outs):
        outs[0][...] = ex[0][...] + FFN_RES * accs[0]

    tn = _tile(D, 1024)
    h_new = _mm_nn(tag + "_out", act, w_out, F32, epilogue=ep_out,
                   extras=[(h, pl.BlockSpec((tm, tn), lambda i, j, k: (i, j)))])[0]
    return h_new, (h, hn, gu, act)


def _ffn_bwd(tag, saved, gain, w_in, w_out, dh, dhb):
    h, hn, gu, act = saved
    T, D = h.shape
    S, _, Ns = w_in.shape
    half = S // 2
    F = half * Ns
    tm, tk = _tile(T, 1024, 16), _tile(D, 512)

    def ep(accs, ex, outs):
        dact = FFN_RES * accs[0]
        g = ex[0][0].astype(F32)
        u = ex[0][1].astype(F32)
        sg = jax.nn.sigmoid(g)
        outs[0][0] = (dact * u * (sg * (1.0 + g * (1.0 - sg)))).astype(BF16)
        outs[0][1] = (dact * (g * sg)).astype(BF16)

    gu_spec = pl.BlockSpec((2, tm, Ns), lambda i, j, k: (0, i, j))
    dgu = _matmul(
        tag + "_dact", (T // tm, half, D // tk),
        [(dhb, pl.BlockSpec((tm, tk), lambda i, j, k: (i, k))),
         (w_out, pl.BlockSpec((Ns, tk), lambda i, j, k: (j, k)))],
        [(0, 0, 1)], NT, [(tm, Ns)],
        [(_sds((2, T, F), BF16), gu_spec)], ep, extras=[(gu, gu_spec)])[0]

    dw_out = _mm_tn(tag + "_dwout", act, dhb, BF16, tm=512, tn=2048, tk=1024, scale=FFN_RES)

    tn = _tile(D, 1024)
    dhn = _matmul(
        tag + "_dhn", (T // tm, D // tn, half),
        [(dgu, pl.BlockSpec((None, tm, Ns), lambda i, n, j: (0, i, j))),
         (dgu, pl.BlockSpec((None, tm, Ns), lambda i, n, j: (1, i, j))),
         (w_in, pl.BlockSpec((None, tn, Ns), lambda i, n, j: (j, n, 0))),
         (w_in, pl.BlockSpec((None, tn, Ns), lambda i, n, j: (j + half, n, 0)))],
        [(0, 0, 2), (0, 1, 3)], NT, [(tm, tn)],
        [(_sds((T, D), F32), pl.BlockSpec((tm, tn), lambda i, n, j: (i, n)))], _store())[0]

    tkd, tt = _tile(D, 1024), _tile(T, 1024)
    dw_in = _matmul(
        tag + "_dwin", (S, D // tkd, T // tt),
        [(hn, pl.BlockSpec((tt, tkd), lambda j, i, k: (k, i))),
         (dgu, pl.BlockSpec((None, tt, Ns), lambda j, i, k: (j // half, k, j % half)))],
        [(0, 0, 1)], TN, [(tkd, Ns)],
        [(_sds((S, D, Ns), BF16), pl.BlockSpec((None, tkd, Ns), lambda j, i, k: (j, i, 0)))], _store())[0]

    dh_in, dh_in_b, dgain = _norm_bwd(tag + "_dnorm", h, gain, dhn, dh)
    return dh_in, dh_in_b, dgain, dw_in, dw_out


def _ple_fwd(tag, h, p, w_proj, ple_norm, gate_norm, w_gate):
    T, D = h.shape
    e_raw = _mm_nn_sm(tag + "_proj", p, w_proj, F32, tm=1024, tk=512)
    hn = _norm_fwd(tag + "_norm", h, gate_norm)
    gate_raw = _mm_nn(tag + "_gate", hn, w_gate, F32)[0]

    def fn(ins, outs, acc):
        e = _rms(ins[1][...], ins[3][...])
        outs[0][...] = ins[0][...] + e * jax.nn.sigmoid(ins[2][...])

    h_new = _rowwise(tag + "_mix", fn, T, 256, [(h, 0), (e_raw, 0), (gate_raw, 0)], [ple_norm],
                     [((T, D), F32, 0)])[0]
    return h_new, (h, hn, e_raw, gate_raw)


def _ple_bwd(tag, saved, p, w_proj, ple_norm, gate_norm, w_gate, dh, dhb):
    h, hn, e_raw, gate_raw = saved
    T, D = h.shape
    S = w_proj.shape[0]

    def fn(ins, outs, acc):
        def f(e_raw_, gate_raw_, g_):
            return _rms(e_raw_, g_) * jax.nn.sigmoid(gate_raw_)
        _, vjp = jax.vjp(f, ins[0][...], ins[1][...], ins[3][...])
        de, dgate, dg = vjp(ins[2][...])
        outs[0][...] = de.astype(BF16)
        outs[1][...] = dgate.astype(BF16)
        acc(0, dg)

    de, dgate, d_ple_norm = _rowwise(tag + "_dmix", fn, T, 256, [(e_raw, 0), (gate_raw, 0), (dh, 0)], [ple_norm],
                                     [((T, D), BF16, 0), ((T, D), BF16, 0)], [(1, D)])
    dw_proj = _mm_tn_sm(tag + "_dwproj", p, de, S, BF16)
    dw_gate = _mm_tn(tag + "_dwgate", hn, dgate, BF16)
    dhn = _mm_nt(tag + "_dhn", dgate, w_gate, F32)[0]
    dh_in, dh_in_b, d_gate_norm = _norm_bwd(tag + "_dnorm", h, gate_norm, dhn, dh)
    return dh_in, dh_in_b, d_ple_norm, d_gate_norm, dw_proj, dw_gate


def _rope(t, c, s1, s2):
    q = D_ROPE // 2
    return t * c + pltpu.roll(t, q, 1) * s1 + pltpu.roll(t, 128 - q, 1) * s2


def _rope_t(d, c, s1, s2):
    q = D_ROPE // 2
    return d * c + pltpu.roll(d * s1, 128 - q, 1) + pltpu.roll(d * s2, q, 1)


def _head_norm(lo, hi, g_lo, g_hi):
    ms = (jnp.sum(lo * lo, axis=-1, keepdims=True) + jnp.sum(hi * hi, axis=-1, keepdims=True)) * (1.0 / QK_DIM)
    inv = lax.rsqrt(ms + EPS)
    return lo * inv * g_lo, hi * inv * g_hi


def _qk_prep(qraw, kvraw, lat, tabs, gq, gk, H):
    T = qraw.shape[0]
    koff = lat.shape[1] - 128

    def fn(ins, outs, acc):
        q_ref, kv_ref, lat_ref, c_ref, s1_ref, s2_ref, gq_ref, gk_ref = ins
        c, s1, s2 = c_ref[...], s1_ref[...], s2_ref[...]
        kr = lat_ref[:, koff:koff + 128]
        for hd in range(H):
            o = hd * HEAD_PAD
            lo, hi = _head_norm(q_ref[:, o:o + 128], q_ref[:, o + 128:o + 256], gq_ref[:, 0:128], gq_ref[:, 128:256])
            outs[0][hd, :, 0:128] = lo.astype(BF16)
            outs[0][hd, :, 128:256] = _rope(hi, c, s1, s2).astype(BF16)
            lo, hi = _head_norm(kv_ref[:, o:o + 128], kr, gk_ref[:, 0:128], gk_ref[:, 128:256])
            outs[1][hd, :, 0:128] = lo.astype(BF16)
            outs[1][hd, :, 128:256] = _rope(hi, c, s1, s2).astype(BF16)
            outs[2][hd] = kv_ref[:, o + 128:o + 256].astype(BF16)

    return _rowwise("mla_qkprep", fn, T, 256, [(qraw, 0), (kvraw, 0), (lat, 0)] + [(t, 0) for t in tabs], [gq, gk],
                    [((H, T, HEAD_PAD), BF16, 1), ((H, T, HEAD_PAD), BF16, 1), ((H, T, D_V), BF16, 1)])


def _qk_prep_bwd(qraw, kvraw, lat, tabs, gq, gk, dQ, dK, dV, H):
    T = qraw.shape[0]
    koff = lat.shape[1] - 128

    def fn(ins, outs, acc):
        q_ref, kv_ref, lat_ref, c_ref, s1_ref, s2_ref, dq_ref, dk_ref, dv_ref, gq_ref, gk_ref = ins
        c, s1, s2 = c_ref[...], s1_ref[...], s2_ref[...]
        kr = lat_ref[:, koff:koff + 128]
        dkr = jnp.zeros_like(kr)
        dg = [None] * 4
        for hd in range(H):
            o = hd * HEAD_PAD
            _, vjp = jax.vjp(_head_norm, q_ref[:, o:o + 128], q_ref[:, o + 128:o + 256],
                             gq_ref[:, 0:128], gq_ref[:, 128:256])
            dlo, dhi, dg0, dg1 = vjp((dq_ref[hd, :, 0:128], _rope_t(dq_ref[hd, :, 128:256], c, s1, s2)))
            outs[0][:, o:o + 128] = dlo.astype(BF16)
            outs[0][:, o + 128:o + 256] = dhi.astype(BF16)
            _, vjp = jax.vjp(_head_norm, kv_ref[:, o:o + 128], kr, gk_ref[:, 0:128], gk_ref[:, 128:256])
            dlo, dhi, dg2, dg3 = vjp((dk_ref[hd, :, 0:128], _rope_t(dk_ref[hd, :, 128:256], c, s1, s2)))
            outs[1][:, o:o + 128] = dlo.astype(BF16)
            outs[1][:, o + 128:o + 256] = dv_ref[hd].astype(BF16)
            dkr = dkr + dhi
            for n, v in enumerate((dg0, dg1, dg2, dg3)):
                dg[n] = v if dg[n] is None else dg[n] + v
        outs[2][...] = dkr
        for n in range(4):
            acc(n, dg[n])

    W = H * HEAD_PAD
    return _rowwise("mla_dqkprep", fn, T, 128,
                    [(qraw, 0), (kvraw, 0), (lat, 0)] + [(t, 0) for t in tabs] + [(dQ, 1), (dK, 1), (dV, 1)], [gq, gk],
                    [((T, W), BF16, 0), ((T, W), BF16, 0), ((T, 128), F32, 0)], [(1, 128)] * 4)


def _attn_probs(q_ref, k_ref, i, tq, T):
    s = lax.dot_general(q_ref[...], k_ref[...], (NT, ((), ())), preferred_element_type=F32) * (QK_DIM ** -0.5)
    row = i * tq + lax.broadcasted_iota(jnp.int32, (tq, T), 0)
    col = lax.broadcasted_iota(jnp.int32, (tq, T), 1)
    s = jnp.where(col <= row, s, -jnp.inf)
    p = jnp.exp(s - jnp.max(s, axis=-1, keepdims=True))
    return p / jnp.sum(p, axis=-1, keepdims=True)


def _attn_fwd(Q, K, V):
    H, T, _ = Q.shape
    tq = _tile(T, 256)

    def body(q_ref, k_ref, v_ref, o_ref):
        p = _attn_probs(q_ref, k_ref, pl.program_id(1), tq, T)
        o_ref[...] = jnp.dot(p.astype(BF16), v_ref[...], preferred_element_type=F32).astype(BF16)

    return pl.pallas_call(
        body, name="mla_attn", grid=(H, T // tq),
        in_specs=[pl.BlockSpec((None, tq, HEAD_PAD), lambda h, i: (h, i, 0)),
                  pl.BlockSpec((None, T, HEAD_PAD), lambda h, i: (h, 0, 0)),
                  pl.BlockSpec((None, T, D_V), lambda h, i: (h, 0, 0))],
        out_specs=pl.BlockSpec((tq, D_V), lambda h, i: (i, h)),
        out_shape=_sds((T, H * D_V), BF16),
        compiler_params=_params(),
    )(Q, K, V)


def _attn_bwd(Q, K, V, dO):
    H, T, _ = Q.shape
    tq = _tile(T, 256)

    def body(q_ref, k_ref, v_ref, do_ref, dq_ref, dk_ref, dv_ref):
        i = pl.program_id(1)
        p = _attn_probs(q_ref, k_ref, i, tq, T)
        do = do_ref[...]
        dv = lax.dot_general(p.astype(BF16), do, (TN, ((), ())), preferred_element_type=F32)
        dp = lax.dot_general(do, v_ref[...], (NT, ((), ())), preferred_element_type=F32)
        ds = p * (dp - jnp.sum(p * dp, axis=-1, keepdims=True)) * (QK_DIM ** -0.5)
        dsb = ds.astype(BF16)
        dq_ref[...] = jnp.dot(dsb, k_ref[...], preferred_element_type=F32)
        dk = lax.dot_general(dsb, q_ref[...], (TN, ((), ())), preferred_element_type=F32)

        @pl.when(i == 0)
        def _():
            dk_ref[...] = dk
            dv_ref[...] = dv

        @pl.when(i > 0)
        def _():
            dk_ref[...] += dk
            dv_ref[...] += dv

    return pl.pallas_call(
        body, name="mla_dattn", grid=(H, T // tq),
        in_specs=[pl.BlockSpec((None, tq, HEAD_PAD), lambda h, i: (h, i, 0)),
                  pl.BlockSpec((None, T, HEAD_PAD), lambda h, i: (h, 0, 0)),
                  pl.BlockSpec((None, T, D_V), lambda h, i: (h, 0, 0)),
                  pl.BlockSpec((tq, D_V), lambda h, i: (i, h))],
        out_specs=[pl.BlockSpec((None, tq, HEAD_PAD), lambda h, i: (h, i, 0)),
                   pl.BlockSpec((None, T, HEAD_PAD), lambda h, i: (h, 0, 0)),
                   pl.BlockSpec((None, T, D_V), lambda h, i: (h, 0, 0))],
        out_shape=[_sds((H, T, HEAD_PAD), F32), _sds((H, T, HEAD_PAD), F32), _sds((H, T, D_V), F32)],
        compiler_params=_params(),
    )(Q, K, V, dO)


def _mla_fwd(h, gain, tabs, w, q_lat_norm, kv_lat_norm, gq, gk):
    T, D = h.shape
    H = w["uq"].shape[0] * w["uq"].shape[2] // HEAD_PAD
    QL, KL = q_lat_norm.shape[1], kv_lat_norm.shape[1]
    hn = _norm_fwd("mla_norm", h, gain)
    lat = _mm_nn("mla_lat", hn, w["m_in"], F32, tn=w["m_in"].shape[1])[0]

    def fn(ins, outs, acc):
        outs[0][...] = _rms(ins[0][:, 0:QL], ins[1][...]).astype(BF16)
        outs[1][...] = _rms(ins[0][:, QL:QL + KL], ins[2][...]).astype(BF16)

    cq, ckv = _rowwise("mla_latnorm", fn, T, 256, [(lat, 0)], [q_lat_norm, kv_lat_norm],
                       [((T, QL), BF16, 0), ((T, KL), BF16, 0)])
    qraw = _mm_nn_sm("mla_uq", cq, w["uq"], F32)
    kvraw = _mm_nn_sm("mla_ukv", ckv, w["ukv"], F32)
    Q, K, V = _qk_prep(qraw, kvraw, lat, tabs, gq, gk, H)
    O = _attn_fwd(Q, K, V)

    def ep_out(accs, ex, outs):
        outs[0][...] = ex[0][...] + accs[0]

    tm, tn = _tile(T, 1024, 16), _tile(D, 1024)
    h_new = _mm_nn("mla_out", O, w["wo"], F32, epilogue=ep_out,
                   extras=[(h, pl.BlockSpec((tm, tn), lambda i, j, k: (i, j)))])[0]
    return h_new, (h, hn, lat, cq, ckv, qraw, kvraw, Q, K, V, O)


def _mla_bwd(saved, gain, tabs, w, q_lat_norm, kv_lat_norm, gq, gk, dh, dhb):
    h, hn, lat, cq, ckv, qraw, kvraw, Q, K, V, O = saved
    T, D = h.shape
    H = Q.shape[0]
    S = w["uq"].shape[0]
    QL, KL = q_lat_norm.shape[1], kv_lat_norm.shape[1]
    dO = _mm_nt("mla_dO", dhb, w["wo"], BF16)[0]
    dwo = _mm_tn("mla_dwo", O, dhb, BF16)
    dQ, dK, dV = _attn_bwd(Q, K, V, dO)
    dqraw, dkvraw, dkr, dgq0, dgq1, dgk0, dgk1 = _qk_prep_bwd(qraw, kvraw, lat, tabs, gq, gk, dQ, dK, dV, H)
    dcq = _mm_nt_sm("mla_dcq", dqraw, w["uq"], F32)
    dckv = _mm_nt_sm("mla_dckv", dkvraw, w["ukv"], F32)
    dwuq = _mm_tn_sm("mla_dwuq", cq, dqraw, S, BF16)
    dwukv = _mm_tn_sm("mla_dwukv", ckv, dkvraw, S, BF16)

    def fn(ins, outs, acc):
        _, vjp = jax.vjp(_rms, ins[0][:, 0:QL], ins[4][...])
        d, dgq_ = vjp(ins[1][...])
        outs[0][:, 0:QL] = d.astype(BF16)
        _, vjp = jax.vjp(_rms, ins[0][:, QL:QL + KL], ins[5][...])
        d, dgkv_ = vjp(ins[2][...])
        outs[0][:, QL:QL + KL] = d.astype(BF16)
        outs[0][:, QL + KL:QL + KL + 128] = ins[3][...].astype(BF16)
        acc(0, dgq_)
        acc(1, dgkv_)

    dlat, d_qln, d_kvln = _rowwise("mla_dlatnorm", fn, T, 256, [(lat, 0), (dcq, 0), (dckv, 0), (dkr, 0)],
                                   [q_lat_norm, kv_lat_norm], [(lat.shape, BF16, 0)], [(1, QL), (1, KL)])
    dhn = _mm_nt("mla_dhn", dlat, w["m_in"], F32, tk=lat.shape[1])[0]
    dw_min = _mm_tn("mla_dwin", hn, dlat, BF16, tn=lat.shape[1])
    dh_in, dh_in_b, dgain = _norm_bwd("mla_dnorm", h, gain, dhn, dh)
    d_gq = jnp.concatenate([dgq0, dgq1], axis=1)[:, :QK_DIM]
    d_gk = jnp.concatenate([dgk0, dgk1], axis=1)[:, :QK_DIM]
    return dh_in, dh_in_b, dgain, d_qln, d_kvln, d_gq, d_gk, dw_min, dwuq, dwukv, dwo


def _conv_rows(T):
    return _tile(T, 128, 8)


def _dwconv_fwd(u, w_dw, b_dw):
    T, C = u.shape
    tc, R = _tile(C, 256), _conv_rows(T)
    off = CONV_PAD - (CONV_WIDTH - 1)

    def body(u_ref, w_ref, b_ref, y_ref, pad_ref):
        pad_ref[0:CONV_PAD, :] = jnp.zeros((CONV_PAD, tc), F32)
        pad_ref[CONV_PAD:CONV_PAD + T, :] = u_ref[...]
        for r in range(T // R):
            acc = jnp.broadcast_to(b_ref[...], (R, tc))
            for j in range(CONV_WIDTH):
                acc = acc + w_ref[j:j + 1, :] * pad_ref[r * R + off + j:r * R + off + j + R, :]
            y_ref[r * R:(r + 1) * R, :] = acc

    return pl.pallas_call(
        body, name="conv_dw", grid=(C // tc,),
        in_specs=[pl.BlockSpec((T, tc), lambda c: (0, c)), pl.BlockSpec((32, tc), lambda c: (0, c)),
                  pl.BlockSpec((1, tc), lambda c: (0, c))],
        out_specs=pl.BlockSpec((T, tc), lambda c: (0, c)),
        out_shape=_sds((T, C), F32),
        scratch_shapes=[pltpu.VMEM((T + CONV_PAD, tc), F32)],
        compiler_params=_params(),
    )(u, w_dw, b_dw)


def _dwconv_bwd(u, w_dw, dy):
    T, C = u.shape
    tc, R = _tile(C, 256), _conv_rows(T)
    off = CONV_PAD - (CONV_WIDTH - 1)

    def body(u_ref, w_ref, dy_ref, du_ref, dw_ref, db_ref, upad_ref, dpad_ref):
        upad_ref[0:CONV_PAD, :] = jnp.zeros((CONV_PAD, tc), F32)
        upad_ref[CONV_PAD:CONV_PAD + T, :] = u_ref[...]
        dpad_ref[0:T, :] = dy_ref[...]
        dpad_ref[T:T + CONV_PAD, :] = jnp.zeros((CONV_PAD, tc), F32)
        for r in range(T // R):
            acc = jnp.zeros((R, tc), F32)
            for j in range(CONV_WIDTH):
                s = r * R + (CONV_WIDTH - 1) - j
                acc = acc + w_ref[j:j + 1, :] * dpad_ref[s:s + R, :]
            du_ref[r * R:(r + 1) * R, :] = acc
        for j in range(CONV_WIDTH):
            acc = jnp.zeros((R, tc), F32)
            for r in range(T // R):
                acc = acc + dy_ref[r * R:(r + 1) * R, :] * upad_ref[r * R + off + j:r * R + off + j + R, :]
            dw_ref[j:j + 1, :] = jnp.sum(acc, axis=0, keepdims=True)
        dw_ref[CONV_WIDTH:32, :] = jnp.zeros((32 - CONV_WIDTH, tc), F32)
        db_ref[...] = jnp.sum(dy_ref[...], axis=0, keepdims=True)

    return pl.pallas_call(
        body, name="conv_ddw", grid=(C // tc,),
        in_specs=[pl.BlockSpec((T, tc), lambda c: (0, c)), pl.BlockSpec((32, tc), lambda c: (0, c)),
                  pl.BlockSpec((T, tc), lambda c: (0, c))],
        out_specs=[pl.BlockSpec((T, tc), lambda c: (0, c)), pl.BlockSpec((32, tc), lambda c: (0, c)),
                   pl.BlockSpec((1, tc), lambda c: (0, c))],
        out_shape=[_sds((T, C), F32), _sds((32, C), F32), _sds((1, C), F32)],
        scratch_shapes=[pltpu.VMEM((T + CONV_PAD, tc), F32), pltpu.VMEM((T + CONV_PAD, tc), F32)],
        compiler_params=_params(),
    )(u, w_dw, dy)


def _ln_silu(y, g, b):
    mu = jnp.mean(y, axis=-1, keepdims=True)
    yc = y - mu
    z = yc * lax.rsqrt(jnp.mean(yc * yc, axis=-1, keepdims=True) + EPS) * g + b
    return z * jax.nn.sigmoid(z)


def _conv_fwd(h, gain, w, b_pw1, w_dw, b_dw, ln_g, ln_b):
    T, D = h.shape
    S, _, Ns = w["pw1"].shape
    half = S // 2
    C = half * Ns
    hn = _norm_fwd("conv_norm", h, gain)
    tm, tk = _tile(T, 1024, 16), _tile(D, 512)

    def ep(accs, ex, outs):
        a = accs[0] + ex[0][...]
        g = accs[1] + ex[1][...]
        outs[0][0] = a.astype(BF16)
        outs[0][1] = g.astype(BF16)
        outs[1][...] = a * jax.nn.sigmoid(g)

    ag, u = _matmul(
        "conv_pw1", (half, T // tm, D // tk),
        [(hn, pl.BlockSpec((tm, tk), lambda j, i, k: (i, k))),
         (w["pw1"], pl.BlockSpec((None, tk, Ns), lambda j, i, k: (j, k, 0))),
         (w["pw1"], pl.BlockSpec((None, tk, Ns), lambda j, i, k: (j + half, k, 0)))],
        [(0, 0, 1), (1, 0, 2)], NN, [(tm, Ns), (tm, Ns)],
        [(_sds((2, T, C), BF16), pl.BlockSpec((2, tm, Ns), lambda j, i, k: (0, i, j))),
         (_sds((T, C), F32), pl.BlockSpec((tm, Ns), lambda j, i, k: (i, j)))],
        ep,
        extras=[(b_pw1, pl.BlockSpec((None, 1, Ns), lambda j, i, k: (j, 0, 0))),
                (b_pw1, pl.BlockSpec((None, 1, Ns), lambda j, i, k: (j + half, 0, 0)))])
    y = _dwconv_fwd(u, w_dw, b_dw)

    def fn(ins, outs, acc):
        outs[0][...] = _ln_silu(ins[0][...], ins[1][...], ins[2][...]).astype(BF16)

    s = _rowwise("conv_ln", fn, T, 256, [(y, 0)], [ln_g, ln_b], [((T, C), BF16, 0)])[0]

    def ep_out(accs, ex, outs):
        outs[0][...] = ex[0][...] + accs[0]

    tn = _tile(D, 1024)
    h_new = _mm_nn("conv_pw2", s, w["pw2"], F32, epilogue=ep_out,
                   extras=[(h, pl.BlockSpec((tm, tn), lambda i, j, k: (i, j)))])[0]
    return h_new, (h, hn, ag, u, y, s)


def _conv_bwd(saved, gain, w, w_dw, ln_g, ln_b, dh, dhb):
    h, hn, ag, u, y, s = saved
    T, D = h.shape
    S, _, Ns = w["pw1"].shape
    half = S // 2
    C = half * Ns
    ds = _mm_nt("conv_ds", dhb, w["pw2"], F32)[0]
    dw_pw2 = _mm_tn("conv_dwpw2", s, dhb, BF16)

    def fn(ins, outs, acc):
        _, vjp = jax.vjp(_ln_silu, ins[0][...], ins[2][...], ins[3][...])
        dy, dg, db = vjp(ins[1][...])
        outs[0][...] = dy
        acc(0, dg)
        acc(1, db)

    dy, d_ln_g, d_ln_b = _rowwise("conv_dln", fn, T, 256, [(y, 0), (ds, 0)], [ln_g, ln_b],
                                  [((T, C), F32, 0)], [(1, C), (1, C)])
    du, d_w_dw, d_b_dw = _dwconv_bwd(u, w_dw, dy)

    def fn2(ins, outs, acc):
        a = ins[0][0].astype(F32)
        g = ins[0][1].astype(F32)
        du_ = ins[1][...]
        sg = jax.nn.sigmoid(g)
        da = du_ * sg
        dg = du_ * a * sg * (1.0 - sg)
        outs[0][0] = da.astype(BF16)
        outs[0][1] = dg.astype(BF16)
        acc(0, jnp.sum(da, axis=0, keepdims=True))
        acc(1, jnp.sum(dg, axis=0, keepdims=True))

    dag, d_b_a, d_b_g = _rowwise("conv_dglu", fn2, T, 256, [(ag, 1), (du, 0)], [],
                                 [((2, T, C), BF16, 1)], [(1, C), (1, C)])
    tm, tn = _tile(T, 1024, 16), _tile(D, 1024)
    dhn = _matmul(
        "conv_dhn", (T // tm, D // tn, half),
        [(dag, pl.BlockSpec((None, tm, Ns), lambda i, n, j: (0, i, j))),
         (dag, pl.BlockSpec((None, tm, Ns), lambda i, n, j: (1, i, j))),
         (w["pw1"], pl.BlockSpec((None, tn, Ns), lambda i, n, j: (j, n, 0))),
         (w["pw1"], pl.BlockSpec((None, tn, Ns), lambda i, n, j: (j + half, n, 0)))],
        [(0, 0, 2), (0, 1, 3)], NT, [(tm, tn)],
        [(_sds((T, D), F32), pl.BlockSpec((tm, tn), lambda i, n, j: (i, n)))], _store())[0]
    tkd, tt = _tile(D, 1024), _tile(T, 1024)
    dw_pw1 = _matmul(
        "conv_dwpw1", (S, D // tkd, T // tt),
        [(hn, pl.BlockSpec((tt, tkd), lambda j, i, k: (k, i))),
         (dag, pl.BlockSpec((None, tt, Ns), lambda j, i, k: (j // half, k, j % half)))],
        [(0, 0, 1)], TN, [(tkd, Ns)],
        [(_sds((S, D, Ns), BF16), pl.BlockSpec((None, tkd, Ns), lambda j, i, k: (j, i, 0)))], _store())[0]
    dh_in, dh_in_b, dgain = _norm_bwd("conv_dnorm", h, gain, dhn, dh)
    d_b_pw1 = jnp.concatenate([d_b_a, d_b_g], axis=1)
    return dh_in, dh_in_b, dgain, dw_pw1, d_b_pw1, d_w_dw, d_b_dw, d_ln_g, d_ln_b, dw_pw2


def _loss_head(y, target):
    T, D = y.shape

    def fn(ins, outs, acc):
        e = ins[0][...] - ins[1][...]
        d = e * (1.0 / D)
        outs[0][...] = d
        outs[1][...] = d.astype(BF16)
        part = jnp.sum(jnp.sum(e * e, axis=-1, keepdims=True), axis=0, keepdims=True) * (0.5 / D)
        acc(0, jnp.broadcast_to(part, (1, 128)))

    return _rowwise("loss_head", fn, T, 256, [(y, 0), (target, 0)], [], [((T, D), F32, 0), ((T, D), BF16, 0)],
                    [(1, 128)])


def _place():
    return lax.axis_index("x"), lax.axis_index("y"), lax.axis_index("c")


def _peer(j):
    x, y, c = _place()
    return (1 - x if j & 4 else x, 1 - y if j & 2 else y, 1 - c if j & 1 else c)


def _index(place):
    return 4 * place[0] + 2 * place[1] + place[2]


def _all_gather(name, shards):
    n = len(shards)

    def body(*refs):
        ins, outs = refs[:n], refs[n:2 * n]
        send, recv, local = refs[2 * n:]
        me = _index(_place())
        started = []
        for w in range(n):
            lc = pltpu.make_async_copy(ins[w], outs[w].at[me], local.at[w])
            lc.start()
            started.append(lc)
        sends = []
        for w in range(n):
            for j in range(1, N_DEV):
                cp = pltpu.make_async_remote_copy(src_ref=ins[w], dst_ref=outs[w].at[me], send_sem=send.at[w, j - 1],
                                                  recv_sem=recv.at[w, j - 1], device_id=_peer(j), device_id_type=MESH)
                cp.start()
                sends.append(cp)
        for w in range(n):
            for j in range(1, N_DEV):
                pltpu.make_async_remote_copy(src_ref=ins[w], dst_ref=outs[w].at[_index(_peer(j))],
                                             send_sem=send.at[w, j - 1], recv_sem=recv.at[w, j - 1],
                                             device_id=_peer(j), device_id_type=MESH).wait_recv()
        for cp in sends:
            cp.wait_send()
        for lc in started:
            lc.wait()

    return pl.pallas_call(
        body, name=name,
        in_specs=[ANY] * n, out_specs=[ANY] * n,
        out_shape=[_sds((N_DEV,) + s.shape, s.dtype) for s in shards],
        scratch_shapes=[pltpu.SemaphoreType.DMA((n, N_DEV - 1)), pltpu.SemaphoreType.DMA((n, N_DEV - 1)),
                        pltpu.SemaphoreType.DMA((n,))],
    )(*shards)


def _reduce_scatter_move(name, groups):
    flat = [(gi, li, g) for gi, grp in enumerate(groups) for li, g in enumerate(grp)]
    n, ng = len(flat), len(groups)

    def body(*refs):
        ins, outs = refs[:n], refs[n:n + ng]
        send, recv, local = refs[n + ng:]
        me = _index(_place())
        started = []
        for w, (gi, li, _) in enumerate(flat):
            lc = pltpu.make_async_copy(ins[w].at[me], outs[gi].at[me, li], local.at[w])
            lc.start()
            started.append(lc)
        sends = []
        for w, (gi, li, _) in enumerate(flat):
            for j in range(1, N_DEV):
                cp = pltpu.make_async_remote_copy(src_ref=ins[w].at[_index(_peer(j))], dst_ref=outs[gi].at[me, li],
                                                  send_sem=send.at[w, j - 1], recv_sem=recv.at[w, j - 1],
                                                  device_id=_peer(j), device_id_type=MESH)
                cp.start()
                sends.append(cp)
        for w, (gi, li, _) in enumerate(flat):
            for j in range(1, N_DEV):
                pltpu.make_async_remote_copy(src_ref=ins[w].at[me], dst_ref=outs[gi].at[_index(_peer(j)), li],
                                             send_sem=send.at[w, j - 1], recv_sem=recv.at[w, j - 1],
                                             device_id=_peer(j), device_id_type=MESH).wait_recv()
        for cp in sends:
            cp.wait_send()
        for lc in started:
            lc.wait()

    return pl.pallas_call(
        body, name=name,
        in_specs=[ANY] * n, out_specs=[ANY] * ng,
        out_shape=[_sds((N_DEV, len(grp)) + grp[0].shape[1:], grp[0].dtype) for grp in groups],
        scratch_shapes=[pltpu.SemaphoreType.DMA((n, N_DEV - 1)), pltpu.SemaphoreType.DMA((n, N_DEV - 1)),
                        pltpu.SemaphoreType.DMA((n,))],
    )(*[g for _, _, g in flat])


def _all_reduce_small(pack):
    R, C = pack.shape

    def body(x_ref, o_ref, all_ref, send, recv):
        me = _index(_place())
        all_ref[me] = x_ref[...]
        sends = []
        for j in range(1, N_DEV):
            cp = pltpu.make_async_remote_copy(src_ref=x_ref, dst_ref=all_ref.at[me], send_sem=send.at[j - 1],
                                              recv_sem=recv.at[j - 1], device_id=_peer(j), device_id_type=MESH)
            cp.start()
            sends.append(cp)
        for j in range(1, N_DEV):
            pltpu.make_async_remote_copy(src_ref=x_ref, dst_ref=all_ref.at[_index(_peer(j))], send_sem=send.at[j - 1],
                                         recv_sem=recv.at[j - 1], device_id=_peer(j), device_id_type=MESH).wait_recv()
        for cp in sends:
            cp.wait_send()
        total = all_ref[0]
        for k in range(1, N_DEV):
            total = total + all_ref[k]
        o_ref[...] = total

    return pl.pallas_call(
        body, name="all_reduce_small",
        in_specs=[pl.BlockSpec(memory_space=pltpu.VMEM)], out_specs=pl.BlockSpec(memory_space=pltpu.VMEM),
        out_shape=_sds((R, C), F32),
        scratch_shapes=[pltpu.VMEM((N_DEV, R, C), F32), pltpu.SemaphoreType.DMA((N_DEV - 1,)),
                        pltpu.SemaphoreType.DMA((N_DEV - 1,))],
        compiler_params=_params(),
    )(pack)


def _cast_layer(name, w, layer):
    _, R, C = w.shape
    tr = _tile(R, 256, 16)

    def body(w_ref, o_ref):
        o_ref[...] = w_ref[...].astype(BF16)

    return pl.pallas_call(
        body, name=name, grid=(R // tr,),
        in_specs=[pl.BlockSpec((None, tr, C), lambda i: (layer, i, 0))],
        out_specs=pl.BlockSpec((tr, C), lambda i: (i, 0)),
        out_shape=_sds((R, C), BF16),
    )(w)


def _slot_sum(name, slots):
    _, L, R, C = slots.shape
    tr = _tile(R, 128, 16)

    def body(s_ref, o_ref):
        total = s_ref[0].astype(F32)
        for k in range(1, N_DEV):
            total = total + s_ref[k].astype(F32)
        o_ref[...] = total

    return pl.pallas_call(
        body, name=name, grid=(L, R // tr),
        in_specs=[pl.BlockSpec((N_DEV, None, tr, C), lambda l, i: (0, l, i, 0))],
        out_specs=pl.BlockSpec((None, tr, C), lambda l, i: (l, i, 0)),
        out_shape=_sds((L, R, C), F32),
        compiler_params=_params(),
    )(slots)


def _adamw(name, w, g, m, v):
    shape = w.shape
    R, C = shape[-2], shape[-1]
    L = 1
    for s in shape[:-2]:
        L *= s
    w3, g3, m3, v3 = (a.reshape(L, R, C) for a in (w, g, m, v))
    tr = _tile(R, 128, 8)
    c1 = 1.0 / (1.0 - ADAM_B1 ** ADAM_STEP)
    c2 = 1.0 / (1.0 - ADAM_B2 ** ADAM_STEP)

    def body(w_ref, g_ref, m_ref, v_ref, d_ref, nm_ref, nv_ref):
        g_ = g_ref[...]
        nm = ADAM_B1 * m_ref[...] + (1.0 - ADAM_B1) * g_
        nv = ADAM_B2 * v_ref[...] + (1.0 - ADAM_B2) * (g_ * g_)
        d_ref[...] = -ADAM_LR * ((nm * c1) / (jnp.sqrt(nv * c2) + ADAM_EPS) + ADAM_WD * w_ref[...])
        nm_ref[...] = nm
        nv_ref[...] = nv

    spec = pl.BlockSpec((None, tr, C), lambda l, i: (l, i, 0))
    outs = pl.pallas_call(
        body, name=name, grid=(L, R // tr),
        in_specs=[spec] * 4, out_specs=[spec] * 3,
        out_shape=[_sds((L, R, C), F32)] * 3,
        compiler_params=_params(),
    )(w3, g3, m3, v3)
    return tuple(o.reshape(shape) for o in outs)


def _pad_rows(a, rows):
    return jnp.pad(a, ((0, rows - a.shape[0]), (0, 0)))


def _pad_cols(a, cols):
    return jnp.pad(a, ((0, 0), (0, cols - a.shape[1])))


def _rope_tables(positions):
    q = D_ROPE // 2
    inv_freq = ROPE_THETA ** (-jnp.arange(0, D_ROPE, 2, dtype=F32) / D_ROPE)
    ang = positions.astype(F32)[:, None] * inv_freq
    cos, sin = jnp.cos(ang), jnp.sin(ang)
    z = jnp.zeros_like(cos)
    zz = jnp.zeros((cos.shape[0], 128 - 2 * q), F32)
    c = jnp.concatenate([cos, cos, zz], axis=1)
    s1 = jnp.concatenate([z, sin, zz], axis=1)
    s2 = jnp.concatenate([-sin, z, zz], axis=1)
    return c, s1, s2


def kernel(x, p, positions, ffn_a_norm, ffn_a_w_in, ffn_a_w_out, ffn_b_norm, ffn_b_w_in, ffn_b_w_out, mix_norm, mla_w_in, mla_q_lat_norm, mla_kv_lat_norm, mla_w_uq, mla_w_ukv, mla_q_gain, mla_k_gain, mla_w_o, conv_w_pw1, conv_b_pw1, conv_w_dw, conv_b_dw, conv_ln_g, conv_ln_b, conv_w_pw2, ple_w_proj, ple_norm, ple_gate_norm, ple_w_gate, loss_target, m_ffn_a_norm, m_ffn_a_w_in, m_ffn_a_w_out, m_ffn_b_norm, m_ffn_b_w_in, m_ffn_b_w_out, m_mix_norm, m_mla_w_in, m_mla_q_lat_norm, m_mla_kv_lat_norm, m_mla_w_uq, m_mla_w_ukv, m_mla_q_gain, m_mla_k_gain, m_mla_w_o, m_conv_w_pw1, m_conv_b_pw1, m_conv_w_dw, m_conv_b_dw, m_conv_ln_g, m_conv_ln_b, m_conv_w_pw2, m_ple_w_proj, m_ple_norm, m_ple_gate_norm, m_ple_w_gate, v_ffn_a_norm, v_ffn_a_w_in, v_ffn_a_w_out, v_ffn_b_norm, v_ffn_b_w_in, v_ffn_b_w_out, v_mix_norm, v_mla_w_in, v_mla_q_lat_norm, v_mla_kv_lat_norm, v_mla_w_uq, v_mla_w_ukv, v_mla_q_gain, v_mla_k_gain, v_mla_w_o, v_conv_w_pw1, v_conv_b_pw1, v_conv_w_dw, v_conv_b_dw, v_conv_ln_g, v_conv_ln_b, v_conv_w_pw2, v_ple_w_proj, v_ple_norm, v_ple_gate_norm, v_ple_w_gate):
    weights = dict(ffn_a_norm=ffn_a_norm, ffn_a_w_in=ffn_a_w_in, ffn_a_w_out=ffn_a_w_out, ffn_b_norm=ffn_b_norm,
                   ffn_b_w_in=ffn_b_w_in, ffn_b_w_out=ffn_b_w_out, mix_norm=mix_norm, mla_w_in=mla_w_in,
                   mla_q_lat_norm=mla_q_lat_norm, mla_kv_lat_norm=mla_kv_lat_norm, mla_w_uq=mla_w_uq,
                   mla_w_ukv=mla_w_ukv, mla_q_gain=mla_q_gain, mla_k_gain=mla_k_gain, mla_w_o=mla_w_o,
                   conv_w_pw1=conv_w_pw1, conv_b_pw1=conv_b_pw1, conv_w_dw=conv_w_dw, conv_b_dw=conv_b_dw,
                   conv_ln_g=conv_ln_g, conv_ln_b=conv_ln_b, conv_w_pw2=conv_w_pw2, ple_w_proj=ple_w_proj,
                   ple_norm=ple_norm, ple_gate_norm=ple_gate_norm, ple_w_gate=ple_w_gate)
    moments_m = dict(ffn_a_norm=m_ffn_a_norm, ffn_a_w_in=m_ffn_a_w_in, ffn_a_w_out=m_ffn_a_w_out,
                     ffn_b_norm=m_ffn_b_norm, ffn_b_w_in=m_ffn_b_w_in, ffn_b_w_out=m_ffn_b_w_out,
                     mix_norm=m_mix_norm, mla_w_in=m_mla_w_in, mla_q_lat_norm=m_mla_q_lat_norm,
                     mla_kv_lat_norm=m_mla_kv_lat_norm, mla_w_uq=m_mla_w_uq, mla_w_ukv=m_mla_w_ukv,
                     mla_q_gain=m_mla_q_gain, mla_k_gain=m_mla_k_gain, mla_w_o=m_mla_w_o,
                     conv_w_pw1=m_conv_w_pw1, conv_b_pw1=m_conv_b_pw1, conv_w_dw=m_conv_w_dw,
                     conv_b_dw=m_conv_b_dw, conv_ln_g=m_conv_ln_g, conv_ln_b=m_conv_ln_b, conv_w_pw2=m_conv_w_pw2,
                     ple_w_proj=m_ple_w_proj, ple_norm=m_ple_norm, ple_gate_norm=m_ple_gate_norm,
                     ple_w_gate=m_ple_w_gate)
    moments_v = dict(ffn_a_norm=v_ffn_a_norm, ffn_a_w_in=v_ffn_a_w_in, ffn_a_w_out=v_ffn_a_w_out,
                     ffn_b_norm=v_ffn_b_norm, ffn_b_w_in=v_ffn_b_w_in, ffn_b_w_out=v_ffn_b_w_out,
                     mix_norm=v_mix_norm, mla_w_in=v_mla_w_in, mla_q_lat_norm=v_mla_q_lat_norm,
                     mla_kv_lat_norm=v_mla_kv_lat_norm, mla_w_uq=v_mla_w_uq, mla_w_ukv=v_mla_w_ukv,
                     mla_q_gain=v_mla_q_gain, mla_k_gain=v_mla_k_gain, mla_w_o=v_mla_w_o,
                     conv_w_pw1=v_conv_w_pw1, conv_b_pw1=v_conv_b_pw1, conv_w_dw=v_conv_w_dw,
                     conv_b_dw=v_conv_b_dw, conv_ln_g=v_conv_ln_g, conv_ln_b=v_conv_ln_b, conv_w_pw2=v_conv_w_pw2,
                     ple_w_proj=v_ple_w_proj, ple_norm=v_ple_norm, ple_gate_norm=v_ple_gate_norm,
                     ple_w_gate=v_ple_w_gate)
    order = list(weights.keys())

    T, D = x.shape[1], x.shape[2]
    me = _index(_place())
    h0 = x[0]
    target = loss_target[0]
    tabs = _rope_tables(positions[0])
    H = N_HEADS
    hps = H // N_DEV
    QL = mla_q_lat_norm.shape[1]
    Cs = conv_b_dw.shape[1]

    def layer_shards(i):
        return [_cast_layer(f"cast_{n}{i}", weights[n], i) for n in
                ("ffn_a_w_in", "ffn_a_w_out", "ffn_b_w_in", "ffn_b_w_out", "ple_w_gate", "ple_w_proj")]

    m_in_pad = _pad_cols(mla_w_in[0], mla_w_in.shape[2] - D_ROPE + 128)[None]
    uq_pad = jnp.pad(mla_w_uq[0].reshape(QL, hps, QK_DIM), ((0, 0), (0, 0), (0, HEAD_PAD - QK_DIM)))
    uq_pad = uq_pad.reshape(1, QL, hps * HEAD_PAD)
    mla_shards = [_cast_layer("cast_mla_in", m_in_pad, 0), _cast_layer("cast_mla_uq", uq_pad, 0),
                  _cast_layer("cast_mla_ukv", mla_w_ukv, 0), _cast_layer("cast_mla_wo", mla_w_o, 0)]
    conv_small = jnp.concatenate([
        _pad_rows(_pad_cols(conv_b_pw1, 2 * Cs), 8),
        _pad_rows(_pad_cols(conv_w_dw[0], 2 * Cs), 32),
        _pad_rows(_pad_cols(jnp.concatenate([conv_b_dw, conv_ln_g, conv_ln_b], axis=0), 2 * Cs), 8)], axis=0)
    conv_shards = [_cast_layer("cast_conv_pw1", conv_w_pw1, 0), _cast_layer("cast_conv_pw2", conv_w_pw2, 0)]

    g0 = _all_gather("gather_layer0", layer_shards(0) + mla_shards)
    g1 = _all_gather("gather_layer1", layer_shards(1) + conv_shards + [conv_small])

    def layer_weights(g):
        a_in, a_out, b_in, b_out, gate, proj = g[:6]
        return dict(a_in=a_in, a_out=a_out.reshape(-1, D), b_in=b_in, b_out=b_out.reshape(-1, D),
                    gate=gate.reshape(-1, D), proj=proj)

    W = [layer_weights(g0), layer_weights(g1)]
    Wm = dict(m_in=g0[6].reshape(D, -1), uq=g0[7], ukv=g0[8], wo=g0[9].reshape(-1, D))
    Wc = dict(pw1=g1[6], pw2=g1[7].reshape(-1, D))
    small = g1[8]
    b_pw1_full = small[:, 0:1, :]
    w_dw_full = jnp.transpose(small[:, 8:40, :Cs], (1, 0, 2)).reshape(32, N_DEV * Cs)
    b_dw_full = small[:, 40, :Cs].reshape(1, N_DEV * Cs)
    ln_g_full = small[:, 41, :Cs].reshape(1, N_DEV * Cs)
    ln_b_full = small[:, 42, :Cs].reshape(1, N_DEV * Cs)
    gq_pad = _pad_cols(mla_q_gain, HEAD_PAD)
    gk_pad = _pad_cols(mla_k_gain, HEAD_PAD)

    saved = []
    h = h0
    for i in range(2):
        h, s_a = _ffn_fwd(f"ffn_a{i}", h, ffn_a_norm[i:i + 1], W[i]["a_in"], W[i]["a_out"])
        if i == 0:
            h, s_m = _mla_fwd(h, mix_norm[0:1], tabs, Wm, mla_q_lat_norm, mla_kv_lat_norm, gq_pad, gk_pad)
        else:
            h, s_m = _conv_fwd(h, mix_norm[1:2], Wc, b_pw1_full, w_dw_full, b_dw_full, ln_g_full, ln_b_full)
        h, s_b = _ffn_fwd(f"ffn_b{i}", h, ffn_b_norm[i:i + 1], W[i]["b_in"], W[i]["b_out"])
        h, s_p = _ple_fwd(f"ple{i}", h, p[i, 0], W[i]["proj"], ple_norm[i:i + 1], ple_gate_norm[i:i + 1],
                          W[i]["gate"])
        saved.append((s_a, s_m, s_b, s_p))

    dh, dhb, loss_row = _loss_head(h, target)

    G = {}
    small_g = {}
    for i in (1, 0):
        s_a, s_m, s_b, s_p = saved[i]
        dh, dhb, d_pn, d_gn, G[("ple_w_proj", i)], G[("ple_w_gate", i)] = _ple_bwd(
            f"ple{i}", s_p, p[i, 0], W[i]["proj"], ple_norm[i:i + 1], ple_gate_norm[i:i + 1], W[i]["gate"], dh, dhb)
        small_g[("ple_norm", i)], small_g[("ple_gate_norm", i)] = d_pn, d_gn
        dh, dhb, small_g[("ffn_b_norm", i)], G[("ffn_b_w_in", i)], G[("ffn_b_w_out", i)] = _ffn_bwd(
            f"ffn_b{i}", s_b, ffn_b_norm[i:i + 1], W[i]["b_in"], W[i]["b_out"], dh, dhb)
        if i == 0:
            (dh, dhb, small_g[("mix_norm", 0)], d_qln, d_kvln, d_gq, d_gk,
             G[("mla_w_in", 0)], G[("mla_w_uq", 0)], G[("mla_w_ukv", 0)], G[("mla_w_o", 0)]) = _mla_bwd(
                s_m, mix_norm[0:1], tabs, Wm, mla_q_lat_norm, mla_kv_lat_norm, gq_pad, gk_pad, dh, dhb)
        else:
            (dh, dhb, small_g[("mix_norm", 1)], G[("conv_w_pw1", 0)], d_b_pw1, d_w_dw, d_b_dw, d_ln_g, d_ln_b,
             G[("conv_w_pw2", 0)]) = _conv_bwd(s_m, mix_norm[1:2], Wc, w_dw_full, ln_g_full, ln_b_full, dh, dhb)
        dh, dhb, small_g[("ffn_a_norm", i)], G[("ffn_a_w_in", i)], G[("ffn_a_w_out", i)] = _ffn_bwd(
            f"ffn_a{i}", s_a, ffn_a_norm[i:i + 1], W[i]["a_in"], W[i]["a_out"], dh, dhb)
    grad_x = dh[None]

    def two(name):
        return _pad_rows(jnp.concatenate([small_g[(name, 0)], small_g[(name, 1)]], axis=0), 8)

    misc = jnp.concatenate([_pad_cols(d_qln, D), _pad_cols(d_kvln, D), _pad_cols(d_gq, D), _pad_cols(d_gk, D),
                            _pad_cols(loss_row, D)], axis=0)
    C = N_DEV * Cs
    pack = jnp.concatenate([
        two("ffn_a_norm"), two("ffn_b_norm"), two("mix_norm"), two("ple_norm"), two("ple_gate_norm"),
        _pad_rows(misc, 8),
        _pad_rows(_pad_cols(d_b_pw1.reshape(2, C), D), 8),
        _pad_cols(d_w_dw, D),
        _pad_rows(_pad_cols(jnp.concatenate([d_b_dw, d_ln_g, d_ln_b], axis=0), D), 8)], axis=0)
    red = _all_reduce_small(pack)
    loss = red[44, 0]
    small_grads = dict(
        ffn_a_norm=red[0:2], ffn_b_norm=red[8:10], mix_norm=red[16:18], ple_norm=red[24:26],
        ple_gate_norm=red[32:34],
        mla_q_lat_norm=red[40:41, :QL], mla_kv_lat_norm=red[41:42, :mla_kv_lat_norm.shape[1]],
        mla_q_gain=red[42:43, :QK_DIM], mla_k_gain=red[43:44, :QK_DIM],
        conv_b_pw1=lax.dynamic_slice_in_dim(red[48:50, :C].reshape(1, 2 * C), me * 2 * Cs, 2 * Cs, axis=1),
        conv_w_dw=lax.dynamic_slice_in_dim(red[56:56 + CONV_WIDTH, :C], me * Cs, Cs, axis=1)[None],
        conv_b_dw=lax.dynamic_slice_in_dim(red[88:89, :C], me * Cs, Cs, axis=1),
        conv_ln_g=lax.dynamic_slice_in_dim(red[89:90, :C], me * Cs, Cs, axis=1),
        conv_ln_b=lax.dynamic_slice_in_dim(red[90:91, :C], me * Cs, Cs, axis=1))

    def rows8(g):
        return g.reshape(N_DEV, g.shape[0] // N_DEV, g.shape[1])

    stacked = ["ffn_a_w_in", "ffn_a_w_out", "ffn_b_w_in", "ffn_b_w_out", "ple_w_proj", "ple_w_gate"]
    single = ["mla_w_in", "mla_w_uq", "mla_w_ukv", "mla_w_o", "conv_w_pw1", "conv_w_pw2"]
    row_sharded = {"ffn_a_w_out", "ffn_b_w_out", "ple_w_gate", "mla_w_in", "mla_w_o", "conv_w_pw2"}

    def sm(name, i):
        g = G[(name, i)]
        return rows8(g) if name in row_sharded else g

    slots = _reduce_scatter_move(
        "scatter_grads",
        [[sm(n, 0), sm(n, 1)] for n in stacked] + [[sm(n, 0)] for n in single])
    sums = {n: _slot_sum(f"sum_{n}", s) for n, s in zip(stacked + single, slots)}
    big_grads = {n: sums[n] for n in stacked}
    big_grads["mla_w_in"] = sums["mla_w_in"][:, :, :mla_w_in.shape[2]]
    big_grads["mla_w_uq"] = sums["mla_w_uq"].reshape(1, QL, hps, HEAD_PAD)[..., :QK_DIM].reshape(mla_w_uq.shape)
    for n in ("mla_w_ukv", "mla_w_o", "conv_w_pw1", "conv_w_pw2"):
        big_grads[n] = sums[n]

    grads = {**small_grads, **big_grads}
    deltas, new_m, new_v = {}, {}, {}
    for n in order:
        deltas[n], new_m[n], new_v[n] = _adamw(f"adamw_{n}", weights[n], grads[n], moments_m[n], moments_v[n])

    return (loss, grad_x, *[grads[n] for n in order], *[deltas[n] for n in order],
            *[new_m[n] for n in order], *[new_v[n] for n in order])
```

```python
import functools

import jax
import jax.numpy as jnp
from jax import lax
from jax.experimental import pallas as pl
from jax.experimental.pallas import tpu as pltpu

F32 = jnp.float32
BF16 = jnp.bfloat16
MESH = pl.DeviceIdType.MESH
ANY = pl.BlockSpec(memory_space=pl.ANY)

N_DEV = 8
N_HEADS = 16
D_NOPE = 128
D_ROPE = 64
D_V = 128
QK_DIM = D_NOPE + D_ROPE
HEAD_PAD = 256
ROPE_THETA = 10000.0
CONV_WIDTH = 31
CONV_PAD = 32
FFN_RES = 0.5
EPS = 1e-6
ADAM_LR = 0.001
ADAM_B1 = 0.9
ADAM_B2 = 0.999
ADAM_EPS = 1e-08
ADAM_WD = 0.01
ADAM_STEP = 10
VMEM_LIMIT = 56 * 1024 * 1024


def _sds(shape, dtype):
    return jax.ShapeDtypeStruct(tuple(int(s) for s in shape), dtype)


def _tile(n, pref, mult=128):
    if n <= pref:
        return n
    t = (pref // mult) * mult
    while t >= mult:
        if n % t == 0:
            return t
        t -= mult
    return n


def _params():
    return pltpu.CompilerParams(vmem_limit_bytes=VMEM_LIMIT)


def _matmul(name, grid, ops, terms, dims, acc_shapes, outs, epilogue, extras=()):
    nk = grid[2]
    n_ops, n_ex, n_out, n_acc = len(ops), len(extras), len(outs), len(acc_shapes)

    def body(*refs):
        op_refs = refs[:n_ops]
        ex_refs = refs[n_ops:n_ops + n_ex]
        out_refs = refs[n_ops + n_ex:n_ops + n_ex + n_out]
        acc_refs = refs[n_ops + n_ex + n_out:]
        vals = {}

        def opval(i):
            if i not in vals:
                v = op_refs[i][...]
                vals[i] = v if v.dtype == BF16 else v.astype(BF16)
            return vals[i]

        parts = [None] * n_acc
        for ai, li, ri in terms:
            d = lax.dot_general(opval(li), opval(ri), (dims, ((), ())), preferred_element_type=F32)
            parts[ai] = d if parts[ai] is None else parts[ai] + d
        if nk == 1:
            epilogue(parts, ex_refs, out_refs)
            return
        k = pl.program_id(2)

        @pl.when(k == 0)
        def _():
            for a_ref, p in zip(acc_refs, parts):
                a_ref[...] = p

        @pl.when(k > 0)
        def _():
            for a_ref, p in zip(acc_refs, parts):
                a_ref[...] += p

        @pl.when(k == nk - 1)
        def _():
            epilogue([a[...] for a in acc_refs], ex_refs, out_refs)

    scratch = [pltpu.VMEM(s, F32) for s in acc_shapes] if nk > 1 else []
    res = pl.pallas_call(
        body, name=name, grid=grid,
        in_specs=[s for _, s in ops] + [s for _, s in extras],
        out_specs=[s for _, s in outs],
        out_shape=[o for o, _ in outs],
        scratch_shapes=scratch,
        compiler_params=_params(),
    )(*[a for a, _ in ops], *[a for a, _ in extras])
    return res


NN = ((1,), (0,))
NT = ((1,), (1,))
TN = ((0,), (0,))


def _store(i=0):
    def ep(accs, ex, outs):
        outs[0][...] = accs[0].astype(outs[0].dtype)
    return ep


def _mm_nn(name, a, b, out_dtype, tm=1024, tn=1024, tk=512, epilogue=None, extras=(), extra_outs=()):
    M, K = a.shape
    N = b.shape[1]
    tm, tn, tk = _tile(M, tm, 16), _tile(N, tn), _tile(K, tk)
    outs = [(_sds((M, N), out_dtype), pl.BlockSpec((tm, tn), lambda i, j, k: (i, j)))] + list(extra_outs)
    return _matmul(name, (M // tm, N // tn, K // tk),
                   [(a, pl.BlockSpec((tm, tk), lambda i, j, k: (i, k))),
                    (b, pl.BlockSpec((tk, tn), lambda i, j, k: (k, j)))],
                   [(0, 0, 1)], NN, [(tm, tn)], outs, epilogue or _store(), extras)


def _mm_nt(name, a, b, out_dtype, tm=1024, tn=1024, tk=512, epilogue=None, extras=()):
    M, K = a.shape
    N = b.shape[0]
    tm, tn, tk = _tile(M, tm, 16), _tile(N, tn), _tile(K, tk)
    outs = [(_sds((M, N), out_dtype), pl.BlockSpec((tm, tn), lambda i, j, k: (i, j)))]
    return _matmul(name, (M // tm, N // tn, K // tk),
                   [(a, pl.BlockSpec((tm, tk), lambda i, j, k: (i, k))),
                    (b, pl.BlockSpec((tn, tk), lambda i, j, k: (j, k)))],
                   [(0, 0, 1)], NT, [(tm, tn)], outs, epilogue or _store(), extras)


def _mm_tn(name, a, b, out_dtype, tm=512, tn=1024, tk=1024, scale=None):
    T, M = a.shape
    N = b.shape[1]
    tm, tn, tk = _tile(M, tm), _tile(N, tn), _tile(T, tk)

    def ep(accs, ex, outs):
        v = accs[0] if scale is None else accs[0] * scale
        outs[0][...] = v.astype(outs[0].dtype)

    outs = [(_sds((M, N), out_dtype), pl.BlockSpec((tm, tn), lambda i, j, k: (i, j)))]
    return _matmul(name, (M // tm, N // tn, T // tk),
                   [(a, pl.BlockSpec((tk, tm), lambda i, j, k: (k, i))),
                    (b, pl.BlockSpec((tk, tn), lambda i, j, k: (k, j)))],
                   [(0, 0, 1)], TN, [(tm, tn)], outs, ep)[0]


def _mm_nn_sm(name, a, w, out_dtype, tm=1024, tk=512, epilogue=None, extras=()):
    M, K = a.shape
    S, _, Ns = w.shape
    tm, tk = _tile(M, tm, 16), _tile(K, tk)
    outs = [(_sds((M, S * Ns), out_dtype), pl.BlockSpec((tm, Ns), lambda j, i, k: (i, j)))]
    return _matmul(name, (S, M // tm, K // tk),
                   [(a, pl.BlockSpec((tm, tk), lambda j, i, k: (i, k))),
                    (w, pl.BlockSpec((None, tk, Ns), lambda j, i, k: (j, k, 0)))],
                   [(0, 0, 1)], NN, [(tm, Ns)], outs, epilogue or _store(), extras)[0]


def _mm_nt_sm(name, a, w, out_dtype, tm=1024, tn=1024):
    M = a.shape[0]
    S, K, Ns = w.shape
    tm, tn = _tile(M, tm, 16), _tile(K, tn)
    outs = [(_sds((M, K), out_dtype), pl.BlockSpec((tm, tn), lambda i, n, j: (i, n)))]
    return _matmul(name, (M // tm, K // tn, S),
                   [(a, pl.BlockSpec((tm, Ns), lambda i, n, j: (i, j))),
                    (w, pl.BlockSpec((None, tn, Ns), lambda i, n, j: (j, n, 0)))],
                   [(0, 0, 1)], NT, [(tm, tn)], outs, _store())[0]


def _mm_tn_sm(name, a, b, S, out_dtype, tm=1024, tk=1024):
    T, M = a.shape
    Ns = b.shape[1] // S
    tm, tk = _tile(M, tm), _tile(T, tk)
    outs = [(_sds((S, M, Ns), out_dtype), pl.BlockSpec((None, tm, Ns), lambda j, i, k: (j, i, 0)))]
    return _matmul(name, (S, M // tm, T // tk),
                   [(a, pl.BlockSpec((tk, tm), lambda j, i, k: (k, i))),
                    (b, pl.BlockSpec((tk, Ns), lambda j, i, k: (k, j)))],
                   [(0, 0, 1)], TN, [(tm, Ns)], outs, _store())[0]


def _row_spec(shape, axis, tm):
    block = tuple(tm if d == axis else s for d, s in enumerate(shape))
    nd = len(shape)

    def imap(i):
        return tuple(i if d == axis else 0 for d in range(nd))
    return pl.BlockSpec(block, imap)


def _full_spec(shape):
    nd = len(shape)
    return pl.BlockSpec(tuple(shape), lambda i: (0,) * nd)


def _rowwise(name, fn, T, tm, rows, consts, outs, accs=()):
    tm = _tile(T, tm, 16)
    n_in = len(rows) + len(consts)
    n_out = len(outs)

    def body(*refs):
        in_refs = refs[:n_in]
        out_refs = refs[n_in:n_in + n_out]
        acc_refs = refs[n_in + n_out:]
        i = pl.program_id(0)

        def acc_add(ai, val):
            @pl.when(i == 0)
            def _():
                acc_refs[ai][...] = val

            @pl.when(i > 0)
            def _():
                acc_refs[ai][...] += val

        fn(in_refs, out_refs, acc_add)

    res = pl.pallas_call(
        body, name=name, grid=(T // tm,),
        in_specs=[_row_spec(a.shape, ax, tm) for a, ax in rows] + [_full_spec(c.shape) for c in consts],
        out_specs=[_row_spec(s, ax, tm) for s, _, ax in outs] + [_full_spec(s) for s in accs],
        out_shape=[_sds(s, d) for s, d, _ in outs] + [_sds(s, F32) for s in accs],
        compiler_params=_params(),
    )(*[a for a, _ in rows], *consts)
    return res


def _rms(x, g, n=None):
    n = x.shape[-1] if n is None else n
    return x * lax.rsqrt(jnp.sum(x * x, axis=-1, keepdims=True) * (1.0 / n) + EPS) * g


def _norm_fwd(name, h, gain):
    T, D = h.shape

    def fn(ins, outs, acc):
        outs[0][...] = _rms(ins[0][...], ins[1][...]).astype(BF16)

    return _rowwise(name, fn, T, 256, [(h, 0)], [gain], [((T, D), BF16, 0)])[0]


def _norm_bwd(name, h, gain, dhn, dh_res):
    T, D = h.shape

    def fn(ins, outs, acc):
        _, vjp = jax.vjp(_rms, ins[0][...], ins[3][...])
        dh, dg = vjp(ins[1][...])
        dh = dh + ins[2][...]
        outs[0][...] = dh
        outs[1][...] = dh.astype(BF16)
        acc(0, dg)

    return _rowwise(name, fn, T, 256, [(h, 0), (dhn, 0), (dh_res, 0)], [gain],
                    [((T, D), F32, 0), ((T, D), BF16, 0)], [(1, D)])


def _ffn_fwd(tag, h, gain, w_in, w_out):
    T, D = h.shape
    S, _, Ns = w_in.shape
    half = S // 2
    F = half * Ns
    hn = _norm_fwd(tag + "_norm", h, gain)
    tm, tk = _tile(T, 1024, 16), _tile(D, 512)

    def ep(accs, ex, outs):
        g, u = accs
        act = g * jax.nn.sigmoid(g) * u
        outs[0][0] = g.astype(BF16)
        outs[0][1] = u.astype(BF16)
        outs[1][...] = act.astype(BF16)

    gu, act = _matmul(
        tag + "_in", (half, T // tm, D // tk),
        [(hn, pl.BlockSpec((tm, tk), lambda j, i, k: (i, k))),
         (w_in, pl.BlockSpec((None, tk, Ns), lambda j, i, k: (j, k, 0))),
         (w_in, pl.BlockSpec((None, tk, Ns), lambda j, i, k: (j + half, k, 0)))],
        [(0, 0, 1), (1, 0, 2)], NN, [(tm, Ns), (tm, Ns)],
        [(_sds((2, T, F), BF16), pl.BlockSpec((2, tm, Ns), lambda j, i, k: (0, i, j))),
         (_sds((T, F), BF16), pl.BlockSpec((tm, Ns), lambda j, i, k: (i, j)))],
        ep)

    def ep_out(accs, ex, outs):
        outs[0][...] = ex[0][...] + FFN_RES * accs[0]

    tn = _tile(D, 1024)
    h_new = _mm_nn(tag + "_out", act, w_out, F32, epilogue=ep_out,
                   extras=[(h, pl.BlockSpec((tm, tn), lambda i, j, k: (i, j)))])[0]
    return h_new, (h, hn, gu, act)


def _ffn_bwd(tag, saved, gain, w_in, w_out, dh, dhb):
    h, hn, gu, act = saved
    T, D = h.shape
    S, _, Ns = w_in.shape
    half = S // 2
    F = half * Ns
    tm, tk = _tile(T, 1024, 16), _tile(D, 512)

    def ep(accs, ex, outs):
        dact = FFN_RES * accs[0]
        g = ex[0][0].astype(F32)
        u = ex[0][1].astype(F32)
        sg = jax.nn.sigmoid(g)
        outs[0][0] = (dact * u * (sg * (1.0 + g * (1.0 - sg)))).astype(BF16)
        outs[0][1] = (dact * (g * sg)).astype(BF16)

    gu_spec = pl.BlockSpec((2, tm, Ns), lambda i, j, k: (0, i, j))
    dgu = _matmul(
        tag + "_dact", (T // tm, half, D // tk),
        [(dhb, pl.BlockSpec((tm, tk), lambda i, j, k: (i, k))),
         (w_out, pl.BlockSpec((Ns, tk), lambda i, j, k: (j, k)))],
        [(0, 0, 1)], NT, [(tm, Ns)],
        [(_sds((2, T, F), BF16), gu_spec)], ep, extras=[(gu, gu_spec)])[0]

    dw_out = _mm_tn(tag + "_dwout", act, dhb, BF16, tm=512, tn=2048, tk=1024, scale=FFN_RES)

    tn = _tile(D, 1024)
    dhn = _matmul(
        tag + "_dhn", (T // tm, D // tn, half),
        [(dgu, pl.BlockSpec((None, tm, Ns), lambda i, n, j: (0, i, j))),
         (dgu, pl.BlockSpec((None, tm, Ns), lambda i, n, j: (1, i, j))),
         (w_in, pl.BlockSpec((None, tn, Ns), lambda i, n, j: (j, n, 0))),
         (w_in, pl.BlockSpec((None, tn, Ns), lambda i, n, j: (j + half, n, 0)))],
        [(0, 0, 2), (0, 1, 3)], NT, [(tm, tn)],
        [(_sds((T, D), F32), pl.BlockSpec((tm, tn), lambda i, n, j: (i, n)))], _store())[0]

    tkd, tt = _tile(D, 1024), _tile(T, 1024)
    dw_in = _matmul(
        tag + "_dwin", (S, D // tkd, T // tt),
        [(hn, pl.BlockSpec((tt, tkd), lambda j, i, k: (k, i))),
         (dgu, pl.BlockSpec((None, tt, Ns), lambda j, i, k: (j // half, k, j % half)))],
        [(0, 0, 1)], TN, [(tkd, Ns)],
        [(_sds((S, D, Ns), BF16), pl.BlockSpec((None, tkd, Ns), lambda j, i, k: (j, i, 0)))], _store())[0]

    dh_in, dh_in_b, dgain = _norm_bwd(tag + "_dnorm", h, gain, dhn, dh)
    return dh_in, dh_in_b, dgain, dw_in, dw_out


def _ple_fwd(tag, h, p, w_proj, ple_norm, gate_norm, w_gate):
    T, D = h.shape
    e_raw = _mm_nn_sm(tag + "_proj", p, w_proj, F32, tm=1024, tk=512)
    hn = _norm_fwd(tag + "_norm", h, gate_norm)
    gate_raw = _mm_nn(tag + "_gate", hn, w_gate, F32)[0]

    def fn(ins, outs, acc):
        e = _rms(ins[1][...], ins[3][...])
        outs[0][...] = ins[0][...] + e * jax.nn.sigmoid(ins[2][...])

    h_new = _rowwise(tag + "_mix", fn, T, 256, [(h, 0), (e_raw, 0), (gate_raw, 0)], [ple_norm],
                     [((T, D), F32, 0)])[0]
    return h_new, (h, hn, e_raw, gate_raw)


def _ple_bwd(tag, saved, p, w_proj, ple_norm, gate_norm, w_gate, dh, dhb):
    h, hn, e_raw, gate_raw = saved
    T, D = h.shape
    S = w_proj.shape[0]

    def fn(ins, outs, acc):
        def f(e_raw_, gate_raw_, g_):
            return _rms(e_raw_, g_) * jax.nn.sigmoid(gate_raw_)
        _, vjp = jax.vjp(f, ins[0][...], ins[1][...], ins[3][...])
        de, dgate, dg = vjp(ins[2][...])
        outs[0][...] = de.astype(BF16)
        outs[1][...] = dgate.astype(BF16)
        acc(0, dg)

    de, dgate, d_ple_norm = _rowwise(tag + "_dmix", fn, T, 256, [(e_raw, 0), (gate_raw, 0), (dh, 0)], [ple_norm],
                                     [((T, D), BF16, 0), ((T, D), BF16, 0)], [(1, D)])
    dw_proj = _mm_tn_sm(tag + "_dwproj", p, de, S, BF16)
    dw_gate = _mm_tn(tag + "_dwgate", hn, dgate, BF16)
    dhn = _mm_nt(tag + "_dhn", dgate, w_gate, F32)[0]
    dh_in, dh_in_b, d_gate_norm = _norm_bwd(tag + "_dnorm", h, gate_norm, dhn, dh)
    return dh_in, dh_in_b, d_ple_norm, d_gate_norm, dw_proj, dw_gate


def _rope(t, c, s1, s2):
    q = D_ROPE // 2
    return t * c + pltpu.roll(t, q, 1) * s1 + pltpu.roll(t, 128 - q, 1) * s2


def _rope_t(d, c, s1, s2):
    q = D_ROPE // 2
    return d * c + pltpu.roll(d * s1, 128 - q, 1) + pltpu.roll(d * s2, q, 1)


def _head_norm(lo, hi, g_lo, g_hi):
    ms = (jnp.sum(lo * lo, axis=-1, keepdims=True) + jnp.sum(hi * hi, axis=-1, keepdims=True)) * (1.0 / QK_DIM)
    inv = lax.rsqrt(ms + EPS)
    return lo * inv * g_lo, hi * inv * g_hi


def _qk_prep(qraw, kvraw, lat, tabs, gq, gk, H):
    T = qraw.shape[0]
    koff = lat.shape[1] - 128

    def fn(ins, outs, acc):
        q_ref, kv_ref, lat_ref, c_ref, s1_ref, s2_ref, gq_ref, gk_ref = ins
        c, s1, s2 = c_ref[...], s1_ref[...], s2_ref[...]
        kr = lat_ref[:, koff:koff + 128]
        for hd in range(H):
            o = hd * HEAD_PAD
            lo, hi = _head_norm(q_ref[:, o:o + 128], q_ref[:, o + 128:o + 256], gq_ref[:, 0:128], gq_ref[:, 128:256])
            outs[0][hd, :, 0:128] = lo.astype(BF16)
            outs[0][hd, :, 128:256] = _rope(hi, c, s1, s2).astype(BF16)
            lo, hi = _head_norm(kv_ref[:, o:o + 128], kr, gk_ref[:, 0:128], gk_ref[:, 128:256])
            outs[1][hd, :, 0:128] = lo.astype(BF16)
            outs[1][hd, :, 128:256] = _rope(hi, c, s1, s2).astype(BF16)
            outs[2][hd] = kv_ref[:, o + 128:o + 256].astype(BF16)

    return _rowwise("mla_qkprep", fn, T, 256, [(qraw, 0), (kvraw, 0), (lat, 0)] + [(t, 0) for t in tabs], [gq, gk],
                    [((H, T, HEAD_PAD), BF16, 1), ((H, T, HEAD_PAD), BF16, 1), ((H, T, D_V), BF16, 1)])


def _qk_prep_bwd(qraw, kvraw, lat, tabs, gq, gk, dQ, dK, dV, H):
    T = qraw.shape[0]
    koff = lat.shape[1] - 128

    def fn(ins, outs, acc):
        q_ref, kv_ref, lat_ref, c_ref, s1_ref, s2_ref, dq_ref, dk_ref, dv_ref, gq_ref, gk_ref = ins
        c, s1, s2 = c_ref[...], s1_ref[...], s2_ref[...]
        kr = lat_ref[:, koff:koff + 128]
        dkr = jnp.zeros_like(kr)
        dg = [None] * 4
        for hd in range(H):
            o = hd * HEAD_PAD
            _, vjp = jax.vjp(_head_norm, q_ref[:, o:o + 128], q_ref[:, o + 128:o + 256],
                             gq_ref[:, 0:128], gq_ref[:, 128:256])
            dlo, dhi, dg0, dg1 = vjp((dq_ref[hd, :, 0:128], _rope_t(dq_ref[hd, :, 128:256], c, s1, s2)))
            outs[0][:, o:o + 128] = dlo.astype(BF16)
            outs[0][:, o + 128:o + 256] = dhi.astype(BF16)
            _, vjp = jax.vjp(_head_norm, kv_ref[:, o:o + 128], kr, gk_ref[:, 0:128], gk_ref[:, 128:256])
            dlo, dhi, dg2, dg3 = vjp((dk_ref[hd, :, 0:128], _rope_t(dk_ref[hd, :, 128:256], c, s1, s2)))
            outs[1][:, o:o + 128] = dlo.astype(BF16)
            outs[1][:, o + 128:o + 256] = dv_ref[hd].astype(BF16)
            dkr = dkr + dhi
            for n, v in enumerate((dg0, dg1, dg2, dg3)):
                dg[n] = v if dg[n] is None else dg[n] + v
        outs[2][...] = dkr
        for n in range(4):
            acc(n, dg[n])

    W = H * HEAD_PAD
    return _rowwise("mla_dqkprep", fn, T, 128,
                    [(qraw, 0), (kvraw, 0), (lat, 0)] + [(t, 0) for t in tabs] + [(dQ, 1), (dK, 1), (dV, 1)], [gq, gk],
                    [((T, W), BF16, 0), ((T, W), BF16, 0), ((T, 128), F32, 0)], [(1, 128)] * 4)


def _attn_probs(q_ref, k_ref, i, tq, T):
    s = lax.dot_general(q_ref[...], k_ref[...], (NT, ((), ())), preferred_element_type=F32) * (QK_DIM ** -0.5)
    row = i * tq + lax.broadcasted_iota(jnp.int32, (tq, T), 0)
    col = lax.broadcasted_iota(jnp.int32, (tq, T), 1)
    s = jnp.where(col <= row, s, -jnp.inf)
    p = jnp.exp(s - jnp.max(s, axis=-1, keepdims=True))
    return p / jnp.sum(p, axis=-1, keepdims=True)


def _attn_fwd(Q, K, V):
    H, T, _ = Q.shape
    tq = _tile(T, 256)

    def body(q_ref, k_ref, v_ref, o_ref):
        p = _attn_probs(q_ref, k_ref, pl.program_id(1), tq, T)
        o_ref[...] = jnp.dot(p.astype(BF16), v_ref[...], preferred_element_type=F32).astype(BF16)

    return pl.pallas_call(
        body, name="mla_attn", grid=(H, T // tq),
        in_specs=[pl.BlockSpec((None, tq, HEAD_PAD), lambda h, i: (h, i, 0)),
                  pl.BlockSpec((None, T, HEAD_PAD), lambda h, i: (h, 0, 0)),
                  pl.BlockSpec((None, T, D_V), lambda h, i: (h, 0, 0))],
        out_specs=pl.BlockSpec((tq, D_V), lambda h, i: (i, h)),
        out_shape=_sds((T, H * D_V), BF16),
        compiler_params=_params(),
    )(Q, K, V)


def _attn_bwd(Q, K, V, dO):
    H, T, _ = Q.shape
    tq = _tile(T, 256)

    def body(q_ref, k_ref, v_ref, do_ref, dq_ref, dk_ref, dv_ref):
        i = pl.program_id(1)
        p = _attn_probs(q_ref, k_ref, i, tq, T)
        do = do_ref[...]
        dv = lax.dot_general(p.astype(BF16), do, (TN, ((), ())), preferred_element_type=F32)
        dp = lax.dot_general(do, v_ref[...], (NT, ((), ())), preferred_element_type=F32)
        ds = p * (dp - jnp.sum(p * dp, axis=-1, keepdims=True)) * (QK_DIM ** -0.5)
        dsb = ds.astype(BF16)
        dq_ref[...] = jnp.dot(dsb, k_ref[...], preferred_element_type=F32)
        dk = lax.dot_general(dsb, q_ref[...], (TN, ((), ())), preferred_element_type=F32)

        @pl.when(i == 0)
        def _():
            dk_ref[...] = dk
            dv_ref[...] = dv

        @pl.when(i > 0)
        def _():
            dk_ref[...] += dk
            dv_ref[...] += dv

    return pl.pallas_call(
        body, name="mla_dattn", grid=(H, T // tq),
        in_specs=[pl.BlockSpec((None, tq, HEAD_PAD), lambda h, i: (h, i, 0)),
                  pl.BlockSpec((None, T, HEAD_PAD), lambda h, i: (h, 0, 0)),
                  pl.BlockSpec((None, T, D_V), lambda h, i: (h, 0, 0)),
                  pl.BlockSpec((tq, D_V), lambda h, i: (i, h))],
        out_specs=[pl.BlockSpec((None, tq, HEAD_PAD), lambda h, i: (h, i, 0)),
                   pl.BlockSpec((None, T, HEAD_PAD), lambda h, i: (h, 0, 0)),
                   pl.BlockSpec((None, T, D_V), lambda h, i: (h, 0, 0))],
        out_shape=[_sds((H, T, HEAD_PAD), F32), _sds((H, T, HEAD_PAD), F32), _sds((H, T, D_V), F32)],
        compiler_params=_params(),
    )(Q, K, V, dO)


def _mla_fwd(h, gain, tabs, w, q_lat_norm, kv_lat_norm, gq, gk):
    T, D = h.shape
    H = w["uq"].shape[0] * w["uq"].shape[2] // HEAD_PAD
    QL, KL = q_lat_norm.shape[1], kv_lat_norm.shape[1]
    hn = _norm_fwd("mla_norm", h, gain)
    lat = _mm_nn("mla_lat", hn, w["m_in"], F32, tn=w["m_in"].shape[1])[0]

    def fn(ins, outs, acc):
        outs[0][...] = _rms(ins[0][:, 0:QL], ins[1][...]).astype(BF16)
        outs[1][...] = _rms(ins[0][:, QL:QL + KL], ins[2][...]).astype(BF16)

    cq, ckv = _rowwise("mla_latnorm", fn, T, 256, [(lat, 0)], [q_lat_norm, kv_lat_norm],
                       [((T, QL), BF16, 0), ((T, KL), BF16, 0)])
    qraw = _mm_nn_sm("mla_uq", cq, w["uq"], F32)
    kvraw = _mm_nn_sm("mla_ukv", ckv, w["ukv"], F32)
    Q, K, V = _qk_prep(qraw, kvraw, lat, tabs, gq, gk, H)
    O = _attn_fwd(Q, K, V)

    def ep_out(accs, ex, outs):
        outs[0][...] = ex[0][...] + accs[0]

    tm, tn = _tile(T, 1024, 16), _tile(D, 1024)
    h_new = _mm_nn("mla_out", O, w["wo"], F32, epilogue=ep_out,
                   extras=[(h, pl.BlockSpec((tm, tn), lambda i, j, k: (i, j)))])[0]
    return h_new, (h, hn, lat, cq, ckv, qraw, kvraw, Q, K, V, O)


def _mla_bwd(saved, gain, tabs, w, q_lat_norm, kv_lat_norm, gq, gk, dh, dhb):
    h, hn, lat, cq, ckv, qraw, kvraw, Q, K, V, O = saved
    T, D = h.shape
    H = Q.shape[0]
    S = w["uq"].shape[0]
    QL, KL = q_lat_norm.shape[1], kv_lat_norm.shape[1]
    dO = _mm_nt("mla_dO", dhb, w["wo"], BF16)[0]
    dwo = _mm_tn("mla_dwo", O, dhb, BF16)
    dQ, dK, dV = _attn_bwd(Q, K, V, dO)
    dqraw, dkvraw, dkr, dgq0, dgq1, dgk0, dgk1 = _qk_prep_bwd(qraw, kvraw, lat, tabs, gq, gk, dQ, dK, dV, H)
    dcq = _mm_nt_sm("mla_dcq", dqraw, w["uq"], F32)
    dckv = _mm_nt_sm("mla_dckv", dkvraw, w["ukv"], F32)
    dwuq = _mm_tn_sm("mla_dwuq", cq, dqraw, S, BF16)
    dwukv = _mm_tn_sm("mla_dwukv", ckv, dkvraw, S, BF16)

    def fn(ins, outs, acc):
        _, vjp = jax.vjp(_rms, ins[0][:, 0:QL], ins[4][...])
        d, dgq_ = vjp(ins[1][...])
        outs[0][:, 0:QL] = d.astype(BF16)
        _, vjp = jax.vjp(_rms, ins[0][:, QL:QL + KL], ins[5][...])
        d, dgkv_ = vjp(ins[2][...])
        outs[0][:, QL:QL + KL] = d.astype(BF16)
        outs[0][:, QL + KL:QL + KL + 128] = ins[3][...].astype(BF16)
        acc(0, dgq_)
        acc(1, dgkv_)

    dlat, d_qln, d_kvln = _rowwise("mla_dlatnorm", fn, T, 256, [(lat, 0), (dcq, 0), (dckv, 0), (dkr, 0)],
                                   [q_lat_norm, kv_lat_norm], [(lat.shape, BF16, 0)], [(1, QL), (1, KL)])
    dhn = _mm_nt("mla_dhn", dlat, w["m_in"], F32, tk=lat.shape[1])[0]
    dw_min = _mm_tn("mla_dwin", hn, dlat, BF16, tn=lat.shape[1])
    dh_in, dh_in_b, dgain = _norm_bwd("mla_dnorm", h, gain, dhn, dh)
    d_gq = jnp.concatenate([dgq0, dgq1], axis=1)[:, :QK_DIM]
    d_gk = jnp.concatenate([dgk0, dgk1], axis=1)[:, :QK_DIM]
    return dh_in, dh_in_b, dgain, d_qln, d_kvln, d_gq, d_gk, dw_min, dwuq, dwukv, dwo


def _conv_rows(T):
    return _tile(T, 128, 8)


def _dwconv_fwd(u, w_dw, b_dw):
    T, C = u.shape
    tc, R = _tile(C, 256), _conv_rows(T)
    off = CONV_PAD - (CONV_WIDTH - 1)

    def body(u_ref, w_ref, b_ref, y_ref, pad_ref):
        pad_ref[0:CONV_PAD, :] = jnp.zeros((CONV_PAD, tc), F32)
        pad_ref[CONV_PAD:CONV_PAD + T, :] = u_ref[...]
        for r in range(T // R):
            acc = jnp.broadcast_to(b_ref[...], (R, tc))
            for j in range(CONV_WIDTH):
                acc = acc + w_ref[j:j + 1, :] * pad_ref[r * R + off + j:r * R + off + j + R, :]
            y_ref[r * R:(r + 1) * R, :] = acc

    return pl.pallas_call(
        body, name="conv_dw", grid=(C // tc,),
        in_specs=[pl.BlockSpec((T, tc), lambda c: (0, c)), pl.BlockSpec((32, tc), lambda c: (0, c)),
                  pl.BlockSpec((1, tc), lambda c: (0, c))],
        out_specs=pl.BlockSpec((T, tc), lambda c: (0, c)),
        out_shape=_sds((T, C), F32),
        scratch_shapes=[pltpu.VMEM((T + CONV_PAD, tc), F32)],
        compiler_params=_params(),
    )(u, w_dw, b_dw)


def _dwconv_bwd(u, w_dw, dy):
    T, C = u.shape
    tc, R = _tile(C, 256), _conv_rows(T)
    off = CONV_PAD - (CONV_WIDTH - 1)

    def body(u_ref, w_ref, dy_ref, du_ref, dw_ref, db_ref, upad_ref, dpad_ref):
        upad_ref[0:CONV_PAD, :] = jnp.zeros((CONV_PAD, tc), F32)
        upad_ref[CONV_PAD:CONV_PAD + T, :] = u_ref[...]
        dpad_ref[0:T, :] = dy_ref[...]
        dpad_ref[T:T + CONV_PAD, :] = jnp.zeros((CONV_PAD, tc), F32)
        for r in range(T // R):
            acc = jnp.zeros((R, tc), F32)
            for j in range(CONV_WIDTH):
                s = r * R + (CONV_WIDTH - 1) - j
                acc = acc + w_ref[j:j + 1, :] * dpad_ref[s:s + R, :]
            du_ref[r * R:(r + 1) * R, :] = acc
        for j in range(CONV_WIDTH):
            acc = jnp.zeros((R, tc), F32)
            for r in range(T // R):
                acc = acc + dy_ref[r * R:(r + 1) * R, :] * upad_ref[r * R + off + j:r * R + off + j + R, :]
            dw_ref[j:j + 1, :] = jnp.sum(acc, axis=0, keepdims=True)
        dw_ref[CONV_WIDTH:32, :] = jnp.zeros((32 - CONV_WIDTH, tc), F32)
        db_ref[...] = jnp.sum(dy_ref[...], axis=0, keepdims=True)

    return pl.pallas_call(
        body, name="conv_ddw", grid=(C // tc,),
        in_specs=[pl.BlockSpec((T, tc), lambda c: (0, c)), pl.BlockSpec((32, tc), lambda c: (0, c)),
                  pl.BlockSpec((T, tc), lambda c: (0, c))],
        out_specs=[pl.BlockSpec((T, tc), lambda c: (0, c)), pl.BlockSpec((32, tc), lambda c: (0, c)),
                   pl.BlockSpec((1, tc), lambda c: (0, c))],
        out_shape=[_sds((T, C), F32), _sds((32, C), F32), _sds((1, C), F32)],
        scratch_shapes=[pltpu.VMEM((T + CONV_PAD, tc), F32), pltpu.VMEM((T + CONV_PAD, tc), F32)],
        compiler_params=_params(),
    )(u, w_dw, dy)


def _ln_silu(y, g, b):
    mu = jnp.mean(y, axis=-1, keepdims=True)
    yc = y - mu
    z = yc * lax.rsqrt(jnp.mean(yc * yc, axis=-1, keepdims=True) + EPS) * g + b
    return z * jax.nn.sigmoid(z)


def _conv_fwd(h, gain, w, b_pw1, w_dw, b_dw, ln_g, ln_b):
    T, D = h.shape
    S, _, Ns = w["pw1"].shape
    half = S // 2
    C = half * Ns
    hn = _norm_fwd("conv_norm", h, gain)
    tm, tk = _tile(T, 1024, 16), _tile(D, 512)

    def ep(accs, ex, outs):
        a = accs[0] + ex[0][...]
        g = accs[1] + ex[1][...]
        outs[0][0] = a.astype(BF16)
        outs[0][1] = g.astype(BF16)
        outs[1][...] = a * jax.nn.sigmoid(g)

    ag, u = _matmul(
        "conv_pw1", (half, T // tm, D // tk),
        [(hn, pl.BlockSpec((tm, tk), lambda j, i, k: (i, k))),
         (w["pw1"], pl.BlockSpec((None, tk, Ns), lambda j, i, k: (j, k, 0))),
         (w["pw1"], pl.BlockSpec((None, tk, Ns), lambda j, i, k: (j + half, k, 0)))],
        [(0, 0, 1), (1, 0, 2)], NN, [(tm, Ns), (tm, Ns)],
        [(_sds((2, T, C), BF16), pl.BlockSpec((2, tm, Ns), lambda j, i, k: (0, i, j))),
         (_sds((T, C), F32), pl.BlockSpec((tm, Ns), lambda j, i, k: (i, j)))],
        ep,
        extras=[(b_pw1, pl.BlockSpec((None, 1, Ns), lambda j, i, k: (j, 0, 0))),
                (b_pw1, pl.BlockSpec((None, 1, Ns), lambda j, i, k: (j + half, 0, 0)))])
    y = _dwconv_fwd(u, w_dw, b_dw)

    def fn(ins, outs, acc):
        outs[0][...] = _ln_silu(ins[0][...], ins[1][...], ins[2][...]).astype(BF16)

    s = _rowwise("conv_ln", fn, T, 256, [(y, 0)], [ln_g, ln_b], [((T, C), BF16, 0)])[0]

    def ep_out(accs, ex, outs):
        outs[0][...] = ex[0][...] + accs[0]

    tn = _tile(D, 1024)
    h_new = _mm_nn("conv_pw2", s, w["pw2"], F32, epilogue=ep_out,
                   extras=[(h, pl.BlockSpec((tm, tn), lambda i, j, k: (i, j)))])[0]
    return h_new, (h, hn, ag, u, y, s)


def _conv_bwd(saved, gain, w, w_dw, ln_g, ln_b, dh, dhb):
    h, hn, ag, u, y, s = saved
    T, D = h.shape
    S, _, Ns = w["pw1"].shape
    half = S // 2
    C = half * Ns
    ds = _mm_nt("conv_ds", dhb, w["pw2"], F32)[0]
    dw_pw2 = _mm_tn("conv_dwpw2", s, dhb, BF16)

    def fn(ins, outs, acc):
        _, vjp = jax.vjp(_ln_silu, ins[0][...], ins[2][...], ins[3][...])
        dy, dg, db = vjp(ins[1][...])
        outs[0][...] = dy
        acc(0, dg)
        acc(1, db)

    dy, d_ln_g, d_ln_b = _rowwise("conv_dln", fn, T, 256, [(y, 0), (ds, 0)], [ln_g, ln_b],
                                  [((T, C), F32, 0)], [(1, C), (1, C)])
    du, d_w_dw, d_b_dw = _dwconv_bwd(u, w_dw, dy)

    def fn2(ins, outs, acc):
        a = ins[0][0].astype(F32)
        g = ins[0][1].astype(F32)
        du_ = ins[1][...]
        sg = jax.nn.sigmoid(g)
        da = du_ * sg
        dg = du_ * a * sg * (1.0 - sg)
        outs[0][0] = da.astype(BF16)
        outs[0][1] = dg.astype(BF16)
        acc(0, jnp.sum(da, axis=0, keepdims=True))
        acc(1, jnp.sum(dg, axis=0, keepdims=True))

    dag, d_b_a, d_b_g = _rowwise("conv_dglu", fn2, T, 256, [(ag, 1), (du, 0)], [],
                                 [((2, T, C), BF16, 1)], [(1, C), (1, C)])
    tm, tn = _tile(T, 1024, 16), _tile(D, 1024)
    dhn = _matmul(
        "conv_dhn", (T // tm, D // tn, half),
        [(dag, pl.BlockSpec((None, tm, Ns), lambda i, n, j: (0, i, j))),
         (dag, pl.BlockSpec((None, tm, Ns), lambda i, n, j: (1, i, j))),
         (w["pw1"], pl.BlockSpec((None, tn, Ns), lambda i, n, j: (j, n, 0))),
         (w["pw1"], pl.BlockSpec((None, tn, Ns), lambda i, n, j: (j + half, n, 0)))],
        [(0, 0, 2), (0, 1, 3)], NT, [(tm, tn)],
        [(_sds((T, D), F32), pl.BlockSpec((tm, tn), lambda i, n, j: (i, n)))], _store())[0]
    tkd, tt = _tile(D, 1024), _tile(T, 1024)
    dw_pw1 = _matmul(
        "conv_dwpw1", (S, D // tkd, T // tt),
        [(hn, pl.BlockSpec((tt, tkd), lambda j, i, k: (k, i))),
         (dag, pl.BlockSpec((None, tt, Ns), lambda j, i, k: (j // half, k, j % half)))],
        [(0, 0, 1)], TN, [(tkd, Ns)],
        [(_sds((S, D, Ns), BF16), pl.BlockSpec((None, tkd, Ns), lambda j, i, k: (j, i, 0)))], _store())[0]
    dh_in, dh_in_b, dgain = _norm_bwd("conv_dnorm", h, gain, dhn, dh)
    d_b_pw1 = jnp.concatenate([d_b_a, d_b_g], axis=1)
    return dh_in, dh_in_b, dgain, dw_pw1, d_b_pw1, d_w_dw, d_b_dw, d_ln_g, d_ln_b, dw_pw2


def _loss_head(y, target):
    T, D = y.shape

    def fn(ins, outs, acc):
        e = ins[0][...] - ins[1][...]
        d = e * (1.0 / D)
        outs[0][...] = d
        outs[1][...] = d.astype(BF16)
        part = jnp.sum(jnp.sum(e * e, axis=-1, keepdims=True), axis=0, keepdims=True) * (0.5 / D)
        acc(0, jnp.broadcast_to(part, (1, 128)))

    return _rowwise("loss_head", fn, T, 256, [(y, 0), (target, 0)], [], [((T, D), F32, 0), ((T, D), BF16, 0)],
                    [(1, 128)])


def _place():
    return lax.axis_index("x"), lax.axis_index("y"), lax.axis_index("c")


def _peer(j):
    x, y, c = _place()
    return (1 - x if j & 4 else x, 1 - y if j & 2 else y, 1 - c if j & 1 else c)


def _index(place):
    return 4 * place[0] + 2 * place[1] + place[2]


HBM = pl.BlockSpec(memory_space=pltpu.HBM)
SEM = pl.BlockSpec(memory_space=pltpu.SEMAPHORE)
EFFECT = pltpu.SideEffectType.DATAFLOW_SIDE_EFFECTING


def _chip(j):
    x, y, _ = _place()
    return (1 - x if j & 2 else x, 1 - y if j & 1 else y)


def _chip_index(chip):
    return 2 * chip[0] + chip[1]


def _remote(src, dst, send, recv, k, device):
    return pltpu.make_async_remote_copy(src_ref=src, dst_ref=dst, send_sem=send.at[k], recv_sem=recv.at[k],
                                        device_id=device, device_id_type=MESH)


def _hbm(arrays):
    return [pltpu.with_memory_space_constraint(a, pltpu.HBM) for a in arrays]


def _split_call(name, body, ins, sems_in, sems_out, after=None, token=False):
    n, ns_in, ns_out = len(ins), len(sems_in), len(sems_out)

    def kernel_body(*refs):
        in_refs = refs[:n]
        si = refs[n:n + ns_in]
        pos = n + ns_in + (1 if after is not None else 0)
        so = refs[pos:pos + ns_out]
        tok = refs[-1] if token else None
        body(in_refs, si, so, tok)
        if token:
            tok[...] = jnp.zeros_like(tok)

    operands = _hbm(ins) + list(sems_in) + ([after] if after is not None else [])
    res = pl.pallas_call(
        kernel_body, name=name,
        in_specs=[HBM] * n + [SEM] * ns_in + ([ANY] if after is not None else []),
        out_specs=[SEM] * ns_out + [HBM] * n + ([pl.BlockSpec(memory_space=pltpu.VMEM)] if token else []),
        out_shape=[pltpu.SemaphoreType.DMA((s,)) for s in sems_out] + [pltpu.HBM(a.shape, a.dtype) for a in ins]
        + ([_sds((8, 128), F32)] if token else []),
        input_output_aliases={i: ns_out + i for i in range(n)},
        compiler_params=pltpu.CompilerParams(has_side_effects=EFFECT),
    )(*operands)
    sems = list(res[:ns_out])
    arrays = list(res[ns_out:ns_out + n])
    return sems, arrays, (res[-1] if token else None)


def _ag_start(name, groups):
    flat = [s for grp in groups for s in grp]
    zones = [lax.empty((N_DEV,) + s.shape, s.dtype) for s in flat]
    n = len(flat)
    sizes = []
    for grp in groups:
        sizes += [7 * len(grp), 7 * len(grp), len(grp)]

    def body(ins, si, so, tok):
        x, y, c = _place()
        me = _index((x, y, c))
        base = 0
        for gi, grp in enumerate(groups):
            send, recv, local = so[3 * gi:3 * gi + 3]
            for w in range(len(grp)):
                src, dst = ins[base + w], ins[n + base + w].at[me]
                pltpu.make_async_copy(src, dst, local.at[w]).start()
                _remote(src, dst, send, recv, 7 * w, (x, y, 1 - c)).start()
                for j in (1, 2, 3):
                    _remote(src, dst, send, recv, 7 * w + j, (*_chip(j), c)).start()
            base += len(grp)

    sems, arrays, token = _split_call(name, body, flat + zones, [], sizes, token=True)
    out, base = [], 0
    for gi, grp in enumerate(groups):
        k = len(grp)
        out.append((sems[3 * gi:3 * gi + 3], arrays[base:base + k], arrays[n + base:n + base + k]))
        base += k
    return out, token


def _ag_forward(name, handle, after):
    sems, shards, zones = handle
    k = len(shards)

    def arrive(ins, si, so, tok):
        send, recv, _ = si
        _, _, c = _place()
        for j in (1, 2, 3):
            for w in range(k):
                blk = ins[k + w].at[_index((*_chip(j), c))]
                _remote(ins[w], blk, send, recv, 7 * w + j, (*_chip(j), c)).wait_recv()

    _, arrays, _ = _split_call(name + "_arrive", arrive, list(shards) + list(zones), sems, [], after=after)

    def pass_on(ins, si, so, tok):
        fsend, frecv = so
        x, y, c = _place()
        for j in (1, 2, 3):
            for w in range(k):
                blk = ins[w].at[_index((*_chip(j), c))]
                _remote(blk, blk, fsend, frecv, 3 * w + j - 1, (x, y, 1 - c)).start()

    fsems, zones2, _ = _split_call(name + "_pass", pass_on, arrays[k:], [], [3 * k, 3 * k])
    return (list(sems) + fsems, arrays[:k], zones2)


def _ag_wait(name, handle, after):
    sems, shards, zones = handle
    k = len(shards)

    def body(ins, si, so, tok):
        send, recv, local, fsend, frecv = si
        x, y, c = _place()
        sib = (x, y, 1 - c)
        for w in range(k):
            zone = ins[k + w]
            _remote(ins[w], zone.at[_index(sib)], send, recv, 7 * w, sib).wait_recv()
            for j in (1, 2, 3):
                blk = zone.at[_index((*_chip(j), 1 - c))]
                _remote(blk, blk, fsend, frecv, 3 * w + j - 1, sib).wait_recv()
        for w in range(k):
            zone = ins[k + w]
            mine = zone.at[_index((x, y, c))]
            _remote(ins[w], mine, send, recv, 7 * w, sib).wait_send()
            for j in (1, 2, 3):
                _remote(ins[w], mine, send, recv, 7 * w + j, (*_chip(j), c)).wait_send()
                blk = zone.at[_index((*_chip(j), c))]
                _remote(blk, blk, fsend, frecv, 3 * w + j - 1, sib).wait_send()
            pltpu.make_async_copy(ins[w], mine, local.at[w]).wait()

    _, arrays, _ = _split_call(name, body, list(shards) + list(zones), sems, [], after=after)
    return arrays[k:]


def _rs_pair_start(name, grads):
    n = len(grads)
    zones = [lax.empty((4,) + g.shape[1:], g.dtype) for g in grads]

    def body(ins, si, so, tok):
        send, recv = so
        x, y, c = _place()
        for w in range(n):
            for q in range(4):
                _remote(ins[w].at[2 * q + 1 - c], ins[n + w].at[q], send, recv, 4 * w + q, (x, y, 1 - c)).start()

    sems, arrays, token = _split_call(name, body, list(grads) + zones, [], [4 * n, 4 * n], token=True)
    return (sems, arrays[:n], arrays[n:]), token


def _rs_pair_wait(name, handle, after):
    sems, grads, zones = handle
    n = len(grads)

    def body(ins, si, so, tok):
        send, recv = si
        x, y, c = _place()
        for w in range(n):
            for q in range(4):
                cp = _remote(ins[w].at[2 * q + 1 - c], ins[n + w].at[q], send, recv, 4 * w + q, (x, y, 1 - c))
                cp.wait_recv()
                cp.wait_send()

    _, arrays, _ = _split_call(name, body, list(grads) + list(zones), sems, [], after=after)
    return arrays[:n], arrays[n:]


def _pair_sum(name, g, got, core):
    _, R, C = g.shape
    g4 = g.reshape(4, 2, R, C)
    tr = _tile(R, 256, 16)

    def body(c_ref, g_ref, a_ref, o_ref):
        o_ref[...] = (g_ref[...].astype(F32) + a_ref[...].astype(F32)).astype(o_ref.dtype)

    return pl.pallas_call(
        body, name=name,
        grid_spec=pltpu.PrefetchScalarGridSpec(
            num_scalar_prefetch=1, grid=(4, R // tr),
            in_specs=[pl.BlockSpec((None, None, tr, C), lambda q, i, c_ref: (q, c_ref[0], i, 0)),
                      pl.BlockSpec((None, tr, C), lambda q, i, c_ref: (q, i, 0))],
            out_specs=pl.BlockSpec((None, tr, C), lambda q, i, c_ref: (q, i, 0))),
        out_shape=_sds((4, R, C), g.dtype),
        compiler_params=_params(),
    )(core, g4, got)


def _rs_chip_start(name, sums):
    n = len(sums)
    zones = [lax.empty(s.shape, s.dtype) for s in sums]

    def body(ins, si, so, tok):
        send, recv, local = so
        x, y, c = _place()
        mine = _chip_index((x, y))
        for w in range(n):
            pltpu.make_async_copy(ins[w].at[mine], ins[n + w].at[mine], local.at[w]).start()
            for j in (1, 2, 3):
                _remote(ins[w].at[_chip_index(_chip(j))], ins[n + w].at[mine], send, recv, 3 * w + j - 1,
                        (*_chip(j), c)).start()

    sems, arrays, token = _split_call(name, body, list(sums) + zones, [], [3 * n, 3 * n, n], token=True)
    return (sems, arrays[:n], arrays[n:]), token


def _rs_chip_wait(name, handle, after):
    sems, sums, zones = handle
    n = len(sums)

    def body(ins, si, so, tok):
        send, recv, local = si
        x, y, c = _place()
        mine = _chip_index((x, y))
        for w in range(n):
            for j in (1, 2, 3):
                _remote(ins[w].at[mine], ins[n + w].at[_chip_index(_chip(j))], send, recv, 3 * w + j - 1,
                        (*_chip(j), c)).wait_recv()
        for w in range(n):
            for j in (1, 2, 3):
                _remote(ins[w].at[_chip_index(_chip(j))], ins[n + w].at[mine], send, recv, 3 * w + j - 1,
                        (*_chip(j), c)).wait_send()
            pltpu.make_async_copy(ins[w].at[mine], ins[n + w].at[mine], local.at[w]).wait()

    _, arrays, _ = _split_call(name, body, list(sums) + list(zones), sems, [], after=after)
    return arrays[n:]


def _chip_sum(name, slots, layer, layers, into=None):
    _, R, C = slots.shape
    tr = _tile(R, 256, 16)

    def body(*refs):
        s_ref, o_ref = refs[0], refs[-1]
        total = s_ref[0].astype(F32)
        for k in range(1, 4):
            total = total + s_ref[k].astype(F32)
        o_ref[...] = total

    extra = [] if into is None else [into]
    return pl.pallas_call(
        body, name=name, grid=(R // tr,),
        in_specs=[pl.BlockSpec((4, tr, C), lambda i: (0, i, 0))] + [ANY] * len(extra),
        out_specs=pl.BlockSpec((None, tr, C), lambda i: (layer, i, 0)),
        out_shape=_sds((layers, R, C), F32),
        input_output_aliases={} if into is None else {1: 0},
        compiler_params=_params(),
    )(slots, *extra)


def _all_reduce_small(pack):
    R, C = pack.shape

    def body(x_ref, o_ref, all_ref, send, recv):
        me = _index(_place())
        all_ref[me] = x_ref[...]
        sends = []
        for j in range(1, N_DEV):
            cp = pltpu.make_async_remote_copy(src_ref=x_ref, dst_ref=all_ref.at[me], send_sem=send.at[j - 1],
                                              recv_sem=recv.at[j - 1], device_id=_peer(j), device_id_type=MESH)
            cp.start()
            sends.append(cp)
        for j in range(1, N_DEV):
            pltpu.make_async_remote_copy(src_ref=x_ref, dst_ref=all_ref.at[_index(_peer(j))], send_sem=send.at[j - 1],
                                         recv_sem=recv.at[j - 1], device_id=_peer(j), device_id_type=MESH).wait_recv()
        for cp in sends:
            cp.wait_send()
        total = all_ref[0]
        for k in range(1, N_DEV):
            total = total + all_ref[k]
        o_ref[...] = total

    return pl.pallas_call(
        body, name="all_reduce_small",
        in_specs=[pl.BlockSpec(memory_space=pltpu.VMEM)], out_specs=pl.BlockSpec(memory_space=pltpu.VMEM),
        out_shape=_sds((R, C), F32),
        scratch_shapes=[pltpu.VMEM((N_DEV, R, C), F32), pltpu.SemaphoreType.DMA((N_DEV - 1,)),
                        pltpu.SemaphoreType.DMA((N_DEV - 1,))],
        compiler_params=_params(),
    )(pack)


def _cast_layer(name, w, layer):
    _, R, C = w.shape
    tr = _tile(R, 256, 16)

    def body(w_ref, o_ref):
        o_ref[...] = w_ref[...].astype(BF16)

    return pl.pallas_call(
        body, name=name, grid=(R // tr,),
        in_specs=[pl.BlockSpec((None, tr, C), lambda i: (layer, i, 0))],
        out_specs=pl.BlockSpec((tr, C), lambda i: (i, 0)),
        out_shape=_sds((R, C), BF16),
    )(w)


def _slot_sum(name, slots):
    _, L, R, C = slots.shape
    tr = _tile(R, 128, 16)

    def body(s_ref, o_ref):
        total = s_ref[0].astype(F32)
        for k in range(1, N_DEV):
            total = total + s_ref[k].astype(F32)
        o_ref[...] = total

    return pl.pallas_call(
        body, name=name, grid=(L, R // tr),
        in_specs=[pl.BlockSpec((N_DEV, None, tr, C), lambda l, i: (0, l, i, 0))],
        out_specs=pl.BlockSpec((None, tr, C), lambda l, i: (l, i, 0)),
        out_shape=_sds((L, R, C), F32),
        compiler_params=_params(),
    )(slots)


def _adamw(name, w, g, m, v):
    shape = w.shape
    R, C = shape[-2], shape[-1]
    L = 1
    for s in shape[:-2]:
        L *= s
    w3, g3, m3, v3 = (a.reshape(L, R, C) for a in (w, g, m, v))
    tr = _tile(R, 128, 8)
    c1 = 1.0 / (1.0 - ADAM_B1 ** ADAM_STEP)
    c2 = 1.0 / (1.0 - ADAM_B2 ** ADAM_STEP)

    def body(w_ref, g_ref, m_ref, v_ref, d_ref, nm_ref, nv_ref):
        g_ = g_ref[...]
        nm = ADAM_B1 * m_ref[...] + (1.0 - ADAM_B1) * g_
        nv = ADAM_B2 * v_ref[...] + (1.0 - ADAM_B2) * (g_ * g_)
        d_ref[...] = -ADAM_LR * ((nm * c1) / (jnp.sqrt(nv * c2) + ADAM_EPS) + ADAM_WD * w_ref[...])
        nm_ref[...] = nm
        nv_ref[...] = nv

    spec = pl.BlockSpec((None, tr, C), lambda l, i: (l, i, 0))
    outs = pl.pallas_call(
        body, name=name, grid=(L, R // tr),
        in_specs=[spec] * 4, out_specs=[spec] * 3,
        out_shape=[_sds((L, R, C), F32)] * 3,
        compiler_params=_params(),
    )(w3, g3, m3, v3)
    return tuple(o.reshape(shape) for o in outs)


def _pad_rows(a, rows):
    return jnp.pad(a, ((0, rows - a.shape[0]), (0, 0)))


def _pad_cols(a, cols):
    return jnp.pad(a, ((0, 0), (0, cols - a.shape[1])))


def _rope_tables(positions):
    q = D_ROPE // 2
    inv_freq = ROPE_THETA ** (-jnp.arange(0, D_ROPE, 2, dtype=F32) / D_ROPE)
    ang = positions.astype(F32)[:, None] * inv_freq
    cos, sin = jnp.cos(ang), jnp.sin(ang)
    z = jnp.zeros_like(cos)
    zz = jnp.zeros((cos.shape[0], 128 - 2 * q), F32)
    c = jnp.concatenate([cos, cos, zz], axis=1)
    s1 = jnp.concatenate([z, sin, zz], axis=1)
    s2 = jnp.concatenate([-sin, z, zz], axis=1)
    return c, s1, s2


def kernel(x, p, positions, ffn_a_norm, ffn_a_w_in, ffn_a_w_out, ffn_b_norm, ffn_b_w_in, ffn_b_w_out, mix_norm, mla_w_in, mla_q_lat_norm, mla_kv_lat_norm, mla_w_uq, mla_w_ukv, mla_q_gain, mla_k_gain, mla_w_o, conv_w_pw1, conv_b_pw1, conv_w_dw, conv_b_dw, conv_ln_g, conv_ln_b, conv_w_pw2, ple_w_proj, ple_norm, ple_gate_norm, ple_w_gate, loss_target, m_ffn_a_norm, m_ffn_a_w_in, m_ffn_a_w_out, m_ffn_b_norm, m_ffn_b_w_in, m_ffn_b_w_out, m_mix_norm, m_mla_w_in, m_mla_q_lat_norm, m_mla_kv_lat_norm, m_mla_w_uq, m_mla_w_ukv, m_mla_q_gain, m_mla_k_gain, m_mla_w_o, m_conv_w_pw1, m_conv_b_pw1, m_conv_w_dw, m_conv_b_dw, m_conv_ln_g, m_conv_ln_b, m_conv_w_pw2, m_ple_w_proj, m_ple_norm, m_ple_gate_norm, m_ple_w_gate, v_ffn_a_norm, v_ffn_a_w_in, v_ffn_a_w_out, v_ffn_b_norm, v_ffn_b_w_in, v_ffn_b_w_out, v_mix_norm, v_mla_w_in, v_mla_q_lat_norm, v_mla_kv_lat_norm, v_mla_w_uq, v_mla_w_ukv, v_mla_q_gain, v_mla_k_gain, v_mla_w_o, v_conv_w_pw1, v_conv_b_pw1, v_conv_w_dw, v_conv_b_dw, v_conv_ln_g, v_conv_ln_b, v_conv_w_pw2, v_ple_w_proj, v_ple_norm, v_ple_gate_norm, v_ple_w_gate):
    weights = dict(ffn_a_norm=ffn_a_norm, ffn_a_w_in=ffn_a_w_in, ffn_a_w_out=ffn_a_w_out, ffn_b_norm=ffn_b_norm,
                   ffn_b_w_in=ffn_b_w_in, ffn_b_w_out=ffn_b_w_out, mix_norm=mix_norm, mla_w_in=mla_w_in,
                   mla_q_lat_norm=mla_q_lat_norm, mla_kv_lat_norm=mla_kv_lat_norm, mla_w_uq=mla_w_uq,
                   mla_w_ukv=mla_w_ukv, mla_q_gain=mla_q_gain, mla_k_gain=mla_k_gain, mla_w_o=mla_w_o,
                   conv_w_pw1=conv_w_pw1, conv_b_pw1=conv_b_pw1, conv_w_dw=conv_w_dw, conv_b_dw=conv_b_dw,
                   conv_ln_g=conv_ln_g, conv_ln_b=conv_ln_b, conv_w_pw2=conv_w_pw2, ple_w_proj=ple_w_proj,
                   ple_norm=ple_norm, ple_gate_norm=ple_gate_norm, ple_w_gate=ple_w_gate)
    moments_m = dict(ffn_a_norm=m_ffn_a_norm, ffn_a_w_in=m_ffn_a_w_in, ffn_a_w_out=m_ffn_a_w_out,
                     ffn_b_norm=m_ffn_b_norm, ffn_b_w_in=m_ffn_b_w_in, ffn_b_w_out=m_ffn_b_w_out,
                     mix_norm=m_mix_norm, mla_w_in=m_mla_w_in, mla_q_lat_norm=m_mla_q_lat_norm,
                     mla_kv_lat_norm=m_mla_kv_lat_norm, mla_w_uq=m_mla_w_uq, mla_w_ukv=m_mla_w_ukv,
                     mla_q_gain=m_mla_q_gain, mla_k_gain=m_mla_k_gain, mla_w_o=m_mla_w_o,
                     conv_w_pw1=m_conv_w_pw1, conv_b_pw1=m_conv_b_pw1, conv_w_dw=m_conv_w_dw,
                     conv_b_dw=m_conv_b_dw, conv_ln_g=m_conv_ln_g, conv_ln_b=m_conv_ln_b, conv_w_pw2=m_conv_w_pw2,
                     ple_w_proj=m_ple_w_proj, ple_norm=m_ple_norm, ple_gate_norm=m_ple_gate_norm,
                     ple_w_gate=m_ple_w_gate)
    moments_v = dict(ffn_a_norm=v_ffn_a_norm, ffn_a_w_in=v_ffn_a_w_in, ffn_a_w_out=v_ffn_a_w_out,
                     ffn_b_norm=v_ffn_b_norm, ffn_b_w_in=v_ffn_b_w_in, ffn_b_w_out=v_ffn_b_w_out,
                     mix_norm=v_mix_norm, mla_w_in=v_mla_w_in, mla_q_lat_norm=v_mla_q_lat_norm,
                     mla_kv_lat_norm=v_mla_kv_lat_norm, mla_w_uq=v_mla_w_uq, mla_w_ukv=v_mla_w_ukv,
                     mla_q_gain=v_mla_q_gain, mla_k_gain=v_mla_k_gain, mla_w_o=v_mla_w_o,
                     conv_w_pw1=v_conv_w_pw1, conv_b_pw1=v_conv_b_pw1, conv_w_dw=v_conv_w_dw,
                     conv_b_dw=v_conv_b_dw, conv_ln_g=v_conv_ln_g, conv_ln_b=v_conv_ln_b, conv_w_pw2=v_conv_w_pw2,
                     ple_w_proj=v_ple_w_proj, ple_norm=v_ple_norm, ple_gate_norm=v_ple_gate_norm,
                     ple_w_gate=v_ple_w_gate)
    order = list(weights.keys())

    T, D = x.shape[1], x.shape[2]
    me = _index(_place())
    h0 = x[0]
    target = loss_target[0]
    tabs = _rope_tables(positions[0])
    H = N_HEADS
    hps = H // N_DEV
    QL = mla_q_lat_norm.shape[1]
    Cs = conv_b_dw.shape[1]

    def layer_shards(i):
        return [_cast_layer(f"cast_{n}{i}", weights[n], i) for n in
                ("ffn_a_w_in", "ffn_a_w_out", "ffn_b_w_in", "ffn_b_w_out", "ple_w_gate", "ple_w_proj")]

    m_in_pad = _pad_cols(mla_w_in[0], mla_w_in.shape[2] - D_ROPE + 128)[None]
    uq_pad = jnp.pad(mla_w_uq[0].reshape(QL, hps, QK_DIM), ((0, 0), (0, 0), (0, HEAD_PAD - QK_DIM)))
    uq_pad = uq_pad.reshape(1, QL, hps * HEAD_PAD)
    mla_shards = [_cast_layer("cast_mla_in", m_in_pad, 0), _cast_layer("cast_mla_uq", uq_pad, 0),
                  _cast_layer("cast_mla_ukv", mla_w_ukv, 0), _cast_layer("cast_mla_wo", mla_w_o, 0)]
    conv_small = jnp.concatenate([
        _pad_rows(_pad_cols(conv_b_pw1, 2 * Cs), 8),
        _pad_rows(_pad_cols(conv_w_dw[0], 2 * Cs), 32),
        _pad_rows(_pad_cols(jnp.concatenate([conv_b_dw, conv_ln_g, conv_ln_b], axis=0), 2 * Cs), 8)], axis=0)
    conv_shards = [_cast_layer("cast_conv_pw1", conv_w_pw1, 0), _cast_layer("cast_conv_pw2", conv_w_pw2, 0)]

    ag, _ = _ag_start("ag_start", [layer_shards(0) + mla_shards, layer_shards(1) + conv_shards + [conv_small]])
    g0 = _ag_wait("ag0_wait", _ag_forward("ag0_forward", ag[0], None), None)

    def layer_weights(g):
        a_in, a_out, b_in, b_out, gate, proj = g[:6]
        return dict(a_in=a_in, a_out=a_out.reshape(-1, D), b_in=b_in, b_out=b_out.reshape(-1, D),
                    gate=gate.reshape(-1, D), proj=proj)

    W = [layer_weights(g0), None]
    Wm = dict(m_in=g0[6].reshape(D, -1), uq=g0[7], ukv=g0[8], wo=g0[9].reshape(-1, D))
    gq_pad = _pad_cols(mla_q_gain, HEAD_PAD)
    gk_pad = _pad_cols(mla_k_gain, HEAD_PAD)

    saved = []
    h = h0
    for i in range(2):
        if i == 1:
            g1 = _ag_wait("ag1_wait", ag1, h)
            W[1] = layer_weights(g1)
            Wc = dict(pw1=g1[6], pw2=g1[7].reshape(-1, D))
            small = g1[8]
            b_pw1_full = small[:, 0:1, :]
            w_dw_full = jnp.transpose(small[:, 8:40, :Cs], (1, 0, 2)).reshape(32, N_DEV * Cs)
            b_dw_full = small[:, 40, :Cs].reshape(1, N_DEV * Cs)
            ln_g_full = small[:, 41, :Cs].reshape(1, N_DEV * Cs)
            ln_b_full = small[:, 42, :Cs].reshape(1, N_DEV * Cs)
        h, s_a = _ffn_fwd(f"ffn_a{i}", h, ffn_a_norm[i:i + 1], W[i]["a_in"], W[i]["a_out"])
        if i == 0:
            h, s_m = _mla_fwd(h, mix_norm[0:1], tabs, Wm, mla_q_lat_norm, mla_kv_lat_norm, gq_pad, gk_pad)
        else:
            h, s_m = _conv_fwd(h, mix_norm[1:2], Wc, b_pw1_full, w_dw_full, b_dw_full, ln_g_full, ln_b_full)
        h, s_b = _ffn_fwd(f"ffn_b{i}", h, ffn_b_norm[i:i + 1], W[i]["b_in"], W[i]["b_out"])
        if i == 0:
            ag1 = _ag_forward("ag1_forward", ag[1], h)
        h, s_p = _ple_fwd(f"ple{i}", h, p[i, 0], W[i]["proj"], ple_norm[i:i + 1], ple_gate_norm[i:i + 1],
                          W[i]["gate"])
        saved.append((s_a, s_m, s_b, s_p))

    dh, dhb, loss_row = _loss_head(h, target)

    G = {}
    small_g = {}
    stacked = ["ffn_a_w_in", "ffn_a_w_out", "ffn_b_w_in", "ffn_b_w_out", "ple_w_proj", "ple_w_gate"]
    single = [["mla_w_in", "mla_w_uq", "mla_w_ukv", "mla_w_o"], ["conv_w_pw1", "conv_w_pw2"]]
    row_sharded = {"ffn_a_w_out", "ffn_b_w_out", "ple_w_gate", "mla_w_in", "mla_w_o", "conv_w_pw2"}
    core = lax.axis_index("c").astype(jnp.int32).reshape(1)

    def layer_grads(i):
        out = []
        for n in stacked + single[i]:
            g = G[(n, i if n in stacked else 0)]
            out.append(g.reshape(N_DEV, g.shape[0] // N_DEV, g.shape[1]) if n in row_sharded else g)
        return out

    def pair_sums(i, mine, got):
        return [_pair_sum(f"pairsum{i}_{n}", a, b, core) for n, a, b in zip(stacked + single[i], mine, got)]

    s_a, s_m, s_b, s_p = saved[1]
    dh, dhb, d_pn, d_gn, G[("ple_w_proj", 1)], G[("ple_w_gate", 1)] = _ple_bwd(
        "ple1", s_p, p[1, 0], W[1]["proj"], ple_norm[1:2], ple_gate_norm[1:2], W[1]["gate"], dh, dhb)
    small_g[("ple_norm", 1)], small_g[("ple_gate_norm", 1)] = d_pn, d_gn
    dh, dhb, small_g[("ffn_b_norm", 1)], G[("ffn_b_w_in", 1)], G[("ffn_b_w_out", 1)] = _ffn_bwd(
        "ffn_b1", s_b, ffn_b_norm[1:2], W[1]["b_in"], W[1]["b_out"], dh, dhb)
    (dh, dhb, small_g[("mix_norm", 1)], G[("conv_w_pw1", 0)], d_b_pw1, d_w_dw, d_b_dw, d_ln_g, d_ln_b,
     G[("conv_w_pw2", 0)]) = _conv_bwd(s_m, mix_norm[1:2], Wc, w_dw_full, ln_g_full, ln_b_full, dh, dhb)
    dh, dhb, small_g[("ffn_a_norm", 1)], G[("ffn_a_w_in", 1)], G[("ffn_a_w_out", 1)] = _ffn_bwd(
        "ffn_a1", s_a, ffn_a_norm[1:2], W[1]["a_in"], W[1]["a_out"], dh, dhb)
    rs1, tok = _rs_pair_start("rs1_pair_start", layer_grads(1))

    s_a, s_m, s_b, s_p = saved[0]
    dh, dhb, d_pn, d_gn, G[("ple_w_proj", 0)], G[("ple_w_gate", 0)] = _ple_bwd(
        "ple0", s_p, p[0, 0], W[0]["proj"], ple_norm[0:1] + tok[0:1, 0:1], ple_gate_norm[0:1], W[0]["gate"], dh, dhb)
    small_g[("ple_norm", 0)], small_g[("ple_gate_norm", 0)] = d_pn, d_gn
    mine, got = _rs_pair_wait("rs1_pair_wait", rs1, dh)
    rs1, tok = _rs_chip_start("rs1_chip_start", pair_sums(1, mine, got))
    dh, dhb, small_g[("ffn_b_norm", 0)], G[("ffn_b_w_in", 0)], G[("ffn_b_w_out", 0)] = _ffn_bwd(
        "ffn_b0", s_b, ffn_b_norm[0:1] + tok[0:1, 0:1], W[0]["b_in"], W[0]["b_out"], dh, dhb)
    (dh, dhb, small_g[("mix_norm", 0)], d_qln, d_kvln, d_gq, d_gk,
     G[("mla_w_in", 0)], G[("mla_w_uq", 0)], G[("mla_w_ukv", 0)], G[("mla_w_o", 0)]) = _mla_bwd(
        s_m, mix_norm[0:1], tabs, Wm, mla_q_lat_norm, mla_kv_lat_norm, gq_pad, gk_pad, dh, dhb)
    dh, dhb, small_g[("ffn_a_norm", 0)], G[("ffn_a_w_in", 0)], G[("ffn_a_w_out", 0)] = _ffn_bwd(
        "ffn_a0", s_a, ffn_a_norm[0:1], W[0]["a_in"], W[0]["a_out"], dh, dhb)
    grad_x = dh[None]
    rs0, tok0 = _rs_pair_start("rs0_pair_start", layer_grads(0))

    def two(name):
        return _pad_rows(jnp.concatenate([small_g[(name, 0)], small_g[(name, 1)]], axis=0), 8)

    misc = jnp.concatenate([_pad_cols(d_qln, D), _pad_cols(d_kvln, D), _pad_cols(d_gq, D), _pad_cols(d_gk, D),
                            _pad_cols(loss_row, D)], axis=0)
    C = N_DEV * Cs
    pack = jnp.concatenate([
        two("ffn_a_norm"), two("ffn_b_norm"), two("mix_norm"), two("ple_norm"), two("ple_gate_norm"),
        _pad_rows(misc, 8),
        _pad_rows(_pad_cols(d_b_pw1.reshape(2, C), D), 8),
        _pad_cols(d_w_dw, D),
        _pad_rows(_pad_cols(jnp.concatenate([d_b_dw, d_ln_g, d_ln_b], axis=0), D), 8)], axis=0)
    red = _all_reduce_small(pack)
    loss = red[44, 0]
    small_grads = dict(
        ffn_a_norm=red[0:2], ffn_b_norm=red[8:10], mix_norm=red[16:18], ple_norm=red[24:26],
        ple_gate_norm=red[32:34],
        mla_q_lat_norm=red[40:41, :QL], mla_kv_lat_norm=red[41:42, :mla_kv_lat_norm.shape[1]],
        mla_q_gain=red[42:43, :QK_DIM], mla_k_gain=red[43:44, :QK_DIM],
        conv_b_pw1=lax.dynamic_slice_in_dim(red[48:50, :C].reshape(1, 2 * C), me * 2 * Cs, 2 * Cs, axis=1),
        conv_w_dw=lax.dynamic_slice_in_dim(red[56:56 + CONV_WIDTH, :C], me * Cs, Cs, axis=1)[None],
        conv_b_dw=lax.dynamic_slice_in_dim(red[88:89, :C], me * Cs, Cs, axis=1),
        conv_ln_g=lax.dynamic_slice_in_dim(red[89:90, :C], me * Cs, Cs, axis=1),
        conv_ln_b=lax.dynamic_slice_in_dim(red[90:91, :C], me * Cs, Cs, axis=1))

    slots1 = _rs_chip_wait("rs1_chip_wait", rs1, red)
    sums = {}
    for n, s in zip(stacked + single[1], slots1):
        sums[n] = _chip_sum(f"chipsum1_{n}", s, 1, 2) if n in stacked else _chip_sum(f"chipsum1_{n}", s, 0, 1)
    mine, got = _rs_pair_wait("rs0_pair_wait", rs0, sums["conv_w_pw2"])
    rs0, _ = _rs_chip_start("rs0_chip_start", pair_sums(0, mine, got))
    slots0 = _rs_chip_wait("rs0_chip_wait", rs0, None)
    for n, s in zip(stacked + single[0], slots0):
        sums[n] = (_chip_sum(f"chipsum0_{n}", s, 0, 2, into=sums[n]) if n in stacked
                   else _chip_sum(f"chipsum0_{n}", s, 0, 1))
    big_grads = {n: sums[n] for n in stacked}
    big_grads["mla_w_in"] = sums["mla_w_in"][:, :, :mla_w_in.shape[2]]
    big_grads["mla_w_uq"] = sums["mla_w_uq"].reshape(1, QL, hps, HEAD_PAD)[..., :QK_DIM].reshape(mla_w_uq.shape)
    for n in ("mla_w_ukv", "mla_w_o", "conv_w_pw1", "conv_w_pw2"):
        big_grads[n] = sums[n]

    grads = {**small_grads, **big_grads}
    deltas, new_m, new_v = {}, {}, {}
    for n in order:
        deltas[n], new_m[n], new_v[n] = _adamw(f"adamw_{n}", weights[n], grads[n], moments_m[n], moments_v[n])

    return (loss, grad_x, *[grads[n] for n in order], *[deltas[n] for n in order],
            *[new_m[n] for n in order], *[new_v[n] for n in order])
```

```python
import functools

import jax
import jax.numpy as jnp
from jax import lax
from jax.experimental import pallas as pl
from jax.experimental.pallas import tpu as pltpu

F32 = jnp.float32
BF16 = jnp.bfloat16
MESH = pl.DeviceIdType.MESH
ANY = pl.BlockSpec(memory_space=pl.ANY)

N_DEV = 8
N_HEADS = 16
D_NOPE = 128
D_ROPE = 64
D_V = 128
QK_DIM = D_NOPE + D_ROPE
HEAD_PAD = 256
ROPE_THETA = 10000.0
CONV_WIDTH = 31
CONV_PAD = 32
FFN_RES = 0.5
EPS = 1e-6
ADAM_LR = 0.001
ADAM_B1 = 0.9
ADAM_B2 = 0.999
ADAM_EPS = 1e-08
ADAM_WD = 0.01
ADAM_STEP = 10
VMEM_LIMIT = 56 * 1024 * 1024


def _sds(shape, dtype):
    return jax.ShapeDtypeStruct(tuple(int(s) for s in shape), dtype)


def _tile(n, pref, mult=128):
    if n <= pref:
        return n
    t = (pref // mult) * mult
    while t >= mult:
        if n % t == 0:
            return t
        t -= mult
    return n


def _params():
    return pltpu.CompilerParams(vmem_limit_bytes=VMEM_LIMIT)


def _matmul(name, grid, ops, terms, dims, acc_shapes, outs, epilogue, extras=()):
    nk = grid[2]
    n_ops, n_ex, n_out, n_acc = len(ops), len(extras), len(outs), len(acc_shapes)

    def body(*refs):
        op_refs = refs[:n_ops]
        ex_refs = refs[n_ops:n_ops + n_ex]
        out_refs = refs[n_ops + n_ex:n_ops + n_ex + n_out]
        acc_refs = refs[n_ops + n_ex + n_out:]
        vals = {}

        def opval(i):
            if i not in vals:
                v = op_refs[i][...]
                vals[i] = v if v.dtype == BF16 else v.astype(BF16)
            return vals[i]

        parts = [None] * n_acc
        for ai, li, ri in terms:
            d = lax.dot_general(opval(li), opval(ri), (dims, ((), ())), preferred_element_type=F32)
            parts[ai] = d if parts[ai] is None else parts[ai] + d
        if nk == 1:
            epilogue(parts, ex_refs, out_refs)
            return
        k = pl.program_id(2)

        @pl.when(k == 0)
        def _():
            for a_ref, p in zip(acc_refs, parts):
                a_ref[...] = p

        @pl.when(k > 0)
        def _():
            for a_ref, p in zip(acc_refs, parts):
                a_ref[...] += p

        @pl.when(k == nk - 1)
        def _():
            epilogue([a[...] for a in acc_refs], ex_refs, out_refs)

    scratch = [pltpu.VMEM(s, F32) for s in acc_shapes] if nk > 1 else []
    res = pl.pallas_call(
        body, name=name, grid=grid,
        in_specs=[s for _, s in ops] + [s for _, s in extras],
        out_specs=[s for _, s in outs],
        out_shape=[o for o, _ in outs],
        scratch_shapes=scratch,
        compiler_params=_params(),
    )(*[a for a, _ in ops], *[a for a, _ in extras])
    return res


NN = ((1,), (0,))
NT = ((1,), (1,))
TN = ((0,), (0,))


def _store(i=0):
    def ep(accs, ex, outs):
        outs[0][...] = accs[0].astype(outs[0].dtype)
    return ep


def _mm_nn(name, a, b, out_dtype, tm=1024, tn=1024, tk=512, epilogue=None, extras=(), extra_outs=()):
    M, K = a.shape
    N = b.shape[1]
    tm, tn, tk = _tile(M, tm, 16), _tile(N, tn), _tile(K, tk)
    outs = [(_sds((M, N), out_dtype), pl.BlockSpec((tm, tn), lambda i, j, k: (i, j)))] + list(extra_outs)
    return _matmul(name, (M // tm, N // tn, K // tk),
                   [(a, pl.BlockSpec((tm, tk), lambda i, j, k: (i, k))),
                    (b, pl.BlockSpec((tk, tn), lambda i, j, k: (k, j)))],
                   [(0, 0, 1)], NN, [(tm, tn)], outs, epilogue or _store(), extras)


def _mm_nt(name, a, b, out_dtype, tm=1024, tn=1024, tk=512, epilogue=None, extras=()):
    M, K = a.shape
    N = b.shape[0]
    tm, tn, tk = _tile(M, tm, 16), _tile(N, tn), _tile(K, tk)
    outs = [(_sds((M, N), out_dtype), pl.BlockSpec((tm, tn), lambda i, j, k: (i, j)))]
    return _matmul(name, (M // tm, N // tn, K // tk),
                   [(a, pl.BlockSpec((tm, tk), lambda i, j, k: (i, k))),
                    (b, pl.BlockSpec((tn, tk), lambda i, j, k: (j, k)))],
                   [(0, 0, 1)], NT, [(tm, tn)], outs, epilogue or _store(), extras)


def _mm_tn(name, a, b, out_dtype, tm=512, tn=1024, tk=1024, scale=None):
    T, M = a.shape
    N = b.shape[1]
    tm, tn, tk = _tile(M, tm), _tile(N, tn), _tile(T, tk)

    def ep(accs, ex, outs):
        v = accs[0] if scale is None else accs[0] * scale
        outs[0][...] = v.astype(outs[0].dtype)

    outs = [(_sds((M, N), out_dtype), pl.BlockSpec((tm, tn), lambda i, j, k: (i, j)))]
    return _matmul(name, (M // tm, N // tn, T // tk),
                   [(a, pl.BlockSpec((tk, tm), lambda i, j, k: (k, i))),
                    (b, pl.BlockSpec((tk, tn), lambda i, j, k: (k, j)))],
                   [(0, 0, 1)], TN, [(tm, tn)], outs, ep)[0]


def _mm_nn_sm(name, a, w, out_dtype, tm=1024, tk=512, epilogue=None, extras=()):
    M, K = a.shape
    S, _, Ns = w.shape
    tm, tk = _tile(M, tm, 16), _tile(K, tk)
    outs = [(_sds((M, S * Ns), out_dtype), pl.BlockSpec((tm, Ns), lambda j, i, k: (i, j)))]
    return _matmul(name, (S, M // tm, K // tk),
                   [(a, pl.BlockSpec((tm, tk), lambda j, i, k: (i, k))),
                    (w, pl.BlockSpec((None, tk, Ns), lambda j, i, k: (j, k, 0)))],
                   [(0, 0, 1)], NN, [(tm, Ns)], outs, epilogue or _store(), extras)[0]


def _mm_nt_sm(name, a, w, out_dtype, tm=1024, tn=1024):
    M = a.shape[0]
    S, K, Ns = w.shape
    tm, tn = _tile(M, tm, 16), _tile(K, tn)
    outs = [(_sds((M, K), out_dtype), pl.BlockSpec((tm, tn), lambda i, n, j: (i, n)))]
    return _matmul(name, (M // tm, K // tn, S),
                   [(a, pl.BlockSpec((tm, Ns), lambda i, n, j: (i, j))),
                    (w, pl.BlockSpec((None, tn, Ns), lambda i, n, j: (j, n, 0)))],
                   [(0, 0, 1)], NT, [(tm, tn)], outs, _store())[0]


def _mm_tn_sm(name, a, b, S, out_dtype, tm=1024, tk=1024):
    T, M = a.shape
    Ns = b.shape[1] // S
    tm, tk = _tile(M, tm), _tile(T, tk)
    outs = [(_sds((S, M, Ns), out_dtype), pl.BlockSpec((None, tm, Ns), lambda j, i, k: (j, i, 0)))]
    return _matmul(name, (S, M // tm, T // tk),
                   [(a, pl.BlockSpec((tk, tm), lambda j, i, k: (k, i))),
                    (b, pl.BlockSpec((tk, Ns), lambda j, i, k: (k, j)))],
                   [(0, 0, 1)], TN, [(tm, Ns)], outs, _store())[0]


def _row_spec(shape, axis, tm):
    block = tuple(tm if d == axis else s for d, s in enumerate(shape))
    nd = len(shape)

    def imap(i):
        return tuple(i if d == axis else 0 for d in range(nd))
    return pl.BlockSpec(block, imap)


def _full_spec(shape):
    nd = len(shape)
    return pl.BlockSpec(tuple(shape), lambda i: (0,) * nd)


def _rowwise(name, fn, T, tm, rows, consts, outs, accs=()):
    tm = _tile(T, tm, 16)
    n_in = len(rows) + len(consts)
    n_out = len(outs)

    def body(*refs):
        in_refs = refs[:n_in]
        out_refs = refs[n_in:n_in + n_out]
        acc_refs = refs[n_in + n_out:]
        i = pl.program_id(0)

        def acc_add(ai, val):
            @pl.when(i == 0)
            def _():
                acc_refs[ai][...] = val

            @pl.when(i > 0)
            def _():
                acc_refs[ai][...] += val

        fn(in_refs, out_refs, acc_add)

    res = pl.pallas_call(
        body, name=name, grid=(T // tm,),
        in_specs=[_row_spec(a.shape, ax, tm) for a, ax in rows] + [_full_spec(c.shape) for c in consts],
        out_specs=[_row_spec(s, ax, tm) for s, _, ax in outs] + [_full_spec(s) for s in accs],
        out_shape=[_sds(s, d) for s, d, _ in outs] + [_sds(s, F32) for s in accs],
        compiler_params=_params(),
    )(*[a for a, _ in rows], *consts)
    return res


def _rms(x, g, n=None):
    n = x.shape[-1] if n is None else n
    return x * lax.rsqrt(jnp.sum(x * x, axis=-1, keepdims=True) * (1.0 / n) + EPS) * g


def _norm_fwd(name, h, gain):
    T, D = h.shape

    def fn(ins, outs, acc):
        outs[0][...] = _rms(ins[0][...], ins[1][...]).astype(BF16)

    return _rowwise(name, fn, T, 256, [(h, 0)], [gain], [((T, D), BF16, 0)])[0]


def _norm_bwd(name, h, gain, dhn, dh_res):
    T, D = h.shape

    def fn(ins, outs, acc):
        _, vjp = jax.vjp(_rms, ins[0][...], ins[3][...])
        dh, dg = vjp(ins[1][...])
        dh = dh + ins[2][...]
        outs[0][...] = dh
        outs[1][...] = dh.astype(BF16)
        acc(0, dg)

    return _rowwise(name, fn, T, 256, [(h, 0), (dhn, 0), (dh_res, 0)], [gain],
                    [((T, D), F32, 0), ((T, D), BF16, 0)], [(1, D)])


def _ffn_fwd(tag, h, gain, get_in, get_out):
    T, D = h.shape
    hn = _norm_fwd(tag + "_norm", h, gain)
    w_in = get_in(hn)
    S, _, Ns = w_in.shape
    half = S // 2
    F = half * Ns
    tm, tk = _tile(T, 1024, 16), _tile(D, 512)

    def ep(accs, ex, outs):
        g, u = accs
        act = g * jax.nn.sigmoid(g) * u
        outs[0][0] = g.astype(BF16)
        outs[0][1] = u.astype(BF16)
        outs[1][...] = act.astype(BF16)

    gu, act = _matmul(
        tag + "_in", (half, T // tm, D // tk),
        [(hn, pl.BlockSpec((tm, tk), lambda j, i, k: (i, k))),
         (w_in, pl.BlockSpec((None, tk, Ns), lambda j, i, k: (j, k, 0))),
         (w_in, pl.BlockSpec((None, tk, Ns), lambda j, i, k: (j + half, k, 0)))],
        [(0, 0, 1), (1, 0, 2)], NN, [(tm, Ns), (tm, Ns)],
        [(_sds((2, T, F), BF16), pl.BlockSpec((2, tm, Ns), lambda j, i, k: (0, i, j))),
         (_sds((T, F), BF16), pl.BlockSpec((tm, Ns), lambda j, i, k: (i, j)))],
        ep)

    def ep_out(accs, ex, outs):
        outs[0][...] = ex[0][...] + FFN_RES * accs[0]

    tn = _tile(D, 1024)
    w_out = get_out(act)
    h_new = _mm_nn(tag + "_out", act, w_out, F32, epilogue=ep_out,
                   extras=[(h, pl.BlockSpec((tm, tn), lambda i, j, k: (i, j)))])[0]
    return h_new, (h, hn, gu, act)


def _ffn_bwd(tag, saved, gain, w_in, w_out, dh, dhb, hook=None):
    h, hn, gu, act = saved
    T, D = h.shape
    S, _, Ns = w_in.shape
    half = S // 2
    F = half * Ns
    tm, tk = _tile(T, 1024, 16), _tile(D, 512)

    def ep(accs, ex, outs):
        dact = FFN_RES * accs[0]
        g = ex[0][0].astype(F32)
        u = ex[0][1].astype(F32)
        sg = jax.nn.sigmoid(g)
        outs[0][0] = (dact * u * (sg * (1.0 + g * (1.0 - sg)))).astype(BF16)
        outs[0][1] = (dact * (g * sg)).astype(BF16)

    gu_spec = pl.BlockSpec((2, tm, Ns), lambda i, j, k: (0, i, j))
    dgu = _matmul(
        tag + "_dact", (T // tm, half, D // tk),
        [(dhb, pl.BlockSpec((tm, tk), lambda i, j, k: (i, k))),
         (w_out, pl.BlockSpec((Ns, tk), lambda i, j, k: (j, k)))],
        [(0, 0, 1)], NT, [(tm, Ns)],
        [(_sds((2, T, F), BF16), gu_spec)], ep, extras=[(gu, gu_spec)])[0]

    dw_out = _mm_tn(tag + "_dwout", act, dhb, BF16, tm=512, tn=2048, tk=1024, scale=FFN_RES)
    if hook is not None:
        hook(dw_out)

    tn = _tile(D, 1024)
    dhn = _matmul(
        tag + "_dhn", (T // tm, D // tn, half),
        [(dgu, pl.BlockSpec((None, tm, Ns), lambda i, n, j: (0, i, j))),
         (dgu, pl.BlockSpec((None, tm, Ns), lambda i, n, j: (1, i, j))),
         (w_in, pl.BlockSpec((None, tn, Ns), lambda i, n, j: (j, n, 0))),
         (w_in, pl.BlockSpec((None, tn, Ns), lambda i, n, j: (j + half, n, 0)))],
        [(0, 0, 2), (0, 1, 3)], NT, [(tm, tn)],
        [(_sds((T, D), F32), pl.BlockSpec((tm, tn), lambda i, n, j: (i, n)))], _store())[0]

    tkd, tt = _tile(D, 1024), _tile(T, 1024)
    dw_in = _matmul(
        tag + "_dwin", (S, D // tkd, T // tt),
        [(hn, pl.BlockSpec((tt, tkd), lambda j, i, k: (k, i))),
         (dgu, pl.BlockSpec((None, tt, Ns), lambda j, i, k: (j // half, k, j % half)))],
        [(0, 0, 1)], TN, [(tkd, Ns)],
        [(_sds((S, D, Ns), BF16), pl.BlockSpec((None, tkd, Ns), lambda j, i, k: (j, i, 0)))], _store())[0]

    dh_in, dh_in_b, dgain = _norm_bwd(tag + "_dnorm", h, gain, dhn, dh)
    return dh_in, dh_in_b, dgain, dw_in, dw_out


def _ple_fwd(tag, h, p, get_proj, ple_norm, gate_norm, get_gate):
    T, D = h.shape
    e_raw = _mm_nn_sm(tag + "_proj", p, get_proj(h), F32, tm=1024, tk=512)
    hn = _norm_fwd(tag + "_norm", h, gate_norm)
    gate_raw = _mm_nn(tag + "_gate", hn, get_gate(hn), F32)[0]

    def fn(ins, outs, acc):
        e = _rms(ins[1][...], ins[3][...])
        outs[0][...] = ins[0][...] + e * jax.nn.sigmoid(ins[2][...])

    h_new = _rowwise(tag + "_mix", fn, T, 256, [(h, 0), (e_raw, 0), (gate_raw, 0)], [ple_norm],
                     [((T, D), F32, 0)])[0]
    return h_new, (h, hn, e_raw, gate_raw)


def _ple_bwd(tag, saved, p, w_proj, ple_norm, gate_norm, w_gate, dh, dhb):
    h, hn, e_raw, gate_raw = saved
    T, D = h.shape
    S = w_proj.shape[0]

    def fn(ins, outs, acc):
        def f(e_raw_, gate_raw_, g_):
            return _rms(e_raw_, g_) * jax.nn.sigmoid(gate_raw_)
        _, vjp = jax.vjp(f, ins[0][...], ins[1][...], ins[3][...])
        de, dgate, dg = vjp(ins[2][...])
        outs[0][...] = de.astype(BF16)
        outs[1][...] = dgate.astype(BF16)
        acc(0, dg)

    de, dgate, d_ple_norm = _rowwise(tag + "_dmix", fn, T, 256, [(e_raw, 0), (gate_raw, 0), (dh, 0)], [ple_norm],
                                     [((T, D), BF16, 0), ((T, D), BF16, 0)], [(1, D)])
    dw_proj = _mm_tn_sm(tag + "_dwproj", p, de, S, BF16)
    dw_gate = _mm_tn(tag + "_dwgate", hn, dgate, BF16)
    dhn = _mm_nt(tag + "_dhn", dgate, w_gate, F32)[0]
    dh_in, dh_in_b, d_gate_norm = _norm_bwd(tag + "_dnorm", h, gate_norm, dhn, dh)
    return dh_in, dh_in_b, d_ple_norm, d_gate_norm, dw_proj, dw_gate


def _rope(t, c, s1, s2):
    q = D_ROPE // 2
    return t * c + pltpu.roll(t, q, 1) * s1 + pltpu.roll(t, 128 - q, 1) * s2


def _rope_t(d, c, s1, s2):
    q = D_ROPE // 2
    return d * c + pltpu.roll(d * s1, 128 - q, 1) + pltpu.roll(d * s2, q, 1)


def _head_norm(lo, hi, g_lo, g_hi):
    ms = (jnp.sum(lo * lo, axis=-1, keepdims=True) + jnp.sum(hi * hi, axis=-1, keepdims=True)) * (1.0 / QK_DIM)
    inv = lax.rsqrt(ms + EPS)
    return lo * inv * g_lo, hi * inv * g_hi


def _qk_prep(qraw, kvraw, lat, tabs, gq, gk, H):
    T = qraw.shape[0]
    koff = lat.shape[1] - 128

    def fn(ins, outs, acc):
        q_ref, kv_ref, lat_ref, c_ref, s1_ref, s2_ref, gq_ref, gk_ref = ins
        c, s1, s2 = c_ref[...], s1_ref[...], s2_ref[...]
        kr = lat_ref[:, koff:koff + 128]
        for hd in range(H):
            o = hd * HEAD_PAD
            lo, hi = _head_norm(q_ref[:, o:o + 128], q_ref[:, o + 128:o + 256], gq_ref[:, 0:128], gq_ref[:, 128:256])
            outs[0][hd, :, 0:128] = lo.astype(BF16)
            outs[0][hd, :, 128:256] = _rope(hi, c, s1, s2).astype(BF16)
            lo, hi = _head_norm(kv_ref[:, o:o + 128], kr, gk_ref[:, 0:128], gk_ref[:, 128:256])
            outs[1][hd, :, 0:128] = lo.astype(BF16)
            outs[1][hd, :, 128:256] = _rope(hi, c, s1, s2).astype(BF16)
            outs[2][hd] = kv_ref[:, o + 128:o + 256].astype(BF16)

    return _rowwise("mla_qkprep", fn, T, 256, [(qraw, 0), (kvraw, 0), (lat, 0)] + [(t, 0) for t in tabs], [gq, gk],
                    [((H, T, HEAD_PAD), BF16, 1), ((H, T, HEAD_PAD), BF16, 1), ((H, T, D_V), BF16, 1)])


def _qk_prep_bwd(qraw, kvraw, lat, tabs, gq, gk, dQ, dK, dV, H):
    T = qraw.shape[0]
    koff = lat.shape[1] - 128

    def fn(ins, outs, acc):
        q_ref, kv_ref, lat_ref, c_ref, s1_ref, s2_ref, dq_ref, dk_ref, dv_ref, gq_ref, gk_ref = ins
        c, s1, s2 = c_ref[...], s1_ref[...], s2_ref[...]
        kr = lat_ref[:, koff:koff + 128]
        dkr = jnp.zeros_like(kr)
        dg = [None] * 4
        for hd in range(H):
            o = hd * HEAD_PAD
            _, vjp = jax.vjp(_head_norm, q_ref[:, o:o + 128], q_ref[:, o + 128:o + 256],
                             gq_ref[:, 0:128], gq_ref[:, 128:256])
            dlo, dhi, dg0, dg1 = vjp((dq_ref[hd, :, 0:128], _rope_t(dq_ref[hd, :, 128:256], c, s1, s2)))
            outs[0][:, o:o + 128] = dlo.astype(BF16)
            outs[0][:, o + 128:o + 256] = dhi.astype(BF16)
            _, vjp = jax.vjp(_head_norm, kv_ref[:, o:o + 128], kr, gk_ref[:, 0:128], gk_ref[:, 128:256])
            dlo, dhi, dg2, dg3 = vjp((dk_ref[hd, :, 0:128], _rope_t(dk_ref[hd, :, 128:256], c, s1, s2)))
            outs[1][:, o:o + 128] = dlo.astype(BF16)
            outs[1][:, o + 128:o + 256] = dv_ref[hd].astype(BF16)
            dkr = dkr + dhi
            for n, v in enumerate((dg0, dg1, dg2, dg3)):
                dg[n] = v if dg[n] is None else dg[n] + v
        outs[2][...] = dkr
        for n in range(4):
            acc(n, dg[n])

    W = H * HEAD_PAD
    return _rowwise("mla_dqkprep", fn, T, 128,
                    [(qraw, 0), (kvraw, 0), (lat, 0)] + [(t, 0) for t in tabs] + [(dQ, 1), (dK, 1), (dV, 1)], [gq, gk],
                    [((T, W), BF16, 0), ((T, W), BF16, 0), ((T, 128), F32, 0)], [(1, 128)] * 4)


def _attn_probs(q_ref, k_ref, i, tq, T):
    s = lax.dot_general(q_ref[...], k_ref[...], (NT, ((), ())), preferred_element_type=F32) * (QK_DIM ** -0.5)
    row = i * tq + lax.broadcasted_iota(jnp.int32, (tq, T), 0)
    col = lax.broadcasted_iota(jnp.int32, (tq, T), 1)
    s = jnp.where(col <= row, s, -jnp.inf)
    p = jnp.exp(s - jnp.max(s, axis=-1, keepdims=True))
    return p / jnp.sum(p, axis=-1, keepdims=True)


def _attn_fwd(Q, K, V):
    H, T, _ = Q.shape
    tq = _tile(T, 256)

    def body(q_ref, k_ref, v_ref, o_ref):
        p = _attn_probs(q_ref, k_ref, pl.program_id(1), tq, T)
        o_ref[...] = jnp.dot(p.astype(BF16), v_ref[...], preferred_element_type=F32).astype(BF16)

    return pl.pallas_call(
        body, name="mla_attn", grid=(H, T // tq),
        in_specs=[pl.BlockSpec((None, tq, HEAD_PAD), lambda h, i: (h, i, 0)),
                  pl.BlockSpec((None, T, HEAD_PAD), lambda h, i: (h, 0, 0)),
                  pl.BlockSpec((None, T, D_V), lambda h, i: (h, 0, 0))],
        out_specs=pl.BlockSpec((tq, D_V), lambda h, i: (i, h)),
        out_shape=_sds((T, H * D_V), BF16),
        compiler_params=_params(),
    )(Q, K, V)


def _attn_bwd(Q, K, V, dO):
    H, T, _ = Q.shape
    tq = _tile(T, 256)

    def body(q_ref, k_ref, v_ref, do_ref, dq_ref, dk_ref, dv_ref):
        i = pl.program_id(1)
        p = _attn_probs(q_ref, k_ref, i, tq, T)
        do = do_ref[...]
        dv = lax.dot_general(p.astype(BF16), do, (TN, ((), ())), preferred_element_type=F32)
        dp = lax.dot_general(do, v_ref[...], (NT, ((), ())), preferred_element_type=F32)
        ds = p * (dp - jnp.sum(p * dp, axis=-1, keepdims=True)) * (QK_DIM ** -0.5)
        dsb = ds.astype(BF16)
        dq_ref[...] = jnp.dot(dsb, k_ref[...], preferred_element_type=F32)
        dk = lax.dot_general(dsb, q_ref[...], (TN, ((), ())), preferred_element_type=F32)

        @pl.when(i == 0)
        def _():
            dk_ref[...] = dk
            dv_ref[...] = dv

        @pl.when(i > 0)
        def _():
            dk_ref[...] += dk
            dv_ref[...] += dv

    return pl.pallas_call(
        body, name="mla_dattn", grid=(H, T // tq),
        in_specs=[pl.BlockSpec((None, tq, HEAD_PAD), lambda h, i: (h, i, 0)),
                  pl.BlockSpec((None, T, HEAD_PAD), lambda h, i: (h, 0, 0)),
                  pl.BlockSpec((None, T, D_V), lambda h, i: (h, 0, 0)),
                  pl.BlockSpec((tq, D_V), lambda h, i: (i, h))],
        out_specs=[pl.BlockSpec((None, tq, HEAD_PAD), lambda h, i: (h, i, 0)),
                   pl.BlockSpec((None, T, HEAD_PAD), lambda h, i: (h, 0, 0)),
                   pl.BlockSpec((None, T, D_V), lambda h, i: (h, 0, 0))],
        out_shape=[_sds((H, T, HEAD_PAD), F32), _sds((H, T, HEAD_PAD), F32), _sds((H, T, D_V), F32)],
        compiler_params=_params(),
    )(Q, K, V, dO)


def _mla_fwd(h, gain, tabs, get, q_lat_norm, kv_lat_norm, gq, gk):
    T, D = h.shape
    QL, KL = q_lat_norm.shape[1], kv_lat_norm.shape[1]
    hn = _norm_fwd("mla_norm", h, gain)
    w = dict(m_in=get["m_in"](hn))
    lat = _mm_nn("mla_lat", hn, w["m_in"], F32, tn=w["m_in"].shape[1])[0]

    def fn(ins, outs, acc):
        outs[0][...] = _rms(ins[0][:, 0:QL], ins[1][...]).astype(BF16)
        outs[1][...] = _rms(ins[0][:, QL:QL + KL], ins[2][...]).astype(BF16)

    cq, ckv = _rowwise("mla_latnorm", fn, T, 256, [(lat, 0)], [q_lat_norm, kv_lat_norm],
                       [((T, QL), BF16, 0), ((T, KL), BF16, 0)])
    w["uq"], w["ukv"] = get["uq"](cq), get["ukv"](ckv)
    H = w["uq"].shape[0] * w["uq"].shape[2] // HEAD_PAD
    qraw = _mm_nn_sm("mla_uq", cq, w["uq"], F32)
    kvraw = _mm_nn_sm("mla_ukv", ckv, w["ukv"], F32)
    Q, K, V = _qk_prep(qraw, kvraw, lat, tabs, gq, gk, H)
    O = _attn_fwd(Q, K, V)
    w["wo"] = get["wo"](O)

    def ep_out(accs, ex, outs):
        outs[0][...] = ex[0][...] + accs[0]

    tm, tn = _tile(T, 1024, 16), _tile(D, 1024)
    h_new = _mm_nn("mla_out", O, w["wo"], F32, epilogue=ep_out,
                   extras=[(h, pl.BlockSpec((tm, tn), lambda i, j, k: (i, j)))])[0]
    return h_new, (h, hn, lat, cq, ckv, qraw, kvraw, Q, K, V, O)


def _mla_bwd(saved, gain, tabs, w, q_lat_norm, kv_lat_norm, gq, gk, dh, dhb):
    h, hn, lat, cq, ckv, qraw, kvraw, Q, K, V, O = saved
    T, D = h.shape
    H = Q.shape[0]
    S = w["uq"].shape[0]
    QL, KL = q_lat_norm.shape[1], kv_lat_norm.shape[1]
    dO = _mm_nt("mla_dO", dhb, w["wo"], BF16)[0]
    dwo = _mm_tn("mla_dwo", O, dhb, BF16)
    dQ, dK, dV = _attn_bwd(Q, K, V, dO)
    dqraw, dkvraw, dkr, dgq0, dgq1, dgk0, dgk1 = _qk_prep_bwd(qraw, kvraw, lat, tabs, gq, gk, dQ, dK, dV, H)
    dcq = _mm_nt_sm("mla_dcq", dqraw, w["uq"], F32)
    dckv = _mm_nt_sm("mla_dckv", dkvraw, w["ukv"], F32)
    dwuq = _mm_tn_sm("mla_dwuq", cq, dqraw, S, BF16)
    dwukv = _mm_tn_sm("mla_dwukv", ckv, dkvraw, S, BF16)

    def fn(ins, outs, acc):
        _, vjp = jax.vjp(_rms, ins[0][:, 0:QL], ins[4][...])
        d, dgq_ = vjp(ins[1][...])
        outs[0][:, 0:QL] = d.astype(BF16)
        _, vjp = jax.vjp(_rms, ins[0][:, QL:QL + KL], ins[5][...])
        d, dgkv_ = vjp(ins[2][...])
        outs[0][:, QL:QL + KL] = d.astype(BF16)
        outs[0][:, QL + KL:QL + KL + 128] = ins[3][...].astype(BF16)
        acc(0, dgq_)
        acc(1, dgkv_)

    dlat, d_qln, d_kvln = _rowwise("mla_dlatnorm", fn, T, 256, [(lat, 0), (dcq, 0), (dckv, 0), (dkr, 0)],
                                   [q_lat_norm, kv_lat_norm], [(lat.shape, BF16, 0)], [(1, QL), (1, KL)])
    dhn = _mm_nt("mla_dhn", dlat, w["m_in"], F32, tk=lat.shape[1])[0]
    dw_min = _mm_tn("mla_dwin", hn, dlat, BF16, tn=lat.shape[1])
    dh_in, dh_in_b, dgain = _norm_bwd("mla_dnorm", h, gain, dhn, dh)
    d_gq = jnp.concatenate([dgq0, dgq1], axis=1)[:, :QK_DIM]
    d_gk = jnp.concatenate([dgk0, dgk1], axis=1)[:, :QK_DIM]
    return dh_in, dh_in_b, dgain, d_qln, d_kvln, d_gq, d_gk, dw_min, dwuq, dwukv, dwo


def _conv_rows(T):
    return _tile(T, 128, 8)


def _dwconv_fwd(u, w_dw, b_dw):
    T, C = u.shape
    tc, R = _tile(C, 256), _conv_rows(T)
    off = CONV_PAD - (CONV_WIDTH - 1)

    def body(u_ref, w_ref, b_ref, y_ref, pad_ref):
        pad_ref[0:CONV_PAD, :] = jnp.zeros((CONV_PAD, tc), F32)
        pad_ref[CONV_PAD:CONV_PAD + T, :] = u_ref[...]
        for r in range(T // R):
            acc = jnp.broadcast_to(b_ref[...], (R, tc))
            for j in range(CONV_WIDTH):
                acc = acc + w_ref[j:j + 1, :] * pad_ref[r * R + off + j:r * R + off + j + R, :]
            y_ref[r * R:(r + 1) * R, :] = acc

    return pl.pallas_call(
        body, name="conv_dw", grid=(C // tc,),
        in_specs=[pl.BlockSpec((T, tc), lambda c: (0, c)), pl.BlockSpec((32, tc), lambda c: (0, c)),
                  pl.BlockSpec((1, tc), lambda c: (0, c))],
        out_specs=pl.BlockSpec((T, tc), lambda c: (0, c)),
        out_shape=_sds((T, C), F32),
        scratch_shapes=[pltpu.VMEM((T + CONV_PAD, tc), F32)],
        compiler_params=_params(),
    )(u, w_dw, b_dw)


def _dwconv_bwd(u, w_dw, dy):
    T, C = u.shape
    tc, R = _tile(C, 256), _conv_rows(T)
    off = CONV_PAD - (CONV_WIDTH - 1)

    def body(u_ref, w_ref, dy_ref, du_ref, dw_ref, db_ref, upad_ref, dpad_ref):
        upad_ref[0:CONV_PAD, :] = jnp.zeros((CONV_PAD, tc), F32)
        upad_ref[CONV_PAD:CONV_PAD + T, :] = u_ref[...]
        dpad_ref[0:T, :] = dy_ref[...]
        dpad_ref[T:T + CONV_PAD, :] = jnp.zeros((CONV_PAD, tc), F32)
        for r in range(T // R):
            acc = jnp.zeros((R, tc), F32)
            for j in range(CONV_WIDTH):
                s = r * R + (CONV_WIDTH - 1) - j
                acc = acc + w_ref[j:j + 1, :] * dpad_ref[s:s + R, :]
            du_ref[r * R:(r + 1) * R, :] = acc
        for j in range(CONV_WIDTH):
            acc = jnp.zeros((R, tc), F32)
            for r in range(T // R):
                acc = acc + dy_ref[r * R:(r + 1) * R, :] * upad_ref[r * R + off + j:r * R + off + j + R, :]
            dw_ref[j:j + 1, :] = jnp.sum(acc, axis=0, keepdims=True)
        dw_ref[CONV_WIDTH:32, :] = jnp.zeros((32 - CONV_WIDTH, tc), F32)
        db_ref[...] = jnp.sum(dy_ref[...], axis=0, keepdims=True)

    return pl.pallas_call(
        body, name="conv_ddw", grid=(C // tc,),
        in_specs=[pl.BlockSpec((T, tc), lambda c: (0, c)), pl.BlockSpec((32, tc), lambda c: (0, c)),
                  pl.BlockSpec((T, tc), lambda c: (0, c))],
        out_specs=[pl.BlockSpec((T, tc), lambda c: (0, c)), pl.BlockSpec((32, tc), lambda c: (0, c)),
                   pl.BlockSpec((1, tc), lambda c: (0, c))],
        out_shape=[_sds((T, C), F32), _sds((32, C), F32), _sds((1, C), F32)],
        scratch_shapes=[pltpu.VMEM((T + CONV_PAD, tc), F32), pltpu.VMEM((T + CONV_PAD, tc), F32)],
        compiler_params=_params(),
    )(u, w_dw, dy)


def _ln_silu(y, g, b):
    mu = jnp.mean(y, axis=-1, keepdims=True)
    yc = y - mu
    z = yc * lax.rsqrt(jnp.mean(yc * yc, axis=-1, keepdims=True) + EPS) * g + b
    return z * jax.nn.sigmoid(z)


def _conv_fwd(h, gain, get):
    T, D = h.shape
    hn = _norm_fwd("conv_norm", h, gain)
    w = dict(pw1=get["pw1"](hn))
    b_pw1, w_dw, b_dw, ln_g, ln_b = get["small"](hn)
    S, _, Ns = w["pw1"].shape
    half = S // 2
    C = half * Ns
    tm, tk = _tile(T, 1024, 16), _tile(D, 512)

    def ep(accs, ex, outs):
        a = accs[0] + ex[0][...]
        g = accs[1] + ex[1][...]
        outs[0][0] = a.astype(BF16)
        outs[0][1] = g.astype(BF16)
        outs[1][...] = a * jax.nn.sigmoid(g)

    ag, u = _matmul(
        "conv_pw1", (half, T // tm, D // tk),
        [(hn, pl.BlockSpec((tm, tk), lambda j, i, k: (i, k))),
         (w["pw1"], pl.BlockSpec((None, tk, Ns), lambda j, i, k: (j, k, 0))),
         (w["pw1"], pl.BlockSpec((None, tk, Ns), lambda j, i, k: (j + half, k, 0)))],
        [(0, 0, 1), (1, 0, 2)], NN, [(tm, Ns), (tm, Ns)],
        [(_sds((2, T, C), BF16), pl.BlockSpec((2, tm, Ns), lambda j, i, k: (0, i, j))),
         (_sds((T, C), F32), pl.BlockSpec((tm, Ns), lambda j, i, k: (i, j)))],
        ep,
        extras=[(b_pw1, pl.BlockSpec((None, 1, Ns), lambda j, i, k: (j, 0, 0))),
                (b_pw1, pl.BlockSpec((None, 1, Ns), lambda j, i, k: (j + half, 0, 0)))])
    y = _dwconv_fwd(u, w_dw, b_dw)

    def fn(ins, outs, acc):
        outs[0][...] = _ln_silu(ins[0][...], ins[1][...], ins[2][...]).astype(BF16)

    s = _rowwise("conv_ln", fn, T, 256, [(y, 0)], [ln_g, ln_b], [((T, C), BF16, 0)])[0]

    def ep_out(accs, ex, outs):
        outs[0][...] = ex[0][...] + accs[0]

    tn = _tile(D, 1024)
    w["pw2"] = get["pw2"](s)
    h_new = _mm_nn("conv_pw2", s, w["pw2"], F32, epilogue=ep_out,
                   extras=[(h, pl.BlockSpec((tm, tn), lambda i, j, k: (i, j)))])[0]
    return h_new, (h, hn, ag, u, y, s)


def _conv_bwd(saved, gain, w, w_dw, ln_g, ln_b, dh, dhb):
    h, hn, ag, u, y, s = saved
    T, D = h.shape
    S, _, Ns = w["pw1"].shape
    half = S // 2
    C = half * Ns
    ds = _mm_nt("conv_ds", dhb, w["pw2"], F32)[0]
    dw_pw2 = _mm_tn("conv_dwpw2", s, dhb, BF16)

    def fn(ins, outs, acc):
        _, vjp = jax.vjp(_ln_silu, ins[0][...], ins[2][...], ins[3][...])
        dy, dg, db = vjp(ins[1][...])
        outs[0][...] = dy
        acc(0, dg)
        acc(1, db)

    dy, d_ln_g, d_ln_b = _rowwise("conv_dln", fn, T, 256, [(y, 0), (ds, 0)], [ln_g, ln_b],
                                  [((T, C), F32, 0)], [(1, C), (1, C)])
    du, d_w_dw, d_b_dw = _dwconv_bwd(u, w_dw, dy)

    def fn2(ins, outs, acc):
        a = ins[0][0].astype(F32)
        g = ins[0][1].astype(F32)
        du_ = ins[1][...]
        sg = jax.nn.sigmoid(g)
        da = du_ * sg
        dg = du_ * a * sg * (1.0 - sg)
        outs[0][0] = da.astype(BF16)
        outs[0][1] = dg.astype(BF16)
        acc(0, jnp.sum(da, axis=0, keepdims=True))
        acc(1, jnp.sum(dg, axis=0, keepdims=True))

    dag, d_b_a, d_b_g = _rowwise("conv_dglu", fn2, T, 256, [(ag, 1), (du, 0)], [],
                                 [((2, T, C), BF16, 1)], [(1, C), (1, C)])
    tm, tn = _tile(T, 1024, 16), _tile(D, 1024)
    dhn = _matmul(
        "conv_dhn", (T // tm, D // tn, half),
        [(dag, pl.BlockSpec((None, tm, Ns), lambda i, n, j: (0, i, j))),
         (dag, pl.BlockSpec((None, tm, Ns), lambda i, n, j: (1, i, j))),
         (w["pw1"], pl.BlockSpec((None, tn, Ns), lambda i, n, j: (j, n, 0))),
         (w["pw1"], pl.BlockSpec((None, tn, Ns), lambda i, n, j: (j + half, n, 0)))],
        [(0, 0, 2), (0, 1, 3)], NT, [(tm, tn)],
        [(_sds((T, D), F32), pl.BlockSpec((tm, tn), lambda i, n, j: (i, n)))], _store())[0]
    tkd, tt = _tile(D, 1024), _tile(T, 1024)
    dw_pw1 = _matmul(
        "conv_dwpw1", (S, D // tkd, T // tt),
        [(hn, pl.BlockSpec((tt, tkd), lambda j, i, k: (k, i))),
         (dag, pl.BlockSpec((None, tt, Ns), lambda j, i, k: (j // half, k, j % half)))],
        [(0, 0, 1)], TN, [(tkd, Ns)],
        [(_sds((S, D, Ns), BF16), pl.BlockSpec((None, tkd, Ns), lambda j, i, k: (j, i, 0)))], _store())[0]
    dh_in, dh_in_b, dgain = _norm_bwd("conv_dnorm", h, gain, dhn, dh)
    d_b_pw1 = jnp.concatenate([d_b_a, d_b_g], axis=1)
    return dh_in, dh_in_b, dgain, dw_pw1, d_b_pw1, d_w_dw, d_b_dw, d_ln_g, d_ln_b, dw_pw2


def _loss_head(y, target):
    T, D = y.shape

    def fn(ins, outs, acc):
        e = ins[0][...] - ins[1][...]
        d = e * (1.0 / D)
        outs[0][...] = d
        outs[1][...] = d.astype(BF16)
        part = jnp.sum(jnp.sum(e * e, axis=-1, keepdims=True), axis=0, keepdims=True) * (0.5 / D)
        acc(0, jnp.broadcast_to(part, (1, 128)))

    return _rowwise("loss_head", fn, T, 256, [(y, 0), (target, 0)], [], [((T, D), F32, 0), ((T, D), BF16, 0)],
                    [(1, 128)])


def _place():
    return lax.axis_index("x"), lax.axis_index("y"), lax.axis_index("c")


def _peer(j):
    x, y, c = _place()
    return (1 - x if j & 4 else x, 1 - y if j & 2 else y, 1 - c if j & 1 else c)


def _index(place):
    return 4 * place[0] + 2 * place[1] + place[2]


HBM = pl.BlockSpec(memory_space=pltpu.HBM)
SEM = pl.BlockSpec(memory_space=pltpu.SEMAPHORE)
EFFECT = pltpu.SideEffectType.DATAFLOW_SIDE_EFFECTING


def _chip(j):
    x, y, _ = _place()
    return (1 - x if j & 2 else x, 1 - y if j & 1 else y)


def _chip_index(chip):
    return 2 * chip[0] + chip[1]


def _remote(src, dst, send, recv, k, device):
    return pltpu.make_async_remote_copy(src_ref=src, dst_ref=dst, send_sem=send.at[k], recv_sem=recv.at[k],
                                        device_id=device, device_id_type=MESH)


def _hbm(arrays):
    return [pltpu.with_memory_space_constraint(a, pltpu.HBM) for a in arrays]


def _split_call(name, body, ins, sems_in, sems_out, after=None, token=False):
    n, ns_in, ns_out = len(ins), len(sems_in), len(sems_out)

    def kernel_body(*refs):
        in_refs = refs[:n]
        si = refs[n:n + ns_in]
        pos = n + ns_in + (1 if after is not None else 0)
        so = refs[pos:pos + ns_out]
        tok = refs[-1] if token else None
        body(in_refs, si, so, tok)
        if token:
            tok[...] = jnp.zeros_like(tok)

    operands = _hbm(ins) + list(sems_in) + ([after] if after is not None else [])
    res = pl.pallas_call(
        kernel_body, name=name,
        in_specs=[HBM] * n + [SEM] * ns_in + ([ANY] if after is not None else []),
        out_specs=[SEM] * ns_out + [HBM] * n + ([pl.BlockSpec(memory_space=pltpu.VMEM)] if token else []),
        out_shape=[pltpu.SemaphoreType.DMA((s,)) for s in sems_out] + [pltpu.HBM(a.shape, a.dtype) for a in ins]
        + ([_sds((8, 128), F32)] if token else []),
        input_output_aliases={i: ns_out + i for i in range(n)},
        compiler_params=pltpu.CompilerParams(has_side_effects=EFFECT),
    )(*operands)
    sems = list(res[:ns_out])
    arrays = list(res[ns_out:ns_out + n])
    return sems, arrays, (res[-1] if token else None)


def _ag_start(name, groups, after=None):
    flat = [s for grp in groups for s in grp]
    zones = [lax.empty((N_DEV,) + s.shape, s.dtype) for s in flat]
    n = len(flat)
    sizes = []
    for grp in groups:
        sizes += [7 * len(grp), 7 * len(grp), len(grp)]

    def body(ins, si, so, tok):
        x, y, c = _place()
        me = _index((x, y, c))
        base = 0
        for gi, grp in enumerate(groups):
            send, recv, local = so[3 * gi:3 * gi + 3]
            for w in range(len(grp)):
                src, dst = ins[base + w], ins[n + base + w].at[me]
                pltpu.make_async_copy(src, dst, local.at[w]).start()
                _remote(src, dst, send, recv, 7 * w, (x, y, 1 - c)).start()
                for j in (1, 2, 3):
                    _remote(src, dst, send, recv, 7 * w + j, (*_chip(j), c)).start()
            base += len(grp)

    sems, arrays, token = _split_call(name, body, flat + zones, [], sizes, after=after, token=True)
    out, base = [], 0
    for gi, grp in enumerate(groups):
        k = len(grp)
        out.append((sems[3 * gi:3 * gi + 3], arrays[base:base + k], arrays[n + base:n + base + k]))
        base += k
    return out, token


def _ag_forward(name, handle, after):
    sems, shards, zones = handle
    k = len(shards)

    def arrive(ins, si, so, tok):
        send, recv, _ = si
        _, _, c = _place()
        for j in (1, 2, 3):
            for w in range(k):
                blk = ins[k + w].at[_index((*_chip(j), c))]
                _remote(ins[w], blk, send, recv, 7 * w + j, (*_chip(j), c)).wait_recv()

    _, arrays, _ = _split_call(name + "_arrive", arrive, list(shards) + list(zones), sems, [], after=after)

    def pass_on(ins, si, so, tok):
        fsend, frecv = so
        x, y, c = _place()
        for j in (1, 2, 3):
            for w in range(k):
                blk = ins[w].at[_index((*_chip(j), c))]
                _remote(blk, blk, fsend, frecv, 3 * w + j - 1, (x, y, 1 - c)).start()

    fsems, zones2, _ = _split_call(name + "_pass", pass_on, arrays[k:], [], [3 * k, 3 * k])
    return (list(sems) + fsems, arrays[:k], zones2)


def _ag_wait(name, handle, after):
    sems, shards, zones = handle
    k = len(shards)

    def body(ins, si, so, tok):
        send, recv, local, fsend, frecv = si
        x, y, c = _place()
        sib = (x, y, 1 - c)
        for w in range(k):
            zone = ins[k + w]
            _remote(ins[w], zone.at[_index(sib)], send, recv, 7 * w, sib).wait_recv()
            for j in (1, 2, 3):
                blk = zone.at[_index((*_chip(j), 1 - c))]
                _remote(blk, blk, fsend, frecv, 3 * w + j - 1, sib).wait_recv()
        for w in range(k):
            zone = ins[k + w]
            mine = zone.at[_index((x, y, c))]
            _remote(ins[w], mine, send, recv, 7 * w, sib).wait_send()
            for j in (1, 2, 3):
                _remote(ins[w], mine, send, recv, 7 * w + j, (*_chip(j), c)).wait_send()
                blk = zone.at[_index((*_chip(j), c))]
                _remote(blk, blk, fsend, frecv, 3 * w + j - 1, sib).wait_send()
            pltpu.make_async_copy(ins[w], mine, local.at[w]).wait()

    _, arrays, _ = _split_call(name, body, list(shards) + list(zones), sems, [], after=after)
    return arrays[k:]


def _rs_pair_start(name, grads, after=None):
    n = len(grads)
    zones = [lax.empty((4,) + g.shape[1:], g.dtype) for g in grads]

    def body(ins, si, so, tok):
        send, recv = so
        x, y, c = _place()
        for w in range(n):
            for q in range(4):
                _remote(ins[w].at[2 * q + 1 - c], ins[n + w].at[q], send, recv, 4 * w + q, (x, y, 1 - c)).start()

    sems, arrays, token = _split_call(name, body, list(grads) + zones, [], [4 * n, 4 * n], after=after, token=True)
    return (sems, arrays[:n], arrays[n:]), token


def _rs_pair_wait(name, handle, after):
    sems, grads, zones = handle
    n = len(grads)

    def body(ins, si, so, tok):
        send, recv = si
        x, y, c = _place()
        for w in range(n):
            for q in range(4):
                cp = _remote(ins[w].at[2 * q + 1 - c], ins[n + w].at[q], send, recv, 4 * w + q, (x, y, 1 - c))
                cp.wait_recv()
                cp.wait_send()

    _, arrays, _ = _split_call(name, body, list(grads) + list(zones), sems, [], after=after)
    return arrays[:n], arrays[n:]


def _pair_sum(name, g, got, core):
    _, R, C = g.shape
    g4 = g.reshape(4, 2, R, C)
    tr = _tile(R, 256, 16)

    def body(c_ref, g_ref, a_ref, o_ref):
        o_ref[...] = (g_ref[...].astype(F32) + a_ref[...].astype(F32)).astype(o_ref.dtype)

    return pl.pallas_call(
        body, name=name,
        grid_spec=pltpu.PrefetchScalarGridSpec(
            num_scalar_prefetch=1, grid=(4, R // tr),
            in_specs=[pl.BlockSpec((None, None, tr, C), lambda q, i, c_ref: (q, c_ref[0], i, 0)),
                      pl.BlockSpec((None, tr, C), lambda q, i, c_ref: (q, i, 0))],
            out_specs=pl.BlockSpec((None, tr, C), lambda q, i, c_ref: (q, i, 0))),
        out_shape=_sds((4, R, C), g.dtype),
        compiler_params=_params(),
    )(core, g4, got)


def _rs_chip_start(name, sums, after=None):
    n = len(sums)
    zones = [lax.empty(s.shape, s.dtype) for s in sums]

    def body(ins, si, so, tok):
        send, recv, local = so
        x, y, c = _place()
        mine = _chip_index((x, y))
        for w in range(n):
            pltpu.make_async_copy(ins[w].at[mine], ins[n + w].at[mine], local.at[w]).start()
            for j in (1, 2, 3):
                _remote(ins[w].at[_chip_index(_chip(j))], ins[n + w].at[mine], send, recv, 3 * w + j - 1,
                        (*_chip(j), c)).start()

    sems, arrays, token = _split_call(name, body, list(sums) + zones, [], [3 * n, 3 * n, n], after=after, token=True)
    return (sems, arrays[:n], arrays[n:]), token


def _rs_chip_wait(name, handle, after):
    sems, sums, zones = handle
    n = len(sums)

    def body(ins, si, so, tok):
        send, recv, local = si
        x, y, c = _place()
        mine = _chip_index((x, y))
        for w in range(n):
            for j in (1, 2, 3):
                _remote(ins[w].at[mine], ins[n + w].at[_chip_index(_chip(j))], send, recv, 3 * w + j - 1,
                        (*_chip(j), c)).wait_recv()
        for w in range(n):
            for j in (1, 2, 3):
                _remote(ins[w].at[_chip_index(_chip(j))], ins[n + w].at[mine], send, recv, 3 * w + j - 1,
                        (*_chip(j), c)).wait_send()
            pltpu.make_async_copy(ins[w].at[mine], ins[n + w].at[mine], local.at[w]).wait()

    _, arrays, _ = _split_call(name, body, list(sums) + list(zones), sems, [], after=after)
    return arrays[n:]


def _chip_sum(name, slots, layer, layers, into=None):
    _, R, C = slots.shape
    tr = _tile(R, 256, 16)

    def body(*refs):
        s_ref, o_ref = refs[0], refs[-1]
        total = s_ref[0].astype(F32)
        for k in range(1, 4):
            total = total + s_ref[k].astype(F32)
        o_ref[...] = total

    extra = [] if into is None else [into]
    return pl.pallas_call(
        body, name=name, grid=(R // tr,),
        in_specs=[pl.BlockSpec((4, tr, C), lambda i: (0, i, 0))] + [ANY] * len(extra),
        out_specs=pl.BlockSpec((None, tr, C), lambda i: (layer, i, 0)),
        out_shape=_sds((layers, R, C), F32),
        input_output_aliases={} if into is None else {1: 0},
        compiler_params=_params(),
    )(slots, *extra)


def _all_reduce_small(pack):
    R, C = pack.shape

    def body(x_ref, o_ref, all_ref, send, recv):
        me = _index(_place())
        all_ref[me] = x_ref[...]
        sends = []
        for j in range(1, N_DEV):
            cp = pltpu.make_async_remote_copy(src_ref=x_ref, dst_ref=all_ref.at[me], send_sem=send.at[j - 1],
                                              recv_sem=recv.at[j - 1], device_id=_peer(j), device_id_type=MESH)
            cp.start()
            sends.append(cp)
        for j in range(1, N_DEV):
            pltpu.make_async_remote_copy(src_ref=x_ref, dst_ref=all_ref.at[_index(_peer(j))], send_sem=send.at[j - 1],
                                         recv_sem=recv.at[j - 1], device_id=_peer(j), device_id_type=MESH).wait_recv()
        for cp in sends:
            cp.wait_send()
        total = all_ref[0]
        for k in range(1, N_DEV):
            total = total + all_ref[k]
        o_ref[...] = total

    return pl.pallas_call(
        body, name="all_reduce_small",
        in_specs=[pl.BlockSpec(memory_space=pltpu.VMEM)], out_specs=pl.BlockSpec(memory_space=pltpu.VMEM),
        out_shape=_sds((R, C), F32),
        scratch_shapes=[pltpu.VMEM((N_DEV, R, C), F32), pltpu.SemaphoreType.DMA((N_DEV - 1,)),
                        pltpu.SemaphoreType.DMA((N_DEV - 1,))],
        compiler_params=_params(),
    )(pack)


def _cast_layer(name, w, layer):
    _, R, C = w.shape
    tr = _tile(R, 256, 16)

    def body(w_ref, o_ref):
        o_ref[...] = w_ref[...].astype(BF16)

    return pl.pallas_call(
        body, name=name, grid=(R // tr,),
        in_specs=[pl.BlockSpec((None, tr, C), lambda i: (layer, i, 0))],
        out_specs=pl.BlockSpec((tr, C), lambda i: (i, 0)),
        out_shape=_sds((R, C), BF16),
    )(w)


def _adam_math(w, g, m, v):
    c1 = 1.0 / (1.0 - ADAM_B1 ** ADAM_STEP)
    c2 = 1.0 / (1.0 - ADAM_B2 ** ADAM_STEP)
    nm = ADAM_B1 * m + (1.0 - ADAM_B1) * g
    nv = ADAM_B2 * v + (1.0 - ADAM_B2) * (g * g)
    return -ADAM_LR * ((nm * c1) / (jnp.sqrt(nv * c2) + ADAM_EPS) + ADAM_WD * w), nm, nv


def _sum_adam(name, slots, w, m, v, layer, into=None):
    L, R, C = w.shape
    tr = _tile(R, 128, 16)

    def body(*refs):
        s_ref, w_ref, m_ref, v_ref = refs[:4]
        g_ref, d_ref, nm_ref, nv_ref = refs[-4:]
        g = s_ref[0].astype(F32)
        for k in range(1, 4):
            g = g + s_ref[k].astype(F32)
        g_ref[...] = g
        d_ref[...], nm_ref[...], nv_ref[...] = _adam_math(w_ref[...], g, m_ref[...], v_ref[...])

    spec = pl.BlockSpec((None, tr, C), lambda i: (layer, i, 0))
    extra = [] if into is None else list(into)
    return pl.pallas_call(
        body, name=name, grid=(R // tr,),
        in_specs=[pl.BlockSpec((4, tr, C), lambda i: (0, i, 0)), spec, spec, spec] + [ANY] * len(extra),
        out_specs=[spec] * 4,
        out_shape=[_sds((L, R, C), F32)] * 4,
        input_output_aliases={4 + k: k for k in range(len(extra))},
        compiler_params=_params(),
    )(slots, w, m, v, *extra)


def _adamw(name, w, g, m, v):
    shape = w.shape
    R, C = shape[-2], shape[-1]
    L = 1
    for s in shape[:-2]:
        L *= s
    w3, g3, m3, v3 = (a.reshape(L, R, C) for a in (w, g, m, v))
    tr = _tile(R, 128, 8)

    def body(w_ref, g_ref, m_ref, v_ref, d_ref, nm_ref, nv_ref):
        d_ref[...], nm_ref[...], nv_ref[...] = _adam_math(w_ref[...], g_ref[...], m_ref[...], v_ref[...])

    spec = pl.BlockSpec((None, tr, C), lambda l, i: (l, i, 0))
    outs = pl.pallas_call(
        body, name=name, grid=(L, R // tr),
        in_specs=[spec] * 4, out_specs=[spec] * 3,
        out_shape=[_sds((L, R, C), F32)] * 3,
        compiler_params=_params(),
    )(w3, g3, m3, v3)
    return tuple(o.reshape(shape) for o in outs)


def _pad_rows(a, rows):
    return jnp.pad(a, ((0, rows - a.shape[0]), (0, 0)))


def _pad_cols(a, cols):
    return jnp.pad(a, ((0, 0), (0, cols - a.shape[1])))


def _rope_tables(positions):
    q = D_ROPE // 2
    inv_freq = ROPE_THETA ** (-jnp.arange(0, D_ROPE, 2, dtype=F32) / D_ROPE)
    ang = positions.astype(F32)[:, None] * inv_freq
    cos, sin = jnp.cos(ang), jnp.sin(ang)
    z = jnp.zeros_like(cos)
    zz = jnp.zeros((cos.shape[0], 128 - 2 * q), F32)
    c = jnp.concatenate([cos, cos, zz], axis=1)
    s1 = jnp.concatenate([z, sin, zz], axis=1)
    s2 = jnp.concatenate([-sin, z, zz], axis=1)
    return c, s1, s2


def kernel(x, p, positions, ffn_a_norm, ffn_a_w_in, ffn_a_w_out, ffn_b_norm, ffn_b_w_in, ffn_b_w_out, mix_norm, mla_w_in, mla_q_lat_norm, mla_kv_lat_norm, mla_w_uq, mla_w_ukv, mla_q_gain, mla_k_gain, mla_w_o, conv_w_pw1, conv_b_pw1, conv_w_dw, conv_b_dw, conv_ln_g, conv_ln_b, conv_w_pw2, ple_w_proj, ple_norm, ple_gate_norm, ple_w_gate, loss_target, m_ffn_a_norm, m_ffn_a_w_in, m_ffn_a_w_out, m_ffn_b_norm, m_ffn_b_w_in, m_ffn_b_w_out, m_mix_norm, m_mla_w_in, m_mla_q_lat_norm, m_mla_kv_lat_norm, m_mla_w_uq, m_mla_w_ukv, m_mla_q_gain, m_mla_k_gain, m_mla_w_o, m_conv_w_pw1, m_conv_b_pw1, m_conv_w_dw, m_conv_b_dw, m_conv_ln_g, m_conv_ln_b, m_conv_w_pw2, m_ple_w_proj, m_ple_norm, m_ple_gate_norm, m_ple_w_gate, v_ffn_a_norm, v_ffn_a_w_in, v_ffn_a_w_out, v_ffn_b_norm, v_ffn_b_w_in, v_ffn_b_w_out, v_mix_norm, v_mla_w_in, v_mla_q_lat_norm, v_mla_kv_lat_norm, v_mla_w_uq, v_mla_w_ukv, v_mla_q_gain, v_mla_k_gain, v_mla_w_o, v_conv_w_pw1, v_conv_b_pw1, v_conv_w_dw, v_conv_b_dw, v_conv_ln_g, v_conv_ln_b, v_conv_w_pw2, v_ple_w_proj, v_ple_norm, v_ple_gate_norm, v_ple_w_gate):
    weights = dict(ffn_a_norm=ffn_a_norm, ffn_a_w_in=ffn_a_w_in, ffn_a_w_out=ffn_a_w_out, ffn_b_norm=ffn_b_norm,
                   ffn_b_w_in=ffn_b_w_in, ffn_b_w_out=ffn_b_w_out, mix_norm=mix_norm, mla_w_in=mla_w_in,
                   mla_q_lat_norm=mla_q_lat_norm, mla_kv_lat_norm=mla_kv_lat_norm, mla_w_uq=mla_w_uq,
                   mla_w_ukv=mla_w_ukv, mla_q_gain=mla_q_gain, mla_k_gain=mla_k_gain, mla_w_o=mla_w_o,
                   conv_w_pw1=conv_w_pw1, conv_b_pw1=conv_b_pw1, conv_w_dw=conv_w_dw, conv_b_dw=conv_b_dw,
                   conv_ln_g=conv_ln_g, conv_ln_b=conv_ln_b, conv_w_pw2=conv_w_pw2, ple_w_proj=ple_w_proj,
                   ple_norm=ple_norm, ple_gate_norm=ple_gate_norm, ple_w_gate=ple_w_gate)
    moments_m = dict(ffn_a_norm=m_ffn_a_norm, ffn_a_w_in=m_ffn_a_w_in, ffn_a_w_out=m_ffn_a_w_out,
                     ffn_b_norm=m_ffn_b_norm, ffn_b_w_in=m_ffn_b_w_in, ffn_b_w_out=m_ffn_b_w_out,
                     mix_norm=m_mix_norm, mla_w_in=m_mla_w_in, mla_q_lat_norm=m_mla_q_lat_norm,
                     mla_kv_lat_norm=m_mla_kv_lat_norm, mla_w_uq=m_mla_w_uq, mla_w_ukv=m_mla_w_ukv,
                     mla_q_gain=m_mla_q_gain, mla_k_gain=m_mla_k_gain, mla_w_o=m_mla_w_o,
                     conv_w_pw1=m_conv_w_pw1, conv_b_pw1=m_conv_b_pw1, conv_w_dw=m_conv_w_dw,
                     conv_b_dw=m_conv_b_dw, conv_ln_g=m_conv_ln_g, conv_ln_b=m_conv_ln_b, conv_w_pw2=m_conv_w_pw2,
                     ple_w_proj=m_ple_w_proj, ple_norm=m_ple_norm, ple_gate_norm=m_ple_gate_norm,
                     ple_w_gate=m_ple_w_gate)
    moments_v = dict(ffn_a_norm=v_ffn_a_norm, ffn_a_w_in=v_ffn_a_w_in, ffn_a_w_out=v_ffn_a_w_out,
                     ffn_b_norm=v_ffn_b_norm, ffn_b_w_in=v_ffn_b_w_in, ffn_b_w_out=v_ffn_b_w_out,
                     mix_norm=v_mix_norm, mla_w_in=v_mla_w_in, mla_q_lat_norm=v_mla_q_lat_norm,
                     mla_kv_lat_norm=v_mla_kv_lat_norm, mla_w_uq=v_mla_w_uq, mla_w_ukv=v_mla_w_ukv,
                     mla_q_gain=v_mla_q_gain, mla_k_gain=v_mla_k_gain, mla_w_o=v_mla_w_o,
                     conv_w_pw1=v_conv_w_pw1, conv_b_pw1=v_conv_b_pw1, conv_w_dw=v_conv_w_dw,
                     conv_b_dw=v_conv_b_dw, conv_ln_g=v_conv_ln_g, conv_ln_b=v_conv_ln_b, conv_w_pw2=v_conv_w_pw2,
                     ple_w_proj=v_ple_w_proj, ple_norm=v_ple_norm, ple_gate_norm=v_ple_gate_norm,
                     ple_w_gate=v_ple_w_gate)
    order = list(weights.keys())

    T, D = x.shape[1], x.shape[2]
    me = _index(_place())
    h0 = x[0]
    target = loss_target[0]
    tabs = _rope_tables(positions[0])
    H = N_HEADS
    hps = H // N_DEV
    QL = mla_q_lat_norm.shape[1]
    Cs = conv_b_dw.shape[1]

    def cast(n, i):
        return _cast_layer(f"cast_{n}{i}", weights[n], i)

    first, tok = _ag_start("ag_start0", [[cast("ffn_a_w_in", 0)]])
    m_in_pad = _pad_cols(mla_w_in[0], mla_w_in.shape[2] - D_ROPE + 128)[None]
    uq_pad = jnp.pad(mla_w_uq[0].reshape(QL, hps, QK_DIM), ((0, 0), (0, 0), (0, HEAD_PAD - QK_DIM)))
    uq_pad = uq_pad.reshape(1, QL, hps * HEAD_PAD)
    conv_small = jnp.concatenate([
        _pad_rows(_pad_cols(conv_b_pw1, 2 * Cs), 8),
        _pad_rows(_pad_cols(conv_w_dw[0], 2 * Cs), 32),
        _pad_rows(_pad_cols(jnp.concatenate([conv_b_dw, conv_ln_g, conv_ln_b], axis=0), 2 * Cs), 8)], axis=0)
    rest, tok = _ag_start("ag_start1", [
        [cast("ffn_a_w_out", 0), _cast_layer("cast_mla_in", m_in_pad, 0), _cast_layer("cast_mla_uq", uq_pad, 0),
         _cast_layer("cast_mla_ukv", mla_w_ukv, 0), _cast_layer("cast_mla_wo", mla_w_o, 0)],
        [cast("ffn_b_w_in", 0), cast("ffn_b_w_out", 0), cast("ple_w_gate", 0), cast("ple_w_proj", 0)],
        [cast("ffn_a_w_in", 1), cast("ffn_a_w_out", 1)],
        [_cast_layer("cast_conv_pw1", conv_w_pw1, 0), _cast_layer("cast_conv_pw2", conv_w_pw2, 0), conv_small],
        [cast("ffn_b_w_in", 1), cast("ffn_b_w_out", 1), cast("ple_w_gate", 1), cast("ple_w_proj", 1)]], after=tok)
    groups = [dict(handle=hd, stage=0, arrays=None) for hd in first + rest]

    def prefetch(gi, after):
        st = groups[gi]
        if st["stage"] == 0:
            st["handle"] = _ag_forward(f"ag{gi}_forward", st["handle"], after)
            st["stage"] = 1

    def fetch(gi, after):
        prefetch(gi, after)
        st = groups[gi]
        if st["stage"] == 1:
            st["arrays"] = _ag_wait(f"ag{gi}_wait", st["handle"], after)
            st["stage"] = 2
        return st["arrays"]

    def getter(gi, k, shape=None, ahead=None):
        def get(after):
            if ahead is not None:
                prefetch(ahead, after)
            a = fetch(gi, after)[k]
            return a if shape is None else a.reshape(shape)
        return get

    def conv_small_params(after):
        small = fetch(4, after)[2]
        return (small[:, 0:1, :],
                jnp.transpose(small[:, 8:40, :Cs], (1, 0, 2)).reshape(32, N_DEV * Cs),
                small[:, 40, :Cs].reshape(1, N_DEV * Cs), small[:, 41, :Cs].reshape(1, N_DEV * Cs),
                small[:, 42, :Cs].reshape(1, N_DEV * Cs))

    rows = (-1, D)
    get_ffn = [dict(a_in=getter(0, 0), a_out=getter(1, 0, rows), b_in=getter(2, 0), b_out=getter(2, 1, rows)),
               dict(a_in=getter(3, 0), a_out=getter(3, 1, rows, ahead=4), b_in=getter(5, 0),
                    b_out=getter(5, 1, rows))]
    get_ple = [dict(proj=getter(2, 3, ahead=3), gate=getter(2, 2, rows)),
               dict(proj=getter(5, 3), gate=getter(5, 2, rows))]
    get_mla = dict(m_in=getter(1, 1, (D, -1)), uq=getter(1, 2), ukv=getter(1, 3), wo=getter(1, 4, rows, ahead=2))
    get_conv = dict(pw1=getter(4, 0), pw2=getter(4, 1, rows, ahead=5), small=conv_small_params)
    gq_pad = _pad_cols(mla_q_gain, HEAD_PAD)
    gk_pad = _pad_cols(mla_k_gain, HEAD_PAD)
    prefetch(0, tok)

    saved = []
    h = h0
    for i in range(2):
        h, s_a = _ffn_fwd(f"ffn_a{i}", h, ffn_a_norm[i:i + 1], get_ffn[i]["a_in"], get_ffn[i]["a_out"])
        if i == 0:
            h, s_m = _mla_fwd(h, mix_norm[0:1], tabs, get_mla, mla_q_lat_norm, mla_kv_lat_norm, gq_pad, gk_pad)
        else:
            h, s_m = _conv_fwd(h, mix_norm[1:2], get_conv)
        h, s_b = _ffn_fwd(f"ffn_b{i}", h, ffn_b_norm[i:i + 1], get_ffn[i]["b_in"], get_ffn[i]["b_out"])
        h, s_p = _ple_fwd(f"ple{i}", h, p[i, 0], get_ple[i]["proj"], ple_norm[i:i + 1], ple_gate_norm[i:i + 1],
                          get_ple[i]["gate"])
        saved.append((s_a, s_m, s_b, s_p))
    W = [dict(a_in=get_ffn[i]["a_in"](None), a_out=get_ffn[i]["a_out"](None), b_in=get_ffn[i]["b_in"](None),
              b_out=get_ffn[i]["b_out"](None), proj=get_ple[i]["proj"](None), gate=get_ple[i]["gate"](None))
         for i in range(2)]
    Wm = {n: g(None) for n, g in get_mla.items()}
    Wc = dict(pw1=get_conv["pw1"](None), pw2=get_conv["pw2"](None))
    _, w_dw_full, _, ln_g_full, ln_b_full = conv_small_params(None)

    dh, dhb, loss_row = _loss_head(h, target)

    G = {}
    small_g = {}
    stacked = ["ffn_a_w_in", "ffn_a_w_out", "ffn_b_w_in", "ffn_b_w_out", "ple_w_proj", "ple_w_gate"]
    row_sharded = {"ffn_a_w_out", "ffn_b_w_out", "ple_w_gate", "mla_w_in", "mla_w_o", "conv_w_pw2"}
    core = lax.axis_index("c").astype(jnp.int32).reshape(1)
    rs_groups = {
        "l1": [(n, 1) for n in stacked] + [("conv_w_pw1", 0), ("conv_w_pw2", 0)],
        "pb": [("ple_w_proj", 0), ("ple_w_gate", 0), ("ffn_b_w_in", 0), ("ffn_b_w_out", 0)],
        "m": [("mla_w_in", 0), ("mla_w_uq", 0), ("mla_w_ukv", 0), ("mla_w_o", 0)],
        "a": [("ffn_a_w_in", 0), ("ffn_a_w_out", 0)]}

    def rs_begin(tag, after=None):
        grads = []
        for n, i in rs_groups[tag]:
            g = G[(n, i)]
            grads.append(g.reshape(N_DEV, g.shape[0] // N_DEV, g.shape[1]) if n in row_sharded else g)
        return _rs_pair_start(f"rs_{tag}_pair_start", grads, after)

    def rs_middle(tag, handle, after):
        mine, got = _rs_pair_wait(f"rs_{tag}_pair_wait", handle, after)
        sums = [_pair_sum(f"pairsum_{n}{i}", a, b, core) for (n, i), a, b in zip(rs_groups[tag], mine, got)]
        return _rs_chip_start(f"rs_{tag}_chip_start", sums)

    def rs_end(tag, handle, after):
        return dict(zip(rs_groups[tag], _rs_chip_wait(f"rs_{tag}_chip_wait", handle, after)))

    s_a, s_m, s_b, s_p = saved[1]
    dh, dhb, d_pn, d_gn, G[("ple_w_proj", 1)], G[("ple_w_gate", 1)] = _ple_bwd(
        "ple1", s_p, p[1, 0], W[1]["proj"], ple_norm[1:2], ple_gate_norm[1:2], W[1]["gate"], dh, dhb)
    small_g[("ple_norm", 1)], small_g[("ple_gate_norm", 1)] = d_pn, d_gn
    dh, dhb, small_g[("ffn_b_norm", 1)], G[("ffn_b_w_in", 1)], G[("ffn_b_w_out", 1)] = _ffn_bwd(
        "ffn_b1", s_b, ffn_b_norm[1:2], W[1]["b_in"], W[1]["b_out"], dh, dhb)
    (dh, dhb, small_g[("mix_norm", 1)], G[("conv_w_pw1", 0)], d_b_pw1, d_w_dw, d_b_dw, d_ln_g, d_ln_b,
     G[("conv_w_pw2", 0)]) = _conv_bwd(s_m, mix_norm[1:2], Wc, w_dw_full, ln_g_full, ln_b_full, dh, dhb)
    dh, dhb, small_g[("ffn_a_norm", 1)], G[("ffn_a_w_in", 1)], G[("ffn_a_w_out", 1)] = _ffn_bwd(
        "ffn_a1", s_a, ffn_a_norm[1:2], W[1]["a_in"], W[1]["a_out"], dh, dhb)
    rs_l1, tok = rs_begin("l1")

    s_a, s_m, s_b, s_p = saved[0]
    dh, dhb, d_pn, d_gn, G[("ple_w_proj", 0)], G[("ple_w_gate", 0)] = _ple_bwd(
        "ple0", s_p, p[0, 0], W[0]["proj"], ple_norm[0:1] + tok[0:1, 0:1], ple_gate_norm[0:1], W[0]["gate"], dh, dhb)
    small_g[("ple_norm", 0)], small_g[("ple_gate_norm", 0)] = d_pn, d_gn
    rs_l1, tok = rs_middle("l1", rs_l1, dh)
    dh, dhb, small_g[("ffn_b_norm", 0)], G[("ffn_b_w_in", 0)], G[("ffn_b_w_out", 0)] = _ffn_bwd(
        "ffn_b0", s_b, ffn_b_norm[0:1] + tok[0:1, 0:1], W[0]["b_in"], W[0]["b_out"], dh, dhb)
    rs_pb, tok = rs_begin("pb")
    (dh, dhb, small_g[("mix_norm", 0)], d_qln, d_kvln, d_gq, d_gk,
     G[("mla_w_in", 0)], G[("mla_w_uq", 0)], G[("mla_w_ukv", 0)], G[("mla_w_o", 0)]) = _mla_bwd(
        s_m, mix_norm[0:1] + tok[0:1, 0:1], tabs, Wm, mla_q_lat_norm, mla_kv_lat_norm, gq_pad, gk_pad, dh, dhb)
    rs_pb, tok = rs_middle("pb", rs_pb, dh)
    rs_m, tok = rs_begin("m", after=tok)
    mid = {}

    def in_ffn_a0(after):
        mid["m"], mid["tok"] = rs_middle("m", rs_m, after)

    dh, dhb, small_g[("ffn_a_norm", 0)], G[("ffn_a_w_in", 0)], G[("ffn_a_w_out", 0)] = _ffn_bwd(
        "ffn_a0", s_a, ffn_a_norm[0:1] + tok[0:1, 0:1], W[0]["a_in"], W[0]["a_out"], dh, dhb, hook=in_ffn_a0)
    grad_x = dh[None]
    rs_a, tok = rs_begin("a")

    def two(name):
        return _pad_rows(jnp.concatenate([small_g[(name, 0)], small_g[(name, 1)]], axis=0), 8)

    misc = jnp.concatenate([_pad_cols(d_qln, D), _pad_cols(d_kvln, D), _pad_cols(d_gq, D), _pad_cols(d_gk, D),
                            _pad_cols(loss_row, D)], axis=0)
    C = N_DEV * Cs
    pack = jnp.concatenate([
        two("ffn_a_norm"), two("ffn_b_norm"), two("mix_norm"), two("ple_norm"), two("ple_gate_norm"),
        _pad_rows(misc, 8),
        _pad_rows(_pad_cols(d_b_pw1.reshape(2, C), D), 8),
        _pad_cols(d_w_dw, D),
        _pad_rows(_pad_cols(jnp.concatenate([d_b_dw, d_ln_g, d_ln_b], axis=0), D), 8)], axis=0)
    red = _all_reduce_small(pack)
    loss = red[44, 0]
    small_grads = dict(
        ffn_a_norm=red[0:2], ffn_b_norm=red[8:10], mix_norm=red[16:18], ple_norm=red[24:26],
        ple_gate_norm=red[32:34],
        mla_q_lat_norm=red[40:41, :QL], mla_kv_lat_norm=red[41:42, :mla_kv_lat_norm.shape[1]],
        mla_q_gain=red[42:43, :QK_DIM], mla_k_gain=red[43:44, :QK_DIM],
        conv_b_pw1=lax.dynamic_slice_in_dim(red[48:50, :C].reshape(1, 2 * C), me * 2 * Cs, 2 * Cs, axis=1),
        conv_w_dw=lax.dynamic_slice_in_dim(red[56:56 + CONV_WIDTH, :C], me * Cs, Cs, axis=1)[None],
        conv_b_dw=lax.dynamic_slice_in_dim(red[88:89, :C], me * Cs, Cs, axis=1),
        conv_ln_g=lax.dynamic_slice_in_dim(red[89:90, :C], me * Cs, Cs, axis=1),
        conv_ln_b=lax.dynamic_slice_in_dim(red[90:91, :C], me * Cs, Cs, axis=1))

    rs_a, tok = rs_middle("a", rs_a, red)
    done = {}

    def plain_adamw(n, g):
        done[n] = (g,) + _adamw(f"adamw_{n}", weights[n], g, moments_m[n], moments_v[n])

    def slot_adamw(n, layer, slots):
        done[n] = tuple(_sum_adam(f"adamw_{n}{layer}", slots, weights[n], moments_m[n], moments_v[n], layer,
                                  into=done.get(n)))

    for n, g in small_grads.items():
        plain_adamw(n, g)
    slots = rs_end("l1", rs_l1, tok)
    for n, i in rs_groups["l1"]:
        slot_adamw(n, i, slots[(n, i)])
    slots = rs_end("pb", rs_pb, done["conv_w_pw2"][0])
    for n, i in rs_groups["pb"]:
        slot_adamw(n, i, slots[(n, i)])
    slots = rs_end("m", mid["m"], done["ffn_b_w_out"][0])
    slot_adamw("mla_w_ukv", 0, slots[("mla_w_ukv", 0)])
    slot_adamw("mla_w_o", 0, slots[("mla_w_o", 0)])
    g_in = _chip_sum("chipsum_mla_w_in", slots[("mla_w_in", 0)], 0, 1)
    plain_adamw("mla_w_in", g_in[:, :, :mla_w_in.shape[2]])
    g_uq = _chip_sum("chipsum_mla_w_uq", slots[("mla_w_uq", 0)], 0, 1)
    plain_adamw("mla_w_uq", g_uq.reshape(1, QL, hps, HEAD_PAD)[..., :QK_DIM].reshape(mla_w_uq.shape))
    slots = rs_end("a", rs_a, done["mla_w_uq"][1])
    for n, i in rs_groups["a"]:
        slot_adamw(n, i, slots[(n, i)])
    grads, deltas, new_m, new_v = ({n: done[n][k] for n in order} for k in range(4))

    return (loss, grad_x, *[grads[n] for n in order], *[deltas[n] for n in order],
            *[new_m[n] for n in order], *[new_v[n] for n in order])
```

```python
import functools

import jax
import jax.numpy as jnp
from jax import lax
from jax.experimental import pallas as pl
from jax.experimental.pallas import tpu as pltpu

F32 = jnp.float32
BF16 = jnp.bfloat16
MESH = pl.DeviceIdType.MESH
ANY = pl.BlockSpec(memory_space=pl.ANY)

N_DEV = 8
N_HEADS = 16
D_NOPE = 128
D_ROPE = 64
D_V = 128
QK_DIM = D_NOPE + D_ROPE
HEAD_PAD = 256
ROPE_THETA = 10000.0
CONV_WIDTH = 31
CONV_PAD = 32
FFN_RES = 0.5
EPS = 1e-6
ADAM_LR = 0.001
ADAM_B1 = 0.9
ADAM_B2 = 0.999
ADAM_EPS = 1e-08
ADAM_WD = 0.01
ADAM_STEP = 10
VMEM_LIMIT = 56 * 1024 * 1024


def _sds(shape, dtype):
    return jax.ShapeDtypeStruct(tuple(int(s) for s in shape), dtype)


def _tile(n, pref, mult=128):
    if n <= pref:
        return n
    t = (pref // mult) * mult
    while t >= mult:
        if n % t == 0:
            return t
        t -= mult
    return n


def _params():
    return pltpu.CompilerParams(vmem_limit_bytes=VMEM_LIMIT)


_LAST = [None]


def _ordered_call(body, operands, chain_out=0, **kw):
    operands = list(operands)
    n_in = len(operands)
    if _LAST[0] is not None:
        inner = body

        def body(*refs):
            inner(*refs[:n_in], *refs[n_in + 1:])

        kw = dict(kw, in_specs=list(kw["in_specs"]) + [ANY])
        operands.append(_LAST[0])
    out = pl.pallas_call(body, **kw)(*operands)
    _LAST[0] = out[chain_out] if isinstance(out, (list, tuple)) else out
    return out


def _matmul(name, grid, ops, terms, dims, acc_shapes, outs, epilogue, extras=()):
    nk = grid[2]
    n_ops, n_ex, n_out, n_acc = len(ops), len(extras), len(outs), len(acc_shapes)

    def body(*refs):
        op_refs = refs[:n_ops]
        ex_refs = refs[n_ops:n_ops + n_ex]
        out_refs = refs[n_ops + n_ex:n_ops + n_ex + n_out]
        acc_refs = refs[n_ops + n_ex + n_out:]
        vals = {}

        def opval(i):
            if i not in vals:
                v = op_refs[i][...]
                vals[i] = v if v.dtype == BF16 else v.astype(BF16)
            return vals[i]

        parts = [None] * n_acc
        for ai, li, ri in terms:
            d = lax.dot_general(opval(li), opval(ri), (dims, ((), ())), preferred_element_type=F32)
            parts[ai] = d if parts[ai] is None else parts[ai] + d
        if nk == 1:
            epilogue(parts, ex_refs, out_refs)
            return
        k = pl.program_id(2)

        @pl.when(k == 0)
        def _():
            for a_ref, p in zip(acc_refs, parts):
                a_ref[...] = p

        @pl.when(k > 0)
        def _():
            for a_ref, p in zip(acc_refs, parts):
                a_ref[...] += p

        @pl.when(k == nk - 1)
        def _():
            epilogue([a[...] for a in acc_refs], ex_refs, out_refs)

    scratch = [pltpu.VMEM(s, F32) for s in acc_shapes] if nk > 1 else []
    return _ordered_call(
        body, [a for a, _ in ops] + [a for a, _ in extras], name=name, grid=grid,
        in_specs=[s for _, s in ops] + [s for _, s in extras],
        out_specs=[s for _, s in outs],
        out_shape=[o for o, _ in outs],
        scratch_shapes=scratch,
        compiler_params=_params(),
    )


NN = ((1,), (0,))
NT = ((1,), (1,))
TN = ((0,), (0,))


def _store(i=0):
    def ep(accs, ex, outs):
        outs[0][...] = accs[0].astype(outs[0].dtype)
    return ep


def _mm_nn(name, a, b, out_dtype, tm=1024, tn=1024, tk=512, epilogue=None, extras=(), extra_outs=()):
    M, K = a.shape
    N = b.shape[1]
    tm, tn, tk = _tile(M, tm, 16), _tile(N, tn), _tile(K, tk)
    outs = [(_sds((M, N), out_dtype), pl.BlockSpec((tm, tn), lambda i, j, k: (i, j)))] + list(extra_outs)
    return _matmul(name, (M // tm, N // tn, K // tk),
                   [(a, pl.BlockSpec((tm, tk), lambda i, j, k: (i, k))),
                    (b, pl.BlockSpec((tk, tn), lambda i, j, k: (k, j)))],
                   [(0, 0, 1)], NN, [(tm, tn)], outs, epilogue or _store(), extras)


def _mm_nt(name, a, b, out_dtype, tm=1024, tn=1024, tk=512, epilogue=None, extras=()):
    M, K = a.shape
    N = b.shape[0]
    tm, tn, tk = _tile(M, tm, 16), _tile(N, tn), _tile(K, tk)
    outs = [(_sds((M, N), out_dtype), pl.BlockSpec((tm, tn), lambda i, j, k: (i, j)))]
    return _matmul(name, (M // tm, N // tn, K // tk),
                   [(a, pl.BlockSpec((tm, tk), lambda i, j, k: (i, k))),
                    (b, pl.BlockSpec((tn, tk), lambda i, j, k: (j, k)))],
                   [(0, 0, 1)], NT, [(tm, tn)], outs, epilogue or _store(), extras)


def _mm_tn(name, a, b, out_dtype, tm=512, tn=1024, tk=1024, scale=None):
    T, M = a.shape
    N = b.shape[1]
    tm, tn, tk = _tile(M, tm), _tile(N, tn), _tile(T, tk)

    def ep(accs, ex, outs):
        v = accs[0] if scale is None else accs[0] * scale
        outs[0][...] = v.astype(outs[0].dtype)

    outs = [(_sds((M, N), out_dtype), pl.BlockSpec((tm, tn), lambda i, j, k: (i, j)))]
    return _matmul(name, (M // tm, N // tn, T // tk),
                   [(a, pl.BlockSpec((tk, tm), lambda i, j, k: (k, i))),
                    (b, pl.BlockSpec((tk, tn), lambda i, j, k: (k, j)))],
                   [(0, 0, 1)], TN, [(tm, tn)], outs, ep)[0]


def _mm_nn_sm(name, a, w, out_dtype, tm=1024, tk=512, epilogue=None, extras=()):
    M, K = a.shape
    S, _, Ns = w.shape
    tm, tk = _tile(M, tm, 16), _tile(K, tk)
    outs = [(_sds((M, S * Ns), out_dtype), pl.BlockSpec((tm, Ns), lambda j, i, k: (i, j)))]
    return _matmul(name, (S, M // tm, K // tk),
                   [(a, pl.BlockSpec((tm, tk), lambda j, i, k: (i, k))),
                    (w, pl.BlockSpec((None, tk, Ns), lambda j, i, k: (j, k, 0)))],
                   [(0, 0, 1)], NN, [(tm, Ns)], outs, epilogue or _store(), extras)[0]


def _mm_nt_sm(name, a, w, out_dtype, tm=1024, tn=1024):
    M = a.shape[0]
    S, K, Ns = w.shape
    tm, tn = _tile(M, tm, 16), _tile(K, tn)
    outs = [(_sds((M, K), out_dtype), pl.BlockSpec((tm, tn), lambda i, n, j: (i, n)))]
    return _matmul(name, (M // tm, K // tn, S),
                   [(a, pl.BlockSpec((tm, Ns), lambda i, n, j: (i, j))),
                    (w, pl.BlockSpec((None, tn, Ns), lambda i, n, j: (j, n, 0)))],
                   [(0, 0, 1)], NT, [(tm, tn)], outs, _store())[0]


def _mm_tn_sm(name, a, b, S, out_dtype, tm=1024, tk=1024):
    T, M = a.shape
    Ns = b.shape[1] // S
    tm, tk = _tile(M, tm), _tile(T, tk)
    outs = [(_sds((S, M, Ns), out_dtype), pl.BlockSpec((None, tm, Ns), lambda j, i, k: (j, i, 0)))]
    return _matmul(name, (S, M // tm, T // tk),
                   [(a, pl.BlockSpec((tk, tm), lambda j, i, k: (k, i))),
                    (b, pl.BlockSpec((tk, Ns), lambda j, i, k: (k, j)))],
                   [(0, 0, 1)], TN, [(tm, Ns)], outs, _store())[0]


def _row_spec(shape, axis, tm):
    block = tuple(tm if d == axis else s for d, s in enumerate(shape))
    nd = len(shape)

    def imap(i):
        return tuple(i if d == axis else 0 for d in range(nd))
    return pl.BlockSpec(block, imap)


def _full_spec(shape):
    nd = len(shape)
    return pl.BlockSpec(tuple(shape), lambda i: (0,) * nd)


def _rowwise(name, fn, T, tm, rows, consts, outs, accs=()):
    tm = _tile(T, tm, 16)
    n_in = len(rows) + len(consts)
    n_out = len(outs)

    def body(*refs):
        in_refs = refs[:n_in]
        out_refs = refs[n_in:n_in + n_out]
        acc_refs = refs[n_in + n_out:]
        i = pl.program_id(0)

        def acc_add(ai, val):
            @pl.when(i == 0)
            def _():
                acc_refs[ai][...] = val

            @pl.when(i > 0)
            def _():
                acc_refs[ai][...] += val

        fn(in_refs, out_refs, acc_add)

    return _ordered_call(
        body, [a for a, _ in rows] + list(consts), name=name, grid=(T // tm,),
        in_specs=[_row_spec(a.shape, ax, tm) for a, ax in rows] + [_full_spec(c.shape) for c in consts],
        out_specs=[_row_spec(s, ax, tm) for s, _, ax in outs] + [_full_spec(s) for s in accs],
        out_shape=[_sds(s, d) for s, d, _ in outs] + [_sds(s, F32) for s in accs],
        compiler_params=_params(),
    )


def _rms(x, g, n=None):
    n = x.shape[-1] if n is None else n
    return x * lax.rsqrt(jnp.sum(x * x, axis=-1, keepdims=True) * (1.0 / n) + EPS) * g


def _norm_fwd(name, h, gain):
    T, D = h.shape

    def fn(ins, outs, acc):
        outs[0][...] = _rms(ins[0][...], ins[1][...]).astype(BF16)

    return _rowwise(name, fn, T, 256, [(h, 0)], [gain], [((T, D), BF16, 0)])[0]


def _norm_bwd(name, h, gain, dhn, dh_res):
    T, D = h.shape

    def fn(ins, outs, acc):
        _, vjp = jax.vjp(_rms, ins[0][...], ins[3][...])
        dh, dg = vjp(ins[1][...])
        dh = dh + ins[2][...]
        outs[0][...] = dh
        outs[1][...] = dh.astype(BF16)
        acc(0, dg)

    return _rowwise(name, fn, T, 256, [(h, 0), (dhn, 0), (dh_res, 0)], [gain],
                    [((T, D), F32, 0), ((T, D), BF16, 0)], [(1, D)])


def _ffn_fwd(tag, h, gain, get_in, get_out):
    T, D = h.shape
    hn = _norm_fwd(tag + "_norm", h, gain)
    w_in = get_in(hn)
    S, _, Ns = w_in.shape
    half = S // 2
    F = half * Ns
    tm, tk = _tile(T, 1024, 16), _tile(D, 512)

    def ep(accs, ex, outs):
        g, u = accs
        act = g * jax.nn.sigmoid(g) * u
        outs[0][0] = g.astype(BF16)
        outs[0][1] = u.astype(BF16)
        outs[1][...] = act.astype(BF16)

    gu, act = _matmul(
        tag + "_in", (half, T // tm, D // tk),
        [(hn, pl.BlockSpec((tm, tk), lambda j, i, k: (i, k))),
         (w_in, pl.BlockSpec((None, tk, Ns), lambda j, i, k: (j, k, 0))),
         (w_in, pl.BlockSpec((None, tk, Ns), lambda j, i, k: (j + half, k, 0)))],
        [(0, 0, 1), (1, 0, 2)], NN, [(tm, Ns), (tm, Ns)],
        [(_sds((2, T, F), BF16), pl.BlockSpec((2, tm, Ns), lambda j, i, k: (0, i, j))),
         (_sds((T, F), BF16), pl.BlockSpec((tm, Ns), lambda j, i, k: (i, j)))],
        ep)

    def ep_out(accs, ex, outs):
        outs[0][...] = ex[0][...] + FFN_RES * accs[0]

    tn = _tile(D, 1024)
    w_out = get_out(act)
    h_new = _mm_nn(tag + "_out", act, w_out, F32, epilogue=ep_out,
                   extras=[(h, pl.BlockSpec((tm, tn), lambda i, j, k: (i, j)))])[0]
    return h_new, (h, hn, gu, act)


def _no_hook(*_):
    return None


def _ffn_bwd(tag, saved, gain, w_in, w_out, dh, dhb, hooks=(_no_hook, _no_hook, _no_hook)):
    h, hn, gu, act = saved
    T, D = h.shape
    S, _, Ns = w_in.shape
    half = S // 2
    F = half * Ns
    tm, tk = _tile(T, 1024, 16), _tile(D, 512)

    def ep(accs, ex, outs):
        dact = FFN_RES * accs[0]
        g = ex[0][0].astype(F32)
        u = ex[0][1].astype(F32)
        sg = jax.nn.sigmoid(g)
        outs[0][0] = (dact * u * (sg * (1.0 + g * (1.0 - sg)))).astype(BF16)
        outs[0][1] = (dact * (g * sg)).astype(BF16)

    gu_spec = pl.BlockSpec((2, tm, Ns), lambda i, j, k: (0, i, j))
    dgu = _matmul(
        tag + "_dact", (T // tm, half, D // tk),
        [(dhb, pl.BlockSpec((tm, tk), lambda i, j, k: (i, k))),
         (w_out, pl.BlockSpec((Ns, tk), lambda i, j, k: (j, k)))],
        [(0, 0, 1)], NT, [(tm, Ns)],
        [(_sds((2, T, F), BF16), gu_spec)], ep, extras=[(gu, gu_spec)])[0]

    dw_out = _mm_tn(tag + "_dwout", act, dhb, BF16, tm=512, tn=2048, tk=1024, scale=FFN_RES)
    hooks[0](dw_out)

    tkd, tt = _tile(D, 1024), _tile(T, 1024)
    dw_in = _matmul(
        tag + "_dwin", (S, D // tkd, T // tt),
        [(hn, pl.BlockSpec((tt, tkd), lambda j, i, k: (k, i))),
         (dgu, pl.BlockSpec((None, tt, Ns), lambda j, i, k: (j // half, k, j % half)))],
        [(0, 0, 1)], TN, [(tkd, Ns)],
        [(_sds((S, D, Ns), BF16), pl.BlockSpec((None, tkd, Ns), lambda j, i, k: (j, i, 0)))], _store())[0]
    hooks[1](dw_in)

    tn = _tile(D, 1024)
    dhn = _matmul(
        tag + "_dhn", (T // tm, D // tn, half),
        [(dgu, pl.BlockSpec((None, tm, Ns), lambda i, n, j: (0, i, j))),
         (dgu, pl.BlockSpec((None, tm, Ns), lambda i, n, j: (1, i, j))),
         (w_in, pl.BlockSpec((None, tn, Ns), lambda i, n, j: (j, n, 0))),
         (w_in, pl.BlockSpec((None, tn, Ns), lambda i, n, j: (j + half, n, 0)))],
        [(0, 0, 2), (0, 1, 3)], NT, [(tm, tn)],
        [(_sds((T, D), F32), pl.BlockSpec((tm, tn), lambda i, n, j: (i, n)))], _store())[0]
    hooks[2]()

    dh_in, dh_in_b, dgain = _norm_bwd(tag + "_dnorm", h, gain, dhn, dh)
    return dh_in, dh_in_b, dgain, dw_in, dw_out


def _ple_fwd(tag, h, p, get_proj, ple_norm, gate_norm, get_gate):
    T, D = h.shape
    e_raw = _mm_nn_sm(tag + "_proj", p, get_proj(h), F32, tm=1024, tk=512)
    hn = _norm_fwd(tag + "_norm", h, gate_norm)
    gate_raw = _mm_nn(tag + "_gate", hn, get_gate(hn), F32)[0]

    def fn(ins, outs, acc):
        e = _rms(ins[1][...], ins[3][...])
        outs[0][...] = ins[0][...] + e * jax.nn.sigmoid(ins[2][...])

    h_new = _rowwise(tag + "_mix", fn, T, 256, [(h, 0), (e_raw, 0), (gate_raw, 0)], [ple_norm],
                     [((T, D), F32, 0)])[0]
    return h_new, (h, hn, e_raw, gate_raw)


def _ple_bwd(tag, saved, p, w_proj, ple_norm, gate_norm, w_gate, dh, dhb):
    h, hn, e_raw, gate_raw = saved
    T, D = h.shape
    S = w_proj.shape[0]

    def fn(ins, outs, acc):
        def f(e_raw_, gate_raw_, g_):
            return _rms(e_raw_, g_) * jax.nn.sigmoid(gate_raw_)
        _, vjp = jax.vjp(f, ins[0][...], ins[1][...], ins[3][...])
        de, dgate, dg = vjp(ins[2][...])
        outs[0][...] = de.astype(BF16)
        outs[1][...] = dgate.astype(BF16)
        acc(0, dg)

    de, dgate, d_ple_norm = _rowwise(tag + "_dmix", fn, T, 256, [(e_raw, 0), (gate_raw, 0), (dh, 0)], [ple_norm],
                                     [((T, D), BF16, 0), ((T, D), BF16, 0)], [(1, D)])
    dw_proj = _mm_tn_sm(tag + "_dwproj", p, de, S, BF16)
    dw_gate = _mm_tn(tag + "_dwgate", hn, dgate, BF16)
    dhn = _mm_nt(tag + "_dhn", dgate, w_gate, F32)[0]
    dh_in, dh_in_b, d_gate_norm = _norm_bwd(tag + "_dnorm", h, gate_norm, dhn, dh)
    return dh_in, dh_in_b, d_ple_norm, d_gate_norm, dw_proj, dw_gate


def _rope(t, c, s1, s2):
    q = D_ROPE // 2
    return t * c + pltpu.roll(t, q, 1) * s1 + pltpu.roll(t, 128 - q, 1) * s2


def _rope_t(d, c, s1, s2):
    q = D_ROPE // 2
    return d * c + pltpu.roll(d * s1, 128 - q, 1) + pltpu.roll(d * s2, q, 1)


def _head_norm(lo, hi, g_lo, g_hi):
    ms = (jnp.sum(lo * lo, axis=-1, keepdims=True) + jnp.sum(hi * hi, axis=-1, keepdims=True)) * (1.0 / QK_DIM)
    inv = lax.rsqrt(ms + EPS)
    return lo * inv * g_lo, hi * inv * g_hi


def _qk_prep(qraw, kvraw, lat, tabs, gq, gk, H):
    T = qraw.shape[0]
    koff = lat.shape[1] - 128

    def fn(ins, outs, acc):
        q_ref, kv_ref, lat_ref, c_ref, s1_ref, s2_ref, gq_ref, gk_ref = ins
        c, s1, s2 = c_ref[...], s1_ref[...], s2_ref[...]
        kr = lat_ref[:, koff:koff + 128]
        for hd in range(H):
            o = hd * HEAD_PAD
            lo, hi = _head_norm(q_ref[:, o:o + 128], q_ref[:, o + 128:o + 256], gq_ref[:, 0:128], gq_ref[:, 128:256])
            outs[0][hd, :, 0:128] = lo.astype(BF16)
            outs[0][hd, :, 128:256] = _rope(hi, c, s1, s2).astype(BF16)
            lo, hi = _head_norm(kv_ref[:, o:o + 128], kr, gk_ref[:, 0:128], gk_ref[:, 128:256])
            outs[1][hd, :, 0:128] = lo.astype(BF16)
            outs[1][hd, :, 128:256] = _rope(hi, c, s1, s2).astype(BF16)
            outs[2][hd] = kv_ref[:, o + 128:o + 256].astype(BF16)

    return _rowwise("mla_qkprep", fn, T, 256, [(qraw, 0), (kvraw, 0), (lat, 0)] + [(t, 0) for t in tabs], [gq, gk],
                    [((H, T, HEAD_PAD), BF16, 1), ((H, T, HEAD_PAD), BF16, 1), ((H, T, D_V), BF16, 1)])


def _qk_prep_bwd(qraw, kvraw, lat, tabs, gq, gk, dQ, dK, dV, H):
    T = qraw.shape[0]
    koff = lat.shape[1] - 128

    def fn(ins, outs, acc):
        q_ref, kv_ref, lat_ref, c_ref, s1_ref, s2_ref, dq_ref, dk_ref, dv_ref, gq_ref, gk_ref = ins
        c, s1, s2 = c_ref[...], s1_ref[...], s2_ref[...]
        kr = lat_ref[:, koff:koff + 128]
        dkr = jnp.zeros_like(kr)
        dg = [None] * 4
        for hd in range(H):
            o = hd * HEAD_PAD
            _, vjp = jax.vjp(_head_norm, q_ref[:, o:o + 128], q_ref[:, o + 128:o + 256],
                             gq_ref[:, 0:128], gq_ref[:, 128:256])
            dlo, dhi, dg0, dg1 = vjp((dq_ref[hd, :, 0:128], _rope_t(dq_ref[hd, :, 128:256], c, s1, s2)))
            outs[0][:, o:o + 128] = dlo.astype(BF16)
            outs[0][:, o + 128:o + 256] = dhi.astype(BF16)
            _, vjp = jax.vjp(_head_norm, kv_ref[:, o:o + 128], kr, gk_ref[:, 0:128], gk_ref[:, 128:256])
            dlo, dhi, dg2, dg3 = vjp((dk_ref[hd, :, 0:128], _rope_t(dk_ref[hd, :, 128:256], c, s1, s2)))
            outs[1][:, o:o + 128] = dlo.astype(BF16)
            outs[1][:, o + 128:o + 256] = dv_ref[hd].astype(BF16)
            dkr = dkr + dhi
            for n, v in enumerate((dg0, dg1, dg2, dg3)):
                dg[n] = v if dg[n] is None else dg[n] + v
        outs[2][...] = dkr
        for n in range(4):
            acc(n, dg[n])

    W = H * HEAD_PAD
    return _rowwise("mla_dqkprep", fn, T, 128,
                    [(qraw, 0), (kvraw, 0), (lat, 0)] + [(t, 0) for t in tabs] + [(dQ, 1), (dK, 1), (dV, 1)], [gq, gk],
                    [((T, W), BF16, 0), ((T, W), BF16, 0), ((T, 128), F32, 0)], [(1, 128)] * 4)


def _attn_probs(q_ref, k_ref, i, tq, T):
    s = lax.dot_general(q_ref[...], k_ref[...], (NT, ((), ())), preferred_element_type=F32) * (QK_DIM ** -0.5)
    row = i * tq + lax.broadcasted_iota(jnp.int32, (tq, T), 0)
    col = lax.broadcasted_iota(jnp.int32, (tq, T), 1)
    s = jnp.where(col <= row, s, -jnp.inf)
    p = jnp.exp(s - jnp.max(s, axis=-1, keepdims=True))
    return p / jnp.sum(p, axis=-1, keepdims=True)


def _attn_fwd(Q, K, V):
    H, T, _ = Q.shape
    tq = _tile(T, 256)

    def body(q_ref, k_ref, v_ref, o_ref):
        p = _attn_probs(q_ref, k_ref, pl.program_id(1), tq, T)
        o_ref[...] = jnp.dot(p.astype(BF16), v_ref[...], preferred_element_type=F32).astype(BF16)

    return _ordered_call(
        body, [Q, K, V], name="mla_attn", grid=(H, T // tq),
        in_specs=[pl.BlockSpec((None, tq, HEAD_PAD), lambda h, i: (h, i, 0)),
                  pl.BlockSpec((None, T, HEAD_PAD), lambda h, i: (h, 0, 0)),
                  pl.BlockSpec((None, T, D_V), lambda h, i: (h, 0, 0))],
        out_specs=pl.BlockSpec((tq, D_V), lambda h, i: (i, h)),
        out_shape=_sds((T, H * D_V), BF16),
        compiler_params=_params(),
    )


def _attn_bwd(Q, K, V, dO):
    H, T, _ = Q.shape
    tq = _tile(T, 256)

    def body(q_ref, k_ref, v_ref, do_ref, dq_ref, dk_ref, dv_ref):
        i = pl.program_id(1)
        p = _attn_probs(q_ref, k_ref, i, tq, T)
        do = do_ref[...]
        dv = lax.dot_general(p.astype(BF16), do, (TN, ((), ())), preferred_element_type=F32)
        dp = lax.dot_general(do, v_ref[...], (NT, ((), ())), preferred_element_type=F32)
        ds = p * (dp - jnp.sum(p * dp, axis=-1, keepdims=True)) * (QK_DIM ** -0.5)
        dsb = ds.astype(BF16)
        dq_ref[...] = jnp.dot(dsb, k_ref[...], preferred_element_type=F32)
        dk = lax.dot_general(dsb, q_ref[...], (TN, ((), ())), preferred_element_type=F32)

        @pl.when(i == 0)
        def _():
            dk_ref[...] = dk
            dv_ref[...] = dv

        @pl.when(i > 0)
        def _():
            dk_ref[...] += dk
            dv_ref[...] += dv

    return _ordered_call(
        body, [Q, K, V, dO], name="mla_dattn", grid=(H, T // tq),
        in_specs=[pl.BlockSpec((None, tq, HEAD_PAD), lambda h, i: (h, i, 0)),
                  pl.BlockSpec((None, T, HEAD_PAD), lambda h, i: (h, 0, 0)),
                  pl.BlockSpec((None, T, D_V), lambda h, i: (h, 0, 0)),
                  pl.BlockSpec((tq, D_V), lambda h, i: (i, h))],
        out_specs=[pl.BlockSpec((None, tq, HEAD_PAD), lambda h, i: (h, i, 0)),
                   pl.BlockSpec((None, T, HEAD_PAD), lambda h, i: (h, 0, 0)),
                   pl.BlockSpec((None, T, D_V), lambda h, i: (h, 0, 0))],
        out_shape=[_sds((H, T, HEAD_PAD), F32), _sds((H, T, HEAD_PAD), F32), _sds((H, T, D_V), F32)],
        compiler_params=_params(),
    )


def _mla_fwd(h, gain, tabs, get, q_lat_norm, kv_lat_norm, gq, gk):
    T, D = h.shape
    QL, KL = q_lat_norm.shape[1], kv_lat_norm.shape[1]
    hn = _norm_fwd("mla_norm", h, gain)
    w = dict(m_in=get["m_in"](hn))
    lat = _mm_nn("mla_lat", hn, w["m_in"], F32, tn=w["m_in"].shape[1])[0]

    def fn(ins, outs, acc):
        outs[0][...] = _rms(ins[0][:, 0:QL], ins[1][...]).astype(BF16)
        outs[1][...] = _rms(ins[0][:, QL:QL + KL], ins[2][...]).astype(BF16)

    cq, ckv = _rowwise("mla_latnorm", fn, T, 256, [(lat, 0)], [q_lat_norm, kv_lat_norm],
                       [((T, QL), BF16, 0), ((T, KL), BF16, 0)])
    w["uq"], w["ukv"] = get["uq"](cq), get["ukv"](ckv)
    H = w["uq"].shape[0] * w["uq"].shape[2] // HEAD_PAD
    qraw = _mm_nn_sm("mla_uq", cq, w["uq"], F32)
    kvraw = _mm_nn_sm("mla_ukv", ckv, w["ukv"], F32)
    Q, K, V = _qk_prep(qraw, kvraw, lat, tabs, gq, gk, H)
    O = _attn_fwd(Q, K, V)
    w["wo"] = get["wo"](O)

    def ep_out(accs, ex, outs):
        outs[0][...] = ex[0][...] + accs[0]

    tm, tn = _tile(T, 1024, 16), _tile(D, 1024)
    h_new = _mm_nn("mla_out", O, w["wo"], F32, epilogue=ep_out,
                   extras=[(h, pl.BlockSpec((tm, tn), lambda i, j, k: (i, j)))])[0]
    return h_new, (h, hn, lat, cq, ckv, qraw, kvraw, Q, K, V, O)


def _mla_bwd(saved, gain, tabs, w, q_lat_norm, kv_lat_norm, gq, gk, dh, dhb, hook=_no_hook):
    h, hn, lat, cq, ckv, qraw, kvraw, Q, K, V, O = saved
    T, D = h.shape
    H = Q.shape[0]
    S = w["uq"].shape[0]
    QL, KL = q_lat_norm.shape[1], kv_lat_norm.shape[1]
    dO = _mm_nt("mla_dO", dhb, w["wo"], BF16)[0]
    dwo = _mm_tn("mla_dwo", O, dhb, BF16)
    hook()
    dQ, dK, dV = _attn_bwd(Q, K, V, dO)
    dqraw, dkvraw, dkr, dgq0, dgq1, dgk0, dgk1 = _qk_prep_bwd(qraw, kvraw, lat, tabs, gq, gk, dQ, dK, dV, H)
    dcq = _mm_nt_sm("mla_dcq", dqraw, w["uq"], F32)
    dckv = _mm_nt_sm("mla_dckv", dkvraw, w["ukv"], F32)
    dwuq = _mm_tn_sm("mla_dwuq", cq, dqraw, S, BF16)
    dwukv = _mm_tn_sm("mla_dwukv", ckv, dkvraw, S, BF16)

    def fn(ins, outs, acc):
        _, vjp = jax.vjp(_rms, ins[0][:, 0:QL], ins[4][...])
        d, dgq_ = vjp(ins[1][...])
        outs[0][:, 0:QL] = d.astype(BF16)
        _, vjp = jax.vjp(_rms, ins[0][:, QL:QL + KL], ins[5][...])
        d, dgkv_ = vjp(ins[2][...])
        outs[0][:, QL:QL + KL] = d.astype(BF16)
        outs[0][:, QL + KL:QL + KL + 128] = ins[3][...].astype(BF16)
        acc(0, dgq_)
        acc(1, dgkv_)

    dlat, d_qln, d_kvln = _rowwise("mla_dlatnorm", fn, T, 256, [(lat, 0), (dcq, 0), (dckv, 0), (dkr, 0)],
                                   [q_lat_norm, kv_lat_norm], [(lat.shape, BF16, 0)], [(1, QL), (1, KL)])
    dhn = _mm_nt("mla_dhn", dlat, w["m_in"], F32, tk=lat.shape[1])[0]
    dw_min = _mm_tn("mla_dwin", hn, dlat, BF16, tn=lat.shape[1])
    dh_in, dh_in_b, dgain = _norm_bwd("mla_dnorm", h, gain, dhn, dh)
    d_gq = jnp.concatenate([dgq0, dgq1], axis=1)[:, :QK_DIM]
    d_gk = jnp.concatenate([dgk0, dgk1], axis=1)[:, :QK_DIM]
    return dh_in, dh_in_b, dgain, d_qln, d_kvln, d_gq, d_gk, dw_min, dwuq, dwukv, dwo


def _conv_rows(T):
    return _tile(T, 128, 8)


def _dwconv_fwd(u, w_dw, b_dw):
    T, C = u.shape
    tc, R = _tile(C, 256), _conv_rows(T)
    off = CONV_PAD - (CONV_WIDTH - 1)

    def body(u_ref, w_ref, b_ref, y_ref, pad_ref):
        pad_ref[0:CONV_PAD, :] = jnp.zeros((CONV_PAD, tc), F32)
        pad_ref[CONV_PAD:CONV_PAD + T, :] = u_ref[...]
        for r in range(T // R):
            acc = jnp.broadcast_to(b_ref[...], (R, tc))
            for j in range(CONV_WIDTH):
                acc = acc + w_ref[j:j + 1, :] * pad_ref[r * R + off + j:r * R + off + j + R, :]
            y_ref[r * R:(r + 1) * R, :] = acc

    return _ordered_call(
        body, [u, w_dw, b_dw], name="conv_dw", grid=(C // tc,),
        in_specs=[pl.BlockSpec((T, tc), lambda c: (0, c)), pl.BlockSpec((32, tc), lambda c: (0, c)),
                  pl.BlockSpec((1, tc), lambda c: (0, c))],
        out_specs=pl.BlockSpec((T, tc), lambda c: (0, c)),
        out_shape=_sds((T, C), F32),
        scratch_shapes=[pltpu.VMEM((T + CONV_PAD, tc), F32)],
        compiler_params=_params(),
    )


def _dwconv_bwd(u, w_dw, dy):
    T, C = u.shape
    tc, R = _tile(C, 256), _conv_rows(T)
    off = CONV_PAD - (CONV_WIDTH - 1)

    def body(u_ref, w_ref, dy_ref, du_ref, dw_ref, db_ref, upad_ref, dpad_ref):
        upad_ref[0:CONV_PAD, :] = jnp.zeros((CONV_PAD, tc), F32)
        upad_ref[CONV_PAD:CONV_PAD + T, :] = u_ref[...]
        dpad_ref[0:T, :] = dy_ref[...]
        dpad_ref[T:T + CONV_PAD, :] = jnp.zeros((CONV_PAD, tc), F32)
        for r in range(T // R):
            acc = jnp.zeros((R, tc), F32)
            for j in range(CONV_WIDTH):
                s = r * R + (CONV_WIDTH - 1) - j
                acc = acc + w_ref[j:j + 1, :] * dpad_ref[s:s + R, :]
            du_ref[r * R:(r + 1) * R, :] = acc
        for j in range(CONV_WIDTH):
            acc = jnp.zeros((R, tc), F32)
            for r in range(T // R):
                acc = acc + dy_ref[r * R:(r + 1) * R, :] * upad_ref[r * R + off + j:r * R + off + j + R, :]
            dw_ref[j:j + 1, :] = jnp.sum(acc, axis=0, keepdims=True)
        dw_ref[CONV_WIDTH:32, :] = jnp.zeros((32 - CONV_WIDTH, tc), F32)
        db_ref[...] = jnp.sum(dy_ref[...], axis=0, keepdims=True)

    return _ordered_call(
        body, [u, w_dw, dy], name="conv_ddw", grid=(C // tc,),
        in_specs=[pl.BlockSpec((T, tc), lambda c: (0, c)), pl.BlockSpec((32, tc), lambda c: (0, c)),
                  pl.BlockSpec((T, tc), lambda c: (0, c))],
        out_specs=[pl.BlockSpec((T, tc), lambda c: (0, c)), pl.BlockSpec((32, tc), lambda c: (0, c)),
                   pl.BlockSpec((1, tc), lambda c: (0, c))],
        out_shape=[_sds((T, C), F32), _sds((32, C), F32), _sds((1, C), F32)],
        scratch_shapes=[pltpu.VMEM((T + CONV_PAD, tc), F32), pltpu.VMEM((T + CONV_PAD, tc), F32)],
        compiler_params=_params(),
    )


def _ln_silu(y, g, b):
    mu = jnp.mean(y, axis=-1, keepdims=True)
    yc = y - mu
    z = yc * lax.rsqrt(jnp.mean(yc * yc, axis=-1, keepdims=True) + EPS) * g + b
    return z * jax.nn.sigmoid(z)


def _conv_fwd(h, gain, get):
    T, D = h.shape
    hn = _norm_fwd("conv_norm", h, gain)
    w = dict(pw1=get["pw1"](hn))
    b_pw1, w_dw, b_dw, ln_g, ln_b = get["small"](hn)
    S, _, Ns = w["pw1"].shape
    half = S // 2
    C = half * Ns
    tm, tk = _tile(T, 1024, 16), _tile(D, 512)

    def ep(accs, ex, outs):
        a = accs[0] + ex[0][...]
        g = accs[1] + ex[1][...]
        outs[0][0] = a.astype(BF16)
        outs[0][1] = g.astype(BF16)
        outs[1][...] = a * jax.nn.sigmoid(g)

    ag, u = _matmul(
        "conv_pw1", (half, T // tm, D // tk),
        [(hn, pl.BlockSpec((tm, tk), lambda j, i, k: (i, k))),
         (w["pw1"], pl.BlockSpec((None, tk, Ns), lambda j, i, k: (j, k, 0))),
         (w["pw1"], pl.BlockSpec((None, tk, Ns), lambda j, i, k: (j + half, k, 0)))],
        [(0, 0, 1), (1, 0, 2)], NN, [(tm, Ns), (tm, Ns)],
        [(_sds((2, T, C), BF16), pl.BlockSpec((2, tm, Ns), lambda j, i, k: (0, i, j))),
         (_sds((T, C), F32), pl.BlockSpec((tm, Ns), lambda j, i, k: (i, j)))],
        ep,
        extras=[(b_pw1, pl.BlockSpec((None, 1, Ns), lambda j, i, k: (j, 0, 0))),
                (b_pw1, pl.BlockSpec((None, 1, Ns), lambda j, i, k: (j + half, 0, 0)))])
    y = _dwconv_fwd(u, w_dw, b_dw)

    def fn(ins, outs, acc):
        outs[0][...] = _ln_silu(ins[0][...], ins[1][...], ins[2][...]).astype(BF16)

    s = _rowwise("conv_ln", fn, T, 256, [(y, 0)], [ln_g, ln_b], [((T, C), BF16, 0)])[0]

    def ep_out(accs, ex, outs):
        outs[0][...] = ex[0][...] + accs[0]

    tn = _tile(D, 1024)
    w["pw2"] = get["pw2"](s)
    h_new = _mm_nn("conv_pw2", s, w["pw2"], F32, epilogue=ep_out,
                   extras=[(h, pl.BlockSpec((tm, tn), lambda i, j, k: (i, j)))])[0]
    return h_new, (h, hn, ag, u, y, s)


def _conv_bwd(saved, gain, w, w_dw, ln_g, ln_b, dh, dhb):
    h, hn, ag, u, y, s = saved
    T, D = h.shape
    S, _, Ns = w["pw1"].shape
    half = S // 2
    C = half * Ns
    ds = _mm_nt("conv_ds", dhb, w["pw2"], F32)[0]
    dw_pw2 = _mm_tn("conv_dwpw2", s, dhb, BF16)

    def fn(ins, outs, acc):
        _, vjp = jax.vjp(_ln_silu, ins[0][...], ins[2][...], ins[3][...])
        dy, dg, db = vjp(ins[1][...])
        outs[0][...] = dy
        acc(0, dg)
        acc(1, db)

    dy, d_ln_g, d_ln_b = _rowwise("conv_dln", fn, T, 256, [(y, 0), (ds, 0)], [ln_g, ln_b],
                                  [((T, C), F32, 0)], [(1, C), (1, C)])
    du, d_w_dw, d_b_dw = _dwconv_bwd(u, w_dw, dy)

    def fn2(ins, outs, acc):
        a = ins[0][0].astype(F32)
        g = ins[0][1].astype(F32)
        du_ = ins[1][...]
        sg = jax.nn.sigmoid(g)
        da = du_ * sg
        dg = du_ * a * sg * (1.0 - sg)
        outs[0][0] = da.astype(BF16)
        outs[0][1] = dg.astype(BF16)
        acc(0, jnp.sum(da, axis=0, keepdims=True))
        acc(1, jnp.sum(dg, axis=0, keepdims=True))

    dag, d_b_a, d_b_g = _rowwise("conv_dglu", fn2, T, 256, [(ag, 1), (du, 0)], [],
                                 [((2, T, C), BF16, 1)], [(1, C), (1, C)])
    tm, tn = _tile(T, 1024, 16), _tile(D, 1024)
    dhn = _matmul(
        "conv_dhn", (T // tm, D // tn, half),
        [(dag, pl.BlockSpec((None, tm, Ns), lambda i, n, j: (0, i, j))),
         (dag, pl.BlockSpec((None, tm, Ns), lambda i, n, j: (1, i, j))),
         (w["pw1"], pl.BlockSpec((None, tn, Ns), lambda i, n, j: (j, n, 0))),
         (w["pw1"], pl.BlockSpec((None, tn, Ns), lambda i, n, j: (j + half, n, 0)))],
        [(0, 0, 2), (0, 1, 3)], NT, [(tm, tn)],
        [(_sds((T, D), F32), pl.BlockSpec((tm, tn), lambda i, n, j: (i, n)))], _store())[0]
    tkd, tt = _tile(D, 1024), _tile(T, 1024)
    dw_pw1 = _matmul(
        "conv_dwpw1", (S, D // tkd, T // tt),
        [(hn, pl.BlockSpec((tt, tkd), lambda j, i, k: (k, i))),
         (dag, pl.BlockSpec((None, tt, Ns), lambda j, i, k: (j // half, k, j % half)))],
        [(0, 0, 1)], TN, [(tkd, Ns)],
        [(_sds((S, D, Ns), BF16), pl.BlockSpec((None, tkd, Ns), lambda j, i, k: (j, i, 0)))], _store())[0]
    dh_in, dh_in_b, dgain = _norm_bwd("conv_dnorm", h, gain, dhn, dh)
    d_b_pw1 = jnp.concatenate([d_b_a, d_b_g], axis=1)
    return dh_in, dh_in_b, dgain, dw_pw1, d_b_pw1, d_w_dw, d_b_dw, d_ln_g, d_ln_b, dw_pw2


def _loss_head(y, target):
    T, D = y.shape

    def fn(ins, outs, acc):
        e = ins[0][...] - ins[1][...]
        d = e * (1.0 / D)
        outs[0][...] = d
        outs[1][...] = d.astype(BF16)
        part = jnp.sum(jnp.sum(e * e, axis=-1, keepdims=True), axis=0, keepdims=True) * (0.5 / D)
        acc(0, jnp.broadcast_to(part, (1, 128)))

    return _rowwise("loss_head", fn, T, 256, [(y, 0), (target, 0)], [], [((T, D), F32, 0), ((T, D), BF16, 0)],
                    [(1, 128)])


def _place():
    return lax.axis_index("x"), lax.axis_index("y"), lax.axis_index("c")


def _peer(j):
    x, y, c = _place()
    return (1 - x if j & 4 else x, 1 - y if j & 2 else y, 1 - c if j & 1 else c)


def _index(place):
    return 4 * place[0] + 2 * place[1] + place[2]


HBM = pl.BlockSpec(memory_space=pltpu.HBM)
SEM = pl.BlockSpec(memory_space=pltpu.SEMAPHORE)
EFFECT = pltpu.SideEffectType.DATAFLOW_SIDE_EFFECTING


def _chip(j):
    x, y, _ = _place()
    return (1 - x if j & 2 else x, 1 - y if j & 1 else y)


def _chip_index(chip):
    return 2 * chip[0] + chip[1]


def _remote(src, dst, send, recv, k, device):
    return pltpu.make_async_remote_copy(src_ref=src, dst_ref=dst, send_sem=send.at[k], recv_sem=recv.at[k],
                                        device_id=device, device_id_type=MESH)


def _hbm(arrays):
    return [pltpu.with_memory_space_constraint(a, pltpu.HBM) for a in arrays]


def _split_call(name, body, ins, sems_in, sems_out, after=None, token=True):
    n, ns_in, ns_out = len(ins), len(sems_in), len(sems_out)

    def kernel_body(*refs):
        in_refs = refs[:n]
        si = refs[n:n + ns_in]
        so = refs[n + ns_in:n + ns_in + ns_out]
        tok = refs[-1]
        body(in_refs, si, so, tok)
        tok[...] = jnp.zeros_like(tok)

    res = _ordered_call(
        kernel_body, _hbm(ins) + list(sems_in), chain_out=ns_out + n, name=name,
        in_specs=[HBM] * n + [SEM] * ns_in,
        out_specs=[SEM] * ns_out + [HBM] * n + [pl.BlockSpec(memory_space=pltpu.VMEM)],
        out_shape=[pltpu.SemaphoreType.DMA((s,)) for s in sems_out] + [pltpu.HBM(a.shape, a.dtype) for a in ins]
        + [_sds((8, 128), F32)],
        input_output_aliases={i: ns_out + i for i in range(n)},
        compiler_params=pltpu.CompilerParams(has_side_effects=EFFECT),
    )
    sems = list(res[:ns_out])
    arrays = list(res[ns_out:ns_out + n])
    return sems, arrays, res[-1]


def _ag_start(name, groups, after=None):
    flat = [s for grp in groups for s in grp]
    zones = [lax.empty((N_DEV,) + s.shape, s.dtype) for s in flat]
    n = len(flat)
    sizes = []
    for grp in groups:
        sizes += [7 * len(grp), 7 * len(grp), len(grp)]

    def body(ins, si, so, tok):
        x, y, c = _place()
        me = _index((x, y, c))
        base = 0
        for gi, grp in enumerate(groups):
            send, recv, local = so[3 * gi:3 * gi + 3]
            for w in range(len(grp)):
                src, dst = ins[base + w], ins[n + base + w].at[me]
                pltpu.make_async_copy(src, dst, local.at[w]).start()
                _remote(src, dst, send, recv, 7 * w, (x, y, 1 - c)).start()
                for j in (1, 2, 3):
                    _remote(src, dst, send, recv, 7 * w + j, (*_chip(j), c)).start()
            base += len(grp)

    sems, arrays, token = _split_call(name, body, flat + zones, [], sizes, after=after, token=True)
    out, base = [], 0
    for gi, grp in enumerate(groups):
        k = len(grp)
        out.append((sems[3 * gi:3 * gi + 3], arrays[base:base + k], arrays[n + base:n + base + k]))
        base += k
    return out, token


def _ag_forward(name, handle, after):
    sems, shards, zones = handle
    k = len(shards)

    def arrive(ins, si, so, tok):
        send, recv, _ = si
        _, _, c = _place()
        for j in (1, 2, 3):
            for w in range(k):
                blk = ins[k + w].at[_index((*_chip(j), c))]
                _remote(ins[w], blk, send, recv, 7 * w + j, (*_chip(j), c)).wait_recv()

    _, arrays, _ = _split_call(name + "_arrive", arrive, list(shards) + list(zones), sems, [], after=after)

    def pass_on(ins, si, so, tok):
        fsend, frecv = so
        x, y, c = _place()
        for j in (1, 2, 3):
            for w in range(k):
                blk = ins[w].at[_index((*_chip(j), c))]
                _remote(blk, blk, fsend, frecv, 3 * w + j - 1, (x, y, 1 - c)).start()

    fsems, zones2, _ = _split_call(name + "_pass", pass_on, arrays[k:], [], [3 * k, 3 * k])
    return (list(sems) + fsems, arrays[:k], zones2)


def _ag_wait(name, handle, after):
    sems, shards, zones = handle
    k = len(shards)

    def body(ins, si, so, tok):
        send, recv, local, fsend, frecv = si
        x, y, c = _place()
        sib = (x, y, 1 - c)
        for w in range(k):
            zone = ins[k + w]
            _remote(ins[w], zone.at[_index(sib)], send, recv, 7 * w, sib).wait_recv()
            for j in (1, 2, 3):
                blk = zone.at[_index((*_chip(j), 1 - c))]
                _remote(blk, blk, fsend, frecv, 3 * w + j - 1, sib).wait_recv()
        for w in range(k):
            zone = ins[k + w]
            mine = zone.at[_index((x, y, c))]
            _remote(ins[w], mine, send, recv, 7 * w, sib).wait_send()
            for j in (1, 2, 3):
                _remote(ins[w], mine, send, recv, 7 * w + j, (*_chip(j), c)).wait_send()
                blk = zone.at[_index((*_chip(j), c))]
                _remote(blk, blk, fsend, frecv, 3 * w + j - 1, sib).wait_send()
            pltpu.make_async_copy(ins[w], mine, local.at[w]).wait()

    _, arrays, _ = _split_call(name, body, list(shards) + list(zones), sems, [], after=after)
    return arrays[k:]


def _rs_pair_start(name, grads, after=None):
    n = len(grads)
    zones = [lax.empty((4,) + g.shape[1:], g.dtype) for g in grads]

    def body(ins, si, so, tok):
        send, recv = so
        x, y, c = _place()
        for w in range(n):
            for q in range(4):
                _remote(ins[w].at[2 * q + 1 - c], ins[n + w].at[q], send, recv, 4 * w + q, (x, y, 1 - c)).start()

    sems, arrays, token = _split_call(name, body, list(grads) + zones, [], [4 * n, 4 * n], after=after, token=True)
    return (sems, arrays[:n], arrays[n:]), token


def _rs_pair_wait(name, handle, after):
    sems, grads, zones = handle
    n = len(grads)

    def body(ins, si, so, tok):
        send, recv = si
        x, y, c = _place()
        for w in range(n):
            for q in range(4):
                cp = _remote(ins[w].at[2 * q + 1 - c], ins[n + w].at[q], send, recv, 4 * w + q, (x, y, 1 - c))
                cp.wait_recv()
                cp.wait_send()

    _, arrays, _ = _split_call(name, body, list(grads) + list(zones), sems, [], after=after)
    return arrays[:n], arrays[n:]


def _pair_sum(name, g, got, core):
    _, R, C = g.shape
    g4 = g.reshape(4, 2, R, C)
    tr = _tile(R, 256, 16)

    def body(c_ref, g_ref, a_ref, *rest):
        o_ref = rest[-1]
        o_ref[...] = (g_ref[...].astype(F32) + a_ref[...].astype(F32)).astype(o_ref.dtype)

    prev = [] if _LAST[0] is None else [_LAST[0]]
    out = pl.pallas_call(
        body, name=name,
        grid_spec=pltpu.PrefetchScalarGridSpec(
            num_scalar_prefetch=1, grid=(4, R // tr),
            in_specs=[pl.BlockSpec((None, None, tr, C), lambda q, i, c_ref: (q, c_ref[0], i, 0)),
                      pl.BlockSpec((None, tr, C), lambda q, i, c_ref: (q, i, 0))] + [ANY] * len(prev),
            out_specs=pl.BlockSpec((None, tr, C), lambda q, i, c_ref: (q, i, 0))),
        out_shape=_sds((4, R, C), g.dtype),
        compiler_params=_params(),
    )(core, g4, got, *prev)
    _LAST[0] = out
    return out


def _rs_chip_start(name, sums, after=None):
    n = len(sums)
    zones = [lax.empty(s.shape, s.dtype) for s in sums]

    def body(ins, si, so, tok):
        send, recv, local = so
        x, y, c = _place()
        mine = _chip_index((x, y))
        for w in range(n):
            pltpu.make_async_copy(ins[w].at[mine], ins[n + w].at[mine], local.at[w]).start()
            for j in (1, 2, 3):
                _remote(ins[w].at[_chip_index(_chip(j))], ins[n + w].at[mine], send, recv, 3 * w + j - 1,
                        (*_chip(j), c)).start()

    sems, arrays, token = _split_call(name, body, list(sums) + zones, [], [3 * n, 3 * n, n], after=after, token=True)
    return (sems, arrays[:n], arrays[n:]), token


def _rs_chip_wait(name, handle, after):
    sems, sums, zones = handle
    n = len(sums)

    def body(ins, si, so, tok):
        send, recv, local = si
        x, y, c = _place()
        mine = _chip_index((x, y))
        for w in range(n):
            for j in (1, 2, 3):
                _remote(ins[w].at[mine], ins[n + w].at[_chip_index(_chip(j))], send, recv, 3 * w + j - 1,
                        (*_chip(j), c)).wait_recv()
        for w in range(n):
            for j in (1, 2, 3):
                _remote(ins[w].at[_chip_index(_chip(j))], ins[n + w].at[mine], send, recv, 3 * w + j - 1,
                        (*_chip(j), c)).wait_send()
            pltpu.make_async_copy(ins[w].at[mine], ins[n + w].at[mine], local.at[w]).wait()

    _, arrays, _ = _split_call(name, body, list(sums) + list(zones), sems, [], after=after)
    return arrays[n:]


def _chip_sum(name, slots, layer, layers, into=None):
    _, R, C = slots.shape
    tr = _tile(R, 256, 16)

    def body(*refs):
        s_ref, o_ref = refs[0], refs[-1]
        total = s_ref[0].astype(F32)
        for k in range(1, 4):
            total = total + s_ref[k].astype(F32)
        o_ref[...] = total

    extra = [] if into is None else [into]
    return _ordered_call(
        body, [slots] + extra, name=name, grid=(R // tr,),
        in_specs=[pl.BlockSpec((4, tr, C), lambda i: (0, i, 0))] + [ANY] * len(extra),
        out_specs=pl.BlockSpec((None, tr, C), lambda i: (layer, i, 0)),
        out_shape=_sds((layers, R, C), F32),
        input_output_aliases={} if into is None else {1: 0},
        compiler_params=_params(),
    )


def _all_reduce_small(pack):
    R, C = pack.shape

    def body(x_ref, o_ref, all_ref, send, recv):
        me = _index(_place())
        all_ref[me] = x_ref[...]
        sends = []
        for j in range(1, N_DEV):
            cp = pltpu.make_async_remote_copy(src_ref=x_ref, dst_ref=all_ref.at[me], send_sem=send.at[j - 1],
                                              recv_sem=recv.at[j - 1], device_id=_peer(j), device_id_type=MESH)
            cp.start()
            sends.append(cp)
        for j in range(1, N_DEV):
            pltpu.make_async_remote_copy(src_ref=x_ref, dst_ref=all_ref.at[_index(_peer(j))], send_sem=send.at[j - 1],
                                         recv_sem=recv.at[j - 1], device_id=_peer(j), device_id_type=MESH).wait_recv()
        for cp in sends:
            cp.wait_send()
        total = all_ref[0]
        for k in range(1, N_DEV):
            total = total + all_ref[k]
        o_ref[...] = total

    return _ordered_call(
        body, [pack], name="all_reduce_small",
        in_specs=[pl.BlockSpec(memory_space=pltpu.VMEM)], out_specs=pl.BlockSpec(memory_space=pltpu.VMEM),
        out_shape=_sds((R, C), F32),
        scratch_shapes=[pltpu.VMEM((N_DEV, R, C), F32), pltpu.SemaphoreType.DMA((N_DEV - 1,)),
                        pltpu.SemaphoreType.DMA((N_DEV - 1,))],
        compiler_params=_params(),
    )


def _cast_layer(name, w, layer):
    _, R, C = w.shape
    tr = _tile(R, 256, 16)

    def body(w_ref, o_ref):
        o_ref[...] = w_ref[...].astype(BF16)

    return _ordered_call(
        body, [w], name=name, grid=(R // tr,),
        in_specs=[pl.BlockSpec((None, tr, C), lambda i: (layer, i, 0))],
        out_specs=pl.BlockSpec((tr, C), lambda i: (i, 0)),
        out_shape=_sds((R, C), BF16),
    )


def _adam_math(w, g, m, v):
    c1 = 1.0 / (1.0 - ADAM_B1 ** ADAM_STEP)
    c2 = 1.0 / (1.0 - ADAM_B2 ** ADAM_STEP)
    nm = ADAM_B1 * m + (1.0 - ADAM_B1) * g
    nv = ADAM_B2 * v + (1.0 - ADAM_B2) * (g * g)
    return -ADAM_LR * ((nm * c1) / (jnp.sqrt(nv * c2) + ADAM_EPS) + ADAM_WD * w), nm, nv


def _sum_adam(name, slots, w, m, v, layer, into=None):
    L, R, C = w.shape
    tr = _tile(R, 128, 16)

    def body(*refs):
        s_ref, w_ref, m_ref, v_ref = refs[:4]
        g_ref, d_ref, nm_ref, nv_ref = refs[-4:]
        g = s_ref[0].astype(F32)
        for k in range(1, 4):
            g = g + s_ref[k].astype(F32)
        g_ref[...] = g
        d_ref[...], nm_ref[...], nv_ref[...] = _adam_math(w_ref[...], g, m_ref[...], v_ref[...])

    spec = pl.BlockSpec((None, tr, C), lambda i: (layer, i, 0))
    extra = [] if into is None else list(into)
    return _ordered_call(
        body, [slots, w, m, v] + extra, name=name, grid=(R // tr,),
        in_specs=[pl.BlockSpec((4, tr, C), lambda i: (0, i, 0)), spec, spec, spec] + [ANY] * len(extra),
        out_specs=[spec] * 4,
        out_shape=[_sds((L, R, C), F32)] * 4,
        input_output_aliases={4 + k: k for k in range(len(extra))},
        compiler_params=_params(),
    )


def _adamw(name, w, g, m, v):
    shape = w.shape
    R, C = shape[-2], shape[-1]
    L = 1
    for s in shape[:-2]:
        L *= s
    w3, g3, m3, v3 = (a.reshape(L, R, C) for a in (w, g, m, v))
    tr = _tile(R, 128, 8)

    def body(w_ref, g_ref, m_ref, v_ref, d_ref, nm_ref, nv_ref):
        d_ref[...], nm_ref[...], nv_ref[...] = _adam_math(w_ref[...], g_ref[...], m_ref[...], v_ref[...])

    spec = pl.BlockSpec((None, tr, C), lambda l, i: (l, i, 0))
    outs = _ordered_call(
        body, [w3, g3, m3, v3], name=name, grid=(L, R // tr),
        in_specs=[spec] * 4, out_specs=[spec] * 3,
        out_shape=[_sds((L, R, C), F32)] * 3,
        compiler_params=_params(),
    )
    return tuple(o.reshape(shape) for o in outs)


def _pad_rows(a, rows):
    return jnp.pad(a, ((0, rows - a.shape[0]), (0, 0)))


def _pad_cols(a, cols):
    return jnp.pad(a, ((0, 0), (0, cols - a.shape[1])))


def _rope_tables(positions):
    q = D_ROPE // 2
    inv_freq = ROPE_THETA ** (-jnp.arange(0, D_ROPE, 2, dtype=F32) / D_ROPE)
    ang = positions.astype(F32)[:, None] * inv_freq
    cos, sin = jnp.cos(ang), jnp.sin(ang)
    z = jnp.zeros_like(cos)
    zz = jnp.zeros((cos.shape[0], 128 - 2 * q), F32)
    c = jnp.concatenate([cos, cos, zz], axis=1)
    s1 = jnp.concatenate([z, sin, zz], axis=1)
    s2 = jnp.concatenate([-sin, z, zz], axis=1)
    return c, s1, s2


def kernel(x, p, positions, ffn_a_norm, ffn_a_w_in, ffn_a_w_out, ffn_b_norm, ffn_b_w_in, ffn_b_w_out, mix_norm, mla_w_in, mla_q_lat_norm, mla_kv_lat_norm, mla_w_uq, mla_w_ukv, mla_q_gain, mla_k_gain, mla_w_o, conv_w_pw1, conv_b_pw1, conv_w_dw, conv_b_dw, conv_ln_g, conv_ln_b, conv_w_pw2, ple_w_proj, ple_norm, ple_gate_norm, ple_w_gate, loss_target, m_ffn_a_norm, m_ffn_a_w_in, m_ffn_a_w_out, m_ffn_b_norm, m_ffn_b_w_in, m_ffn_b_w_out, m_mix_norm, m_mla_w_in, m_mla_q_lat_norm, m_mla_kv_lat_norm, m_mla_w_uq, m_mla_w_ukv, m_mla_q_gain, m_mla_k_gain, m_mla_w_o, m_conv_w_pw1, m_conv_b_pw1, m_conv_w_dw, m_conv_b_dw, m_conv_ln_g, m_conv_ln_b, m_conv_w_pw2, m_ple_w_proj, m_ple_norm, m_ple_gate_norm, m_ple_w_gate, v_ffn_a_norm, v_ffn_a_w_in, v_ffn_a_w_out, v_ffn_b_norm, v_ffn_b_w_in, v_ffn_b_w_out, v_mix_norm, v_mla_w_in, v_mla_q_lat_norm, v_mla_kv_lat_norm, v_mla_w_uq, v_mla_w_ukv, v_mla_q_gain, v_mla_k_gain, v_mla_w_o, v_conv_w_pw1, v_conv_b_pw1, v_conv_w_dw, v_conv_b_dw, v_conv_ln_g, v_conv_ln_b, v_conv_w_pw2, v_ple_w_proj, v_ple_norm, v_ple_gate_norm, v_ple_w_gate):
    weights = dict(ffn_a_norm=ffn_a_norm, ffn_a_w_in=ffn_a_w_in, ffn_a_w_out=ffn_a_w_out, ffn_b_norm=ffn_b_norm,
                   ffn_b_w_in=ffn_b_w_in, ffn_b_w_out=ffn_b_w_out, mix_norm=mix_norm, mla_w_in=mla_w_in,
                   mla_q_lat_norm=mla_q_lat_norm, mla_kv_lat_norm=mla_kv_lat_norm, mla_w_uq=mla_w_uq,
                   mla_w_ukv=mla_w_ukv, mla_q_gain=mla_q_gain, mla_k_gain=mla_k_gain, mla_w_o=mla_w_o,
                   conv_w_pw1=conv_w_pw1, conv_b_pw1=conv_b_pw1, conv_w_dw=conv_w_dw, conv_b_dw=conv_b_dw,
                   conv_ln_g=conv_ln_g, conv_ln_b=conv_ln_b, conv_w_pw2=conv_w_pw2, ple_w_proj=ple_w_proj,
                   ple_norm=ple_norm, ple_gate_norm=ple_gate_norm, ple_w_gate=ple_w_gate)
    moments_m = dict(ffn_a_norm=m_ffn_a_norm, ffn_a_w_in=m_ffn_a_w_in, ffn_a_w_out=m_ffn_a_w_out,
                     ffn_b_norm=m_ffn_b_norm, ffn_b_w_in=m_ffn_b_w_in, ffn_b_w_out=m_ffn_b_w_out,
                     mix_norm=m_mix_norm, mla_w_in=m_mla_w_in, mla_q_lat_norm=m_mla_q_lat_norm,
                     mla_kv_lat_norm=m_mla_kv_lat_norm, mla_w_uq=m_mla_w_uq, mla_w_ukv=m_mla_w_ukv,
                     mla_q_gain=m_mla_q_gain, mla_k_gain=m_mla_k_gain, mla_w_o=m_mla_w_o,
                     conv_w_pw1=m_conv_w_pw1, conv_b_pw1=m_conv_b_pw1, conv_w_dw=m_conv_w_dw,
                     conv_b_dw=m_conv_b_dw, conv_ln_g=m_conv_ln_g, conv_ln_b=m_conv_ln_b, conv_w_pw2=m_conv_w_pw2,
                     ple_w_proj=m_ple_w_proj, ple_norm=m_ple_norm, ple_gate_norm=m_ple_gate_norm,
                     ple_w_gate=m_ple_w_gate)
    moments_v = dict(ffn_a_norm=v_ffn_a_norm, ffn_a_w_in=v_ffn_a_w_in, ffn_a_w_out=v_ffn_a_w_out,
                     ffn_b_norm=v_ffn_b_norm, ffn_b_w_in=v_ffn_b_w_in, ffn_b_w_out=v_ffn_b_w_out,
                     mix_norm=v_mix_norm, mla_w_in=v_mla_w_in, mla_q_lat_norm=v_mla_q_lat_norm,
                     mla_kv_lat_norm=v_mla_kv_lat_norm, mla_w_uq=v_mla_w_uq, mla_w_ukv=v_mla_w_ukv,
                     mla_q_gain=v_mla_q_gain, mla_k_gain=v_mla_k_gain, mla_w_o=v_mla_w_o,
                     conv_w_pw1=v_conv_w_pw1, conv_b_pw1=v_conv_b_pw1, conv_w_dw=v_conv_w_dw,
                     conv_b_dw=v_conv_b_dw, conv_ln_g=v_conv_ln_g, conv_ln_b=v_conv_ln_b, conv_w_pw2=v_conv_w_pw2,
                     ple_w_proj=v_ple_w_proj, ple_norm=v_ple_norm, ple_gate_norm=v_ple_gate_norm,
                     ple_w_gate=v_ple_w_gate)
    order = list(weights.keys())
    _LAST[0] = None

    T, D = x.shape[1], x.shape[2]
    me = _index(_place())
    h0 = x[0]
    target = loss_target[0]
    tabs = _rope_tables(positions[0])
    H = N_HEADS
    hps = H // N_DEV
    QL = mla_q_lat_norm.shape[1]
    Cs = conv_b_dw.shape[1]

    def cast(n, i):
        return _cast_layer(f"cast_{n}{i}", weights[n], i)

    first, tok = _ag_start("ag_start0", [[cast("ffn_a_w_in", 0)]])
    m_in_pad = _pad_cols(mla_w_in[0], mla_w_in.shape[2] - D_ROPE + 128)[None]
    uq_pad = jnp.pad(mla_w_uq[0].reshape(QL, hps, QK_DIM), ((0, 0), (0, 0), (0, HEAD_PAD - QK_DIM)))
    uq_pad = uq_pad.reshape(1, QL, hps * HEAD_PAD)
    conv_small = jnp.concatenate([
        _pad_rows(_pad_cols(conv_b_pw1, 2 * Cs), 8),
        _pad_rows(_pad_cols(conv_w_dw[0], 2 * Cs), 32),
        _pad_rows(_pad_cols(jnp.concatenate([conv_b_dw, conv_ln_g, conv_ln_b], axis=0), 2 * Cs), 8)], axis=0)
    rest, tok = _ag_start("ag_start1", [
        [cast("ffn_a_w_out", 0), _cast_layer("cast_mla_in", m_in_pad, 0), _cast_layer("cast_mla_uq", uq_pad, 0),
         _cast_layer("cast_mla_ukv", mla_w_ukv, 0), _cast_layer("cast_mla_wo", mla_w_o, 0)],
        [cast("ffn_b_w_in", 0), cast("ffn_b_w_out", 0), cast("ple_w_gate", 0), cast("ple_w_proj", 0)],
        [cast("ffn_a_w_in", 1), cast("ffn_a_w_out", 1)],
        [_cast_layer("cast_conv_pw1", conv_w_pw1, 0), _cast_layer("cast_conv_pw2", conv_w_pw2, 0), conv_small],
        [cast("ffn_b_w_in", 1), cast("ffn_b_w_out", 1), cast("ple_w_gate", 1), cast("ple_w_proj", 1)]], after=tok)
    groups = [dict(handle=hd, stage=0, arrays=None) for hd in first + rest]

    def prefetch(gi, after):
        st = groups[gi]
        if st["stage"] == 0:
            st["handle"] = _ag_forward(f"ag{gi}_forward", st["handle"], after)
            st["stage"] = 1

    def fetch(gi, after):
        prefetch(gi, after)
        st = groups[gi]
        if st["stage"] == 1:
            st["arrays"] = _ag_wait(f"ag{gi}_wait", st["handle"], after)
            st["stage"] = 2
        return st["arrays"]

    def getter(gi, k, shape=None, ahead=None):
        def get(after):
            if ahead is not None:
                prefetch(ahead, after)
            a = fetch(gi, after)[k]
            return a if shape is None else a.reshape(shape)
        return get

    def conv_small_params(after):
        small = fetch(4, after)[2]
        return (small[:, 0:1, :],
                jnp.transpose(small[:, 8:40, :Cs], (1, 0, 2)).reshape(32, N_DEV * Cs),
                small[:, 40, :Cs].reshape(1, N_DEV * Cs), small[:, 41, :Cs].reshape(1, N_DEV * Cs),
                small[:, 42, :Cs].reshape(1, N_DEV * Cs))

    rows = (-1, D)
    get_ffn = [dict(a_in=getter(0, 0), a_out=getter(1, 0, rows), b_in=getter(2, 0), b_out=getter(2, 1, rows)),
               dict(a_in=getter(3, 0), a_out=getter(3, 1, rows, ahead=4), b_in=getter(5, 0),
                    b_out=getter(5, 1, rows))]
    get_ple = [dict(proj=getter(2, 3, ahead=3), gate=getter(2, 2, rows)),
               dict(proj=getter(5, 3), gate=getter(5, 2, rows))]
    get_mla = dict(m_in=getter(1, 1, (D, -1)), uq=getter(1, 2), ukv=getter(1, 3), wo=getter(1, 4, rows, ahead=2))
    get_conv = dict(pw1=getter(4, 0), pw2=getter(4, 1, rows, ahead=5), small=conv_small_params)
    gq_pad = _pad_cols(mla_q_gain, HEAD_PAD)
    gk_pad = _pad_cols(mla_k_gain, HEAD_PAD)
    prefetch(0, tok)

    saved = []
    h = h0
    for i in range(2):
        h, s_a = _ffn_fwd(f"ffn_a{i}", h, ffn_a_norm[i:i + 1], get_ffn[i]["a_in"], get_ffn[i]["a_out"])
        if i == 0:
            h, s_m = _mla_fwd(h, mix_norm[0:1], tabs, get_mla, mla_q_lat_norm, mla_kv_lat_norm, gq_pad, gk_pad)
        else:
            h, s_m = _conv_fwd(h, mix_norm[1:2], get_conv)
        h, s_b = _ffn_fwd(f"ffn_b{i}", h, ffn_b_norm[i:i + 1], get_ffn[i]["b_in"], get_ffn[i]["b_out"])
        h, s_p = _ple_fwd(f"ple{i}", h, p[i, 0], get_ple[i]["proj"], ple_norm[i:i + 1], ple_gate_norm[i:i + 1],
                          get_ple[i]["gate"])
        saved.append((s_a, s_m, s_b, s_p))
    W = [dict(a_in=get_ffn[i]["a_in"](None), a_out=get_ffn[i]["a_out"](None), b_in=get_ffn[i]["b_in"](None),
              b_out=get_ffn[i]["b_out"](None), proj=get_ple[i]["proj"](None), gate=get_ple[i]["gate"](None))
         for i in range(2)]
    Wm = {n: g(None) for n, g in get_mla.items()}
    Wc = dict(pw1=get_conv["pw1"](None), pw2=get_conv["pw2"](None))
    _, w_dw_full, _, ln_g_full, ln_b_full = conv_small_params(None)

    dh, dhb, loss_row = _loss_head(h, target)

    G = {}
    small_g = {}
    stacked = ["ffn_a_w_in", "ffn_a_w_out", "ffn_b_w_in", "ffn_b_w_out", "ple_w_proj", "ple_w_gate"]
    row_sharded = {"ffn_a_w_out", "ffn_b_w_out", "ple_w_gate", "mla_w_in", "mla_w_o", "conv_w_pw2"}
    core = lax.axis_index("c").astype(jnp.int32).reshape(1)
    rs_groups = {
        "pb1": [("ple_w_proj", 1), ("ple_w_gate", 1), ("ffn_b_w_in", 1), ("ffn_b_w_out", 1)],
        "c1": [("conv_w_pw1", 0), ("conv_w_pw2", 0)],
        "a1": [("ffn_a_w_in", 1), ("ffn_a_w_out", 1)],
        "pb0": [("ple_w_proj", 0), ("ple_w_gate", 0), ("ffn_b_w_in", 0), ("ffn_b_w_out", 0)],
        "m0": [("mla_w_in", 0), ("mla_w_uq", 0), ("mla_w_ukv", 0), ("mla_w_o", 0)],
        "ao0": [("ffn_a_w_out", 0)],
        "ai0": [("ffn_a_w_in", 0)]}
    rs = {}

    def rs_begin(tag):
        grads = []
        for n, i in rs_groups[tag]:
            g = G[(n, i)]
            grads.append(g.reshape(N_DEV, g.shape[0] // N_DEV, g.shape[1]) if n in row_sharded else g)
        rs[tag] = _rs_pair_start(f"rs_{tag}_pair_start", grads)[0]

    def rs_pairs(tag):
        mine, got = _rs_pair_wait(f"rs_{tag}_pair_wait", rs[tag], None)
        rs[tag] = [_pair_sum(f"pairsum_{n}{i}", a, b, core) for (n, i), a, b in zip(rs_groups[tag], mine, got)]

    def rs_chips(tag):
        rs[tag] = _rs_chip_start(f"rs_{tag}_chip_start", rs[tag])[0]

    def rs_end(tag):
        return dict(zip(rs_groups[tag], _rs_chip_wait(f"rs_{tag}_chip_wait", rs[tag], None)))

    s_a, s_m, s_b, s_p = saved[1]
    dh, dhb, d_pn, d_gn, G[("ple_w_proj", 1)], G[("ple_w_gate", 1)] = _ple_bwd(
        "ple1", s_p, p[1, 0], W[1]["proj"], ple_norm[1:2], ple_gate_norm[1:2], W[1]["gate"], dh, dhb)
    small_g[("ple_norm", 1)], small_g[("ple_gate_norm", 1)] = d_pn, d_gn
    dh, dhb, small_g[("ffn_b_norm", 1)], G[("ffn_b_w_in", 1)], G[("ffn_b_w_out", 1)] = _ffn_bwd(
        "ffn_b1", s_b, ffn_b_norm[1:2], W[1]["b_in"], W[1]["b_out"], dh, dhb)
    rs_begin("pb1")
    (dh, dhb, small_g[("mix_norm", 1)], G[("conv_w_pw1", 0)], d_b_pw1, d_w_dw, d_b_dw, d_ln_g, d_ln_b,
     G[("conv_w_pw2", 0)]) = _conv_bwd(s_m, mix_norm[1:2], Wc, w_dw_full, ln_g_full, ln_b_full, dh, dhb)
    rs_pairs("pb1")
    rs_chips("pb1")
    rs_begin("c1")
    dh, dhb, small_g[("ffn_a_norm", 1)], G[("ffn_a_w_in", 1)], G[("ffn_a_w_out", 1)] = _ffn_bwd(
        "ffn_a1", s_a, ffn_a_norm[1:2], W[1]["a_in"], W[1]["a_out"], dh, dhb)
    rs_pairs("c1")
    rs_chips("c1")
    rs_begin("a1")

    s_a, s_m, s_b, s_p = saved[0]
    dh, dhb, d_pn, d_gn, G[("ple_w_proj", 0)], G[("ple_w_gate", 0)] = _ple_bwd(
        "ple0", s_p, p[0, 0], W[0]["proj"], ple_norm[0:1], ple_gate_norm[0:1], W[0]["gate"], dh, dhb)
    small_g[("ple_norm", 0)], small_g[("ple_gate_norm", 0)] = d_pn, d_gn
    rs_pairs("a1")
    rs_chips("a1")
    dh, dhb, small_g[("ffn_b_norm", 0)], G[("ffn_b_w_in", 0)], G[("ffn_b_w_out", 0)] = _ffn_bwd(
        "ffn_b0", s_b, ffn_b_norm[0:1], W[0]["b_in"], W[0]["b_out"], dh, dhb)
    rs_begin("pb0")

    def in_mla():
        rs_pairs("pb0")
        rs_chips("pb0")

    (dh, dhb, small_g[("mix_norm", 0)], d_qln, d_kvln, d_gq, d_gk,
     G[("mla_w_in", 0)], G[("mla_w_uq", 0)], G[("mla_w_ukv", 0)], G[("mla_w_o", 0)]) = _mla_bwd(
        s_m, mix_norm[0:1], tabs, Wm, mla_q_lat_norm, mla_kv_lat_norm, gq_pad, gk_pad, dh, dhb, hook=in_mla)
    rs_begin("m0")

    def with_dw_out(dw_out):
        G[("ffn_a_w_out", 0)] = dw_out
        rs_pairs("m0")
        rs_chips("m0")
        rs_begin("ao0")

    def with_dw_in(dw_in):
        G[("ffn_a_w_in", 0)] = dw_in
        rs_pairs("ao0")
        rs_chips("ao0")
        rs_begin("ai0")

    dh, dhb, small_g[("ffn_a_norm", 0)], _, _ = _ffn_bwd(
        "ffn_a0", s_a, ffn_a_norm[0:1], W[0]["a_in"], W[0]["a_out"], dh, dhb,
        hooks=(with_dw_out, with_dw_in, lambda: rs_pairs("ai0")))
    grad_x = dh[None]

    def two(name):
        return _pad_rows(jnp.concatenate([small_g[(name, 0)], small_g[(name, 1)]], axis=0), 8)

    misc = jnp.concatenate([_pad_cols(d_qln, D), _pad_cols(d_kvln, D), _pad_cols(d_gq, D), _pad_cols(d_gk, D),
                            _pad_cols(loss_row, D)], axis=0)
    C = N_DEV * Cs
    pack = jnp.concatenate([
        two("ffn_a_norm"), two("ffn_b_norm"), two("mix_norm"), two("ple_norm"), two("ple_gate_norm"),
        _pad_rows(misc, 8),
        _pad_rows(_pad_cols(d_b_pw1.reshape(2, C), D), 8),
        _pad_cols(d_w_dw, D),
        _pad_rows(_pad_cols(jnp.concatenate([d_b_dw, d_ln_g, d_ln_b], axis=0), D), 8)], axis=0)
    red = _all_reduce_small(pack)
    loss = red[44, 0]
    small_grads = dict(
        ffn_a_norm=red[0:2], ffn_b_norm=red[8:10], mix_norm=red[16:18], ple_norm=red[24:26],
        ple_gate_norm=red[32:34],
        mla_q_lat_norm=red[40:41, :QL], mla_kv_lat_norm=red[41:42, :mla_kv_lat_norm.shape[1]],
        mla_q_gain=red[42:43, :QK_DIM], mla_k_gain=red[43:44, :QK_DIM],
        conv_b_pw1=lax.dynamic_slice_in_dim(red[48:50, :C].reshape(1, 2 * C), me * 2 * Cs, 2 * Cs, axis=1),
        conv_w_dw=lax.dynamic_slice_in_dim(red[56:56 + CONV_WIDTH, :C], me * Cs, Cs, axis=1)[None],
        conv_b_dw=lax.dynamic_slice_in_dim(red[88:89, :C], me * Cs, Cs, axis=1),
        conv_ln_g=lax.dynamic_slice_in_dim(red[89:90, :C], me * Cs, Cs, axis=1),
        conv_ln_b=lax.dynamic_slice_in_dim(red[90:91, :C], me * Cs, Cs, axis=1))

    rs_chips("ai0")
    done = {}

    def plain_adamw(n, g):
        done[n] = (g,) + _adamw(f"adamw_{n}", weights[n], g, moments_m[n], moments_v[n])

    def slot_adamw(n, layer, slots):
        done[n] = tuple(_sum_adam(f"adamw_{n}{layer}", slots, weights[n], moments_m[n], moments_v[n], layer,
                                  into=done.get(n)))

    for n, g in small_grads.items():
        plain_adamw(n, g)
    for tag in ("pb1", "c1", "a1", "pb0"):
        slots = rs_end(tag)
        for n, i in rs_groups[tag]:
            slot_adamw(n, i, slots[(n, i)])
    slots = rs_end("m0")
    slot_adamw("mla_w_ukv", 0, slots[("mla_w_ukv", 0)])
    slot_adamw("mla_w_o", 0, slots[("mla_w_o", 0)])
    g_in = _chip_sum("chipsum_mla_w_in", slots[("mla_w_in", 0)], 0, 1)
    plain_adamw("mla_w_in", g_in[:, :, :mla_w_in.shape[2]])
    g_uq = _chip_sum("chipsum_mla_w_uq", slots[("mla_w_uq", 0)], 0, 1)
    plain_adamw("mla_w_uq", g_uq.reshape(1, QL, hps, HEAD_PAD)[..., :QK_DIM].reshape(mla_w_uq.shape))
    for tag in ("ao0", "ai0"):
        slots = rs_end(tag)
        for n, i in rs_groups[tag]:
            slot_adamw(n, i, slots[(n, i)])
    grads, deltas, new_m, new_v = ({n: done[n][k] for n in order} for k in range(4))

    return (loss, grad_x, *[grads[n] for n in order], *[deltas[n] for n in order],
            *[new_m[n] for n in order], *[new_v[n] for n in order])
```

```python
import functools

import jax
import jax.numpy as jnp
from jax import lax
from jax.experimental import pallas as pl
from jax.experimental.pallas import tpu as pltpu

F32 = jnp.float32
BF16 = jnp.bfloat16
MESH = pl.DeviceIdType.MESH
ANY = pl.BlockSpec(memory_space=pl.ANY)

N_DEV = 8
N_HEADS = 16
D_NOPE = 128
D_ROPE = 64
D_V = 128
QK_DIM = D_NOPE + D_ROPE
HEAD_PAD = 256
ROPE_THETA = 10000.0
CONV_WIDTH = 31
CONV_PAD = 32
FFN_RES = 0.5
EPS = 1e-6
ADAM_LR = 0.001
ADAM_B1 = 0.9
ADAM_B2 = 0.999
ADAM_EPS = 1e-08
ADAM_WD = 0.01
ADAM_STEP = 10
VMEM_LIMIT = 56 * 1024 * 1024


def _sds(shape, dtype):
    return jax.ShapeDtypeStruct(tuple(int(s) for s in shape), dtype)


def _tile(n, pref, mult=128):
    if n <= pref:
        return n
    t = (pref // mult) * mult
    while t >= mult:
        if n % t == 0:
            return t
        t -= mult
    return n


def _params():
    return pltpu.CompilerParams(vmem_limit_bytes=VMEM_LIMIT)


_LAST = [None]


def _ordered_call(body, operands, chain_out=0, **kw):
    operands = list(operands)
    n_in = len(operands)
    if _LAST[0] is not None:
        inner = body

        def body(*refs):
            inner(*refs[:n_in], *refs[n_in + 1:])

        kw = dict(kw, in_specs=list(kw["in_specs"]) + [ANY])
        operands.append(_LAST[0])
    out = pl.pallas_call(body, **kw)(*operands)
    _LAST[0] = out[chain_out] if isinstance(out, (list, tuple)) else out
    return out


def _matmul(name, grid, ops, terms, dims, acc_shapes, outs, epilogue, extras=()):
    nk = grid[2]
    n_ops, n_ex, n_out, n_acc = len(ops), len(extras), len(outs), len(acc_shapes)

    def body(*refs):
        op_refs = refs[:n_ops]
        ex_refs = refs[n_ops:n_ops + n_ex]
        out_refs = refs[n_ops + n_ex:n_ops + n_ex + n_out]
        acc_refs = refs[n_ops + n_ex + n_out:]
        vals = {}

        def opval(i):
            if i not in vals:
                v = op_refs[i][...]
                vals[i] = v if v.dtype == BF16 else v.astype(BF16)
            return vals[i]

        parts = [None] * n_acc
        for ai, li, ri in terms:
            d = lax.dot_general(opval(li), opval(ri), (dims, ((), ())), preferred_element_type=F32)
            parts[ai] = d if parts[ai] is None else parts[ai] + d
        if nk == 1:
            epilogue(parts, ex_refs, out_refs)
            return
        k = pl.program_id(2)

        @pl.when(k == 0)
        def _():
            for a_ref, p in zip(acc_refs, parts):
                a_ref[...] = p

        @pl.when(k > 0)
        def _():
            for a_ref, p in zip(acc_refs, parts):
                a_ref[...] += p

        @pl.when(k == nk - 1)
        def _():
            epilogue([a[...] for a in acc_refs], ex_refs, out_refs)

    scratch = [pltpu.VMEM(s, F32) for s in acc_shapes] if nk > 1 else []
    return _ordered_call(
        body, [a for a, _ in ops] + [a for a, _ in extras], name=name, grid=grid,
        in_specs=[s for _, s in ops] + [s for _, s in extras],
        out_specs=[s for _, s in outs],
        out_shape=[o for o, _ in outs],
        scratch_shapes=scratch,
        compiler_params=_params(),
    )


NN = ((1,), (0,))
NT = ((1,), (1,))
TN = ((0,), (0,))


def _store(i=0):
    def ep(accs, ex, outs):
        outs[0][...] = accs[0].astype(outs[0].dtype)
    return ep


def _mm_nn(name, a, b, out_dtype, tm=512, tn=1024, tk=2048, res=None, scale=1.0):
    M, K = a.shape
    N = b.shape[1]
    tm, tn, tk = _tile(M, tm, 16), _tile(N, tn), _tile(K, tk)
    spec = pl.BlockSpec((tm, tn), lambda i, j, k: (i, j))

    def ep(accs, ex, outs):
        v = accs[0] if scale == 1.0 else accs[0] * scale
        outs[0][...] = (v if res is None else ex[0][...] + v).astype(outs[0].dtype)

    return _matmul(name, (M // tm, N // tn, K // tk),
                   [(a, pl.BlockSpec((tm, tk), lambda i, j, k: (i, k))),
                    (b, pl.BlockSpec((tk, tn), lambda i, j, k: (k, j)))],
                   [(0, 0, 1)], NN, [(tm, tn)], [(_sds((M, N), out_dtype), spec)], ep,
                   [] if res is None else [(res, spec)])


def _mm_nt(name, a, b, out_dtype, tm=512, tn=1024, tk=2048, epilogue=None, extras=()):
    M, K = a.shape
    N = b.shape[0]
    tm, tn, tk = _tile(M, tm, 16), _tile(N, tn), _tile(K, tk)
    outs = [(_sds((M, N), out_dtype), pl.BlockSpec((tm, tn), lambda i, j, k: (i, j)))]
    return _matmul(name, (M // tm, N // tn, K // tk),
                   [(a, pl.BlockSpec((tm, tk), lambda i, j, k: (i, k))),
                    (b, pl.BlockSpec((tn, tk), lambda i, j, k: (j, k)))],
                   [(0, 0, 1)], NT, [(tm, tn)], outs, epilogue or _store(), extras)


def _mm_tn(name, a, b, out_dtype, tm=512, tn=1024, tk=2048, scale=None):
    T, M = a.shape
    N = b.shape[1]
    tm, tn, tk = _tile(M, tm), _tile(N, tn), _tile(T, tk)

    def ep(accs, ex, outs):
        v = accs[0] if scale is None else accs[0] * scale
        outs[0][...] = v.astype(outs[0].dtype)

    outs = [(_sds((M, N), out_dtype), pl.BlockSpec((tm, tn), lambda i, j, k: (i, j)))]
    return _matmul(name, (M // tm, N // tn, T // tk),
                   [(a, pl.BlockSpec((tk, tm), lambda i, j, k: (k, i))),
                    (b, pl.BlockSpec((tk, tn), lambda i, j, k: (k, j)))],
                   [(0, 0, 1)], TN, [(tm, tn)], outs, ep)[0]


def _mm_nn_sm(name, a, w, out_dtype, tm=512, tk=2048, epilogue=None, extras=()):
    M, K = a.shape
    S, _, Ns = w.shape
    tm, tk = _tile(M, tm, 16), _tile(K, tk)
    outs = [(_sds((M, S * Ns), out_dtype), pl.BlockSpec((tm, Ns), lambda j, i, k: (i, j)))]
    return _matmul(name, (S, M // tm, K // tk),
                   [(a, pl.BlockSpec((tm, tk), lambda j, i, k: (i, k))),
                    (w, pl.BlockSpec((None, tk, Ns), lambda j, i, k: (j, k, 0)))],
                   [(0, 0, 1)], NN, [(tm, Ns)], outs, epilogue or _store(), extras)[0]


def _mm_nt_sm(name, a, w, out_dtype, tm=1024, tn=1024):
    M = a.shape[0]
    S, K, Ns = w.shape
    tm, tn = _tile(M, tm, 16), _tile(K, tn)
    outs = [(_sds((M, K), out_dtype), pl.BlockSpec((tm, tn), lambda i, n, j: (i, n)))]
    return _matmul(name, (M // tm, K // tn, S),
                   [(a, pl.BlockSpec((tm, Ns), lambda i, n, j: (i, j))),
                    (w, pl.BlockSpec((None, tn, Ns), lambda i, n, j: (j, n, 0)))],
                   [(0, 0, 1)], NT, [(tm, tn)], outs, _store())[0]


def _mm_tn_sm(name, a, b, S, out_dtype, tm=1024, tk=2048):
    T, M = a.shape
    Ns = b.shape[1] // S
    tm, tk = _tile(M, tm), _tile(T, tk)
    outs = [(_sds((S, M, Ns), out_dtype), pl.BlockSpec((None, tm, Ns), lambda j, i, k: (j, i, 0)))]
    return _matmul(name, (S, M // tm, T // tk),
                   [(a, pl.BlockSpec((tk, tm), lambda j, i, k: (k, i))),
                    (b, pl.BlockSpec((tk, Ns), lambda j, i, k: (k, j)))],
                   [(0, 0, 1)], TN, [(tm, Ns)], outs, _store())[0]


def _row_spec(shape, axis, tm):
    block = tuple(tm if d == axis else s for d, s in enumerate(shape))
    nd = len(shape)

    def imap(i):
        return tuple(i if d == axis else 0 for d in range(nd))
    return pl.BlockSpec(block, imap)


def _full_spec(shape):
    nd = len(shape)
    return pl.BlockSpec(tuple(shape), lambda i: (0,) * nd)


def _rowwise(name, fn, T, tm, rows, consts, outs, accs=()):
    tm = _tile(T, tm, 16)
    n_in = len(rows) + len(consts)
    n_out = len(outs)

    def body(*refs):
        in_refs = refs[:n_in]
        out_refs = refs[n_in:n_in + n_out]
        acc_refs = refs[n_in + n_out:]
        i = pl.program_id(0)

        def acc_add(ai, val):
            @pl.when(i == 0)
            def _():
                acc_refs[ai][...] = val

            @pl.when(i > 0)
            def _():
                acc_refs[ai][...] += val

        fn(in_refs, out_refs, acc_add)

    return _ordered_call(
        body, [a for a, _ in rows] + list(consts), name=name, grid=(T // tm,),
        in_specs=[_row_spec(a.shape, ax, tm) for a, ax in rows] + [_full_spec(c.shape) for c in consts],
        out_specs=[_row_spec(s, ax, tm) for s, _, ax in outs] + [_full_spec(s) for s in accs],
        out_shape=[_sds(s, d) for s, d, _ in outs] + [_sds(s, F32) for s in accs],
        compiler_params=_params(),
    )


def _rms(x, g, n=None):
    n = x.shape[-1] if n is None else n
    return x * lax.rsqrt(jnp.sum(x * x, axis=-1, keepdims=True) * (1.0 / n) + EPS) * g


def _norm_fwd(name, h, gain):
    T, D = h.shape

    def fn(ins, outs, acc):
        outs[0][...] = _rms(ins[0][...], ins[1][...]).astype(BF16)

    return _rowwise(name, fn, T, 256, [(h, 0)], [gain], [((T, D), BF16, 0)])[0]


def _norm_bwd(name, h, gain, dhn, dh_res):
    T, D = h.shape

    def fn(ins, outs, acc):
        _, vjp = jax.vjp(_rms, ins[0][...], ins[3][...])
        dh, dg = vjp(ins[1][...])
        dh = dh + ins[2][...]
        outs[0][...] = dh
        outs[1][...] = dh.astype(BF16)
        acc(0, dg)

    return _rowwise(name, fn, T, 256, [(h, 0), (dhn, 0), (dh_res, 0)], [gain],
                    [((T, D), F32, 0), ((T, D), BF16, 0)], [(1, D)])


def _ffn_fwd(tag, h, gain, get_in, get_out):
    T, D = h.shape
    hn = _norm_fwd(tag + "_norm", h, gain)
    w_in = get_in(hn)
    S, _, Ns = w_in.shape
    half = S // 2
    F = half * Ns
    tm, tk = _tile(T, 256, 16), _tile(D, 2048)

    def ep(accs, ex, outs):
        g, u = accs
        act = g * jax.nn.sigmoid(g) * u
        outs[0][0] = g.astype(BF16)
        outs[0][1] = u.astype(BF16)
        outs[1][...] = act.astype(BF16)

    gu, act = _matmul(
        tag + "_in", (half, T // tm, D // tk),
        [(hn, pl.BlockSpec((tm, tk), lambda j, i, k: (i, k))),
         (w_in, pl.BlockSpec((None, tk, Ns), lambda j, i, k: (j, k, 0))),
         (w_in, pl.BlockSpec((None, tk, Ns), lambda j, i, k: (j + half, k, 0)))],
        [(0, 0, 1), (1, 0, 2)], NN, [(tm, Ns), (tm, Ns)],
        [(_sds((2, T, F), BF16), pl.BlockSpec((2, tm, Ns), lambda j, i, k: (0, i, j))),
         (_sds((T, F), BF16), pl.BlockSpec((tm, Ns), lambda j, i, k: (i, j)))],
        ep)

    w_out = get_out(act)
    h_new = _mm_nn(tag + "_out", act, w_out, F32, tm=512, tn=512, tk=F, res=h, scale=FFN_RES)[0]
    return h_new, (h, hn, gu, act)


def _no_hook(*_):
    return None


def _ffn_bwd(tag, saved, gain, w_in, w_out, dh, dhb, hooks=(_no_hook, _no_hook, _no_hook)):
    h, hn, gu, act = saved
    T, D = h.shape
    S, _, Ns = w_in.shape
    half = S // 2
    F = half * Ns
    tm, tk = _tile(T, 512, 16), _tile(D, 2048)

    def ep(accs, ex, outs):
        dact = FFN_RES * accs[0]
        g = ex[0][0].astype(F32)
        u = ex[0][1].astype(F32)
        sg = jax.nn.sigmoid(g)
        outs[0][0] = (dact * u * (sg * (1.0 + g * (1.0 - sg)))).astype(BF16)
        outs[0][1] = (dact * (g * sg)).astype(BF16)

    gu_spec = pl.BlockSpec((2, tm, Ns), lambda j, i, k: (0, i, j))
    dgu = _matmul(
        tag + "_dact", (half, T // tm, D // tk),
        [(dhb, pl.BlockSpec((tm, tk), lambda j, i, k: (i, k))),
         (w_out, pl.BlockSpec((Ns, tk), lambda j, i, k: (j, k)))],
        [(0, 0, 1)], NT, [(tm, Ns)],
        [(_sds((2, T, F), BF16), gu_spec)], ep, extras=[(gu, gu_spec)])[0]

    dw_out = _mm_tn(tag + "_dwout", act, dhb, BF16, tm=512, tn=2048, scale=FFN_RES)
    hooks[0](dw_out)

    tm = _tile(T, 1024, 16)
    tkd, tt = _tile(D, 512), _tile(T, 2048)
    dw_in = _matmul(
        tag + "_dwin", (S, D // tkd, T // tt),
        [(hn, pl.BlockSpec((tt, tkd), lambda j, i, k: (k, i))),
         (dgu, pl.BlockSpec((None, tt, Ns), lambda j, i, k: (j // half, k, j % half)))],
        [(0, 0, 1)], TN, [(tkd, Ns)],
        [(_sds((S, D, Ns), BF16), pl.BlockSpec((None, tkd, Ns), lambda j, i, k: (j, i, 0)))], _store())[0]
    hooks[1](dw_in)

    tn = _tile(D, 1024)
    dhn = _matmul(
        tag + "_dhn", (T // tm, D // tn, half),
        [(dgu, pl.BlockSpec((None, tm, Ns), lambda i, n, j: (0, i, j))),
         (dgu, pl.BlockSpec((None, tm, Ns), lambda i, n, j: (1, i, j))),
         (w_in, pl.BlockSpec((None, tn, Ns), lambda i, n, j: (j, n, 0))),
         (w_in, pl.BlockSpec((None, tn, Ns), lambda i, n, j: (j + half, n, 0)))],
        [(0, 0, 2), (0, 1, 3)], NT, [(tm, tn)],
        [(_sds((T, D), F32), pl.BlockSpec((tm, tn), lambda i, n, j: (i, n)))], _store())[0]
    hooks[2]()

    dh_in, dh_in_b, dgain = _norm_bwd(tag + "_dnorm", h, gain, dhn, dh)
    return dh_in, dh_in_b, dgain, dw_in, dw_out


def _ple_fwd(tag, h, p, get_proj, ple_norm, gate_norm, get_gate):
    T, D = h.shape
    e_raw = _mm_nn_sm(tag + "_proj", p, get_proj(h), F32)
    hn = _norm_fwd(tag + "_norm", h, gate_norm)
    gate_raw = _mm_nn(tag + "_gate", hn, get_gate(hn), F32)[0]

    def fn(ins, outs, acc):
        e = _rms(ins[1][...], ins[3][...])
        outs[0][...] = ins[0][...] + e * jax.nn.sigmoid(ins[2][...])

    h_new = _rowwise(tag + "_mix", fn, T, 256, [(h, 0), (e_raw, 0), (gate_raw, 0)], [ple_norm],
                     [((T, D), F32, 0)])[0]
    return h_new, (h, hn, e_raw, gate_raw)


def _ple_bwd(tag, saved, p, w_proj, ple_norm, gate_norm, w_gate, dh, dhb):
    h, hn, e_raw, gate_raw = saved
    T, D = h.shape
    S = w_proj.shape[0]

    def fn(ins, outs, acc):
        def f(e_raw_, gate_raw_, g_):
            return _rms(e_raw_, g_) * jax.nn.sigmoid(gate_raw_)
        _, vjp = jax.vjp(f, ins[0][...], ins[1][...], ins[3][...])
        de, dgate, dg = vjp(ins[2][...])
        outs[0][...] = de.astype(BF16)
        outs[1][...] = dgate.astype(BF16)
        acc(0, dg)

    de, dgate, d_ple_norm = _rowwise(tag + "_dmix", fn, T, 256, [(e_raw, 0), (gate_raw, 0), (dh, 0)], [ple_norm],
                                     [((T, D), BF16, 0), ((T, D), BF16, 0)], [(1, D)])
    dw_proj = _mm_tn_sm(tag + "_dwproj", p, de, S, BF16)
    dw_gate = _mm_tn(tag + "_dwgate", hn, dgate, BF16)
    dhn = _mm_nt(tag + "_dhn", dgate, w_gate, F32)[0]
    dh_in, dh_in_b, d_gate_norm = _norm_bwd(tag + "_dnorm", h, gate_norm, dhn, dh)
    return dh_in, dh_in_b, d_ple_norm, d_gate_norm, dw_proj, dw_gate


def _rope(t, c, s1, s2):
    q = D_ROPE // 2
    return t * c + pltpu.roll(t, q, 1) * s1 + pltpu.roll(t, 128 - q, 1) * s2


def _rope_t(d, c, s1, s2):
    q = D_ROPE // 2
    return d * c + pltpu.roll(d * s1, 128 - q, 1) + pltpu.roll(d * s2, q, 1)


def _head_norm(lo, hi, g_lo, g_hi):
    ms = (jnp.sum(lo * lo, axis=-1, keepdims=True) + jnp.sum(hi * hi, axis=-1, keepdims=True)) * (1.0 / QK_DIM)
    inv = lax.rsqrt(ms + EPS)
    return lo * inv * g_lo, hi * inv * g_hi


def _qk_prep(qraw, kvraw, lat, tabs, gq, gk, H):
    T = qraw.shape[0]
    koff = lat.shape[1] - 128

    def fn(ins, outs, acc):
        q_ref, kv_ref, lat_ref, c_ref, s1_ref, s2_ref, gq_ref, gk_ref = ins
        c, s1, s2 = c_ref[...], s1_ref[...], s2_ref[...]
        kr = lat_ref[:, koff:koff + 128]
        for hd in range(H):
            o = hd * HEAD_PAD
            lo, hi = _head_norm(q_ref[:, o:o + 128], q_ref[:, o + 128:o + 256], gq_ref[:, 0:128], gq_ref[:, 128:256])
            outs[0][hd, :, 0:128] = lo.astype(BF16)
            outs[0][hd, :, 128:256] = _rope(hi, c, s1, s2).astype(BF16)
            lo, hi = _head_norm(kv_ref[:, o:o + 128], kr, gk_ref[:, 0:128], gk_ref[:, 128:256])
            outs[1][hd, :, 0:128] = lo.astype(BF16)
            outs[1][hd, :, 128:256] = _rope(hi, c, s1, s2).astype(BF16)
            outs[2][hd] = kv_ref[:, o + 128:o + 256].astype(BF16)

    return _rowwise("mla_qkprep", fn, T, 256, [(qraw, 0), (kvraw, 0), (lat, 0)] + [(t, 0) for t in tabs], [gq, gk],
                    [((H, T, HEAD_PAD), BF16, 1), ((H, T, HEAD_PAD), BF16, 1), ((H, T, D_V), BF16, 1)])


def _qk_prep_bwd(qraw, kvraw, lat, tabs, gq, gk, dQ, dK, dV, H):
    T = qraw.shape[0]
    koff = lat.shape[1] - 128

    def fn(ins, outs, acc):
        q_ref, kv_ref, lat_ref, c_ref, s1_ref, s2_ref, dq_ref, dk_ref, dv_ref, gq_ref, gk_ref = ins
        c, s1, s2 = c_ref[...], s1_ref[...], s2_ref[...]
        kr = lat_ref[:, koff:koff + 128]
        dkr = jnp.zeros_like(kr)
        dg = [None] * 4
        for hd in range(H):
            o = hd * HEAD_PAD
            _, vjp = jax.vjp(_head_norm, q_ref[:, o:o + 128], q_ref[:, o + 128:o + 256],
                             gq_ref[:, 0:128], gq_ref[:, 128:256])
            dlo, dhi, dg0, dg1 = vjp((dq_ref[hd, :, 0:128], _rope_t(dq_ref[hd, :, 128:256], c, s1, s2)))
            outs[0][:, o:o + 128] = dlo.astype(BF16)
            outs[0][:, o + 128:o + 256] = dhi.astype(BF16)
            _, vjp = jax.vjp(_head_norm, kv_ref[:, o:o + 128], kr, gk_ref[:, 0:128], gk_ref[:, 128:256])
            dlo, dhi, dg2, dg3 = vjp((dk_ref[hd, :, 0:128], _rope_t(dk_ref[hd, :, 128:256], c, s1, s2)))
            outs[1][:, o:o + 128] = dlo.astype(BF16)
            outs[1][:, o + 128:o + 256] = dv_ref[hd].astype(BF16)
            dkr = dkr + dhi
            for n, v in enumerate((dg0, dg1, dg2, dg3)):
                dg[n] = v if dg[n] is None else dg[n] + v
        outs[2][...] = dkr
        for n in range(4):
            acc(n, dg[n])

    W = H * HEAD_PAD
    return _rowwise("mla_dqkprep", fn, T, 128,
                    [(qraw, 0), (kvraw, 0), (lat, 0)] + [(t, 0) for t in tabs] + [(dQ, 1), (dK, 1), (dV, 1)], [gq, gk],
                    [((T, W), BF16, 0), ((T, W), BF16, 0), ((T, 128), F32, 0)], [(1, 128)] * 4)


def _attn_probs(q, k, c, tq):
    nk = k.shape[0]
    s = lax.dot_general(q, k, (NT, ((), ())), preferred_element_type=F32) * (QK_DIM ** -0.5)
    row = c * tq + lax.broadcasted_iota(jnp.int32, (tq, nk), 0)
    col = lax.broadcasted_iota(jnp.int32, (tq, nk), 1)
    s = jnp.where(col <= row, s, -jnp.inf)
    p = jnp.exp(s - jnp.max(s, axis=-1, keepdims=True))
    return p / jnp.sum(p, axis=-1, keepdims=True)


def _per_query_block(nq, fn):
    i = pl.program_id(1)
    for c in range(nq):
        pl.when(i == c)(functools.partial(fn, c))


def _attn_fwd(Q, K, V):
    H, T, _ = Q.shape
    tq = _tile(T, 256)

    def body(q_ref, k_ref, v_ref, o_ref):
        def block(c):
            nk = (c + 1) * tq
            p = _attn_probs(q_ref[...], k_ref[0:nk, :], c, tq)
            o_ref[...] = jnp.dot(p.astype(BF16), v_ref[0:nk, :], preferred_element_type=F32).astype(BF16)

        _per_query_block(T // tq, block)

    return _ordered_call(
        body, [Q, K, V], name="mla_attn", grid=(H, T // tq),
        in_specs=[pl.BlockSpec((None, tq, HEAD_PAD), lambda h, i: (h, i, 0)),
                  pl.BlockSpec((None, T, HEAD_PAD), lambda h, i: (h, 0, 0)),
                  pl.BlockSpec((None, T, D_V), lambda h, i: (h, 0, 0))],
        out_specs=pl.BlockSpec((tq, D_V), lambda h, i: (i, h)),
        out_shape=_sds((T, H * D_V), BF16),
        compiler_params=_params(),
    )


def _attn_bwd(Q, K, V, dO):
    H, T, _ = Q.shape
    tq = _tile(T, 256)

    def body(q_ref, k_ref, v_ref, do_ref, dq_ref, dk_ref, dv_ref):
        @pl.when(pl.program_id(1) == 0)
        def _():
            dk_ref[...] = jnp.zeros_like(dk_ref)
            dv_ref[...] = jnp.zeros_like(dv_ref)

        def block(c):
            nk = (c + 1) * tq
            q, k, do = q_ref[...], k_ref[0:nk, :], do_ref[...]
            p = _attn_probs(q, k, c, tq)
            dv_ref[0:nk, :] += lax.dot_general(p.astype(BF16), do, (TN, ((), ())), preferred_element_type=F32)
            dp = lax.dot_general(do, v_ref[0:nk, :], (NT, ((), ())), preferred_element_type=F32)
            ds = p * (dp - jnp.sum(p * dp, axis=-1, keepdims=True)) * (QK_DIM ** -0.5)
            dsb = ds.astype(BF16)
            dq_ref[...] = jnp.dot(dsb, k, preferred_element_type=F32)
            dk_ref[0:nk, :] += lax.dot_general(dsb, q, (TN, ((), ())), preferred_element_type=F32)

        _per_query_block(T // tq, block)

    return _ordered_call(
        body, [Q, K, V, dO], name="mla_dattn", grid=(H, T // tq),
        in_specs=[pl.BlockSpec((None, tq, HEAD_PAD), lambda h, i: (h, i, 0)),
                  pl.BlockSpec((None, T, HEAD_PAD), lambda h, i: (h, 0, 0)),
                  pl.BlockSpec((None, T, D_V), lambda h, i: (h, 0, 0)),
                  pl.BlockSpec((tq, D_V), lambda h, i: (i, h))],
        out_specs=[pl.BlockSpec((None, tq, HEAD_PAD), lambda h, i: (h, i, 0)),
                   pl.BlockSpec((None, T, HEAD_PAD), lambda h, i: (h, 0, 0)),
                   pl.BlockSpec((None, T, D_V), lambda h, i: (h, 0, 0))],
        out_shape=[_sds((H, T, HEAD_PAD), F32), _sds((H, T, HEAD_PAD), F32), _sds((H, T, D_V), F32)],
        compiler_params=_params(),
    )


def _mla_fwd(h, gain, tabs, get, q_lat_norm, kv_lat_norm, gq, gk):
    T, D = h.shape
    QL, KL = q_lat_norm.shape[1], kv_lat_norm.shape[1]
    hn = _norm_fwd("mla_norm", h, gain)
    w = dict(m_in=get["m_in"](hn))
    lat = _mm_nn("mla_lat", hn, w["m_in"], F32, tn=w["m_in"].shape[1])[0]

    def fn(ins, outs, acc):
        outs[0][...] = _rms(ins[0][:, 0:QL], ins[1][...]).astype(BF16)
        outs[1][...] = _rms(ins[0][:, QL:QL + KL], ins[2][...]).astype(BF16)

    cq, ckv = _rowwise("mla_latnorm", fn, T, 256, [(lat, 0)], [q_lat_norm, kv_lat_norm],
                       [((T, QL), BF16, 0), ((T, KL), BF16, 0)])
    w["uq"], w["ukv"] = get["uq"](cq), get["ukv"](ckv)
    H = w["uq"].shape[0] * w["uq"].shape[2] // HEAD_PAD
    qraw = _mm_nn_sm("mla_uq", cq, w["uq"], F32)
    kvraw = _mm_nn_sm("mla_ukv", ckv, w["ukv"], F32)
    Q, K, V = _qk_prep(qraw, kvraw, lat, tabs, gq, gk, H)
    O = _attn_fwd(Q, K, V)
    w["wo"] = get["wo"](O)

    h_new = _mm_nn("mla_out", O, w["wo"], F32, res=h)[0]
    return h_new, (h, hn, lat, cq, ckv, qraw, kvraw, Q, K, V, O)


def _mla_bwd(saved, gain, tabs, w, q_lat_norm, kv_lat_norm, gq, gk, dh, dhb, hook=_no_hook):
    h, hn, lat, cq, ckv, qraw, kvraw, Q, K, V, O = saved
    T, D = h.shape
    H = Q.shape[0]
    S = w["uq"].shape[0]
    QL, KL = q_lat_norm.shape[1], kv_lat_norm.shape[1]
    dO = _mm_nt("mla_dO", dhb, w["wo"], BF16)[0]
    dwo = _mm_tn("mla_dwo", O, dhb, BF16)
    hook()
    dQ, dK, dV = _attn_bwd(Q, K, V, dO)
    dqraw, dkvraw, dkr, dgq0, dgq1, dgk0, dgk1 = _qk_prep_bwd(qraw, kvraw, lat, tabs, gq, gk, dQ, dK, dV, H)
    dcq = _mm_nt_sm("mla_dcq", dqraw, w["uq"], F32)
    dckv = _mm_nt_sm("mla_dckv", dkvraw, w["ukv"], F32)
    dwuq = _mm_tn_sm("mla_dwuq", cq, dqraw, S, BF16)
    dwukv = _mm_tn_sm("mla_dwukv", ckv, dkvraw, S, BF16)

    def fn(ins, outs, acc):
        _, vjp = jax.vjp(_rms, ins[0][:, 0:QL], ins[4][...])
        d, dgq_ = vjp(ins[1][...])
        outs[0][:, 0:QL] = d.astype(BF16)
        _, vjp = jax.vjp(_rms, ins[0][:, QL:QL + KL], ins[5][...])
        d, dgkv_ = vjp(ins[2][...])
        outs[0][:, QL:QL + KL] = d.astype(BF16)
        outs[0][:, QL + KL:QL + KL + 128] = ins[3][...].astype(BF16)
        acc(0, dgq_)
        acc(1, dgkv_)

    dlat, d_qln, d_kvln = _rowwise("mla_dlatnorm", fn, T, 256, [(lat, 0), (dcq, 0), (dckv, 0), (dkr, 0)],
                                   [q_lat_norm, kv_lat_norm], [(lat.shape, BF16, 0)], [(1, QL), (1, KL)])
    dhn = _mm_nt("mla_dhn", dlat, w["m_in"], F32, tk=lat.shape[1])[0]
    dw_min = _mm_tn("mla_dwin", hn, dlat, BF16, tn=lat.shape[1])
    dh_in, dh_in_b, dgain = _norm_bwd("mla_dnorm", h, gain, dhn, dh)
    d_gq = jnp.concatenate([dgq0, dgq1], axis=1)[:, :QK_DIM]
    d_gk = jnp.concatenate([dgk0, dgk1], axis=1)[:, :QK_DIM]
    return dh_in, dh_in_b, dgain, d_qln, d_kvln, d_gq, d_gk, dw_min, dwuq, dwukv, dwo


def _conv_rows(T):
    return _tile(T, 128, 8)


def _dwconv_fwd(u, w_dw, b_dw):
    T, C = u.shape
    tc, R = _tile(C, 256), _conv_rows(T)
    off = CONV_PAD - (CONV_WIDTH - 1)

    def body(u_ref, w_ref, b_ref, y_ref, pad_ref):
        pad_ref[0:CONV_PAD, :] = jnp.zeros((CONV_PAD, tc), F32)
        pad_ref[CONV_PAD:CONV_PAD + T, :] = u_ref[...]
        for r in range(T // R):
            acc = jnp.broadcast_to(b_ref[...], (R, tc))
            for j in range(CONV_WIDTH):
                acc = acc + w_ref[j:j + 1, :] * pad_ref[r * R + off + j:r * R + off + j + R, :]
            y_ref[r * R:(r + 1) * R, :] = acc

    return _ordered_call(
        body, [u, w_dw, b_dw], name="conv_dw", grid=(C // tc,),
        in_specs=[pl.BlockSpec((T, tc), lambda c: (0, c)), pl.BlockSpec((32, tc), lambda c: (0, c)),
                  pl.BlockSpec((1, tc), lambda c: (0, c))],
        out_specs=pl.BlockSpec((T, tc), lambda c: (0, c)),
        out_shape=_sds((T, C), F32),
        scratch_shapes=[pltpu.VMEM((T + CONV_PAD, tc), F32)],
        compiler_params=_params(),
    )


def _dwconv_bwd(u, w_dw, dy):
    T, C = u.shape
    tc, R = _tile(C, 256), _conv_rows(T)
    off = CONV_PAD - (CONV_WIDTH - 1)

    def body(u_ref, w_ref, dy_ref, du_ref, dw_ref, db_ref, upad_ref, dpad_ref):
        upad_ref[0:CONV_PAD, :] = jnp.zeros((CONV_PAD, tc), F32)
        upad_ref[CONV_PAD:CONV_PAD + T, :] = u_ref[...]
        dpad_ref[0:T, :] = dy_ref[...]
        dpad_ref[T:T + CONV_PAD, :] = jnp.zeros((CONV_PAD, tc), F32)
        for r in range(T // R):
            acc = jnp.zeros((R, tc), F32)
            for j in range(CONV_WIDTH):
                s = r * R + (CONV_WIDTH - 1) - j
                acc = acc + w_ref[j:j + 1, :] * dpad_ref[s:s + R, :]
            du_ref[r * R:(r + 1) * R, :] = acc
        for j in range(CONV_WIDTH):
            acc = jnp.zeros((R, tc), F32)
            for r in range(T // R):
                acc = acc + dy_ref[r * R:(r + 1) * R, :] * upad_ref[r * R + off + j:r * R + off + j + R, :]
            dw_ref[j:j + 1, :] = jnp.sum(acc, axis=0, keepdims=True)
        dw_ref[CONV_WIDTH:32, :] = jnp.zeros((32 - CONV_WIDTH, tc), F32)
        db_ref[...] = jnp.sum(dy_ref[...], axis=0, keepdims=True)

    return _ordered_call(
        body, [u, w_dw, dy], name="conv_ddw", grid=(C // tc,),
        in_specs=[pl.BlockSpec((T, tc), lambda c: (0, c)), pl.BlockSpec((32, tc), lambda c: (0, c)),
                  pl.BlockSpec((T, tc), lambda c: (0, c))],
        out_specs=[pl.BlockSpec((T, tc), lambda c: (0, c)), pl.BlockSpec((32, tc), lambda c: (0, c)),
                   pl.BlockSpec((1, tc), lambda c: (0, c))],
        out_shape=[_sds((T, C), F32), _sds((32, C), F32), _sds((1, C), F32)],
        scratch_shapes=[pltpu.VMEM((T + CONV_PAD, tc), F32), pltpu.VMEM((T + CONV_PAD, tc), F32)],
        compiler_params=_params(),
    )


def _ln_silu(y, g, b):
    mu = jnp.mean(y, axis=-1, keepdims=True)
    yc = y - mu
    z = yc * lax.rsqrt(jnp.mean(yc * yc, axis=-1, keepdims=True) + EPS) * g + b
    return z * jax.nn.sigmoid(z)


def _conv_fwd(h, gain, get):
    T, D = h.shape
    hn = _norm_fwd("conv_norm", h, gain)
    w = dict(pw1=get["pw1"](hn))
    b_pw1, w_dw, b_dw, ln_g, ln_b = get["small"](hn)
    S, _, Ns = w["pw1"].shape
    half = S // 2
    C = half * Ns
    tm, tk = _tile(T, 512, 16), _tile(D, 2048)

    def ep(accs, ex, outs):
        a = accs[0] + ex[0][...]
        g = accs[1] + ex[1][...]
        outs[0][0] = a.astype(BF16)
        outs[0][1] = g.astype(BF16)
        outs[1][...] = a * jax.nn.sigmoid(g)

    ag, u = _matmul(
        "conv_pw1", (half, T // tm, D // tk),
        [(hn, pl.BlockSpec((tm, tk), lambda j, i, k: (i, k))),
         (w["pw1"], pl.BlockSpec((None, tk, Ns), lambda j, i, k: (j, k, 0))),
         (w["pw1"], pl.BlockSpec((None, tk, Ns), lambda j, i, k: (j + half, k, 0)))],
        [(0, 0, 1), (1, 0, 2)], NN, [(tm, Ns), (tm, Ns)],
        [(_sds((2, T, C), BF16), pl.BlockSpec((2, tm, Ns), lambda j, i, k: (0, i, j))),
         (_sds((T, C), F32), pl.BlockSpec((tm, Ns), lambda j, i, k: (i, j)))],
        ep,
        extras=[(b_pw1, pl.BlockSpec((None, 1, Ns), lambda j, i, k: (j, 0, 0))),
                (b_pw1, pl.BlockSpec((None, 1, Ns), lambda j, i, k: (j + half, 0, 0)))])
    y = _dwconv_fwd(u, w_dw, b_dw)

    def fn(ins, outs, acc):
        outs[0][...] = _ln_silu(ins[0][...], ins[1][...], ins[2][...]).astype(BF16)

    s = _rowwise("conv_ln", fn, T, 256, [(y, 0)], [ln_g, ln_b], [((T, C), BF16, 0)])[0]

    w["pw2"] = get["pw2"](s)
    h_new = _mm_nn("conv_pw2", s, w["pw2"], F32, res=h)[0]
    return h_new, (h, hn, ag, u, y, s)


def _conv_bwd(saved, gain, w, w_dw, ln_g, ln_b, dh, dhb):
    h, hn, ag, u, y, s = saved
    T, D = h.shape
    S, _, Ns = w["pw1"].shape
    half = S // 2
    C = half * Ns
    ds = _mm_nt("conv_ds", dhb, w["pw2"], F32)[0]
    dw_pw2 = _mm_tn("conv_dwpw2", s, dhb, BF16)

    def fn(ins, outs, acc):
        _, vjp = jax.vjp(_ln_silu, ins[0][...], ins[2][...], ins[3][...])
        dy, dg, db = vjp(ins[1][...])
        outs[0][...] = dy
        acc(0, dg)
        acc(1, db)

    dy, d_ln_g, d_ln_b = _rowwise("conv_dln", fn, T, 256, [(y, 0), (ds, 0)], [ln_g, ln_b],
                                  [((T, C), F32, 0)], [(1, C), (1, C)])
    du, d_w_dw, d_b_dw = _dwconv_bwd(u, w_dw, dy)

    def fn2(ins, outs, acc):
        a = ins[0][0].astype(F32)
        g = ins[0][1].astype(F32)
        du_ = ins[1][...]
        sg = jax.nn.sigmoid(g)
        da = du_ * sg
        dg = du_ * a * sg * (1.0 - sg)
        outs[0][0] = da.astype(BF16)
        outs[0][1] = dg.astype(BF16)
        acc(0, jnp.sum(da, axis=0, keepdims=True))
        acc(1, jnp.sum(dg, axis=0, keepdims=True))

    dag, d_b_a, d_b_g = _rowwise("conv_dglu", fn2, T, 256, [(ag, 1), (du, 0)], [],
                                 [((2, T, C), BF16, 1)], [(1, C), (1, C)])
    tm, tn = _tile(T, 1024, 16), _tile(D, 1024)
    dhn = _matmul(
        "conv_dhn", (T // tm, D // tn, half),
        [(dag, pl.BlockSpec((None, tm, Ns), lambda i, n, j: (0, i, j))),
         (dag, pl.BlockSpec((None, tm, Ns), lambda i, n, j: (1, i, j))),
         (w["pw1"], pl.BlockSpec((None, tn, Ns), lambda i, n, j: (j, n, 0))),
         (w["pw1"], pl.BlockSpec((None, tn, Ns), lambda i, n, j: (j + half, n, 0)))],
        [(0, 0, 2), (0, 1, 3)], NT, [(tm, tn)],
        [(_sds((T, D), F32), pl.BlockSpec((tm, tn), lambda i, n, j: (i, n)))], _store())[0]
    tkd, tt = _tile(D, 1024), _tile(T, 2048)
    dw_pw1 = _matmul(
        "conv_dwpw1", (S, D // tkd, T // tt),
        [(hn, pl.BlockSpec((tt, tkd), lambda j, i, k: (k, i))),
         (dag, pl.BlockSpec((None, tt, Ns), lambda j, i, k: (j // half, k, j % half)))],
        [(0, 0, 1)], TN, [(tkd, Ns)],
        [(_sds((S, D, Ns), BF16), pl.BlockSpec((None, tkd, Ns), lambda j, i, k: (j, i, 0)))], _store())[0]
    dh_in, dh_in_b, dgain = _norm_bwd("conv_dnorm", h, gain, dhn, dh)
    d_b_pw1 = jnp.concatenate([d_b_a, d_b_g], axis=1)
    return dh_in, dh_in_b, dgain, dw_pw1, d_b_pw1, d_w_dw, d_b_dw, d_ln_g, d_ln_b, dw_pw2


def _loss_head(y, target):
    T, D = y.shape

    def fn(ins, outs, acc):
        e = ins[0][...] - ins[1][...]
        d = e * (1.0 / D)
        outs[0][...] = d
        outs[1][...] = d.astype(BF16)
        part = jnp.sum(jnp.sum(e * e, axis=-1, keepdims=True), axis=0, keepdims=True) * (0.5 / D)
        acc(0, jnp.broadcast_to(part, (1, 128)))

    return _rowwise("loss_head", fn, T, 256, [(y, 0), (target, 0)], [], [((T, D), F32, 0), ((T, D), BF16, 0)],
                    [(1, 128)])


def _place():
    return lax.axis_index("x"), lax.axis_index("y"), lax.axis_index("c")


def _peer(j):
    x, y, c = _place()
    return (1 - x if j & 4 else x, 1 - y if j & 2 else y, 1 - c if j & 1 else c)


def _index(place):
    return 4 * place[0] + 2 * place[1] + place[2]


HBM = pl.BlockSpec(memory_space=pltpu.HBM)
SEM = pl.BlockSpec(memory_space=pltpu.SEMAPHORE)
EFFECT = pltpu.SideEffectType.DATAFLOW_SIDE_EFFECTING


def _chip(j):
    x, y, _ = _place()
    return (1 - x if j & 2 else x, 1 - y if j & 1 else y)


def _chip_index(chip):
    return 2 * chip[0] + chip[1]


def _remote(src, dst, send, recv, k, device):
    return pltpu.make_async_remote_copy(src_ref=src, dst_ref=dst, send_sem=send.at[k], recv_sem=recv.at[k],
                                        device_id=device, device_id_type=MESH)


def _hbm(arrays):
    return [pltpu.with_memory_space_constraint(a, pltpu.HBM) for a in arrays]


def _split_call(name, body, ins, sems_in, sems_out, after=None, token=True):
    n, ns_in, ns_out = len(ins), len(sems_in), len(sems_out)

    def kernel_body(*refs):
        in_refs = refs[:n]
        si = refs[n:n + ns_in]
        so = refs[n + ns_in:n + ns_in + ns_out]
        tok = refs[-1]
        body(in_refs, si, so, tok)
        tok[...] = jnp.zeros_like(tok)

    res = _ordered_call(
        kernel_body, _hbm(ins) + list(sems_in), chain_out=ns_out + n, name=name,
        in_specs=[HBM] * n + [SEM] * ns_in,
        out_specs=[SEM] * ns_out + [HBM] * n + [pl.BlockSpec(memory_space=pltpu.VMEM)],
        out_shape=[pltpu.SemaphoreType.DMA((s,)) for s in sems_out] + [pltpu.HBM(a.shape, a.dtype) for a in ins]
        + [_sds((8, 128), F32)],
        input_output_aliases={i: ns_out + i for i in range(n)},
        compiler_params=pltpu.CompilerParams(has_side_effects=EFFECT),
    )
    sems = list(res[:ns_out])
    arrays = list(res[ns_out:ns_out + n])
    return sems, arrays, res[-1]


def _ag_start(name, groups, after=None):
    flat = [s for grp in groups for s in grp]
    zones = [lax.empty((N_DEV,) + s.shape, s.dtype) for s in flat]
    n = len(flat)
    sizes = []
    for grp in groups:
        sizes += [7 * len(grp), 7 * len(grp), len(grp)]

    def body(ins, si, so, tok):
        x, y, c = _place()
        me = _index((x, y, c))
        base = 0
        for gi, grp in enumerate(groups):
            send, recv, local = so[3 * gi:3 * gi + 3]
            for w in range(len(grp)):
                src, dst = ins[base + w], ins[n + base + w].at[me]
                pltpu.make_async_copy(src, dst, local.at[w]).start()
                _remote(src, dst, send, recv, 7 * w, (x, y, 1 - c)).start()
                for j in (1, 2, 3):
                    _remote(src, dst, send, recv, 7 * w + j, (*_chip(j), c)).start()
            base += len(grp)

    sems, arrays, token = _split_call(name, body, flat + zones, [], sizes, after=after, token=True)
    out, base = [], 0
    for gi, grp in enumerate(groups):
        k = len(grp)
        out.append((sems[3 * gi:3 * gi + 3], arrays[base:base + k], arrays[n + base:n + base + k]))
        base += k
    return out, token


def _ag_forward(name, handle, after):
    sems, shards, zones = handle
    k = len(shards)

    def arrive(ins, si, so, tok):
        send, recv, _ = si
        _, _, c = _place()
        for j in (1, 2, 3):
            for w in range(k):
                blk = ins[k + w].at[_index((*_chip(j), c))]
                _remote(ins[w], blk, send, recv, 7 * w + j, (*_chip(j), c)).wait_recv()

    _, arrays, _ = _split_call(name + "_arrive", arrive, list(shards) + list(zones), sems, [], after=after)

    def pass_on(ins, si, so, tok):
        fsend, frecv = so
        x, y, c = _place()
        for j in (1, 2, 3):
            for w in range(k):
                blk = ins[w].at[_index((*_chip(j), c))]
                _remote(blk, blk, fsend, frecv, 3 * w + j - 1, (x, y, 1 - c)).start()

    fsems, zones2, _ = _split_call(name + "_pass", pass_on, arrays[k:], [], [3 * k, 3 * k])
    return (list(sems) + fsems, arrays[:k], zones2)


def _ag_wait(name, handle, after):
    sems, shards, zones = handle
    k = len(shards)

    def body(ins, si, so, tok):
        send, recv, local, fsend, frecv = si
        x, y, c = _place()
        sib = (x, y, 1 - c)
        for w in range(k):
            zone = ins[k + w]
            _remote(ins[w], zone.at[_index(sib)], send, recv, 7 * w, sib).wait_recv()
            for j in (1, 2, 3):
                blk = zone.at[_index((*_chip(j), 1 - c))]
                _remote(blk, blk, fsend, frecv, 3 * w + j - 1, sib).wait_recv()
        for w in range(k):
            zone = ins[k + w]
            mine = zone.at[_index((x, y, c))]
            _remote(ins[w], mine, send, recv, 7 * w, sib).wait_send()
            for j in (1, 2, 3):
                _remote(ins[w], mine, send, recv, 7 * w + j, (*_chip(j), c)).wait_send()
                blk = zone.at[_index((*_chip(j), c))]
                _remote(blk, blk, fsend, frecv, 3 * w + j - 1, sib).wait_send()
            pltpu.make_async_copy(ins[w], mine, local.at[w]).wait()

    _, arrays, _ = _split_call(name, body, list(shards) + list(zones), sems, [], after=after)
    return arrays[k:]


def _rs_pair_start(name, grads, after=None):
    n = len(grads)
    zones = [lax.empty((4,) + g.shape[1:], g.dtype) for g in grads]

    def body(ins, si, so, tok):
        send, recv = so
        x, y, c = _place()
        for w in range(n):
            for q in range(4):
                _remote(ins[w].at[2 * q + 1 - c], ins[n + w].at[q], send, recv, 4 * w + q, (x, y, 1 - c)).start()

    sems, arrays, token = _split_call(name, body, list(grads) + zones, [], [4 * n, 4 * n], after=after, token=True)
    return (sems, arrays[:n], arrays[n:]), token


def _rs_pair_wait(name, handle, after):
    sems, grads, zones = handle
    n = len(grads)

    def body(ins, si, so, tok):
        send, recv = si
        x, y, c = _place()
        for w in range(n):
            for q in range(4):
                cp = _remote(ins[w].at[2 * q + 1 - c], ins[n + w].at[q], send, recv, 4 * w + q, (x, y, 1 - c))
                cp.wait_recv()
                cp.wait_send()

    _, arrays, _ = _split_call(name, body, list(grads) + list(zones), sems, [], after=after)
    return arrays[:n], arrays[n:]


def _pair_sum(name, g, got, core):
    _, R, C = g.shape
    g4 = g.reshape(4, 2, R, C)
    tr = _tile(R, 256, 16)

    def body(c_ref, g_ref, a_ref, *rest):
        o_ref = rest[-1]
        o_ref[...] = (g_ref[...].astype(F32) + a_ref[...].astype(F32)).astype(o_ref.dtype)

    prev = [] if _LAST[0] is None else [_LAST[0]]
    out = pl.pallas_call(
        body, name=name,
        grid_spec=pltpu.PrefetchScalarGridSpec(
            num_scalar_prefetch=1, grid=(4, R // tr),
            in_specs=[pl.BlockSpec((None, None, tr, C), lambda q, i, c_ref: (q, c_ref[0], i, 0)),
                      pl.BlockSpec((None, tr, C), lambda q, i, c_ref: (q, i, 0))] + [ANY] * len(prev),
            out_specs=pl.BlockSpec((None, tr, C), lambda q, i, c_ref: (q, i, 0))),
        out_shape=_sds((4, R, C), g.dtype),
        compiler_params=_params(),
    )(core, g4, got, *prev)
    _LAST[0] = out
    return out


def _rs_chip_start(name, sums, after=None):
    n = len(sums)
    zones = [lax.empty(s.shape, s.dtype) for s in sums]

    def body(ins, si, so, tok):
        send, recv, local = so
        x, y, c = _place()
        mine = _chip_index((x, y))
        for w in range(n):
            pltpu.make_async_copy(ins[w].at[mine], ins[n + w].at[mine], local.at[w]).start()
            for j in (1, 2, 3):
                _remote(ins[w].at[_chip_index(_chip(j))], ins[n + w].at[mine], send, recv, 3 * w + j - 1,
                        (*_chip(j), c)).start()

    sems, arrays, token = _split_call(name, body, list(sums) + zones, [], [3 * n, 3 * n, n], after=after, token=True)
    return (sems, arrays[:n], arrays[n:]), token


def _rs_chip_wait(name, handle, after):
    sems, sums, zones = handle
    n = len(sums)

    def body(ins, si, so, tok):
        send, recv, local = si
        x, y, c = _place()
        mine = _chip_index((x, y))
        for w in range(n):
            for j in (1, 2, 3):
                _remote(ins[w].at[mine], ins[n + w].at[_chip_index(_chip(j))], send, recv, 3 * w + j - 1,
                        (*_chip(j), c)).wait_recv()
        for w in range(n):
            for j in (1, 2, 3):
                _remote(ins[w].at[_chip_index(_chip(j))], ins[n + w].at[mine], send, recv, 3 * w + j - 1,
                        (*_chip(j), c)).wait_send()
            pltpu.make_async_copy(ins[w].at[mine], ins[n + w].at[mine], local.at[w]).wait()

    _, arrays, _ = _split_call(name, body, list(sums) + list(zones), sems, [], after=after)
    return arrays[n:]


def _chip_sum(name, slots, layer, layers, into=None):
    _, R, C = slots.shape
    tr = _tile(R, 256, 16)

    def body(*refs):
        s_ref, o_ref = refs[0], refs[-1]
        total = s_ref[0].astype(F32)
        for k in range(1, 4):
            total = total + s_ref[k].astype(F32)
        o_ref[...] = total

    extra = [] if into is None else [into]
    return _ordered_call(
        body, [slots] + extra, name=name, grid=(R // tr,),
        in_specs=[pl.BlockSpec((4, tr, C), lambda i: (0, i, 0))] + [ANY] * len(extra),
        out_specs=pl.BlockSpec((None, tr, C), lambda i: (layer, i, 0)),
        out_shape=_sds((layers, R, C), F32),
        input_output_aliases={} if into is None else {1: 0},
        compiler_params=_params(),
    )


def _all_reduce_small(pack):
    R, C = pack.shape

    def body(x_ref, o_ref, all_ref, send, recv):
        me = _index(_place())
        all_ref[me] = x_ref[...]
        sends = []
        for j in range(1, N_DEV):
            cp = pltpu.make_async_remote_copy(src_ref=x_ref, dst_ref=all_ref.at[me], send_sem=send.at[j - 1],
                                              recv_sem=recv.at[j - 1], device_id=_peer(j), device_id_type=MESH)
            cp.start()
            sends.append(cp)
        for j in range(1, N_DEV):
            pltpu.make_async_remote_copy(src_ref=x_ref, dst_ref=all_ref.at[_index(_peer(j))], send_sem=send.at[j - 1],
                                         recv_sem=recv.at[j - 1], device_id=_peer(j), device_id_type=MESH).wait_recv()
        for cp in sends:
            cp.wait_send()
        total = all_ref[0]
        for k in range(1, N_DEV):
            total = total + all_ref[k]
        o_ref[...] = total

    return _ordered_call(
        body, [pack], name="all_reduce_small",
        in_specs=[pl.BlockSpec(memory_space=pltpu.VMEM)], out_specs=pl.BlockSpec(memory_space=pltpu.VMEM),
        out_shape=_sds((R, C), F32),
        scratch_shapes=[pltpu.VMEM((N_DEV, R, C), F32), pltpu.SemaphoreType.DMA((N_DEV - 1,)),
                        pltpu.SemaphoreType.DMA((N_DEV - 1,))],
        compiler_params=_params(),
    )


def _cast_layer(name, w, layer):
    _, R, C = w.shape
    tr = _tile(R, 256, 16)

    def body(w_ref, o_ref):
        o_ref[...] = w_ref[...].astype(BF16)

    return _ordered_call(
        body, [w], name=name, grid=(R // tr,),
        in_specs=[pl.BlockSpec((None, tr, C), lambda i: (layer, i, 0))],
        out_specs=pl.BlockSpec((tr, C), lambda i: (i, 0)),
        out_shape=_sds((R, C), BF16),
    )


def _adam_math(w, g, m, v):
    c1 = 1.0 / (1.0 - ADAM_B1 ** ADAM_STEP)
    c2 = 1.0 / (1.0 - ADAM_B2 ** ADAM_STEP)
    nm = ADAM_B1 * m + (1.0 - ADAM_B1) * g
    nv = ADAM_B2 * v + (1.0 - ADAM_B2) * (g * g)
    return -ADAM_LR * ((nm * c1) / (jnp.sqrt(nv * c2) + ADAM_EPS) + ADAM_WD * w), nm, nv


def _sum_adam(name, slots, w, m, v, layer, into=None):
    L, R, C = w.shape
    tr = _tile(R, 128, 16)

    def body(*refs):
        s_ref, w_ref, m_ref, v_ref = refs[:4]
        g_ref, d_ref, nm_ref, nv_ref = refs[-4:]
        g = s_ref[0].astype(F32)
        for k in range(1, 4):
            g = g + s_ref[k].astype(F32)
        g_ref[...] = g
        d_ref[...], nm_ref[...], nv_ref[...] = _adam_math(w_ref[...], g, m_ref[...], v_ref[...])

    spec = pl.BlockSpec((None, tr, C), lambda i: (layer, i, 0))
    extra = [] if into is None else list(into)
    return _ordered_call(
        body, [slots, w, m, v] + extra, name=name, grid=(R // tr,),
        in_specs=[pl.BlockSpec((4, tr, C), lambda i: (0, i, 0)), spec, spec, spec] + [ANY] * len(extra),
        out_specs=[spec] * 4,
        out_shape=[_sds((L, R, C), F32)] * 4,
        input_output_aliases={4 + k: k for k in range(len(extra))},
        compiler_params=_params(),
    )


def _adamw(name, w, g, m, v):
    shape = w.shape
    R, C = shape[-2], shape[-1]
    L = 1
    for s in shape[:-2]:
        L *= s
    w3, g3, m3, v3 = (a.reshape(L, R, C) for a in (w, g, m, v))
    tr = _tile(R, 128, 8)

    def body(w_ref, g_ref, m_ref, v_ref, d_ref, nm_ref, nv_ref):
        d_ref[...], nm_ref[...], nv_ref[...] = _adam_math(w_ref[...], g_ref[...], m_ref[...], v_ref[...])

    spec = pl.BlockSpec((None, tr, C), lambda l, i: (l, i, 0))
    outs = _ordered_call(
        body, [w3, g3, m3, v3], name=name, grid=(L, R // tr),
        in_specs=[spec] * 4, out_specs=[spec] * 3,
        out_shape=[_sds((L, R, C), F32)] * 3,
        compiler_params=_params(),
    )
    return tuple(o.reshape(shape) for o in outs)


def _pad_rows(a, rows):
    return jnp.pad(a, ((0, rows - a.shape[0]), (0, 0)))


def _pad_cols(a, cols):
    return jnp.pad(a, ((0, 0), (0, cols - a.shape[1])))


def _rope_tables(positions):
    q = D_ROPE // 2
    inv_freq = ROPE_THETA ** (-jnp.arange(0, D_ROPE, 2, dtype=F32) / D_ROPE)
    ang = positions.astype(F32)[:, None] * inv_freq
    cos, sin = jnp.cos(ang), jnp.sin(ang)
    z = jnp.zeros_like(cos)
    zz = jnp.zeros((cos.shape[0], 128 - 2 * q), F32)
    c = jnp.concatenate([cos, cos, zz], axis=1)
    s1 = jnp.concatenate([z, sin, zz], axis=1)
    s2 = jnp.concatenate([-sin, z, zz], axis=1)
    return c, s1, s2


def kernel(x, p, positions, ffn_a_norm, ffn_a_w_in, ffn_a_w_out, ffn_b_norm, ffn_b_w_in, ffn_b_w_out, mix_norm, mla_w_in, mla_q_lat_norm, mla_kv_lat_norm, mla_w_uq, mla_w_ukv, mla_q_gain, mla_k_gain, mla_w_o, conv_w_pw1, conv_b_pw1, conv_w_dw, conv_b_dw, conv_ln_g, conv_ln_b, conv_w_pw2, ple_w_proj, ple_norm, ple_gate_norm, ple_w_gate, loss_target, m_ffn_a_norm, m_ffn_a_w_in, m_ffn_a_w_out, m_ffn_b_norm, m_ffn_b_w_in, m_ffn_b_w_out, m_mix_norm, m_mla_w_in, m_mla_q_lat_norm, m_mla_kv_lat_norm, m_mla_w_uq, m_mla_w_ukv, m_mla_q_gain, m_mla_k_gain, m_mla_w_o, m_conv_w_pw1, m_conv_b_pw1, m_conv_w_dw, m_conv_b_dw, m_conv_ln_g, m_conv_ln_b, m_conv_w_pw2, m_ple_w_proj, m_ple_norm, m_ple_gate_norm, m_ple_w_gate, v_ffn_a_norm, v_ffn_a_w_in, v_ffn_a_w_out, v_ffn_b_norm, v_ffn_b_w_in, v_ffn_b_w_out, v_mix_norm, v_mla_w_in, v_mla_q_lat_norm, v_mla_kv_lat_norm, v_mla_w_uq, v_mla_w_ukv, v_mla_q_gain, v_mla_k_gain, v_mla_w_o, v_conv_w_pw1, v_conv_b_pw1, v_conv_w_dw, v_conv_b_dw, v_conv_ln_g, v_conv_ln_b, v_conv_w_pw2, v_ple_w_proj, v_ple_norm, v_ple_gate_norm, v_ple_w_gate):
    weights = dict(ffn_a_norm=ffn_a_norm, ffn_a_w_in=ffn_a_w_in, ffn_a_w_out=ffn_a_w_out, ffn_b_norm=ffn_b_norm,
                   ffn_b_w_in=ffn_b_w_in, ffn_b_w_out=ffn_b_w_out, mix_norm=mix_norm, mla_w_in=mla_w_in,
                   mla_q_lat_norm=mla_q_lat_norm, mla_kv_lat_norm=mla_kv_lat_norm, mla_w_uq=mla_w_uq,
                   mla_w_ukv=mla_w_ukv, mla_q_gain=mla_q_gain, mla_k_gain=mla_k_gain, mla_w_o=mla_w_o,
                   conv_w_pw1=conv_w_pw1, conv_b_pw1=conv_b_pw1, conv_w_dw=conv_w_dw, conv_b_dw=conv_b_dw,
                   conv_ln_g=conv_ln_g, conv_ln_b=conv_ln_b, conv_w_pw2=conv_w_pw2, ple_w_proj=ple_w_proj,
                   ple_norm=ple_norm, ple_gate_norm=ple_gate_norm, ple_w_gate=ple_w_gate)
    moments_m = dict(ffn_a_norm=m_ffn_a_norm, ffn_a_w_in=m_ffn_a_w_in, ffn_a_w_out=m_ffn_a_w_out,
                     ffn_b_norm=m_ffn_b_norm, ffn_b_w_in=m_ffn_b_w_in, ffn_b_w_out=m_ffn_b_w_out,
                     mix_norm=m_mix_norm, mla_w_in=m_mla_w_in, mla_q_lat_norm=m_mla_q_lat_norm,
                     mla_kv_lat_norm=m_mla_kv_lat_norm, mla_w_uq=m_mla_w_uq, mla_w_ukv=m_mla_w_ukv,
                     mla_q_gain=m_mla_q_gain, mla_k_gain=m_mla_k_gain, mla_w_o=m_mla_w_o,
                     conv_w_pw1=m_conv_w_pw1, conv_b_pw1=m_conv_b_pw1, conv_w_dw=m_conv_w_dw,
                     conv_b_dw=m_conv_b_dw, conv_ln_g=m_conv_ln_g, conv_ln_b=m_conv_ln_b, conv_w_pw2=m_conv_w_pw2,
                     ple_w_proj=m_ple_w_proj, ple_norm=m_ple_norm, ple_gate_norm=m_ple_gate_norm,
                     ple_w_gate=m_ple_w_gate)
    moments_v = dict(ffn_a_norm=v_ffn_a_norm, ffn_a_w_in=v_ffn_a_w_in, ffn_a_w_out=v_ffn_a_w_out,
                     ffn_b_norm=v_ffn_b_norm, ffn_b_w_in=v_ffn_b_w_in, ffn_b_w_out=v_ffn_b_w_out,
                     mix_norm=v_mix_norm, mla_w_in=v_mla_w_in, mla_q_lat_norm=v_mla_q_lat_norm,
                     mla_kv_lat_norm=v_mla_kv_lat_norm, mla_w_uq=v_mla_w_uq, mla_w_ukv=v_mla_w_ukv,
                     mla_q_gain=v_mla_q_gain, mla_k_gain=v_mla_k_gain, mla_w_o=v_mla_w_o,
                     conv_w_pw1=v_conv_w_pw1, conv_b_pw1=v_conv_b_pw1, conv_w_dw=v_conv_w_dw,
                     conv_b_dw=v_conv_b_dw, conv_ln_g=v_conv_ln_g, conv_ln_b=v_conv_ln_b, conv_w_pw2=v_conv_w_pw2,
                     ple_w_proj=v_ple_w_proj, ple_norm=v_ple_norm, ple_gate_norm=v_ple_gate_norm,
                     ple_w_gate=v_ple_w_gate)
    order = list(weights.keys())
    _LAST[0] = None

    T, D = x.shape[1], x.shape[2]
    me = _index(_place())
    h0 = x[0]
    target = loss_target[0]
    tabs = _rope_tables(positions[0])
    H = N_HEADS
    hps = H // N_DEV
    QL = mla_q_lat_norm.shape[1]
    Cs = conv_b_dw.shape[1]

    def cast(n, i):
        return _cast_layer(f"cast_{n}{i}", weights[n], i)

    first, tok = _ag_start("ag_start0", [[cast("ffn_a_w_in", 0)]])
    m_in_pad = _pad_cols(mla_w_in[0], mla_w_in.shape[2] - D_ROPE + 128)[None]
    uq_pad = jnp.pad(mla_w_uq[0].reshape(QL, hps, QK_DIM), ((0, 0), (0, 0), (0, HEAD_PAD - QK_DIM)))
    uq_pad = uq_pad.reshape(1, QL, hps * HEAD_PAD)
    conv_small = jnp.concatenate([
        _pad_rows(_pad_cols(conv_b_pw1, 2 * Cs), 8),
        _pad_rows(_pad_cols(conv_w_dw[0], 2 * Cs), 32),
        _pad_rows(_pad_cols(jnp.concatenate([conv_b_dw, conv_ln_g, conv_ln_b], axis=0), 2 * Cs), 8)], axis=0)
    rest, tok = _ag_start("ag_start1", [
        [cast("ffn_a_w_out", 0), _cast_layer("cast_mla_in", m_in_pad, 0), _cast_layer("cast_mla_uq", uq_pad, 0),
         _cast_layer("cast_mla_ukv", mla_w_ukv, 0), _cast_layer("cast_mla_wo", mla_w_o, 0)],
        [cast("ffn_b_w_in", 0), cast("ffn_b_w_out", 0), cast("ple_w_gate", 0), cast("ple_w_proj", 0)],
        [cast("ffn_a_w_in", 1), cast("ffn_a_w_out", 1)],
        [_cast_layer("cast_conv_pw1", conv_w_pw1, 0), _cast_layer("cast_conv_pw2", conv_w_pw2, 0), conv_small],
        [cast("ffn_b_w_in", 1), cast("ffn_b_w_out", 1), cast("ple_w_gate", 1), cast("ple_w_proj", 1)]], after=tok)
    groups = [dict(handle=hd, stage=0, arrays=None) for hd in first + rest]

    def prefetch(gi, after):
        st = groups[gi]
        if st["stage"] == 0:
            st["handle"] = _ag_forward(f"ag{gi}_forward", st["handle"], after)
            st["stage"] = 1

    def fetch(gi, after):
        prefetch(gi, after)
        st = groups[gi]
        if st["stage"] == 1:
            st["arrays"] = _ag_wait(f"ag{gi}_wait", st["handle"], after)
            st["stage"] = 2
        return st["arrays"]

    def getter(gi, k, shape=None, ahead=None):
        def get(after):
            if ahead is not None:
                prefetch(ahead, after)
            a = fetch(gi, after)[k]
            return a if shape is None else a.reshape(shape)
        return get

    def conv_small_params(after):
        small = fetch(4, after)[2]
        return (small[:, 0:1, :],
                jnp.transpose(small[:, 8:40, :Cs], (1, 0, 2)).reshape(32, N_DEV * Cs),
                small[:, 40, :Cs].reshape(1, N_DEV * Cs), small[:, 41, :Cs].reshape(1, N_DEV * Cs),
                small[:, 42, :Cs].reshape(1, N_DEV * Cs))

    rows = (-1, D)
    get_ffn = [dict(a_in=getter(0, 0), a_out=getter(1, 0, rows), b_in=getter(2, 0), b_out=getter(2, 1, rows)),
               dict(a_in=getter(3, 0), a_out=getter(3, 1, rows, ahead=4), b_in=getter(5, 0),
                    b_out=getter(5, 1, rows))]
    get_ple = [dict(proj=getter(2, 3, ahead=3), gate=getter(2, 2, rows)),
               dict(proj=getter(5, 3), gate=getter(5, 2, rows))]
    get_mla = dict(m_in=getter(1, 1, (D, -1)), uq=getter(1, 2), ukv=getter(1, 3), wo=getter(1, 4, rows, ahead=2))
    get_conv = dict(pw1=getter(4, 0), pw2=getter(4, 1, rows, ahead=5), small=conv_small_params)
    gq_pad = _pad_cols(mla_q_gain, HEAD_PAD)
    gk_pad = _pad_cols(mla_k_gain, HEAD_PAD)
    prefetch(0, tok)

    saved = []
    h = h0
    for i in range(2):
        h, s_a = _ffn_fwd(f"ffn_a{i}", h, ffn_a_norm[i:i + 1], get_ffn[i]["a_in"], get_ffn[i]["a_out"])
        if i == 0:
            h, s_m = _mla_fwd(h, mix_norm[0:1], tabs, get_mla, mla_q_lat_norm, mla_kv_lat_norm, gq_pad, gk_pad)
        else:
            h, s_m = _conv_fwd(h, mix_norm[1:2], get_conv)
        h, s_b = _ffn_fwd(f"ffn_b{i}", h, ffn_b_norm[i:i + 1], get_ffn[i]["b_in"], get_ffn[i]["b_out"])
        h, s_p = _ple_fwd(f"ple{i}", h, p[i, 0], get_ple[i]["proj"], ple_norm[i:i + 1], ple_gate_norm[i:i + 1],
                          get_ple[i]["gate"])
        saved.append((s_a, s_m, s_b, s_p))
    W = [dict(a_in=get_ffn[i]["a_in"](None), a_out=get_ffn[i]["a_out"](None), b_in=get_ffn[i]["b_in"](None),
              b_out=get_ffn[i]["b_out"](None), proj=get_ple[i]["proj"](None), gate=get_ple[i]["gate"](None))
         for i in range(2)]
    Wm = {n: g(None) for n, g in get_mla.items()}
    Wc = dict(pw1=get_conv["pw1"](None), pw2=get_conv["pw2"](None))
    _, w_dw_full, _, ln_g_full, ln_b_full = conv_small_params(None)

    dh, dhb, loss_row = _loss_head(h, target)

    G = {}
    small_g = {}
    stacked = ["ffn_a_w_in", "ffn_a_w_out", "ffn_b_w_in", "ffn_b_w_out", "ple_w_proj", "ple_w_gate"]
    row_sharded = {"ffn_a_w_out", "ffn_b_w_out", "ple_w_gate", "mla_w_in", "mla_w_o", "conv_w_pw2"}
    core = lax.axis_index("c").astype(jnp.int32).reshape(1)
    rs_groups = {
        "pb1": [("ple_w_proj", 1), ("ple_w_gate", 1), ("ffn_b_w_in", 1), ("ffn_b_w_out", 1)],
        "c1": [("conv_w_pw1", 0), ("conv_w_pw2", 0)],
        "a1": [("ffn_a_w_in", 1), ("ffn_a_w_out", 1)],
        "pb0": [("ple_w_proj", 0), ("ple_w_gate", 0), ("ffn_b_w_in", 0), ("ffn_b_w_out", 0)],
        "m0": [("mla_w_in", 0), ("mla_w_uq", 0), ("mla_w_ukv", 0), ("mla_w_o", 0)],
        "ao0": [("ffn_a_w_out", 0)],
        "ai0": [("ffn_a_w_in", 0)]}
    rs = {}

    def rs_begin(tag):
        grads = []
        for n, i in rs_groups[tag]:
            g = G[(n, i)]
            grads.append(g.reshape(N_DEV, g.shape[0] // N_DEV, g.shape[1]) if n in row_sharded else g)
        rs[tag] = _rs_pair_start(f"rs_{tag}_pair_start", grads)[0]

    def rs_pairs(tag):
        mine, got = _rs_pair_wait(f"rs_{tag}_pair_wait", rs[tag], None)
        rs[tag] = [_pair_sum(f"pairsum_{n}{i}", a, b, core) for (n, i), a, b in zip(rs_groups[tag], mine, got)]

    def rs_chips(tag):
        rs[tag] = _rs_chip_start(f"rs_{tag}_chip_start", rs[tag])[0]

    def rs_end(tag):
        return dict(zip(rs_groups[tag], _rs_chip_wait(f"rs_{tag}_chip_wait", rs[tag], None)))

    s_a, s_m, s_b, s_p = saved[1]
    dh, dhb, d_pn, d_gn, G[("ple_w_proj", 1)], G[("ple_w_gate", 1)] = _ple_bwd(
        "ple1", s_p, p[1, 0], W[1]["proj"], ple_norm[1:2], ple_gate_norm[1:2], W[1]["gate"], dh, dhb)
    small_g[("ple_norm", 1)], small_g[("ple_gate_norm", 1)] = d_pn, d_gn
    dh, dhb, small_g[("ffn_b_norm", 1)], G[("ffn_b_w_in", 1)], G[("ffn_b_w_out", 1)] = _ffn_bwd(
        "ffn_b1", s_b, ffn_b_norm[1:2], W[1]["b_in"], W[1]["b_out"], dh, dhb)
    rs_begin("pb1")
    (dh, dhb, small_g[("mix_norm", 1)], G[("conv_w_pw1", 0)], d_b_pw1, d_w_dw, d_b_dw, d_ln_g, d_ln_b,
     G[("conv_w_pw2", 0)]) = _conv_bwd(s_m, mix_norm[1:2], Wc, w_dw_full, ln_g_full, ln_b_full, dh, dhb)
    rs_pairs("pb1")
    rs_chips("pb1")
    rs_begin("c1")
    dh, dhb, small_g[("ffn_a_norm", 1)], G[("ffn_a_w_in", 1)], G[("ffn_a_w_out", 1)] = _ffn_bwd(
        "ffn_a1", s_a, ffn_a_norm[1:2], W[1]["a_in"], W[1]["a_out"], dh, dhb)
    rs_pairs("c1")
    rs_chips("c1")
    rs_begin("a1")

    s_a, s_m, s_b, s_p = saved[0]
    dh, dhb, d_pn, d_gn, G[("ple_w_proj", 0)], G[("ple_w_gate", 0)] = _ple_bwd(
        "ple0", s_p, p[0, 0], W[0]["proj"], ple_norm[0:1], ple_gate_norm[0:1], W[0]["gate"], dh, dhb)
    small_g[("ple_norm", 0)], small_g[("ple_gate_norm", 0)] = d_pn, d_gn
    rs_pairs("a1")
    rs_chips("a1")
    dh, dhb, small_g[("ffn_b_norm", 0)], G[("ffn_b_w_in", 0)], G[("ffn_b_w_out", 0)] = _ffn_bwd(
        "ffn_b0", s_b, ffn_b_norm[0:1], W[0]["b_in"], W[0]["b_out"], dh, dhb)
    rs_begin("pb0")

    def in_mla():
        rs_pairs("pb0")
        rs_chips("pb0")

    (dh, dhb, small_g[("mix_norm", 0)], d_qln, d_kvln, d_gq, d_gk,
     G[("mla_w_in", 0)], G[("mla_w_uq", 0)], G[("mla_w_ukv", 0)], G[("mla_w_o", 0)]) = _mla_bwd(
        s_m, mix_norm[0:1], tabs, Wm, mla_q_lat_norm, mla_kv_lat_norm, gq_pad, gk_pad, dh, dhb, hook=in_mla)
    rs_begin("m0")

    def with_dw_out(dw_out):
        G[("ffn_a_w_out", 0)] = dw_out
        rs_pairs("m0")
        rs_chips("m0")
        rs_begin("ao0")

    def with_dw_in(dw_in):
        G[("ffn_a_w_in", 0)] = dw_in
        rs_pairs("ao0")
        rs_chips("ao0")
        rs_begin("ai0")

    dh, dhb, small_g[("ffn_a_norm", 0)], _, _ = _ffn_bwd(
        "ffn_a0", s_a, ffn_a_norm[0:1], W[0]["a_in"], W[0]["a_out"], dh, dhb,
        hooks=(with_dw_out, with_dw_in, lambda: rs_pairs("ai0")))
    grad_x = dh[None]

    def two(name):
        return _pad_rows(jnp.concatenate([small_g[(name, 0)], small_g[(name, 1)]], axis=0), 8)

    misc = jnp.concatenate([_pad_cols(d_qln, D), _pad_cols(d_kvln, D), _pad_cols(d_gq, D), _pad_cols(d_gk, D),
                            _pad_cols(loss_row, D)], axis=0)
    C = N_DEV * Cs
    pack = jnp.concatenate([
        two("ffn_a_norm"), two("ffn_b_norm"), two("mix_norm"), two("ple_norm"), two("ple_gate_norm"),
        _pad_rows(misc, 8),
        _pad_rows(_pad_cols(d_b_pw1.reshape(2, C), D), 8),
        _pad_cols(d_w_dw, D),
        _pad_rows(_pad_cols(jnp.concatenate([d_b_dw, d_ln_g, d_ln_b], axis=0), D), 8)], axis=0)
    red = _all_reduce_small(pack)
    loss = red[44, 0]
    small_grads = dict(
        ffn_a_norm=red[0:2], ffn_b_norm=red[8:10], mix_norm=red[16:18], ple_norm=red[24:26],
        ple_gate_norm=red[32:34],
        mla_q_lat_norm=red[40:41, :QL], mla_kv_lat_norm=red[41:42, :mla_kv_lat_norm.shape[1]],
        mla_q_gain=red[42:43, :QK_DIM], mla_k_gain=red[43:44, :QK_DIM],
        conv_b_pw1=lax.dynamic_slice_in_dim(red[48:50, :C].reshape(1, 2 * C), me * 2 * Cs, 2 * Cs, axis=1),
        conv_w_dw=lax.dynamic_slice_in_dim(red[56:56 + CONV_WIDTH, :C], me * Cs, Cs, axis=1)[None],
        conv_b_dw=lax.dynamic_slice_in_dim(red[88:89, :C], me * Cs, Cs, axis=1),
        conv_ln_g=lax.dynamic_slice_in_dim(red[89:90, :C], me * Cs, Cs, axis=1),
        conv_ln_b=lax.dynamic_slice_in_dim(red[90:91, :C], me * Cs, Cs, axis=1))

    rs_chips("ai0")
    done = {}

    def plain_adamw(n, g):
        done[n] = (g,) + _adamw(f"adamw_{n}", weights[n], g, moments_m[n], moments_v[n])

    def slot_adamw(n, layer, slots):
        done[n] = tuple(_sum_adam(f"adamw_{n}{layer}", slots, weights[n], moments_m[n], moments_v[n], layer,
                                  into=done.get(n)))

    for n, g in small_grads.items():
        plain_adamw(n, g)
    for tag in ("pb1", "c1", "a1", "pb0"):
        slots = rs_end(tag)
        for n, i in rs_groups[tag]:
            slot_adamw(n, i, slots[(n, i)])
    slots = rs_end("m0")
    slot_adamw("mla_w_ukv", 0, slots[("mla_w_ukv", 0)])
    slot_adamw("mla_w_o", 0, slots[("mla_w_o", 0)])
    g_in = _chip_sum("chipsum_mla_w_in", slots[("mla_w_in", 0)], 0, 1)
    plain_adamw("mla_w_in", g_in[:, :, :mla_w_in.shape[2]])
    g_uq = _chip_sum("chipsum_mla_w_uq", slots[("mla_w_uq", 0)], 0, 1)
    plain_adamw("mla_w_uq", g_uq.reshape(1, QL, hps, HEAD_PAD)[..., :QK_DIM].reshape(mla_w_uq.shape))
    for tag in ("ao0", "ai0"):
        slots = rs_end(tag)
        for n, i in rs_groups[tag]:
            slot_adamw(n, i, slots[(n, i)])
    grads, deltas, new_m, new_v = ({n: done[n][k] for n in order} for k in range(4))

    return (loss, grad_x, *[grads[n] for n in order], *[deltas[n] for n in order],
            *[new_m[n] for n in order], *[new_v[n] for n in order])
```

```python
import functools

import jax
import jax.numpy as jnp
from jax import lax
from jax.experimental import pallas as pl
from jax.experimental.pallas import tpu as pltpu

F32 = jnp.float32
BF16 = jnp.bfloat16
MESH = pl.DeviceIdType.MESH
ANY = pl.BlockSpec(memory_space=pl.ANY)

N_DEV = 8
N_HEADS = 16
D_NOPE = 128
D_ROPE = 64
D_V = 128
QK_DIM = D_NOPE + D_ROPE
HEAD_PAD = 256
ROPE_THETA = 10000.0
CONV_WIDTH = 31
CONV_PAD = 32
FFN_RES = 0.5
EPS = 1e-6
ADAM_LR = 0.001
ADAM_B1 = 0.9
ADAM_B2 = 0.999
ADAM_EPS = 1e-08
ADAM_WD = 0.01
ADAM_STEP = 10
VMEM_LIMIT = 56 * 1024 * 1024


def _sds(shape, dtype):
    return jax.ShapeDtypeStruct(tuple(int(s) for s in shape), dtype)


def _tile(n, pref, mult=128):
    if n <= pref:
        return n
    t = (pref // mult) * mult
    while t >= mult:
        if n % t == 0:
            return t
        t -= mult
    return n


def _params():
    return pltpu.CompilerParams(vmem_limit_bytes=VMEM_LIMIT)


_LAST = [None]


def _ordered_call(body, operands, chain_out=0, **kw):
    operands = list(operands)
    n_in = len(operands)
    if _LAST[0] is not None and not any(op is _LAST[0] for op in operands):
        inner = body

        def body(*refs):
            inner(*refs[:n_in], *refs[n_in + 1:])

        kw = dict(kw, in_specs=list(kw["in_specs"]) + [ANY])
        operands.append(_LAST[0])
    out = pl.pallas_call(body, **kw)(*operands)
    _LAST[0] = out[chain_out] if isinstance(out, (list, tuple)) else out
    return out


def _matmul(name, grid, ops, terms, dims, acc_shapes, outs, epilogue, extras=()):
    nk = grid[2]
    n_ops, n_ex, n_out, n_acc = len(ops), len(extras), len(outs), len(acc_shapes)

    def body(*refs):
        op_refs = refs[:n_ops]
        ex_refs = refs[n_ops:n_ops + n_ex]
        out_refs = refs[n_ops + n_ex:n_ops + n_ex + n_out]
        acc_refs = refs[n_ops + n_ex + n_out:]
        vals = {}

        def opval(i):
            if i not in vals:
                v = op_refs[i][...]
                vals[i] = v if v.dtype == BF16 else v.astype(BF16)
            return vals[i]

        parts = [None] * n_acc
        for ai, li, ri in terms:
            d = lax.dot_general(opval(li), opval(ri), (dims, ((), ())), preferred_element_type=F32)
            parts[ai] = d if parts[ai] is None else parts[ai] + d
        if nk == 1:
            epilogue(parts, ex_refs, out_refs)
            return
        k = pl.program_id(2)

        @pl.when(k == 0)
        def _():
            for a_ref, p in zip(acc_refs, parts):
                a_ref[...] = p

        @pl.when(k > 0)
        def _():
            for a_ref, p in zip(acc_refs, parts):
                a_ref[...] += p

        @pl.when(k == nk - 1)
        def _():
            epilogue([a[...] for a in acc_refs], ex_refs, out_refs)

    scratch = [pltpu.VMEM(s, F32) for s in acc_shapes] if nk > 1 else []
    return _ordered_call(
        body, [a for a, _ in ops] + [a for a, _ in extras], name=name, grid=grid,
        in_specs=[s for _, s in ops] + [s for _, s in extras],
        out_specs=[s for _, s in outs],
        out_shape=[o for o, _ in outs],
        scratch_shapes=scratch,
        compiler_params=_params(),
    )


NN = ((1,), (0,))
NT = ((1,), (1,))
TN = ((0,), (0,))


def _store(i=0):
    def ep(accs, ex, outs):
        outs[0][...] = accs[0].astype(outs[0].dtype)
    return ep


def _mm_nn(name, a, b, out_dtype, tm=512, tn=1024, tk=2048, res=None, scale=1.0):
    M, K = a.shape
    N = b.shape[1]
    tm, tn, tk = _tile(M, tm, 16), _tile(N, tn), _tile(K, tk)
    spec = pl.BlockSpec((tm, tn), lambda i, j, k: (i, j))

    def ep(accs, ex, outs):
        v = accs[0] if scale == 1.0 else accs[0] * scale
        outs[0][...] = (v if res is None else ex[0][...] + v).astype(outs[0].dtype)

    return _matmul(name, (M // tm, N // tn, K // tk),
                   [(a, pl.BlockSpec((tm, tk), lambda i, j, k: (i, k))),
                    (b, pl.BlockSpec((tk, tn), lambda i, j, k: (k, j)))],
                   [(0, 0, 1)], NN, [(tm, tn)], [(_sds((M, N), out_dtype), spec)], ep,
                   [] if res is None else [(res, spec)])


def _mm_nt(name, a, b, out_dtype, tm=512, tn=1024, tk=2048, epilogue=None, extras=()):
    M, K = a.shape
    N = b.shape[0]
    tm, tn, tk = _tile(M, tm, 16), _tile(N, tn), _tile(K, tk)
    outs = [(_sds((M, N), out_dtype), pl.BlockSpec((tm, tn), lambda i, j, k: (i, j)))]
    return _matmul(name, (M // tm, N // tn, K // tk),
                   [(a, pl.BlockSpec((tm, tk), lambda i, j, k: (i, k))),
                    (b, pl.BlockSpec((tn, tk), lambda i, j, k: (j, k)))],
                   [(0, 0, 1)], NT, [(tm, tn)], outs, epilogue or _store(), extras)


def _mm_tn(name, a, b, out_dtype, tm=512, tn=1024, tk=2048, scale=None):
    T, M = a.shape
    N = b.shape[1]
    tm, tn, tk = _tile(M, tm), _tile(N, tn), _tile(T, tk)

    def ep(accs, ex, outs):
        v = accs[0] if scale is None else accs[0] * scale
        outs[0][...] = v.astype(outs[0].dtype)

    outs = [(_sds((M, N), out_dtype), pl.BlockSpec((tm, tn), lambda i, j, k: (i, j)))]
    return _matmul(name, (M // tm, N // tn, T // tk),
                   [(a, pl.BlockSpec((tk, tm), lambda i, j, k: (k, i))),
                    (b, pl.BlockSpec((tk, tn), lambda i, j, k: (k, j)))],
                   [(0, 0, 1)], TN, [(tm, tn)], outs, ep)[0]


def _mm_nn_sm(name, a, w, out_dtype, tm=512, tk=2048, epilogue=None, extras=()):
    M, K = a.shape
    S, _, Ns = w.shape
    tm, tk = _tile(M, tm, 16), _tile(K, tk)
    outs = [(_sds((M, S * Ns), out_dtype), pl.BlockSpec((tm, Ns), lambda j, i, k: (i, j)))]
    return _matmul(name, (S, M // tm, K // tk),
                   [(a, pl.BlockSpec((tm, tk), lambda j, i, k: (i, k))),
                    (w, pl.BlockSpec((None, tk, Ns), lambda j, i, k: (j, k, 0)))],
                   [(0, 0, 1)], NN, [(tm, Ns)], outs, epilogue or _store(), extras)[0]


def _mm_nt_sm(name, a, w, out_dtype, tm=1024, tn=1024):
    M = a.shape[0]
    S, K, Ns = w.shape
    tm, tn = _tile(M, tm, 16), _tile(K, tn)
    outs = [(_sds((M, K), out_dtype), pl.BlockSpec((tm, tn), lambda i, n, j: (i, n)))]
    return _matmul(name, (M // tm, K // tn, S),
                   [(a, pl.BlockSpec((tm, Ns), lambda i, n, j: (i, j))),
                    (w, pl.BlockSpec((None, tn, Ns), lambda i, n, j: (j, n, 0)))],
                   [(0, 0, 1)], NT, [(tm, tn)], outs, _store())[0]


def _mm_tn_sm(name, a, b, S, out_dtype, tm=1024, tk=2048):
    T, M = a.shape
    Ns = b.shape[1] // S
    tm, tk = _tile(M, tm), _tile(T, tk)
    outs = [(_sds((S, M, Ns), out_dtype), pl.BlockSpec((None, tm, Ns), lambda j, i, k: (j, i, 0)))]
    return _matmul(name, (S, M // tm, T // tk),
                   [(a, pl.BlockSpec((tk, tm), lambda j, i, k: (k, i))),
                    (b, pl.BlockSpec((tk, Ns), lambda j, i, k: (k, j)))],
                   [(0, 0, 1)], TN, [(tm, Ns)], outs, _store())[0]


def _row_spec(shape, axis, tm):
    block = tuple(tm if d == axis else s for d, s in enumerate(shape))
    nd = len(shape)

    def imap(i):
        return tuple(i if d == axis else 0 for d in range(nd))
    return pl.BlockSpec(block, imap)


def _full_spec(shape):
    nd = len(shape)
    return pl.BlockSpec(tuple(shape), lambda i: (0,) * nd)


def _rowwise(name, fn, T, tm, rows, consts, outs, accs=()):
    tm = _tile(T, tm, 16)
    n_in = len(rows) + len(consts)
    n_out = len(outs)

    def body(*refs):
        in_refs = refs[:n_in]
        out_refs = refs[n_in:n_in + n_out]
        acc_refs = refs[n_in + n_out:]
        i = pl.program_id(0)

        def acc_add(ai, val):
            @pl.when(i == 0)
            def _():
                acc_refs[ai][...] = val

            @pl.when(i > 0)
            def _():
                acc_refs[ai][...] += val

        fn(in_refs, out_refs, acc_add)

    return _ordered_call(
        body, [a for a, _ in rows] + list(consts), name=name, grid=(T // tm,),
        in_specs=[_row_spec(a.shape, ax, tm) for a, ax in rows] + [_full_spec(c.shape) for c in consts],
        out_specs=[_row_spec(s, ax, tm) for s, _, ax in outs] + [_full_spec(s) for s in accs],
        out_shape=[_sds(s, d) for s, d, _ in outs] + [_sds(s, F32) for s in accs],
        compiler_params=_params(),
    )


def _rms(x, g, n=None):
    n = x.shape[-1] if n is None else n
    return x * lax.rsqrt(jnp.sum(x * x, axis=-1, keepdims=True) * (1.0 / n) + EPS) * g


def _norm_fwd(name, h, gain):
    T, D = h.shape

    def fn(ins, outs, acc):
        outs[0][...] = _rms(ins[0][...], ins[1][...]).astype(BF16)

    return _rowwise(name, fn, T, 256, [(h, 0)], [gain], [((T, D), BF16, 0)])[0]


def _norm_bwd(name, h, gain, dhn, dh_res):
    T, D = h.shape

    def fn(ins, outs, acc):
        _, vjp = jax.vjp(_rms, ins[0][...], ins[3][...])
        dh, dg = vjp(ins[1][...])
        dh = dh + ins[2][...]
        outs[0][...] = dh
        outs[1][...] = dh.astype(BF16)
        acc(0, dg)

    return _rowwise(name, fn, T, 256, [(h, 0), (dhn, 0), (dh_res, 0)], [gain],
                    [((T, D), F32, 0), ((T, D), BF16, 0)], [(1, D)])


def _ffn_fwd(tag, h, gain, get_in, get_out):
    T, D = h.shape
    hn = _norm_fwd(tag + "_norm", h, gain)
    w_in = get_in(hn)
    S, _, Ns = w_in.shape
    half = S // 2
    F = half * Ns
    tm, tk = _tile(T, 256, 16), _tile(D, 2048)

    def ep(accs, ex, outs):
        g, u = accs
        act = g * jax.nn.sigmoid(g) * u
        outs[0][0] = g.astype(BF16)
        outs[0][1] = u.astype(BF16)
        outs[1][...] = act.astype(BF16)

    gu, act = _matmul(
        tag + "_in", (half, T // tm, D // tk),
        [(hn, pl.BlockSpec((tm, tk), lambda j, i, k: (i, k))),
         (w_in, pl.BlockSpec((None, tk, Ns), lambda j, i, k: (j, k, 0))),
         (w_in, pl.BlockSpec((None, tk, Ns), lambda j, i, k: (j + half, k, 0)))],
        [(0, 0, 1), (1, 0, 2)], NN, [(tm, Ns), (tm, Ns)],
        [(_sds((2, T, F), BF16), pl.BlockSpec((2, tm, Ns), lambda j, i, k: (0, i, j))),
         (_sds((T, F), BF16), pl.BlockSpec((tm, Ns), lambda j, i, k: (i, j)))],
        ep)

    w_out = get_out(act)
    h_new = _mm_nn(tag + "_out", act, w_out, F32, tm=512, tn=512, tk=F, res=h, scale=FFN_RES)[0]
    return h_new, (h, hn, gu, act)


def _no_hook(*_):
    return None


def _ffn_bwd(tag, saved, gain, w_in, w_out, dh, dhb, hooks=(_no_hook, _no_hook, _no_hook)):
    h, hn, gu, act = saved
    T, D = h.shape
    S, _, Ns = w_in.shape
    half = S // 2
    F = half * Ns
    tm, tk = _tile(T, 512, 16), _tile(D, 2048)

    def ep(accs, ex, outs):
        dact = FFN_RES * accs[0]
        g = ex[0][0].astype(F32)
        u = ex[0][1].astype(F32)
        sg = jax.nn.sigmoid(g)
        outs[0][0] = (dact * u * (sg * (1.0 + g * (1.0 - sg)))).astype(BF16)
        outs[0][1] = (dact * (g * sg)).astype(BF16)

    gu_spec = pl.BlockSpec((2, tm, Ns), lambda j, i, k: (0, i, j))
    dgu = _matmul(
        tag + "_dact", (half, T // tm, D // tk),
        [(dhb, pl.BlockSpec((tm, tk), lambda j, i, k: (i, k))),
         (w_out, pl.BlockSpec((Ns, tk), lambda j, i, k: (j, k)))],
        [(0, 0, 1)], NT, [(tm, Ns)],
        [(_sds((2, T, F), BF16), gu_spec)], ep, extras=[(gu, gu_spec)])[0]

    dw_out = _mm_tn(tag + "_dwout", act, dhb, BF16, tm=512, tn=2048, scale=FFN_RES)
    hooks[0](dw_out)

    tm = _tile(T, 1024, 16)
    tkd, tt = _tile(D, 512), _tile(T, 2048)
    dw_in = _matmul(
        tag + "_dwin", (S, D // tkd, T // tt),
        [(hn, pl.BlockSpec((tt, tkd), lambda j, i, k: (k, i))),
         (dgu, pl.BlockSpec((None, tt, Ns), lambda j, i, k: (j // half, k, j % half)))],
        [(0, 0, 1)], TN, [(tkd, Ns)],
        [(_sds((S, D, Ns), BF16), pl.BlockSpec((None, tkd, Ns), lambda j, i, k: (j, i, 0)))], _store())[0]
    hooks[1](dw_in)

    tn = _tile(D, 1024)
    dhn = _matmul(
        tag + "_dhn", (T // tm, D // tn, half),
        [(dgu, pl.BlockSpec((None, tm, Ns), lambda i, n, j: (0, i, j))),
         (dgu, pl.BlockSpec((None, tm, Ns), lambda i, n, j: (1, i, j))),
         (w_in, pl.BlockSpec((None, tn, Ns), lambda i, n, j: (j, n, 0))),
         (w_in, pl.BlockSpec((None, tn, Ns), lambda i, n, j: (j + half, n, 0)))],
        [(0, 0, 2), (0, 1, 3)], NT, [(tm, tn)],
        [(_sds((T, D), F32), pl.BlockSpec((tm, tn), lambda i, n, j: (i, n)))], _store())[0]
    hooks[2]()

    dh_in, dh_in_b, dgain = _norm_bwd(tag + "_dnorm", h, gain, dhn, dh)
    return dh_in, dh_in_b, dgain, dw_in, dw_out


def _ple_fwd(tag, h, p, get_proj, ple_norm, gate_norm, get_gate):
    T, D = h.shape
    e_raw = _mm_nn_sm(tag + "_proj", p, get_proj(h), F32)
    hn = _norm_fwd(tag + "_norm", h, gate_norm)
    gate_raw = _mm_nn(tag + "_gate", hn, get_gate(hn), F32)[0]

    def fn(ins, outs, acc):
        e = _rms(ins[1][...], ins[3][...])
        outs[0][...] = ins[0][...] + e * jax.nn.sigmoid(ins[2][...])

    h_new = _rowwise(tag + "_mix", fn, T, 256, [(h, 0), (e_raw, 0), (gate_raw, 0)], [ple_norm],
                     [((T, D), F32, 0)])[0]
    return h_new, (h, hn, e_raw, gate_raw)


def _ple_bwd(tag, saved, p, w_proj, ple_norm, gate_norm, w_gate, dh, dhb):
    h, hn, e_raw, gate_raw = saved
    T, D = h.shape
    S = w_proj.shape[0]

    def fn(ins, outs, acc):
        def f(e_raw_, gate_raw_, g_):
            return _rms(e_raw_, g_) * jax.nn.sigmoid(gate_raw_)
        _, vjp = jax.vjp(f, ins[0][...], ins[1][...], ins[3][...])
        de, dgate, dg = vjp(ins[2][...])
        outs[0][...] = de.astype(BF16)
        outs[1][...] = dgate.astype(BF16)
        acc(0, dg)

    de, dgate, d_ple_norm = _rowwise(tag + "_dmix", fn, T, 256, [(e_raw, 0), (gate_raw, 0), (dh, 0)], [ple_norm],
                                     [((T, D), BF16, 0), ((T, D), BF16, 0)], [(1, D)])
    dw_proj = _mm_tn_sm(tag + "_dwproj", p, de, S, BF16)
    dw_gate = _mm_tn(tag + "_dwgate", hn, dgate, BF16)
    dhn = _mm_nt(tag + "_dhn", dgate, w_gate, F32)[0]
    dh_in, dh_in_b, d_gate_norm = _norm_bwd(tag + "_dnorm", h, gate_norm, dhn, dh)
    return dh_in, dh_in_b, d_ple_norm, d_gate_norm, dw_proj, dw_gate


def _rope(t, c, s1, s2):
    q = D_ROPE // 2
    return t * c + pltpu.roll(t, q, 1) * s1 + pltpu.roll(t, 128 - q, 1) * s2


def _rope_t(d, c, s1, s2):
    q = D_ROPE // 2
    return d * c + pltpu.roll(d * s1, 128 - q, 1) + pltpu.roll(d * s2, q, 1)


def _head_norm(lo, hi, g_lo, g_hi):
    ms = (jnp.sum(lo * lo, axis=-1, keepdims=True) + jnp.sum(hi * hi, axis=-1, keepdims=True)) * (1.0 / QK_DIM)
    inv = lax.rsqrt(ms + EPS)
    return lo * inv * g_lo, hi * inv * g_hi


def _qk_prep(qraw, kvraw, lat, tabs, gq, gk, H):
    T = qraw.shape[0]
    koff = lat.shape[1] - 128

    def fn(ins, outs, acc):
        q_ref, kv_ref, lat_ref, c_ref, s1_ref, s2_ref, gq_ref, gk_ref = ins
        c, s1, s2 = c_ref[...], s1_ref[...], s2_ref[...]
        kr = lat_ref[:, koff:koff + 128]
        for hd in range(H):
            o = hd * HEAD_PAD
            lo, hi = _head_norm(q_ref[:, o:o + 128], q_ref[:, o + 128:o + 256], gq_ref[:, 0:128], gq_ref[:, 128:256])
            outs[0][hd, :, 0:128] = lo.astype(BF16)
            outs[0][hd, :, 128:256] = _rope(hi, c, s1, s2).astype(BF16)
            lo, hi = _head_norm(kv_ref[:, o:o + 128], kr, gk_ref[:, 0:128], gk_ref[:, 128:256])
            outs[1][hd, :, 0:128] = lo.astype(BF16)
            outs[1][hd, :, 128:256] = _rope(hi, c, s1, s2).astype(BF16)
            outs[2][hd] = kv_ref[:, o + 128:o + 256].astype(BF16)

    return _rowwise("mla_qkprep", fn, T, 256, [(qraw, 0), (kvraw, 0), (lat, 0)] + [(t, 0) for t in tabs], [gq, gk],
                    [((H, T, HEAD_PAD), BF16, 1), ((H, T, HEAD_PAD), BF16, 1), ((H, T, D_V), BF16, 1)])


def _qk_prep_bwd(qraw, kvraw, lat, tabs, gq, gk, dQ, dK, dV, H):
    T = qraw.shape[0]
    koff = lat.shape[1] - 128

    def fn(ins, outs, acc):
        q_ref, kv_ref, lat_ref, c_ref, s1_ref, s2_ref, dq_ref, dk_ref, dv_ref, gq_ref, gk_ref = ins
        c, s1, s2 = c_ref[...], s1_ref[...], s2_ref[...]
        kr = lat_ref[:, koff:koff + 128]
        dkr = jnp.zeros_like(kr)
        dg = [None] * 4
        for hd in range(H):
            o = hd * HEAD_PAD
            _, vjp = jax.vjp(_head_norm, q_ref[:, o:o + 128], q_ref[:, o + 128:o + 256],
                             gq_ref[:, 0:128], gq_ref[:, 128:256])
            dlo, dhi, dg0, dg1 = vjp((dq_ref[hd, :, 0:128], _rope_t(dq_ref[hd, :, 128:256], c, s1, s2)))
            outs[0][:, o:o + 128] = dlo.astype(BF16)
            outs[0][:, o + 128:o + 256] = dhi.astype(BF16)
            _, vjp = jax.vjp(_head_norm, kv_ref[:, o:o + 128], kr, gk_ref[:, 0:128], gk_ref[:, 128:256])
            dlo, dhi, dg2, dg3 = vjp((dk_ref[hd, :, 0:128], _rope_t(dk_ref[hd, :, 128:256], c, s1, s2)))
            outs[1][:, o:o + 128] = dlo.astype(BF16)
            outs[1][:, o + 128:o + 256] = dv_ref[hd].astype(BF16)
            dkr = dkr + dhi
            for n, v in enumerate((dg0, dg1, dg2, dg3)):
                dg[n] = v if dg[n] is None else dg[n] + v
        outs[2][...] = dkr
        for n in range(4):
            acc(n, dg[n])

    W = H * HEAD_PAD
    return _rowwise("mla_dqkprep", fn, T, 128,
                    [(qraw, 0), (kvraw, 0), (lat, 0)] + [(t, 0) for t in tabs] + [(dQ, 1), (dK, 1), (dV, 1)], [gq, gk],
                    [((T, W), BF16, 0), ((T, W), BF16, 0), ((T, 128), F32, 0)], [(1, 128)] * 4)


def _attn_probs(q, k, c, tq):
    nk = k.shape[0]
    s = lax.dot_general(q, k, (NT, ((), ())), preferred_element_type=F32) * (QK_DIM ** -0.5)
    row = c * tq + lax.broadcasted_iota(jnp.int32, (tq, nk), 0)
    col = lax.broadcasted_iota(jnp.int32, (tq, nk), 1)
    s = jnp.where(col <= row, s, -jnp.inf)
    p = jnp.exp(s - jnp.max(s, axis=-1, keepdims=True))
    return p / jnp.sum(p, axis=-1, keepdims=True)


def _per_query_block(nq, fn):
    i = pl.program_id(1)
    for c in range(nq):
        pl.when(i == c)(functools.partial(fn, c))


def _attn_fwd(Q, K, V):
    H, T, _ = Q.shape
    tq = _tile(T, 256)

    def body(q_ref, k_ref, v_ref, o_ref):
        def block(c):
            nk = (c + 1) * tq
            p = _attn_probs(q_ref[...], k_ref[0:nk, :], c, tq)
            o_ref[...] = jnp.dot(p.astype(BF16), v_ref[0:nk, :], preferred_element_type=F32).astype(BF16)

        _per_query_block(T // tq, block)

    return _ordered_call(
        body, [Q, K, V], name="mla_attn", grid=(H, T // tq),
        in_specs=[pl.BlockSpec((None, tq, HEAD_PAD), lambda h, i: (h, i, 0)),
                  pl.BlockSpec((None, T, HEAD_PAD), lambda h, i: (h, 0, 0)),
                  pl.BlockSpec((None, T, D_V), lambda h, i: (h, 0, 0))],
        out_specs=pl.BlockSpec((tq, D_V), lambda h, i: (i, h)),
        out_shape=_sds((T, H * D_V), BF16),
        compiler_params=_params(),
    )


def _attn_bwd(Q, K, V, dO):
    H, T, _ = Q.shape
    tq = _tile(T, 256)

    def body(q_ref, k_ref, v_ref, do_ref, dq_ref, dk_ref, dv_ref):
        @pl.when(pl.program_id(1) == 0)
        def _():
            dk_ref[...] = jnp.zeros_like(dk_ref)
            dv_ref[...] = jnp.zeros_like(dv_ref)

        def block(c):
            nk = (c + 1) * tq
            q, k, do = q_ref[...], k_ref[0:nk, :], do_ref[...]
            p = _attn_probs(q, k, c, tq)
            dv_ref[0:nk, :] += lax.dot_general(p.astype(BF16), do, (TN, ((), ())), preferred_element_type=F32)
            dp = lax.dot_general(do, v_ref[0:nk, :], (NT, ((), ())), preferred_element_type=F32)
            ds = p * (dp - jnp.sum(p * dp, axis=-1, keepdims=True)) * (QK_DIM ** -0.5)
            dsb = ds.astype(BF16)
            dq_ref[...] = jnp.dot(dsb, k, preferred_element_type=F32)
            dk_ref[0:nk, :] += lax.dot_general(dsb, q, (TN, ((), ())), preferred_element_type=F32)

        _per_query_block(T // tq, block)

    return _ordered_call(
        body, [Q, K, V, dO], name="mla_dattn", grid=(H, T // tq),
        in_specs=[pl.BlockSpec((None, tq, HEAD_PAD), lambda h, i: (h, i, 0)),
                  pl.BlockSpec((None, T, HEAD_PAD), lambda h, i: (h, 0, 0)),
                  pl.BlockSpec((None, T, D_V), lambda h, i: (h, 0, 0)),
                  pl.BlockSpec((tq, D_V), lambda h, i: (i, h))],
        out_specs=[pl.BlockSpec((None, tq, HEAD_PAD), lambda h, i: (h, i, 0)),
                   pl.BlockSpec((None, T, HEAD_PAD), lambda h, i: (h, 0, 0)),
                   pl.BlockSpec((None, T, D_V), lambda h, i: (h, 0, 0))],
        out_shape=[_sds((H, T, HEAD_PAD), F32), _sds((H, T, HEAD_PAD), F32), _sds((H, T, D_V), F32)],
        compiler_params=_params(),
    )


def _mla_fwd(h, gain, tabs, get, q_lat_norm, kv_lat_norm, gq, gk):
    T, D = h.shape
    QL, KL = q_lat_norm.shape[1], kv_lat_norm.shape[1]
    hn = _norm_fwd("mla_norm", h, gain)
    w = dict(m_in=get["m_in"](hn))
    lat = _mm_nn("mla_lat", hn, w["m_in"], F32, tn=w["m_in"].shape[1])[0]

    def fn(ins, outs, acc):
        outs[0][...] = _rms(ins[0][:, 0:QL], ins[1][...]).astype(BF16)
        outs[1][...] = _rms(ins[0][:, QL:QL + KL], ins[2][...]).astype(BF16)

    cq, ckv = _rowwise("mla_latnorm", fn, T, 256, [(lat, 0)], [q_lat_norm, kv_lat_norm],
                       [((T, QL), BF16, 0), ((T, KL), BF16, 0)])
    w["uq"], w["ukv"] = get["uq"](cq), get["ukv"](ckv)
    H = w["uq"].shape[0] * w["uq"].shape[2] // HEAD_PAD
    qraw = _mm_nn_sm("mla_uq", cq, w["uq"], F32)
    kvraw = _mm_nn_sm("mla_ukv", ckv, w["ukv"], F32)
    Q, K, V = _qk_prep(qraw, kvraw, lat, tabs, gq, gk, H)
    O = _attn_fwd(Q, K, V)
    w["wo"] = get["wo"](O)

    h_new = _mm_nn("mla_out", O, w["wo"], F32, res=h)[0]
    return h_new, (h, hn, lat, cq, ckv, qraw, kvraw, Q, K, V, O)


def _mla_bwd(saved, gain, tabs, w, q_lat_norm, kv_lat_norm, gq, gk, dh, dhb, hook=_no_hook):
    h, hn, lat, cq, ckv, qraw, kvraw, Q, K, V, O = saved
    T, D = h.shape
    H = Q.shape[0]
    S = w["uq"].shape[0]
    QL, KL = q_lat_norm.shape[1], kv_lat_norm.shape[1]
    dO = _mm_nt("mla_dO", dhb, w["wo"], BF16)[0]
    dwo = _mm_tn("mla_dwo", O, dhb, BF16)
    hook()
    dQ, dK, dV = _attn_bwd(Q, K, V, dO)
    dqraw, dkvraw, dkr, dgq0, dgq1, dgk0, dgk1 = _qk_prep_bwd(qraw, kvraw, lat, tabs, gq, gk, dQ, dK, dV, H)
    dcq = _mm_nt_sm("mla_dcq", dqraw, w["uq"], F32)
    dckv = _mm_nt_sm("mla_dckv", dkvraw, w["ukv"], F32)
    dwuq = _mm_tn_sm("mla_dwuq", cq, dqraw, S, BF16)
    dwukv = _mm_tn_sm("mla_dwukv", ckv, dkvraw, S, BF16)

    def fn(ins, outs, acc):
        _, vjp = jax.vjp(_rms, ins[0][:, 0:QL], ins[4][...])
        d, dgq_ = vjp(ins[1][...])
        outs[0][:, 0:QL] = d.astype(BF16)
        _, vjp = jax.vjp(_rms, ins[0][:, QL:QL + KL], ins[5][...])
        d, dgkv_ = vjp(ins[2][...])
        outs[0][:, QL:QL + KL] = d.astype(BF16)
        outs[0][:, QL + KL:QL + KL + 128] = ins[3][...].astype(BF16)
        acc(0, dgq_)
        acc(1, dgkv_)

    dlat, d_qln, d_kvln = _rowwise("mla_dlatnorm", fn, T, 256, [(lat, 0), (dcq, 0), (dckv, 0), (dkr, 0)],
                                   [q_lat_norm, kv_lat_norm], [(lat.shape, BF16, 0)], [(1, QL), (1, KL)])
    dhn = _mm_nt("mla_dhn", dlat, w["m_in"], F32, tk=lat.shape[1])[0]
    dw_min = _mm_tn("mla_dwin", hn, dlat, BF16, tn=lat.shape[1])
    dh_in, dh_in_b, dgain = _norm_bwd("mla_dnorm", h, gain, dhn, dh)
    d_gq = jnp.concatenate([dgq0, dgq1], axis=1)[:, :QK_DIM]
    d_gk = jnp.concatenate([dgk0, dgk1], axis=1)[:, :QK_DIM]
    return dh_in, dh_in_b, dgain, d_qln, d_kvln, d_gq, d_gk, dw_min, dwuq, dwukv, dwo


def _conv_rows(T):
    return _tile(T, 128, 8)


def _dwconv_fwd(u, w_dw, b_dw):
    T, C = u.shape
    tc, R = _tile(C, 256), _conv_rows(T)
    off = CONV_PAD - (CONV_WIDTH - 1)

    def body(u_ref, w_ref, b_ref, y_ref, pad_ref):
        pad_ref[0:CONV_PAD, :] = jnp.zeros((CONV_PAD, tc), F32)
        pad_ref[CONV_PAD:CONV_PAD + T, :] = u_ref[...]
        for r in range(T // R):
            acc = jnp.broadcast_to(b_ref[...], (R, tc))
            for j in range(CONV_WIDTH):
                acc = acc + w_ref[j:j + 1, :] * pad_ref[r * R + off + j:r * R + off + j + R, :]
            y_ref[r * R:(r + 1) * R, :] = acc

    return _ordered_call(
        body, [u, w_dw, b_dw], name="conv_dw", grid=(C // tc,),
        in_specs=[pl.BlockSpec((T, tc), lambda c: (0, c)), pl.BlockSpec((32, tc), lambda c: (0, c)),
                  pl.BlockSpec((1, tc), lambda c: (0, c))],
        out_specs=pl.BlockSpec((T, tc), lambda c: (0, c)),
        out_shape=_sds((T, C), F32),
        scratch_shapes=[pltpu.VMEM((T + CONV_PAD, tc), F32)],
        compiler_params=_params(),
    )


def _dwconv_bwd(u, w_dw, dy):
    T, C = u.shape
    tc, R = _tile(C, 256), _conv_rows(T)
    off = CONV_PAD - (CONV_WIDTH - 1)

    def body(u_ref, w_ref, dy_ref, du_ref, dw_ref, db_ref, upad_ref, dpad_ref):
        upad_ref[0:CONV_PAD, :] = jnp.zeros((CONV_PAD, tc), F32)
        upad_ref[CONV_PAD:CONV_PAD + T, :] = u_ref[...]
        dpad_ref[0:T, :] = dy_ref[...]
        dpad_ref[T:T + CONV_PAD, :] = jnp.zeros((CONV_PAD, tc), F32)
        for r in range(T // R):
            acc = jnp.zeros((R, tc), F32)
            for j in range(CONV_WIDTH):
                s = r * R + (CONV_WIDTH - 1) - j
                acc = acc + w_ref[j:j + 1, :] * dpad_ref[s:s + R, :]
            du_ref[r * R:(r + 1) * R, :] = acc
        for j in range(CONV_WIDTH):
            acc = jnp.zeros((R, tc), F32)
            for r in range(T // R):
                acc = acc + dy_ref[r * R:(r + 1) * R, :] * upad_ref[r * R + off + j:r * R + off + j + R, :]
            dw_ref[j:j + 1, :] = jnp.sum(acc, axis=0, keepdims=True)
        dw_ref[CONV_WIDTH:32, :] = jnp.zeros((32 - CONV_WIDTH, tc), F32)
        db_ref[...] = jnp.sum(dy_ref[...], axis=0, keepdims=True)

    return _ordered_call(
        body, [u, w_dw, dy], name="conv_ddw", grid=(C // tc,),
        in_specs=[pl.BlockSpec((T, tc), lambda c: (0, c)), pl.BlockSpec((32, tc), lambda c: (0, c)),
                  pl.BlockSpec((T, tc), lambda c: (0, c))],
        out_specs=[pl.BlockSpec((T, tc), lambda c: (0, c)), pl.BlockSpec((32, tc), lambda c: (0, c)),
                   pl.BlockSpec((1, tc), lambda c: (0, c))],
        out_shape=[_sds((T, C), F32), _sds((32, C), F32), _sds((1, C), F32)],
        scratch_shapes=[pltpu.VMEM((T + CONV_PAD, tc), F32), pltpu.VMEM((T + CONV_PAD, tc), F32)],
        compiler_params=_params(),
    )


def _ln_silu(y, g, b):
    mu = jnp.mean(y, axis=-1, keepdims=True)
    yc = y - mu
    z = yc * lax.rsqrt(jnp.mean(yc * yc, axis=-1, keepdims=True) + EPS) * g + b
    return z * jax.nn.sigmoid(z)


def _conv_fwd(h, gain, get):
    T, D = h.shape
    hn = _norm_fwd("conv_norm", h, gain)
    w = dict(pw1=get["pw1"](hn))
    b_pw1, w_dw, b_dw, ln_g, ln_b = get["small"](hn)
    S, _, Ns = w["pw1"].shape
    half = S // 2
    C = half * Ns
    tm, tk = _tile(T, 512, 16), _tile(D, 2048)

    def ep(accs, ex, outs):
        a = accs[0] + ex[0][...]
        g = accs[1] + ex[1][...]
        outs[0][0] = a.astype(BF16)
        outs[0][1] = g.astype(BF16)
        outs[1][...] = a * jax.nn.sigmoid(g)

    ag, u = _matmul(
        "conv_pw1", (half, T // tm, D // tk),
        [(hn, pl.BlockSpec((tm, tk), lambda j, i, k: (i, k))),
         (w["pw1"], pl.BlockSpec((None, tk, Ns), lambda j, i, k: (j, k, 0))),
         (w["pw1"], pl.BlockSpec((None, tk, Ns), lambda j, i, k: (j + half, k, 0)))],
        [(0, 0, 1), (1, 0, 2)], NN, [(tm, Ns), (tm, Ns)],
        [(_sds((2, T, C), BF16), pl.BlockSpec((2, tm, Ns), lambda j, i, k: (0, i, j))),
         (_sds((T, C), F32), pl.BlockSpec((tm, Ns), lambda j, i, k: (i, j)))],
        ep,
        extras=[(b_pw1, pl.BlockSpec((None, 1, Ns), lambda j, i, k: (j, 0, 0))),
                (b_pw1, pl.BlockSpec((None, 1, Ns), lambda j, i, k: (j + half, 0, 0)))])
    y = _dwconv_fwd(u, w_dw, b_dw)

    def fn(ins, outs, acc):
        outs[0][...] = _ln_silu(ins[0][...], ins[1][...], ins[2][...]).astype(BF16)

    s = _rowwise("conv_ln", fn, T, 256, [(y, 0)], [ln_g, ln_b], [((T, C), BF16, 0)])[0]

    w["pw2"] = get["pw2"](s)
    h_new = _mm_nn("conv_pw2", s, w["pw2"], F32, res=h)[0]
    return h_new, (h, hn, ag, u, y, s)


def _conv_bwd(saved, gain, w, w_dw, ln_g, ln_b, dh, dhb):
    h, hn, ag, u, y, s = saved
    T, D = h.shape
    S, _, Ns = w["pw1"].shape
    half = S // 2
    C = half * Ns
    ds = _mm_nt("conv_ds", dhb, w["pw2"], F32)[0]
    dw_pw2 = _mm_tn("conv_dwpw2", s, dhb, BF16)

    def fn(ins, outs, acc):
        _, vjp = jax.vjp(_ln_silu, ins[0][...], ins[2][...], ins[3][...])
        dy, dg, db = vjp(ins[1][...])
        outs[0][...] = dy
        acc(0, dg)
        acc(1, db)

    dy, d_ln_g, d_ln_b = _rowwise("conv_dln", fn, T, 256, [(y, 0), (ds, 0)], [ln_g, ln_b],
                                  [((T, C), F32, 0)], [(1, C), (1, C)])
    du, d_w_dw, d_b_dw = _dwconv_bwd(u, w_dw, dy)

    def fn2(ins, outs, acc):
        a = ins[0][0].astype(F32)
        g = ins[0][1].astype(F32)
        du_ = ins[1][...]
        sg = jax.nn.sigmoid(g)
        da = du_ * sg
        dg = du_ * a * sg * (1.0 - sg)
        outs[0][0] = da.astype(BF16)
        outs[0][1] = dg.astype(BF16)
        acc(0, jnp.sum(da, axis=0, keepdims=True))
        acc(1, jnp.sum(dg, axis=0, keepdims=True))

    dag, d_b_a, d_b_g = _rowwise("conv_dglu", fn2, T, 256, [(ag, 1), (du, 0)], [],
                                 [((2, T, C), BF16, 1)], [(1, C), (1, C)])
    tm, tn = _tile(T, 1024, 16), _tile(D, 1024)
    dhn = _matmul(
        "conv_dhn", (T // tm, D // tn, half),
        [(dag, pl.BlockSpec((None, tm, Ns), lambda i, n, j: (0, i, j))),
         (dag, pl.BlockSpec((None, tm, Ns), lambda i, n, j: (1, i, j))),
         (w["pw1"], pl.BlockSpec((None, tn, Ns), lambda i, n, j: (j, n, 0))),
         (w["pw1"], pl.BlockSpec((None, tn, Ns), lambda i, n, j: (j + half, n, 0)))],
        [(0, 0, 2), (0, 1, 3)], NT, [(tm, tn)],
        [(_sds((T, D), F32), pl.BlockSpec((tm, tn), lambda i, n, j: (i, n)))], _store())[0]
    tkd, tt = _tile(D, 1024), _tile(T, 2048)
    dw_pw1 = _matmul(
        "conv_dwpw1", (S, D // tkd, T // tt),
        [(hn, pl.BlockSpec((tt, tkd), lambda j, i, k: (k, i))),
         (dag, pl.BlockSpec((None, tt, Ns), lambda j, i, k: (j // half, k, j % half)))],
        [(0, 0, 1)], TN, [(tkd, Ns)],
        [(_sds((S, D, Ns), BF16), pl.BlockSpec((None, tkd, Ns), lambda j, i, k: (j, i, 0)))], _store())[0]
    dh_in, dh_in_b, dgain = _norm_bwd("conv_dnorm", h, gain, dhn, dh)
    d_b_pw1 = jnp.concatenate([d_b_a, d_b_g], axis=1)
    return dh_in, dh_in_b, dgain, dw_pw1, d_b_pw1, d_w_dw, d_b_dw, d_ln_g, d_ln_b, dw_pw2


def _loss_head(y, target):
    T, D = y.shape

    def fn(ins, outs, acc):
        e = ins[0][...] - ins[1][...]
        d = e * (1.0 / D)
        outs[0][...] = d
        outs[1][...] = d.astype(BF16)
        part = jnp.sum(jnp.sum(e * e, axis=-1, keepdims=True), axis=0, keepdims=True) * (0.5 / D)
        acc(0, jnp.broadcast_to(part, (1, 128)))

    return _rowwise("loss_head", fn, T, 256, [(y, 0), (target, 0)], [], [((T, D), F32, 0), ((T, D), BF16, 0)],
                    [(1, 128)])


def _place():
    return lax.axis_index("x"), lax.axis_index("y"), lax.axis_index("c")


def _peer(j):
    x, y, c = _place()
    return (1 - x if j & 4 else x, 1 - y if j & 2 else y, 1 - c if j & 1 else c)


def _index(place):
    return 4 * place[0] + 2 * place[1] + place[2]


HBM = pl.BlockSpec(memory_space=pltpu.HBM)
SEM = pl.BlockSpec(memory_space=pltpu.SEMAPHORE)
EFFECT = pltpu.SideEffectType.DATAFLOW_SIDE_EFFECTING


def _chip(j):
    x, y, _ = _place()
    return (1 - x if j & 2 else x, 1 - y if j & 1 else y)


def _chip_index(chip):
    return 2 * chip[0] + chip[1]


def _remote(src, dst, send, recv, k, device):
    return pltpu.make_async_remote_copy(src_ref=src, dst_ref=dst, send_sem=send.at[k], recv_sem=recv.at[k],
                                        device_id=device, device_id_type=MESH)


def _hbm(arrays):
    return [pltpu.with_memory_space_constraint(a, pltpu.HBM) for a in arrays]


def _split_call(name, body, ins, sems_in, sems_out, after=None, token=True):
    n, ns_in, ns_out = len(ins), len(sems_in), len(sems_out)
    if any(a is _LAST[0] for a in ins):
        _LAST[0] = None

    def kernel_body(*refs):
        in_refs = refs[:n]
        si = refs[n:n + ns_in]
        so = refs[n + ns_in:n + ns_in + ns_out]
        tok = refs[-1]
        body(in_refs, si, so, tok)
        tok[...] = jnp.zeros_like(tok)

    res = _ordered_call(
        kernel_body, _hbm(ins) + list(sems_in), chain_out=ns_out + n, name=name,
        in_specs=[HBM] * n + [SEM] * ns_in,
        out_specs=[SEM] * ns_out + [HBM] * n + [pl.BlockSpec(memory_space=pltpu.VMEM)],
        out_shape=[pltpu.SemaphoreType.DMA((s,)) for s in sems_out] + [pltpu.HBM(a.shape, a.dtype) for a in ins]
        + [_sds((8, 128), F32)],
        input_output_aliases={i: ns_out + i for i in range(n)},
        compiler_params=pltpu.CompilerParams(has_side_effects=EFFECT),
    )
    sems = list(res[:ns_out])
    arrays = list(res[ns_out:ns_out + n])
    return sems, arrays, res[-1]


def _ag_start(name, groups, after=None):
    flat = [s for grp in groups for s in grp]
    zones = [lax.empty((N_DEV,) + s.shape, s.dtype) for s in flat]
    n = len(flat)
    sizes = []
    for grp in groups:
        sizes += [7 * len(grp), 7 * len(grp), len(grp)]

    def body(ins, si, so, tok):
        x, y, c = _place()
        me = _index((x, y, c))
        base = 0
        for gi, grp in enumerate(groups):
            send, recv, local = so[3 * gi:3 * gi + 3]
            for w in range(len(grp)):
                src, dst = ins[base + w], ins[n + base + w].at[me]
                pltpu.make_async_copy(src, dst, local.at[w]).start()
                _remote(src, dst, send, recv, 7 * w, (x, y, 1 - c)).start()
                for j in (1, 2, 3):
                    _remote(src, dst, send, recv, 7 * w + j, (*_chip(j), c)).start()
            base += len(grp)

    sems, arrays, token = _split_call(name, body, flat + zones, [], sizes, after=after, token=True)
    out, base = [], 0
    for gi, grp in enumerate(groups):
        k = len(grp)
        out.append((sems[3 * gi:3 * gi + 3], arrays[base:base + k], arrays[n + base:n + base + k]))
        base += k
    return out, token


def _ag_forward(name, handle, after):
    sems, shards, zones = handle
    k = len(shards)

    def arrive(ins, si, so, tok):
        send, recv, _ = si
        _, _, c = _place()
        for j in (1, 2, 3):
            for w in range(k):
                blk = ins[k + w].at[_index((*_chip(j), c))]
                _remote(ins[w], blk, send, recv, 7 * w + j, (*_chip(j), c)).wait_recv()

    _, arrays, _ = _split_call(name + "_arrive", arrive, list(shards) + list(zones), sems, [], after=after)

    def pass_on(ins, si, so, tok):
        fsend, frecv = so
        x, y, c = _place()
        for j in (1, 2, 3):
            for w in range(k):
                blk = ins[w].at[_index((*_chip(j), c))]
                _remote(blk, blk, fsend, frecv, 3 * w + j - 1, (x, y, 1 - c)).start()

    fsems, zones2, _ = _split_call(name + "_pass", pass_on, arrays[k:], [], [3 * k, 3 * k])
    return (list(sems) + fsems, arrays[:k], zones2)


def _ag_wait(name, handle, after):
    sems, shards, zones = handle
    k = len(shards)

    def body(ins, si, so, tok):
        send, recv, local, fsend, frecv = si
        x, y, c = _place()
        sib = (x, y, 1 - c)
        for w in range(k):
            zone = ins[k + w]
            _remote(ins[w], zone.at[_index(sib)], send, recv, 7 * w, sib).wait_recv()
            for j in (1, 2, 3):
                blk = zone.at[_index((*_chip(j), 1 - c))]
                _remote(blk, blk, fsend, frecv, 3 * w + j - 1, sib).wait_recv()
        for w in range(k):
            zone = ins[k + w]
            mine = zone.at[_index((x, y, c))]
            _remote(ins[w], mine, send, recv, 7 * w, sib).wait_send()
            for j in (1, 2, 3):
                _remote(ins[w], mine, send, recv, 7 * w + j, (*_chip(j), c)).wait_send()
                blk = zone.at[_index((*_chip(j), c))]
                _remote(blk, blk, fsend, frecv, 3 * w + j - 1, sib).wait_send()
            pltpu.make_async_copy(ins[w], mine, local.at[w]).wait()

    _, arrays, _ = _split_call(name, body, list(shards) + list(zones), sems, [], after=after)
    return arrays[k:]


def _rs_pair_start(name, grads, after=None):
    n = len(grads)
    zones = [lax.empty((4,) + g.shape[1:], g.dtype) for g in grads]

    def body(ins, si, so, tok):
        send, recv = so
        x, y, c = _place()
        for w in range(n):
            for q in range(4):
                _remote(ins[w].at[2 * q + 1 - c], ins[n + w].at[q], send, recv, 4 * w + q, (x, y, 1 - c)).start()

    sems, arrays, token = _split_call(name, body, list(grads) + zones, [], [4 * n, 4 * n], after=after, token=True)
    return (sems, arrays[:n], arrays[n:]), token


def _rs_pair_wait(name, handle, after):
    sems, grads, zones = handle
    n = len(grads)

    def body(ins, si, so, tok):
        send, recv = si
        x, y, c = _place()
        for w in range(n):
            for q in range(4):
                cp = _remote(ins[w].at[2 * q + 1 - c], ins[n + w].at[q], send, recv, 4 * w + q, (x, y, 1 - c))
                cp.wait_recv()
                cp.wait_send()

    _, arrays, _ = _split_call(name, body, list(grads) + list(zones), sems, [], after=after)
    return arrays[:n], arrays[n:]


def _pair_sum(name, g, got, core):
    _, R, C = g.shape
    g4 = g.reshape(4, 2, R, C)
    tr = _tile(R, 1024, 16)

    def body(c_ref, g_ref, a_ref, *rest):
        o_ref = rest[-1]
        o_ref[...] = (g_ref[...].astype(F32) + a_ref[...].astype(F32)).astype(o_ref.dtype)

    prev = [] if _LAST[0] is None or _LAST[0] is g or _LAST[0] is got else [_LAST[0]]
    out = pl.pallas_call(
        body, name=name,
        grid_spec=pltpu.PrefetchScalarGridSpec(
            num_scalar_prefetch=1, grid=(4, R // tr),
            in_specs=[pl.BlockSpec((None, None, tr, C), lambda q, i, c_ref: (q, c_ref[0], i, 0)),
                      pl.BlockSpec((None, tr, C), lambda q, i, c_ref: (q, i, 0))] + [ANY] * len(prev),
            out_specs=pl.BlockSpec((None, tr, C), lambda q, i, c_ref: (q, i, 0))),
        out_shape=_sds((4, R, C), g.dtype),
        compiler_params=_params(),
    )(core, g4, got, *prev)
    _LAST[0] = out
    return out


def _rs_chip_start(name, sums, after=None):
    n = len(sums)
    zones = [lax.empty(s.shape, s.dtype) for s in sums]

    def body(ins, si, so, tok):
        send, recv, local = so
        x, y, c = _place()
        mine = _chip_index((x, y))
        for w in range(n):
            pltpu.make_async_copy(ins[w].at[mine], ins[n + w].at[mine], local.at[w]).start()
            for j in (1, 2, 3):
                _remote(ins[w].at[_chip_index(_chip(j))], ins[n + w].at[mine], send, recv, 3 * w + j - 1,
                        (*_chip(j), c)).start()

    sems, arrays, token = _split_call(name, body, list(sums) + zones, [], [3 * n, 3 * n, n], after=after, token=True)
    return (sems, arrays[:n], arrays[n:]), token


def _rs_chip_wait(name, handle, after):
    sems, sums, zones = handle
    n = len(sums)

    def body(ins, si, so, tok):
        send, recv, local = si
        x, y, c = _place()
        mine = _chip_index((x, y))
        for w in range(n):
            for j in (1, 2, 3):
                _remote(ins[w].at[mine], ins[n + w].at[_chip_index(_chip(j))], send, recv, 3 * w + j - 1,
                        (*_chip(j), c)).wait_recv()
        for w in range(n):
            for j in (1, 2, 3):
                _remote(ins[w].at[_chip_index(_chip(j))], ins[n + w].at[mine], send, recv, 3 * w + j - 1,
                        (*_chip(j), c)).wait_send()
            pltpu.make_async_copy(ins[w].at[mine], ins[n + w].at[mine], local.at[w]).wait()

    _, arrays, _ = _split_call(name, body, list(sums) + list(zones), sems, [], after=after)
    return arrays[n:]


def _chip_sum(name, slots, layer, layers, into=None):
    _, R, C = slots.shape
    tr = _tile(R, 256, 16)

    def body(*refs):
        s_ref, o_ref = refs[0], refs[-1]
        total = s_ref[0].astype(F32)
        for k in range(1, 4):
            total = total + s_ref[k].astype(F32)
        o_ref[...] = total

    extra = [] if into is None else [into]
    return _ordered_call(
        body, [slots] + extra, name=name, grid=(R // tr,),
        in_specs=[pl.BlockSpec((4, tr, C), lambda i: (0, i, 0))] + [ANY] * len(extra),
        out_specs=pl.BlockSpec((None, tr, C), lambda i: (layer, i, 0)),
        out_shape=_sds((layers, R, C), F32),
        input_output_aliases={} if into is None else {1: 0},
        compiler_params=_params(),
    )


def _pack_rows(name, parts, rows):
    C = parts[0].shape[1]

    def body(*refs):
        o_ref = refs[-1]
        off = 0
        for r in refs[:-1]:
            o_ref[off:off + r.shape[0], :] = r[...]
            off += r.shape[0]
        if off < rows:
            o_ref[off:rows, :] = jnp.zeros((rows - off, C), F32)

    vmem = pl.BlockSpec(memory_space=pltpu.VMEM)
    return _ordered_call(body, list(parts), name=name, in_specs=[vmem] * len(parts), out_specs=vmem,
                         out_shape=_sds((rows, C), F32))


def _reduce_small(rep, shd):
    R, C = rep.shape
    _, Rs, Cs = shd.shape

    def body(r_ref, s_ref, or_ref, os_ref, all_r, all_s, send, recv):
        me = _index(_place())
        all_r[me] = r_ref[...]
        all_s[me] = s_ref[me]
        copies = []
        for j in range(1, N_DEV):
            there = _index(_peer(j))
            copies.append((
                pltpu.make_async_remote_copy(src_ref=r_ref, dst_ref=all_r.at[me], send_sem=send.at[2 * j - 2],
                                             recv_sem=recv.at[2 * j - 2], device_id=_peer(j), device_id_type=MESH),
                pltpu.make_async_remote_copy(src_ref=s_ref.at[there], dst_ref=all_s.at[me],
                                             send_sem=send.at[2 * j - 1], recv_sem=recv.at[2 * j - 1],
                                             device_id=_peer(j), device_id_type=MESH)))
        for a, b in copies:
            a.start()
            b.start()
        for j in range(1, N_DEV):
            there = _index(_peer(j))
            pltpu.make_async_remote_copy(src_ref=r_ref, dst_ref=all_r.at[there], send_sem=send.at[2 * j - 2],
                                         recv_sem=recv.at[2 * j - 2], device_id=_peer(j),
                                         device_id_type=MESH).wait_recv()
            pltpu.make_async_remote_copy(src_ref=s_ref.at[me], dst_ref=all_s.at[there], send_sem=send.at[2 * j - 1],
                                         recv_sem=recv.at[2 * j - 1], device_id=_peer(j),
                                         device_id_type=MESH).wait_recv()
        for a, b in copies:
            a.wait_send()
            b.wait_send()
        tot_r, tot_s = all_r[0], all_s[0]
        for k in range(1, N_DEV):
            tot_r = tot_r + all_r[k]
            tot_s = tot_s + all_s[k]
        or_ref[...] = tot_r
        os_ref[...] = tot_s

    vmem = pl.BlockSpec(memory_space=pltpu.VMEM)
    return _ordered_call(
        body, [rep, shd], name="reduce_small",
        in_specs=[vmem, vmem], out_specs=[vmem, vmem],
        out_shape=[_sds((R, C), F32), _sds((Rs, Cs), F32)],
        scratch_shapes=[pltpu.VMEM((N_DEV, R, C), F32), pltpu.VMEM((N_DEV, Rs, Cs), F32),
                        pltpu.SemaphoreType.DMA((2 * N_DEV - 2,)), pltpu.SemaphoreType.DMA((2 * N_DEV - 2,))],
        compiler_params=_params(),
    )


def _cast_layer(name, w, layer):
    _, R, C = w.shape
    tr = _tile(R, 1024, 16)

    def body(w_ref, o_ref):
        o_ref[...] = w_ref[...].astype(BF16)

    return _ordered_call(
        body, [w], name=name, grid=(R // tr,),
        in_specs=[pl.BlockSpec((None, tr, C), lambda i: (layer, i, 0))],
        out_specs=pl.BlockSpec((tr, C), lambda i: (i, 0)),
        out_shape=_sds((R, C), BF16),
        compiler_params=_params(),
    )


def _adam_math(w, g, m, v):
    c1 = 1.0 / (1.0 - ADAM_B1 ** ADAM_STEP)
    c2 = 1.0 / (1.0 - ADAM_B2 ** ADAM_STEP)
    nm = ADAM_B1 * m + (1.0 - ADAM_B1) * g
    nv = ADAM_B2 * v + (1.0 - ADAM_B2) * (g * g)
    return -ADAM_LR * ((nm * c1) / (jnp.sqrt(nv * c2) + ADAM_EPS) + ADAM_WD * w), nm, nv


def _sum_adam(name, slots, w, m, v, layer, into=None):
    L, R, C = w.shape
    tr = _tile(R, 256, 16)

    def body(*refs):
        s_ref, w_ref, m_ref, v_ref = refs[:4]
        g_ref, d_ref, nm_ref, nv_ref = refs[-4:]
        g = s_ref[0].astype(F32)
        for k in range(1, 4):
            g = g + s_ref[k].astype(F32)
        g_ref[...] = g
        d_ref[...], nm_ref[...], nv_ref[...] = _adam_math(w_ref[...], g, m_ref[...], v_ref[...])

    spec = pl.BlockSpec((None, tr, C), lambda i: (layer, i, 0))
    extra = [] if into is None else list(into)
    return _ordered_call(
        body, [slots, w, m, v] + extra, name=name, grid=(R // tr,),
        in_specs=[pl.BlockSpec((4, tr, C), lambda i: (0, i, 0)), spec, spec, spec] + [ANY] * len(extra),
        out_specs=[spec] * 4,
        out_shape=[_sds((L, R, C), F32)] * 4,
        input_output_aliases={4 + k: k for k in range(len(extra))},
        compiler_params=_params(),
    )


def _adamw(name, w, g, m, v):
    shape = w.shape
    R, C = shape[-2], shape[-1]
    L = 1
    for s in shape[:-2]:
        L *= s
    w3, g3, m3, v3 = (a.reshape(L, R, C) for a in (w, g, m, v))
    tr = _tile(R, 256, 8)

    def body(w_ref, g_ref, m_ref, v_ref, d_ref, nm_ref, nv_ref):
        d_ref[...], nm_ref[...], nv_ref[...] = _adam_math(w_ref[...], g_ref[...], m_ref[...], v_ref[...])

    spec = pl.BlockSpec((None, tr, C), lambda l, i: (l, i, 0))
    outs = _ordered_call(
        body, [w3, g3, m3, v3], name=name, grid=(L, R // tr),
        in_specs=[spec] * 4, out_specs=[spec] * 3,
        out_shape=[_sds((L, R, C), F32)] * 3,
        compiler_params=_params(),
    )
    return tuple(o.reshape(shape) for o in outs)


def _pad_rows(a, rows):
    return jnp.pad(a, ((0, rows - a.shape[0]), (0, 0)))


def _pad_cols(a, cols):
    return jnp.pad(a, ((0, 0), (0, cols - a.shape[1])))


def _rope_tables(positions):
    q = D_ROPE // 2
    inv_freq = ROPE_THETA ** (-jnp.arange(0, D_ROPE, 2, dtype=F32) / D_ROPE)
    ang = positions.astype(F32)[:, None] * inv_freq
    cos, sin = jnp.cos(ang), jnp.sin(ang)
    z = jnp.zeros_like(cos)
    zz = jnp.zeros((cos.shape[0], 128 - 2 * q), F32)
    c = jnp.concatenate([cos, cos, zz], axis=1)
    s1 = jnp.concatenate([z, sin, zz], axis=1)
    s2 = jnp.concatenate([-sin, z, zz], axis=1)
    return c, s1, s2


def kernel(x, p, positions, ffn_a_norm, ffn_a_w_in, ffn_a_w_out, ffn_b_norm, ffn_b_w_in, ffn_b_w_out, mix_norm, mla_w_in, mla_q_lat_norm, mla_kv_lat_norm, mla_w_uq, mla_w_ukv, mla_q_gain, mla_k_gain, mla_w_o, conv_w_pw1, conv_b_pw1, conv_w_dw, conv_b_dw, conv_ln_g, conv_ln_b, conv_w_pw2, ple_w_proj, ple_norm, ple_gate_norm, ple_w_gate, loss_target, m_ffn_a_norm, m_ffn_a_w_in, m_ffn_a_w_out, m_ffn_b_norm, m_ffn_b_w_in, m_ffn_b_w_out, m_mix_norm, m_mla_w_in, m_mla_q_lat_norm, m_mla_kv_lat_norm, m_mla_w_uq, m_mla_w_ukv, m_mla_q_gain, m_mla_k_gain, m_mla_w_o, m_conv_w_pw1, m_conv_b_pw1, m_conv_w_dw, m_conv_b_dw, m_conv_ln_g, m_conv_ln_b, m_conv_w_pw2, m_ple_w_proj, m_ple_norm, m_ple_gate_norm, m_ple_w_gate, v_ffn_a_norm, v_ffn_a_w_in, v_ffn_a_w_out, v_ffn_b_norm, v_ffn_b_w_in, v_ffn_b_w_out, v_mix_norm, v_mla_w_in, v_mla_q_lat_norm, v_mla_kv_lat_norm, v_mla_w_uq, v_mla_w_ukv, v_mla_q_gain, v_mla_k_gain, v_mla_w_o, v_conv_w_pw1, v_conv_b_pw1, v_conv_w_dw, v_conv_b_dw, v_conv_ln_g, v_conv_ln_b, v_conv_w_pw2, v_ple_w_proj, v_ple_norm, v_ple_gate_norm, v_ple_w_gate):
    weights = dict(ffn_a_norm=ffn_a_norm, ffn_a_w_in=ffn_a_w_in, ffn_a_w_out=ffn_a_w_out, ffn_b_norm=ffn_b_norm,
                   ffn_b_w_in=ffn_b_w_in, ffn_b_w_out=ffn_b_w_out, mix_norm=mix_norm, mla_w_in=mla_w_in,
                   mla_q_lat_norm=mla_q_lat_norm, mla_kv_lat_norm=mla_kv_lat_norm, mla_w_uq=mla_w_uq,
                   mla_w_ukv=mla_w_ukv, mla_q_gain=mla_q_gain, mla_k_gain=mla_k_gain, mla_w_o=mla_w_o,
                   conv_w_pw1=conv_w_pw1, conv_b_pw1=conv_b_pw1, conv_w_dw=conv_w_dw, conv_b_dw=conv_b_dw,
                   conv_ln_g=conv_ln_g, conv_ln_b=conv_ln_b, conv_w_pw2=conv_w_pw2, ple_w_proj=ple_w_proj,
                   ple_norm=ple_norm, ple_gate_norm=ple_gate_norm, ple_w_gate=ple_w_gate)
    moments_m = dict(ffn_a_norm=m_ffn_a_norm, ffn_a_w_in=m_ffn_a_w_in, ffn_a_w_out=m_ffn_a_w_out,
                     ffn_b_norm=m_ffn_b_norm, ffn_b_w_in=m_ffn_b_w_in, ffn_b_w_out=m_ffn_b_w_out,
                     mix_norm=m_mix_norm, mla_w_in=m_mla_w_in, mla_q_lat_norm=m_mla_q_lat_norm,
                     mla_kv_lat_norm=m_mla_kv_lat_norm, mla_w_uq=m_mla_w_uq, mla_w_ukv=m_mla_w_ukv,
                     mla_q_gain=m_mla_q_gain, mla_k_gain=m_mla_k_gain, mla_w_o=m_mla_w_o,
                     conv_w_pw1=m_conv_w_pw1, conv_b_pw1=m_conv_b_pw1, conv_w_dw=m_conv_w_dw,
                     conv_b_dw=m_conv_b_dw, conv_ln_g=m_conv_ln_g, conv_ln_b=m_conv_ln_b, conv_w_pw2=m_conv_w_pw2,
                     ple_w_proj=m_ple_w_proj, ple_norm=m_ple_norm, ple_gate_norm=m_ple_gate_norm,
                     ple_w_gate=m_ple_w_gate)
    moments_v = dict(ffn_a_norm=v_ffn_a_norm, ffn_a_w_in=v_ffn_a_w_in, ffn_a_w_out=v_ffn_a_w_out,
                     ffn_b_norm=v_ffn_b_norm, ffn_b_w_in=v_ffn_b_w_in, ffn_b_w_out=v_ffn_b_w_out,
                     mix_norm=v_mix_norm, mla_w_in=v_mla_w_in, mla_q_lat_norm=v_mla_q_lat_norm,
                     mla_kv_lat_norm=v_mla_kv_lat_norm, mla_w_uq=v_mla_w_uq, mla_w_ukv=v_mla_w_ukv,
                     mla_q_gain=v_mla_q_gain, mla_k_gain=v_mla_k_gain, mla_w_o=v_mla_w_o,
                     conv_w_pw1=v_conv_w_pw1, conv_b_pw1=v_conv_b_pw1, conv_w_dw=v_conv_w_dw,
                     conv_b_dw=v_conv_b_dw, conv_ln_g=v_conv_ln_g, conv_ln_b=v_conv_ln_b, conv_w_pw2=v_conv_w_pw2,
                     ple_w_proj=v_ple_w_proj, ple_norm=v_ple_norm, ple_gate_norm=v_ple_gate_norm,
                     ple_w_gate=v_ple_w_gate)
    order = list(weights.keys())
    _LAST[0] = None

    T, D = x.shape[1], x.shape[2]
    me = _index(_place())
    h0 = x[0]
    target = loss_target[0]
    tabs = _rope_tables(positions[0])
    H = N_HEADS
    hps = H // N_DEV
    QL = mla_q_lat_norm.shape[1]
    Cs = conv_b_dw.shape[1]

    def cast(n, i):
        return _cast_layer(f"cast_{n}{i}", weights[n], i)

    first, tok = _ag_start("ag_start0", [[cast("ffn_a_w_in", 0)]])
    m_in_pad = _pad_cols(mla_w_in[0], mla_w_in.shape[2] - D_ROPE + 128)[None]
    uq_pad = jnp.pad(mla_w_uq[0].reshape(QL, hps, QK_DIM), ((0, 0), (0, 0), (0, HEAD_PAD - QK_DIM)))
    uq_pad = uq_pad.reshape(1, QL, hps * HEAD_PAD)
    conv_small = jnp.concatenate([
        _pad_rows(_pad_cols(conv_b_pw1, 2 * Cs), 8),
        _pad_rows(_pad_cols(conv_w_dw[0], 2 * Cs), 32),
        _pad_rows(_pad_cols(jnp.concatenate([conv_b_dw, conv_ln_g, conv_ln_b], axis=0), 2 * Cs), 8)], axis=0)
    rest, tok = _ag_start("ag_start1", [
        [cast("ffn_a_w_out", 0), _cast_layer("cast_mla_in", m_in_pad, 0), _cast_layer("cast_mla_uq", uq_pad, 0),
         _cast_layer("cast_mla_ukv", mla_w_ukv, 0), _cast_layer("cast_mla_wo", mla_w_o, 0)],
        [cast("ffn_b_w_in", 0), cast("ffn_b_w_out", 0), cast("ple_w_gate", 0), cast("ple_w_proj", 0)],
        [cast("ffn_a_w_in", 1), cast("ffn_a_w_out", 1)],
        [_cast_layer("cast_conv_pw1", conv_w_pw1, 0), _cast_layer("cast_conv_pw2", conv_w_pw2, 0), conv_small],
        [cast("ffn_b_w_in", 1), cast("ffn_b_w_out", 1), cast("ple_w_gate", 1), cast("ple_w_proj", 1)]], after=tok)
    groups = [dict(handle=hd, stage=0, arrays=None) for hd in first + rest]

    def prefetch(gi, after):
        st = groups[gi]
        if st["stage"] == 0:
            st["handle"] = _ag_forward(f"ag{gi}_forward", st["handle"], after)
            st["stage"] = 1

    def fetch(gi, after):
        prefetch(gi, after)
        st = groups[gi]
        if st["stage"] == 1:
            st["arrays"] = _ag_wait(f"ag{gi}_wait", st["handle"], after)
            st["stage"] = 2
        return st["arrays"]

    def getter(gi, k, shape=None, ahead=None):
        def get(after):
            if ahead is not None:
                prefetch(ahead, after)
            a = fetch(gi, after)[k]
            return a if shape is None else a.reshape(shape)
        return get

    def conv_small_params(after):
        small = fetch(4, after)[2]
        return (small[:, 0:1, :],
                jnp.transpose(small[:, 8:40, :Cs], (1, 0, 2)).reshape(32, N_DEV * Cs),
                small[:, 40, :Cs].reshape(1, N_DEV * Cs), small[:, 41, :Cs].reshape(1, N_DEV * Cs),
                small[:, 42, :Cs].reshape(1, N_DEV * Cs))

    rows = (-1, D)
    get_ffn = [dict(a_in=getter(0, 0), a_out=getter(1, 0, rows), b_in=getter(2, 0), b_out=getter(2, 1, rows)),
               dict(a_in=getter(3, 0), a_out=getter(3, 1, rows, ahead=4), b_in=getter(5, 0),
                    b_out=getter(5, 1, rows))]
    get_ple = [dict(proj=getter(2, 3, ahead=3), gate=getter(2, 2, rows)),
               dict(proj=getter(5, 3), gate=getter(5, 2, rows))]
    get_mla = dict(m_in=getter(1, 1, (D, -1)), uq=getter(1, 2), ukv=getter(1, 3), wo=getter(1, 4, rows, ahead=2))
    get_conv = dict(pw1=getter(4, 0), pw2=getter(4, 1, rows, ahead=5), small=conv_small_params)
    gq_pad = _pad_cols(mla_q_gain, HEAD_PAD)
    gk_pad = _pad_cols(mla_k_gain, HEAD_PAD)
    prefetch(0, tok)

    saved = []
    h = h0
    for i in range(2):
        h, s_a = _ffn_fwd(f"ffn_a{i}", h, ffn_a_norm[i:i + 1], get_ffn[i]["a_in"], get_ffn[i]["a_out"])
        if i == 0:
            h, s_m = _mla_fwd(h, mix_norm[0:1], tabs, get_mla, mla_q_lat_norm, mla_kv_lat_norm, gq_pad, gk_pad)
        else:
            h, s_m = _conv_fwd(h, mix_norm[1:2], get_conv)
        h, s_b = _ffn_fwd(f"ffn_b{i}", h, ffn_b_norm[i:i + 1], get_ffn[i]["b_in"], get_ffn[i]["b_out"])
        h, s_p = _ple_fwd(f"ple{i}", h, p[i, 0], get_ple[i]["proj"], ple_norm[i:i + 1], ple_gate_norm[i:i + 1],
                          get_ple[i]["gate"])
        saved.append((s_a, s_m, s_b, s_p))
    W = [dict(a_in=get_ffn[i]["a_in"](None), a_out=get_ffn[i]["a_out"](None), b_in=get_ffn[i]["b_in"](None),
              b_out=get_ffn[i]["b_out"](None), proj=get_ple[i]["proj"](None), gate=get_ple[i]["gate"](None))
         for i in range(2)]
    Wm = {n: g(None) for n, g in get_mla.items()}
    Wc = dict(pw1=get_conv["pw1"](None), pw2=get_conv["pw2"](None))
    _, w_dw_full, _, ln_g_full, ln_b_full = conv_small_params(None)

    dh, dhb, loss_row = _loss_head(h, target)

    G = {}
    small_g = {}
    stacked = ["ffn_a_w_in", "ffn_a_w_out", "ffn_b_w_in", "ffn_b_w_out", "ple_w_proj", "ple_w_gate"]
    row_sharded = {"ffn_a_w_out", "ffn_b_w_out", "ple_w_gate", "mla_w_in", "mla_w_o", "conv_w_pw2"}
    core = lax.axis_index("c").astype(jnp.int32).reshape(1)
    rs_groups = {
        "pb1": [("ple_w_proj", 1), ("ple_w_gate", 1), ("ffn_b_w_in", 1), ("ffn_b_w_out", 1)],
        "c1": [("conv_w_pw1", 0), ("conv_w_pw2", 0)],
        "a1": [("ffn_a_w_in", 1), ("ffn_a_w_out", 1)],
        "pb0": [("ple_w_proj", 0), ("ple_w_gate", 0), ("ffn_b_w_in", 0), ("ffn_b_w_out", 0)],
        "m0": [("mla_w_in", 0), ("mla_w_uq", 0), ("mla_w_ukv", 0), ("mla_w_o", 0)],
        "ao0": [("ffn_a_w_out", 0)],
        "ai0": [("ffn_a_w_in", 0)]}
    rs = {}

    def rs_begin(tag):
        grads = []
        for n, i in rs_groups[tag]:
            g = G[(n, i)]
            grads.append(g.reshape(N_DEV, g.shape[0] // N_DEV, g.shape[1]) if n in row_sharded else g)
        rs[tag] = _rs_pair_start(f"rs_{tag}_pair_start", grads)[0]

    def rs_pairs(tag):
        mine, got = _rs_pair_wait(f"rs_{tag}_pair_wait", rs[tag], None)
        rs[tag] = [_pair_sum(f"pairsum_{n}{i}", a, b, core) for (n, i), a, b in zip(rs_groups[tag], mine, got)]

    def rs_chips(tag):
        rs[tag] = _rs_chip_start(f"rs_{tag}_chip_start", rs[tag])[0]

    def rs_end(tag):
        return dict(zip(rs_groups[tag], _rs_chip_wait(f"rs_{tag}_chip_wait", rs[tag], None)))

    s_a, s_m, s_b, s_p = saved[1]
    dh, dhb, d_pn, d_gn, G[("ple_w_proj", 1)], G[("ple_w_gate", 1)] = _ple_bwd(
        "ple1", s_p, p[1, 0], W[1]["proj"], ple_norm[1:2], ple_gate_norm[1:2], W[1]["gate"], dh, dhb)
    small_g[("ple_norm", 1)], small_g[("ple_gate_norm", 1)] = d_pn, d_gn
    dh, dhb, small_g[("ffn_b_norm", 1)], G[("ffn_b_w_in", 1)], G[("ffn_b_w_out", 1)] = _ffn_bwd(
        "ffn_b1", s_b, ffn_b_norm[1:2], W[1]["b_in"], W[1]["b_out"], dh, dhb)
    rs_begin("pb1")
    (dh, dhb, small_g[("mix_norm", 1)], G[("conv_w_pw1", 0)], d_b_pw1, d_w_dw, d_b_dw, d_ln_g, d_ln_b,
     G[("conv_w_pw2", 0)]) = _conv_bwd(s_m, mix_norm[1:2], Wc, w_dw_full, ln_g_full, ln_b_full, dh, dhb)
    rs_pairs("pb1")
    rs_chips("pb1")
    rs_begin("c1")
    dh, dhb, small_g[("ffn_a_norm", 1)], G[("ffn_a_w_in", 1)], G[("ffn_a_w_out", 1)] = _ffn_bwd(
        "ffn_a1", s_a, ffn_a_norm[1:2], W[1]["a_in"], W[1]["a_out"], dh, dhb)
    rs_pairs("c1")
    rs_chips("c1")
    rs_begin("a1")

    s_a, s_m, s_b, s_p = saved[0]
    dh, dhb, d_pn, d_gn, G[("ple_w_proj", 0)], G[("ple_w_gate", 0)] = _ple_bwd(
        "ple0", s_p, p[0, 0], W[0]["proj"], ple_norm[0:1], ple_gate_norm[0:1], W[0]["gate"], dh, dhb)
    small_g[("ple_norm", 0)], small_g[("ple_gate_norm", 0)] = d_pn, d_gn
    rs_pairs("a1")
    rs_chips("a1")
    dh, dhb, small_g[("ffn_b_norm", 0)], G[("ffn_b_w_in", 0)], G[("ffn_b_w_out", 0)] = _ffn_bwd(
        "ffn_b0", s_b, ffn_b_norm[0:1], W[0]["b_in"], W[0]["b_out"], dh, dhb)
    rs_begin("pb0")

    def in_mla():
        rs_pairs("pb0")
        rs_chips("pb0")

    (dh, dhb, small_g[("mix_norm", 0)], d_qln, d_kvln, d_gq, d_gk,
     G[("mla_w_in", 0)], G[("mla_w_uq", 0)], G[("mla_w_ukv", 0)], G[("mla_w_o", 0)]) = _mla_bwd(
        s_m, mix_norm[0:1], tabs, Wm, mla_q_lat_norm, mla_kv_lat_norm, gq_pad, gk_pad, dh, dhb, hook=in_mla)
    rs_begin("m0")

    def with_dw_out(dw_out):
        G[("ffn_a_w_out", 0)] = dw_out
        rs_pairs("m0")
        rs_chips("m0")
        rs_begin("ao0")

    def with_dw_in(dw_in):
        G[("ffn_a_w_in", 0)] = dw_in
        rs_pairs("ao0")
        rs_chips("ao0")
        rs_begin("ai0")

    dh, dhb, small_g[("ffn_a_norm", 0)], _, _ = _ffn_bwd(
        "ffn_a0", s_a, ffn_a_norm[0:1], W[0]["a_in"], W[0]["a_out"], dh, dhb,
        hooks=(with_dw_out, with_dw_in, lambda: rs_pairs("ai0")))
    grad_x = dh[None]

    replicated = ["ffn_a_norm", "ffn_b_norm", "mix_norm", "ple_norm", "ple_gate_norm"]
    rep = _pack_rows("pack_small", [small_g[(n, i)] for n in replicated for i in (0, 1)]
                     + [_pad_cols(v, D) for v in (d_qln, d_kvln, d_gq, d_gk, loss_row)], 16)

    def dest_major(v, rows):
        r, w = v.shape[0], v.shape[1] // N_DEV
        v = jnp.transpose(v.reshape(r, N_DEV, w), (1, 0, 2))
        return jnp.pad(v, ((0, 0), (0, rows - r), (0, 2 * Cs - w)))

    shd = jnp.concatenate([dest_major(d_b_pw1, 8), dest_major(d_w_dw, 32),
                           dest_major(jnp.concatenate([d_b_dw, d_ln_g, d_ln_b], axis=0), 8)], axis=1)
    red, red_s = _reduce_small(rep, shd)
    loss = red[14, 0]
    small_grads = {n: red[2 * k:2 * k + 2] for k, n in enumerate(replicated)}
    small_grads.update(
        mla_q_lat_norm=red[10:11, :QL], mla_kv_lat_norm=red[11:12, :mla_kv_lat_norm.shape[1]],
        mla_q_gain=red[12:13, :QK_DIM], mla_k_gain=red[13:14, :QK_DIM],
        conv_b_pw1=red_s[0:1], conv_w_dw=red_s[8:8 + CONV_WIDTH, :Cs][None],
        conv_b_dw=red_s[40:41, :Cs], conv_ln_g=red_s[41:42, :Cs], conv_ln_b=red_s[42:43, :Cs])

    rs_chips("ai0")
    done = {}

    def plain_adamw(n, g):
        done[n] = (g,) + _adamw(f"adamw_{n}", weights[n], g, moments_m[n], moments_v[n])

    def slot_adamw(n, layer, slots):
        done[n] = tuple(_sum_adam(f"adamw_{n}{layer}", slots, weights[n], moments_m[n], moments_v[n], layer,
                                  into=done.get(n)))

    for n, g in small_grads.items():
        plain_adamw(n, g)
    for tag in ("pb1", "c1", "a1", "pb0"):
        slots = rs_end(tag)
        for n, i in rs_groups[tag]:
            slot_adamw(n, i, slots[(n, i)])
    slots = rs_end("m0")
    slot_adamw("mla_w_ukv", 0, slots[("mla_w_ukv", 0)])
    slot_adamw("mla_w_o", 0, slots[("mla_w_o", 0)])
    g_in = _chip_sum("chipsum_mla_w_in", slots[("mla_w_in", 0)], 0, 1)
    plain_adamw("mla_w_in", g_in[:, :, :mla_w_in.shape[2]])
    g_uq = _chip_sum("chipsum_mla_w_uq", slots[("mla_w_uq", 0)], 0, 1)
    plain_adamw("mla_w_uq", g_uq.reshape(1, QL, hps, HEAD_PAD)[..., :QK_DIM].reshape(mla_w_uq.shape))
    for tag in ("ao0", "ai0"):
        slots = rs_end(tag)
        for n, i in rs_groups[tag]:
            slot_adamw(n, i, slots[(n, i)])
    grads, deltas, new_m, new_v = ({n: done[n][k] for n in order} for k in range(4))

    return (loss, grad_x, *[grads[n] for n in order], *[deltas[n] for n in order],
            *[new_m[n] for n in order], *[new_v[n] for n in order])
```

```python
import functools

import jax
import jax.numpy as jnp
from jax import lax
from jax.experimental import pallas as pl
from jax.experimental.pallas import tpu as pltpu

F32 = jnp.float32
BF16 = jnp.bfloat16
MESH = pl.DeviceIdType.MESH
ANY = pl.BlockSpec(memory_space=pl.ANY)

N_DEV = 8
N_HEADS = 16
D_NOPE = 128
D_ROPE = 64
D_V = 128
QK_DIM = D_NOPE + D_ROPE
HEAD_PAD = 256
ROPE_THETA = 10000.0
CONV_WIDTH = 31
CONV_PAD = 32
FFN_RES = 0.5
EPS = 1e-6
ADAM_LR = 0.001
ADAM_B1 = 0.9
ADAM_B2 = 0.999
ADAM_EPS = 1e-08
ADAM_WD = 0.01
ADAM_STEP = 10
VMEM_LIMIT = 56 * 1024 * 1024


def _sds(shape, dtype):
    return jax.ShapeDtypeStruct(tuple(int(s) for s in shape), dtype)


def _tile(n, pref, mult=128):
    if n <= pref:
        return n
    t = (pref // mult) * mult
    while t >= mult:
        if n % t == 0:
            return t
        t -= mult
    return n


def _params():
    return pltpu.CompilerParams(vmem_limit_bytes=VMEM_LIMIT)


_LAST = [None]


def _ordered_call(body, operands, chain_out=0, **kw):
    operands = list(operands)
    n_in = len(operands)
    if _LAST[0] is not None and not any(op is _LAST[0] for op in operands):
        inner = body

        def body(*refs):
            inner(*refs[:n_in], *refs[n_in + 1:])

        kw = dict(kw, in_specs=list(kw["in_specs"]) + [ANY])
        operands.append(_LAST[0])
    out = pl.pallas_call(body, **kw)(*operands)
    _LAST[0] = out[chain_out] if isinstance(out, (list, tuple)) else out
    return out


def _matmul(name, grid, ops, terms, dims, acc_shapes, outs, epilogue, extras=()):
    nk = grid[2]
    n_ops, n_ex, n_out, n_acc = len(ops), len(extras), len(outs), len(acc_shapes)

    def body(*refs):
        op_refs = refs[:n_ops]
        ex_refs = refs[n_ops:n_ops + n_ex]
        out_refs = refs[n_ops + n_ex:n_ops + n_ex + n_out]
        acc_refs = refs[n_ops + n_ex + n_out:]
        vals = {}

        def opval(i):
            if i not in vals:
                v = op_refs[i][...]
                vals[i] = v if v.dtype == BF16 else v.astype(BF16)
            return vals[i]

        parts = [None] * n_acc
        for ai, li, ri in terms:
            d = lax.dot_general(opval(li), opval(ri), (dims, ((), ())), preferred_element_type=F32)
            parts[ai] = d if parts[ai] is None else parts[ai] + d
        if nk == 1:
            epilogue(parts, ex_refs, out_refs)
            return
        k = pl.program_id(2)

        @pl.when(k == 0)
        def _():
            for a_ref, p in zip(acc_refs, parts):
                a_ref[...] = p

        @pl.when(k > 0)
        def _():
            for a_ref, p in zip(acc_refs, parts):
                a_ref[...] += p

        @pl.when(k == nk - 1)
        def _():
            epilogue([a[...] for a in acc_refs], ex_refs, out_refs)

    scratch = [pltpu.VMEM(s, F32) for s in acc_shapes] if nk > 1 else []
    return _ordered_call(
        body, [a for a, _ in ops] + [a for a, _ in extras], name=name, grid=grid,
        in_specs=[s for _, s in ops] + [s for _, s in extras],
        out_specs=[s for _, s in outs],
        out_shape=[o for o, _ in outs],
        scratch_shapes=scratch,
        compiler_params=_params(),
    )


NN = ((1,), (0,))
NT = ((1,), (1,))
TN = ((0,), (0,))


def _store(i=0):
    def ep(accs, ex, outs):
        outs[0][...] = accs[0].astype(outs[0].dtype)
    return ep


def _mm_nn(name, a, b, out_dtype, tm=1024, tn=1024, tk=2048, res=None, scale=1.0):
    M, K = a.shape
    N = b.shape[1]
    tm, tn, tk = _tile(M, tm, 16), _tile(N, tn), _tile(K, tk)
    spec = pl.BlockSpec((tm, tn), lambda i, j, k: (i, j))

    def ep(accs, ex, outs):
        v = accs[0] if scale == 1.0 else accs[0] * scale
        outs[0][...] = (v if res is None else ex[0][...] + v).astype(outs[0].dtype)

    return _matmul(name, (M // tm, N // tn, K // tk),
                   [(a, pl.BlockSpec((tm, tk), lambda i, j, k: (i, k))),
                    (b, pl.BlockSpec((tk, tn), lambda i, j, k: (k, j)))],
                   [(0, 0, 1)], NN, [(tm, tn)], [(_sds((M, N), out_dtype), spec)], ep,
                   [] if res is None else [(res, spec)])


def _mm_nt(name, a, b, out_dtype, tm=1024, tn=1024, tk=2048, epilogue=None, extras=()):
    M, K = a.shape
    N = b.shape[0]
    tm, tn, tk = _tile(M, tm, 16), _tile(N, tn), _tile(K, tk)
    outs = [(_sds((M, N), out_dtype), pl.BlockSpec((tm, tn), lambda i, j, k: (i, j)))]
    return _matmul(name, (M // tm, N // tn, K // tk),
                   [(a, pl.BlockSpec((tm, tk), lambda i, j, k: (i, k))),
                    (b, pl.BlockSpec((tn, tk), lambda i, j, k: (j, k)))],
                   [(0, 0, 1)], NT, [(tm, tn)], outs, epilogue or _store(), extras)


def _mm_tn(name, a, b, out_dtype, tm=512, tn=2048, tk=2048, scale=None):
    T, M = a.shape
    N = b.shape[1]
    tm, tn, tk = _tile(M, tm), _tile(N, tn), _tile(T, tk)

    def ep(accs, ex, outs):
        v = accs[0] if scale is None else accs[0] * scale
        outs[0][...] = v.astype(outs[0].dtype)

    outs = [(_sds((M, N), out_dtype), pl.BlockSpec((tm, tn), lambda i, j, k: (i, j)))]
    return _matmul(name, (M // tm, N // tn, T // tk),
                   [(a, pl.BlockSpec((tk, tm), lambda i, j, k: (k, i))),
                    (b, pl.BlockSpec((tk, tn), lambda i, j, k: (k, j)))],
                   [(0, 0, 1)], TN, [(tm, tn)], outs, ep)[0]


def _mm_nn_sm(name, a, w, out_dtype, tm=512, tk=2048, epilogue=None, extras=()):
    M, K = a.shape
    S, _, Ns = w.shape
    tm, tk = _tile(M, tm, 16), _tile(K, tk)
    outs = [(_sds((M, S * Ns), out_dtype), pl.BlockSpec((tm, Ns), lambda j, i, k: (i, j)))]
    return _matmul(name, (S, M // tm, K // tk),
                   [(a, pl.BlockSpec((tm, tk), lambda j, i, k: (i, k))),
                    (w, pl.BlockSpec((None, tk, Ns), lambda j, i, k: (j, k, 0)))],
                   [(0, 0, 1)], NN, [(tm, Ns)], outs, epilogue or _store(), extras)[0]


def _mm_nt_sm(name, a, w, out_dtype, tm=1024, tn=1024):
    M = a.shape[0]
    S, K, Ns = w.shape
    tm, tn = _tile(M, tm, 16), _tile(K, tn)
    outs = [(_sds((M, K), out_dtype), pl.BlockSpec((tm, tn), lambda i, n, j: (i, n)))]
    return _matmul(name, (M // tm, K // tn, S),
                   [(a, pl.BlockSpec((tm, Ns), lambda i, n, j: (i, j))),
                    (w, pl.BlockSpec((None, tn, Ns), lambda i, n, j: (j, n, 0)))],
                   [(0, 0, 1)], NT, [(tm, tn)], outs, _store())[0]


def _mm_tn_sm(name, a, b, S, out_dtype, tm=1024, tk=2048):
    T, M = a.shape
    Ns = b.shape[1] // S
    tm, tk = _tile(M, tm), _tile(T, tk)
    outs = [(_sds((S, M, Ns), out_dtype), pl.BlockSpec((None, tm, Ns), lambda j, i, k: (j, i, 0)))]
    return _matmul(name, (S, M // tm, T // tk),
                   [(a, pl.BlockSpec((tk, tm), lambda j, i, k: (k, i))),
                    (b, pl.BlockSpec((tk, Ns), lambda j, i, k: (k, j)))],
                   [(0, 0, 1)], TN, [(tm, Ns)], outs, _store())[0]


def _row_spec(shape, axis, tm):
    block = tuple(tm if d == axis else s for d, s in enumerate(shape))
    nd = len(shape)

    def imap(i):
        return tuple(i if d == axis else 0 for d in range(nd))
    return pl.BlockSpec(block, imap)


def _full_spec(shape):
    nd = len(shape)
    return pl.BlockSpec(tuple(shape), lambda i: (0,) * nd)


def _rowwise(name, fn, T, tm, rows, consts, outs, accs=()):
    tm = _tile(T, tm, 16)
    n_in = len(rows) + len(consts)
    n_out = len(outs)

    def body(*refs):
        in_refs = refs[:n_in]
        out_refs = refs[n_in:n_in + n_out]
        acc_refs = refs[n_in + n_out:]
        i = pl.program_id(0)

        def acc_add(ai, val):
            @pl.when(i == 0)
            def _():
                acc_refs[ai][...] = val

            @pl.when(i > 0)
            def _():
                acc_refs[ai][...] += val

        fn(in_refs, out_refs, acc_add)

    return _ordered_call(
        body, [a for a, _ in rows] + list(consts), name=name, grid=(T // tm,),
        in_specs=[_row_spec(a.shape, ax, tm) for a, ax in rows] + [_full_spec(c.shape) for c in consts],
        out_specs=[_row_spec(s, ax, tm) for s, _, ax in outs] + [_full_spec(s) for s in accs],
        out_shape=[_sds(s, d) for s, d, _ in outs] + [_sds(s, F32) for s in accs],
        compiler_params=_params(),
    )


def _rms(x, g, n=None):
    n = x.shape[-1] if n is None else n
    return x * lax.rsqrt(jnp.sum(x * x, axis=-1, keepdims=True) * (1.0 / n) + EPS) * g


def _norm_fwd(name, h, gain):
    T, D = h.shape

    def fn(ins, outs, acc):
        outs[0][...] = _rms(ins[0][...], ins[1][...]).astype(BF16)

    return _rowwise(name, fn, T, 256, [(h, 0)], [gain], [((T, D), BF16, 0)])[0]


def _norm_bwd(name, h, gain, dhn, dh_res):
    T, D = h.shape

    def fn(ins, outs, acc):
        _, vjp = jax.vjp(_rms, ins[0][...], ins[3][...])
        dh, dg = vjp(ins[1][...])
        dh = dh + ins[2][...]
        outs[0][...] = dh
        outs[1][...] = dh.astype(BF16)
        acc(0, dg)

    return _rowwise(name, fn, T, 256, [(h, 0), (dhn, 0), (dh_res, 0)], [gain],
                    [((T, D), F32, 0), ((T, D), BF16, 0)], [(1, D)])


def _ffn_fwd(tag, h, gain, get_in, get_out):
    T, D = h.shape
    hn = _norm_fwd(tag + "_norm", h, gain)
    w_in = get_in()
    S, _, Ns = w_in.shape
    half = S // 2
    F = half * Ns
    tm, tk = _tile(T, 256, 16), _tile(D, 2048)

    def ep(accs, ex, outs):
        g, u = accs
        act = g * jax.nn.sigmoid(g) * u
        outs[0][0] = g.astype(BF16)
        outs[0][1] = u.astype(BF16)
        outs[1][...] = act.astype(BF16)

    gu, act = _matmul(
        tag + "_in", (half, T // tm, D // tk),
        [(hn, pl.BlockSpec((tm, tk), lambda j, i, k: (i, k))),
         (w_in, pl.BlockSpec((None, tk, Ns), lambda j, i, k: (j, k, 0))),
         (w_in, pl.BlockSpec((None, tk, Ns), lambda j, i, k: (j + half, k, 0)))],
        [(0, 0, 1), (1, 0, 2)], NN, [(tm, Ns), (tm, Ns)],
        [(_sds((2, T, F), BF16), pl.BlockSpec((2, tm, Ns), lambda j, i, k: (0, i, j))),
         (_sds((T, F), BF16), pl.BlockSpec((tm, Ns), lambda j, i, k: (i, j)))],
        ep)

    w_out = get_out()
    h_new = _mm_nn(tag + "_out", act, w_out, F32, tm=1024, tn=512, tk=F, res=h, scale=FFN_RES)[0]
    return h_new, (h, hn, gu, act)


def _no_hook(*_):
    return None


def _ffn_bwd(tag, saved, gain, w_in, w_out, dh, dhb, hooks=(_no_hook, _no_hook, _no_hook)):
    h, hn, gu, act = saved
    T, D = h.shape
    S, _, Ns = w_in.shape
    half = S // 2
    F = half * Ns
    tm, tk = _tile(T, 512, 16), _tile(D, 2048)

    def ep(accs, ex, outs):
        dact = FFN_RES * accs[0]
        g = ex[0][0].astype(F32)
        u = ex[0][1].astype(F32)
        sg = jax.nn.sigmoid(g)
        outs[0][0] = (dact * u * (sg * (1.0 + g * (1.0 - sg)))).astype(BF16)
        outs[0][1] = (dact * (g * sg)).astype(BF16)

    gu_spec = pl.BlockSpec((2, tm, Ns), lambda j, i, k: (0, i, j))
    dgu = _matmul(
        tag + "_dact", (half, T // tm, D // tk),
        [(dhb, pl.BlockSpec((tm, tk), lambda j, i, k: (i, k))),
         (w_out, pl.BlockSpec((Ns, tk), lambda j, i, k: (j, k)))],
        [(0, 0, 1)], NT, [(tm, Ns)],
        [(_sds((2, T, F), BF16), gu_spec)], ep, extras=[(gu, gu_spec)])[0]

    dw_out = _mm_tn(tag + "_dwout", act, dhb, BF16, scale=FFN_RES)
    hooks[0](dw_out)

    tm = _tile(T, 1024, 16)
    tkd, tt = _tile(D, 512), _tile(T, 2048)
    dw_in = _matmul(
        tag + "_dwin", (S, D // tkd, T // tt),
        [(hn, pl.BlockSpec((tt, tkd), lambda j, i, k: (k, i))),
         (dgu, pl.BlockSpec((None, tt, Ns), lambda j, i, k: (j // half, k, j % half)))],
        [(0, 0, 1)], TN, [(tkd, Ns)],
        [(_sds((S, D, Ns), BF16), pl.BlockSpec((None, tkd, Ns), lambda j, i, k: (j, i, 0)))], _store())[0]
    hooks[1](dw_in)

    tn = _tile(D, 1024)
    dhn = _matmul(
        tag + "_dhn", (T // tm, D // tn, half),
        [(dgu, pl.BlockSpec((None, tm, Ns), lambda i, n, j: (0, i, j))),
         (dgu, pl.BlockSpec((None, tm, Ns), lambda i, n, j: (1, i, j))),
         (w_in, pl.BlockSpec((None, tn, Ns), lambda i, n, j: (j, n, 0))),
         (w_in, pl.BlockSpec((None, tn, Ns), lambda i, n, j: (j + half, n, 0)))],
        [(0, 0, 2), (0, 1, 3)], NT, [(tm, tn)],
        [(_sds((T, D), F32), pl.BlockSpec((tm, tn), lambda i, n, j: (i, n)))], _store())[0]
    hooks[2]()

    dh_in, dh_in_b, dgain = _norm_bwd(tag + "_dnorm", h, gain, dhn, dh)
    return dh_in, dh_in_b, dgain, dw_in, dw_out


def _ple_fwd(tag, h, p, get_proj, ple_norm, gate_norm, get_gate):
    T, D = h.shape
    e_raw = _mm_nn_sm(tag + "_proj", p, get_proj(), F32)
    hn = _norm_fwd(tag + "_norm", h, gate_norm)
    gate_raw = _mm_nn(tag + "_gate", hn, get_gate(), F32)[0]

    def fn(ins, outs, acc):
        e = _rms(ins[1][...], ins[3][...])
        outs[0][...] = ins[0][...] + e * jax.nn.sigmoid(ins[2][...])

    h_new = _rowwise(tag + "_mix", fn, T, 256, [(h, 0), (e_raw, 0), (gate_raw, 0)], [ple_norm],
                     [((T, D), F32, 0)])[0]
    return h_new, (h, hn, e_raw, gate_raw)


def _ple_bwd(tag, saved, p, w_proj, ple_norm, gate_norm, w_gate, dh, dhb):
    h, hn, e_raw, gate_raw = saved
    T, D = h.shape
    S = w_proj.shape[0]

    def fn(ins, outs, acc):
        def f(e_raw_, gate_raw_, g_):
            return _rms(e_raw_, g_) * jax.nn.sigmoid(gate_raw_)
        _, vjp = jax.vjp(f, ins[0][...], ins[1][...], ins[3][...])
        de, dgate, dg = vjp(ins[2][...])
        outs[0][...] = de.astype(BF16)
        outs[1][...] = dgate.astype(BF16)
        acc(0, dg)

    de, dgate, d_ple_norm = _rowwise(tag + "_dmix", fn, T, 256, [(e_raw, 0), (gate_raw, 0), (dh, 0)], [ple_norm],
                                     [((T, D), BF16, 0), ((T, D), BF16, 0)], [(1, D)])
    dw_proj = _mm_tn_sm(tag + "_dwproj", p, de, S, BF16)
    dw_gate = _mm_tn(tag + "_dwgate", hn, dgate, BF16)
    dhn = _mm_nt(tag + "_dhn", dgate, w_gate, F32)[0]
    dh_in, dh_in_b, d_gate_norm = _norm_bwd(tag + "_dnorm", h, gate_norm, dhn, dh)
    return dh_in, dh_in_b, d_ple_norm, d_gate_norm, dw_proj, dw_gate


def _rope(t, c, s1, s2):
    q = D_ROPE // 2
    return t * c + pltpu.roll(t, q, 1) * s1 + pltpu.roll(t, 128 - q, 1) * s2


def _rope_t(d, c, s1, s2):
    q = D_ROPE // 2
    return d * c + pltpu.roll(d * s1, 128 - q, 1) + pltpu.roll(d * s2, q, 1)


def _head_norm(lo, hi, g_lo, g_hi):
    ms = (jnp.sum(lo * lo, axis=-1, keepdims=True) + jnp.sum(hi * hi, axis=-1, keepdims=True)) * (1.0 / QK_DIM)
    inv = lax.rsqrt(ms + EPS)
    return lo * inv * g_lo, hi * inv * g_hi


def _qk_prep(qraw, kvraw, lat, tabs, gq, gk, H):
    T = qraw.shape[0]
    koff = lat.shape[1] - 128

    def fn(ins, outs, acc):
        q_ref, kv_ref, lat_ref, c_ref, s1_ref, s2_ref, gq_ref, gk_ref = ins
        c, s1, s2 = c_ref[...], s1_ref[...], s2_ref[...]
        kr = lat_ref[:, koff:koff + 128]
        for hd in range(H):
            o = hd * HEAD_PAD
            lo, hi = _head_norm(q_ref[:, o:o + 128], q_ref[:, o + 128:o + 256], gq_ref[:, 0:128], gq_ref[:, 128:256])
            outs[0][hd, :, 0:128] = lo.astype(BF16)
            outs[0][hd, :, 128:256] = _rope(hi, c, s1, s2).astype(BF16)
            lo, hi = _head_norm(kv_ref[:, o:o + 128], kr, gk_ref[:, 0:128], gk_ref[:, 128:256])
            outs[1][hd, :, 0:128] = lo.astype(BF16)
            outs[1][hd, :, 128:256] = _rope(hi, c, s1, s2).astype(BF16)
            outs[2][hd] = kv_ref[:, o + 128:o + 256].astype(BF16)

    return _rowwise("mla_qkprep", fn, T, 256, [(qraw, 0), (kvraw, 0), (lat, 0)] + [(t, 0) for t in tabs], [gq, gk],
                    [((H, T, HEAD_PAD), BF16, 1), ((H, T, HEAD_PAD), BF16, 1), ((H, T, D_V), BF16, 1)])


def _qk_prep_bwd(qraw, kvraw, lat, tabs, gq, gk, dQ, dK, dV, H):
    T = qraw.shape[0]
    koff = lat.shape[1] - 128

    def fn(ins, outs, acc):
        q_ref, kv_ref, lat_ref, c_ref, s1_ref, s2_ref, dq_ref, dk_ref, dv_ref, gq_ref, gk_ref = ins
        c, s1, s2 = c_ref[...], s1_ref[...], s2_ref[...]
        kr = lat_ref[:, koff:koff + 128]
        dkr = jnp.zeros_like(kr)
        dg = [None] * 4
        for hd in range(H):
            o = hd * HEAD_PAD
            _, vjp = jax.vjp(_head_norm, q_ref[:, o:o + 128], q_ref[:, o + 128:o + 256],
                             gq_ref[:, 0:128], gq_ref[:, 128:256])
            dlo, dhi, dg0, dg1 = vjp((dq_ref[hd, :, 0:128], _rope_t(dq_ref[hd, :, 128:256], c, s1, s2)))
            outs[0][:, o:o + 128] = dlo.astype(BF16)
            outs[0][:, o + 128:o + 256] = dhi.astype(BF16)
            _, vjp = jax.vjp(_head_norm, kv_ref[:, o:o + 128], kr, gk_ref[:, 0:128], gk_ref[:, 128:256])
            dlo, dhi, dg2, dg3 = vjp((dk_ref[hd, :, 0:128], _rope_t(dk_ref[hd, :, 128:256], c, s1, s2)))
            outs[1][:, o:o + 128] = dlo.astype(BF16)
            outs[1][:, o + 128:o + 256] = dv_ref[hd].astype(BF16)
            dkr = dkr + dhi
            for n, v in enumerate((dg0, dg1, dg2, dg3)):
                dg[n] = v if dg[n] is None else dg[n] + v
        outs[2][...] = dkr
        for n in range(4):
            acc(n, dg[n])

    W = H * HEAD_PAD
    return _rowwise("mla_dqkprep", fn, T, 128,
                    [(qraw, 0), (kvraw, 0), (lat, 0)] + [(t, 0) for t in tabs] + [(dQ, 1), (dK, 1), (dV, 1)], [gq, gk],
                    [((T, W), BF16, 0), ((T, W), BF16, 0), ((T, 128), F32, 0)], [(1, 128)] * 4)


def _attn_probs(q, k, c, tq):
    nk = k.shape[0]
    s = lax.dot_general(q, k, (NT, ((), ())), preferred_element_type=F32) * (QK_DIM ** -0.5)
    row = c * tq + lax.broadcasted_iota(jnp.int32, (tq, nk), 0)
    col = lax.broadcasted_iota(jnp.int32, (tq, nk), 1)
    s = jnp.where(col <= row, s, -jnp.inf)
    p = jnp.exp(s - jnp.max(s, axis=-1, keepdims=True))
    return p / jnp.sum(p, axis=-1, keepdims=True)


def _per_query_block(nq, fn):
    i = pl.program_id(1)
    for c in range(nq):
        pl.when(i == c)(functools.partial(fn, c))


def _attn_fwd(Q, K, V):
    H, T, _ = Q.shape
    tq = _tile(T, 256)

    def body(q_ref, k_ref, v_ref, o_ref):
        def block(c):
            nk = (c + 1) * tq
            p = _attn_probs(q_ref[...], k_ref[0:nk, :], c, tq)
            o_ref[...] = jnp.dot(p.astype(BF16), v_ref[0:nk, :], preferred_element_type=F32).astype(BF16)

        _per_query_block(T // tq, block)

    return _ordered_call(
        body, [Q, K, V], name="mla_attn", grid=(H, T // tq),
        in_specs=[pl.BlockSpec((None, tq, HEAD_PAD), lambda h, i: (h, i, 0)),
                  pl.BlockSpec((None, T, HEAD_PAD), lambda h, i: (h, 0, 0)),
                  pl.BlockSpec((None, T, D_V), lambda h, i: (h, 0, 0))],
        out_specs=pl.BlockSpec((tq, D_V), lambda h, i: (i, h)),
        out_shape=_sds((T, H * D_V), BF16),
        compiler_params=_params(),
    )


def _attn_bwd(Q, K, V, dO):
    H, T, _ = Q.shape
    tq = _tile(T, 256)

    def body(q_ref, k_ref, v_ref, do_ref, dq_ref, dk_ref, dv_ref):
        @pl.when(pl.program_id(1) == 0)
        def _():
            dk_ref[...] = jnp.zeros_like(dk_ref)
            dv_ref[...] = jnp.zeros_like(dv_ref)

        def block(c):
            nk = (c + 1) * tq
            q, k, do = q_ref[...], k_ref[0:nk, :], do_ref[...]
            p = _attn_probs(q, k, c, tq)
            dv_ref[0:nk, :] += lax.dot_general(p.astype(BF16), do, (TN, ((), ())), preferred_element_type=F32)
            dp = lax.dot_general(do, v_ref[0:nk, :], (NT, ((), ())), preferred_element_type=F32)
            ds = p * (dp - jnp.sum(p * dp, axis=-1, keepdims=True)) * (QK_DIM ** -0.5)
            dsb = ds.astype(BF16)
            dq_ref[...] = jnp.dot(dsb, k, preferred_element_type=F32)
            dk_ref[0:nk, :] += lax.dot_general(dsb, q, (TN, ((), ())), preferred_element_type=F32)

        _per_query_block(T // tq, block)

    return _ordered_call(
        body, [Q, K, V, dO], name="mla_dattn", grid=(H, T // tq),
        in_specs=[pl.BlockSpec((None, tq, HEAD_PAD), lambda h, i: (h, i, 0)),
                  pl.BlockSpec((None, T, HEAD_PAD), lambda h, i: (h, 0, 0)),
                  pl.BlockSpec((None, T, D_V), lambda h, i: (h, 0, 0)),
                  pl.BlockSpec((tq, D_V), lambda h, i: (i, h))],
        out_specs=[pl.BlockSpec((None, tq, HEAD_PAD), lambda h, i: (h, i, 0)),
                   pl.BlockSpec((None, T, HEAD_PAD), lambda h, i: (h, 0, 0)),
                   pl.BlockSpec((None, T, D_V), lambda h, i: (h, 0, 0))],
        out_shape=[_sds((H, T, HEAD_PAD), F32), _sds((H, T, HEAD_PAD), F32), _sds((H, T, D_V), F32)],
        compiler_params=_params(),
    )


def _mla_fwd(h, gain, tabs, get, q_lat_norm, kv_lat_norm, gq, gk):
    T, D = h.shape
    QL, KL = q_lat_norm.shape[1], kv_lat_norm.shape[1]
    hn = _norm_fwd("mla_norm", h, gain)
    w = dict(m_in=get["m_in"]())
    lat = _mm_nn("mla_lat", hn, w["m_in"], F32, tn=w["m_in"].shape[1])[0]

    def fn(ins, outs, acc):
        outs[0][...] = _rms(ins[0][:, 0:QL], ins[1][...]).astype(BF16)
        outs[1][...] = _rms(ins[0][:, QL:QL + KL], ins[2][...]).astype(BF16)

    cq, ckv = _rowwise("mla_latnorm", fn, T, 256, [(lat, 0)], [q_lat_norm, kv_lat_norm],
                       [((T, QL), BF16, 0), ((T, KL), BF16, 0)])
    w["uq"], w["ukv"] = get["uq"](), get["ukv"]()
    H = w["uq"].shape[0] * w["uq"].shape[2] // HEAD_PAD
    qraw = _mm_nn_sm("mla_uq", cq, w["uq"], F32)
    kvraw = _mm_nn_sm("mla_ukv", ckv, w["ukv"], F32)
    Q, K, V = _qk_prep(qraw, kvraw, lat, tabs, gq, gk, H)
    O = _attn_fwd(Q, K, V)
    w["wo"] = get["wo"]()

    h_new = _mm_nn("mla_out", O, w["wo"], F32, res=h)[0]
    return h_new, (h, hn, lat, cq, ckv, qraw, kvraw, Q, K, V, O)


def _mla_bwd(saved, gain, tabs, w, q_lat_norm, kv_lat_norm, gq, gk, dh, dhb, hook=_no_hook):
    h, hn, lat, cq, ckv, qraw, kvraw, Q, K, V, O = saved
    T, D = h.shape
    H = Q.shape[0]
    S = w["uq"].shape[0]
    QL, KL = q_lat_norm.shape[1], kv_lat_norm.shape[1]
    dO = _mm_nt("mla_dO", dhb, w["wo"], BF16)[0]
    dwo = _mm_tn("mla_dwo", O, dhb, BF16)
    hook()
    dQ, dK, dV = _attn_bwd(Q, K, V, dO)
    dqraw, dkvraw, dkr, dgq0, dgq1, dgk0, dgk1 = _qk_prep_bwd(qraw, kvraw, lat, tabs, gq, gk, dQ, dK, dV, H)
    dcq = _mm_nt_sm("mla_dcq", dqraw, w["uq"], F32)
    dckv = _mm_nt_sm("mla_dckv", dkvraw, w["ukv"], F32)
    dwuq = _mm_tn_sm("mla_dwuq", cq, dqraw, S, BF16)
    dwukv = _mm_tn_sm("mla_dwukv", ckv, dkvraw, S, BF16)

    def fn(ins, outs, acc):
        _, vjp = jax.vjp(_rms, ins[0][:, 0:QL], ins[4][...])
        d, dgq_ = vjp(ins[1][...])
        outs[0][:, 0:QL] = d.astype(BF16)
        _, vjp = jax.vjp(_rms, ins[0][:, QL:QL + KL], ins[5][...])
        d, dgkv_ = vjp(ins[2][...])
        outs[0][:, QL:QL + KL] = d.astype(BF16)
        outs[0][:, QL + KL:QL + KL + 128] = ins[3][...].astype(BF16)
        acc(0, dgq_)
        acc(1, dgkv_)

    dlat, d_qln, d_kvln = _rowwise("mla_dlatnorm", fn, T, 256, [(lat, 0), (dcq, 0), (dckv, 0), (dkr, 0)],
                                   [q_lat_norm, kv_lat_norm], [(lat.shape, BF16, 0)], [(1, QL), (1, KL)])
    dhn = _mm_nt("mla_dhn", dlat, w["m_in"], F32, tk=lat.shape[1])[0]
    dw_min = _mm_tn("mla_dwin", hn, dlat, BF16, tn=lat.shape[1])
    dh_in, dh_in_b, dgain = _norm_bwd("mla_dnorm", h, gain, dhn, dh)
    d_gq = jnp.concatenate([dgq0, dgq1], axis=1)[:, :QK_DIM]
    d_gk = jnp.concatenate([dgk0, dgk1], axis=1)[:, :QK_DIM]
    return dh_in, dh_in_b, dgain, d_qln, d_kvln, d_gq, d_gk, dw_min, dwuq, dwukv, dwo


def _conv_rows(T):
    return _tile(T, 128, 8)


def _dwconv_fwd(u, w_dw, b_dw):
    T, C = u.shape
    tc, R = _tile(C, 256), _conv_rows(T)
    off = CONV_PAD - (CONV_WIDTH - 1)

    def body(u_ref, w_ref, b_ref, y_ref, pad_ref):
        pad_ref[0:CONV_PAD, :] = jnp.zeros((CONV_PAD, tc), F32)
        pad_ref[CONV_PAD:CONV_PAD + T, :] = u_ref[...]
        for r in range(T // R):
            acc = jnp.broadcast_to(b_ref[...], (R, tc))
            for j in range(CONV_WIDTH):
                acc = acc + w_ref[j:j + 1, :] * pad_ref[r * R + off + j:r * R + off + j + R, :]
            y_ref[r * R:(r + 1) * R, :] = acc

    return _ordered_call(
        body, [u, w_dw, b_dw], name="conv_dw", grid=(C // tc,),
        in_specs=[pl.BlockSpec((T, tc), lambda c: (0, c)), pl.BlockSpec((32, tc), lambda c: (0, c)),
                  pl.BlockSpec((1, tc), lambda c: (0, c))],
        out_specs=pl.BlockSpec((T, tc), lambda c: (0, c)),
        out_shape=_sds((T, C), F32),
        scratch_shapes=[pltpu.VMEM((T + CONV_PAD, tc), F32)],
        compiler_params=_params(),
    )


def _dwconv_bwd(u, w_dw, dy):
    T, C = u.shape
    tc, R = _tile(C, 256), _conv_rows(T)
    off = CONV_PAD - (CONV_WIDTH - 1)

    def body(u_ref, w_ref, dy_ref, du_ref, dw_ref, db_ref, upad_ref, dpad_ref):
        upad_ref[0:CONV_PAD, :] = jnp.zeros((CONV_PAD, tc), F32)
        upad_ref[CONV_PAD:CONV_PAD + T, :] = u_ref[...]
        dpad_ref[0:T, :] = dy_ref[...]
        dpad_ref[T:T + CONV_PAD, :] = jnp.zeros((CONV_PAD, tc), F32)
        for r in range(T // R):
            acc = jnp.zeros((R, tc), F32)
            for j in range(CONV_WIDTH):
                s = r * R + (CONV_WIDTH - 1) - j
                acc = acc + w_ref[j:j + 1, :] * dpad_ref[s:s + R, :]
            du_ref[r * R:(r + 1) * R, :] = acc
        for j in range(CONV_WIDTH):
            acc = jnp.zeros((R, tc), F32)
            for r in range(T // R):
                acc = acc + dy_ref[r * R:(r + 1) * R, :] * upad_ref[r * R + off + j:r * R + off + j + R, :]
            dw_ref[j:j + 1, :] = jnp.sum(acc, axis=0, keepdims=True)
        dw_ref[CONV_WIDTH:32, :] = jnp.zeros((32 - CONV_WIDTH, tc), F32)
        db_ref[...] = jnp.sum(dy_ref[...], axis=0, keepdims=True)

    return _ordered_call(
        body, [u, w_dw, dy], name="conv_ddw", grid=(C // tc,),
        in_specs=[pl.BlockSpec((T, tc), lambda c: (0, c)), pl.BlockSpec((32, tc), lambda c: (0, c)),
                  pl.BlockSpec((T, tc), lambda c: (0, c))],
        out_specs=[pl.BlockSpec((T, tc), lambda c: (0, c)), pl.BlockSpec((32, tc), lambda c: (0, c)),
                   pl.BlockSpec((1, tc), lambda c: (0, c))],
        out_shape=[_sds((T, C), F32), _sds((32, C), F32), _sds((1, C), F32)],
        scratch_shapes=[pltpu.VMEM((T + CONV_PAD, tc), F32), pltpu.VMEM((T + CONV_PAD, tc), F32)],
        compiler_params=_params(),
    )


def _ln_silu(y, g, b):
    mu = jnp.mean(y, axis=-1, keepdims=True)
    yc = y - mu
    z = yc * lax.rsqrt(jnp.mean(yc * yc, axis=-1, keepdims=True) + EPS) * g + b
    return z * jax.nn.sigmoid(z)


def _conv_fwd(h, gain, get):
    T, D = h.shape
    hn = _norm_fwd("conv_norm", h, gain)
    w = dict(pw1=get["pw1"]())
    b_pw1, w_dw, b_dw, ln_g, ln_b = get["small"]()
    S, _, Ns = w["pw1"].shape
    half = S // 2
    C = half * Ns
    tm, tk = _tile(T, 512, 16), _tile(D, 2048)

    def ep(accs, ex, outs):
        a = accs[0] + ex[0][...]
        g = accs[1] + ex[1][...]
        outs[0][0] = a.astype(BF16)
        outs[0][1] = g.astype(BF16)
        outs[1][...] = a * jax.nn.sigmoid(g)

    ag, u = _matmul(
        "conv_pw1", (half, T // tm, D // tk),
        [(hn, pl.BlockSpec((tm, tk), lambda j, i, k: (i, k))),
         (w["pw1"], pl.BlockSpec((None, tk, Ns), lambda j, i, k: (j, k, 0))),
         (w["pw1"], pl.BlockSpec((None, tk, Ns), lambda j, i, k: (j + half, k, 0)))],
        [(0, 0, 1), (1, 0, 2)], NN, [(tm, Ns), (tm, Ns)],
        [(_sds((2, T, C), BF16), pl.BlockSpec((2, tm, Ns), lambda j, i, k: (0, i, j))),
         (_sds((T, C), F32), pl.BlockSpec((tm, Ns), lambda j, i, k: (i, j)))],
        ep,
        extras=[(b_pw1, pl.BlockSpec((None, 1, Ns), lambda j, i, k: (j, 0, 0))),
                (b_pw1, pl.BlockSpec((None, 1, Ns), lambda j, i, k: (j + half, 0, 0)))])
    y = _dwconv_fwd(u, w_dw, b_dw)

    def fn(ins, outs, acc):
        outs[0][...] = _ln_silu(ins[0][...], ins[1][...], ins[2][...]).astype(BF16)

    s = _rowwise("conv_ln", fn, T, 256, [(y, 0)], [ln_g, ln_b], [((T, C), BF16, 0)])[0]

    w["pw2"] = get["pw2"]()
    h_new = _mm_nn("conv_pw2", s, w["pw2"], F32, res=h)[0]
    return h_new, (h, hn, ag, u, y, s)


def _conv_bwd(saved, gain, w, w_dw, ln_g, ln_b, dh, dhb):
    h, hn, ag, u, y, s = saved
    T, D = h.shape
    S, _, Ns = w["pw1"].shape
    half = S // 2
    C = half * Ns
    ds = _mm_nt("conv_ds", dhb, w["pw2"], F32)[0]
    dw_pw2 = _mm_tn("conv_dwpw2", s, dhb, BF16)

    def fn(ins, outs, acc):
        _, vjp = jax.vjp(_ln_silu, ins[0][...], ins[2][...], ins[3][...])
        dy, dg, db = vjp(ins[1][...])
        outs[0][...] = dy
        acc(0, dg)
        acc(1, db)

    dy, d_ln_g, d_ln_b = _rowwise("conv_dln", fn, T, 256, [(y, 0), (ds, 0)], [ln_g, ln_b],
                                  [((T, C), F32, 0)], [(1, C), (1, C)])
    du, d_w_dw, d_b_dw = _dwconv_bwd(u, w_dw, dy)

    def fn2(ins, outs, acc):
        a = ins[0][0].astype(F32)
        g = ins[0][1].astype(F32)
        du_ = ins[1][...]
        sg = jax.nn.sigmoid(g)
        da = du_ * sg
        dg = du_ * a * sg * (1.0 - sg)
        outs[0][0] = da.astype(BF16)
        outs[0][1] = dg.astype(BF16)
        acc(0, jnp.sum(da, axis=0, keepdims=True))
        acc(1, jnp.sum(dg, axis=0, keepdims=True))

    dag, d_b_a, d_b_g = _rowwise("conv_dglu", fn2, T, 256, [(ag, 1), (du, 0)], [],
                                 [((2, T, C), BF16, 1)], [(1, C), (1, C)])
    tm, tn = _tile(T, 1024, 16), _tile(D, 1024)
    dhn = _matmul(
        "conv_dhn", (T // tm, D // tn, half),
        [(dag, pl.BlockSpec((None, tm, Ns), lambda i, n, j: (0, i, j))),
         (dag, pl.BlockSpec((None, tm, Ns), lambda i, n, j: (1, i, j))),
         (w["pw1"], pl.BlockSpec((None, tn, Ns), lambda i, n, j: (j, n, 0))),
         (w["pw1"], pl.BlockSpec((None, tn, Ns), lambda i, n, j: (j + half, n, 0)))],
        [(0, 0, 2), (0, 1, 3)], NT, [(tm, tn)],
        [(_sds((T, D), F32), pl.BlockSpec((tm, tn), lambda i, n, j: (i, n)))], _store())[0]
    tkd, tt = _tile(D, 1024), _tile(T, 2048)
    dw_pw1 = _matmul(
        "conv_dwpw1", (S, D // tkd, T // tt),
        [(hn, pl.BlockSpec((tt, tkd), lambda j, i, k: (k, i))),
         (dag, pl.BlockSpec((None, tt, Ns), lambda j, i, k: (j // half, k, j % half)))],
        [(0, 0, 1)], TN, [(tkd, Ns)],
        [(_sds((S, D, Ns), BF16), pl.BlockSpec((None, tkd, Ns), lambda j, i, k: (j, i, 0)))], _store())[0]
    dh_in, dh_in_b, dgain = _norm_bwd("conv_dnorm", h, gain, dhn, dh)
    d_b_pw1 = jnp.concatenate([d_b_a, d_b_g], axis=1)
    return dh_in, dh_in_b, dgain, dw_pw1, d_b_pw1, d_w_dw, d_b_dw, d_ln_g, d_ln_b, dw_pw2


def _loss_head(y, target):
    T, D = y.shape

    def fn(ins, outs, acc):
        e = ins[0][...] - ins[1][...]
        d = e * (1.0 / D)
        outs[0][...] = d
        outs[1][...] = d.astype(BF16)
        part = jnp.sum(jnp.sum(e * e, axis=-1, keepdims=True), axis=0, keepdims=True) * (0.5 / D)
        acc(0, jnp.broadcast_to(part, (1, 128)))

    return _rowwise("loss_head", fn, T, 256, [(y, 0), (target, 0)], [], [((T, D), F32, 0), ((T, D), BF16, 0)],
                    [(1, 128)])


def _place():
    return lax.axis_index("x"), lax.axis_index("y"), lax.axis_index("c")


def _peer(j):
    x, y, c = _place()
    return (1 - x if j & 4 else x, 1 - y if j & 2 else y, 1 - c if j & 1 else c)


def _index(place):
    return 4 * place[0] + 2 * place[1] + place[2]


HBM = pl.BlockSpec(memory_space=pltpu.HBM)
SEM = pl.BlockSpec(memory_space=pltpu.SEMAPHORE)
EFFECT = pltpu.SideEffectType.DATAFLOW_SIDE_EFFECTING


def _chip(j):
    x, y, _ = _place()
    return (1 - x if j & 2 else x, 1 - y if j & 1 else y)


def _chip_index(chip):
    return 2 * chip[0] + chip[1]


def _remote(src, dst, send, recv, k, device):
    return pltpu.make_async_remote_copy(src_ref=src, dst_ref=dst, send_sem=send.at[k], recv_sem=recv.at[k],
                                        device_id=device, device_id_type=MESH)


def _hbm(arrays):
    return [pltpu.with_memory_space_constraint(a, pltpu.HBM) for a in arrays]


def _split_call(name, body, ins, sems_in, sems_out):
    n, ns_in, ns_out = len(ins), len(sems_in), len(sems_out)
    if any(a is _LAST[0] for a in ins):
        _LAST[0] = None

    def kernel_body(*refs):
        in_refs = refs[:n]
        si = refs[n:n + ns_in]
        so = refs[n + ns_in:n + ns_in + ns_out]
        tok = refs[-1]
        body(in_refs, si, so, tok)
        tok[...] = jnp.zeros_like(tok)

    res = _ordered_call(
        kernel_body, _hbm(ins) + list(sems_in), chain_out=ns_out + n, name=name,
        in_specs=[HBM] * n + [SEM] * ns_in,
        out_specs=[SEM] * ns_out + [HBM] * n + [pl.BlockSpec(memory_space=pltpu.VMEM)],
        out_shape=[pltpu.SemaphoreType.DMA((s,)) for s in sems_out] + [pltpu.HBM(a.shape, a.dtype) for a in ins]
        + [_sds((8, 128), F32)],
        input_output_aliases={i: ns_out + i for i in range(n)},
        compiler_params=pltpu.CompilerParams(has_side_effects=EFFECT),
    )
    sems = list(res[:ns_out])
    arrays = list(res[ns_out:ns_out + n])
    return sems, arrays, res[-1]


def _ag_start(name, groups):
    flat = [s for grp in groups for s in grp]
    zones = [lax.empty((N_DEV,) + s.shape, s.dtype) for s in flat]
    n = len(flat)
    sizes = []
    for grp in groups:
        sizes += [7 * len(grp), 7 * len(grp), len(grp)]

    def body(ins, si, so, tok):
        x, y, c = _place()
        me = _index((x, y, c))
        base = 0
        for gi, grp in enumerate(groups):
            send, recv, local = so[3 * gi:3 * gi + 3]
            for w in range(len(grp)):
                src, dst = ins[base + w], ins[n + base + w].at[me]
                pltpu.make_async_copy(src, dst, local.at[w]).start()
                _remote(src, dst, send, recv, 7 * w, (x, y, 1 - c)).start()
                for j in (1, 2, 3):
                    _remote(src, dst, send, recv, 7 * w + j, (*_chip(j), c)).start()
            base += len(grp)

    sems, arrays, token = _split_call(name, body, flat + zones, [], sizes)
    out, base = [], 0
    for gi, grp in enumerate(groups):
        k = len(grp)
        out.append((sems[3 * gi:3 * gi + 3], arrays[base:base + k], arrays[n + base:n + base + k]))
        base += k
    return out, token


def _ag_forward(name, handle):
    sems, shards, zones = handle
    k = len(shards)

    def arrive(ins, si, so, tok):
        send, recv, _ = si
        _, _, c = _place()
        for j in (1, 2, 3):
            for w in range(k):
                blk = ins[k + w].at[_index((*_chip(j), c))]
                _remote(ins[w], blk, send, recv, 7 * w + j, (*_chip(j), c)).wait_recv()

    _, arrays, _ = _split_call(name + "_arrive", arrive, list(shards) + list(zones), sems, [])

    def pass_on(ins, si, so, tok):
        fsend, frecv = so
        x, y, c = _place()
        for j in (1, 2, 3):
            for w in range(k):
                blk = ins[w].at[_index((*_chip(j), c))]
                _remote(blk, blk, fsend, frecv, 3 * w + j - 1, (x, y, 1 - c)).start()

    fsems, zones2, _ = _split_call(name + "_pass", pass_on, arrays[k:], [], [3 * k, 3 * k])
    return (list(sems) + fsems, arrays[:k], zones2)


def _ag_wait(name, handle):
    sems, shards, zones = handle
    k = len(shards)

    def body(ins, si, so, tok):
        send, recv, local, fsend, frecv = si
        x, y, c = _place()
        sib = (x, y, 1 - c)
        for w in range(k):
            zone = ins[k + w]
            _remote(ins[w], zone.at[_index(sib)], send, recv, 7 * w, sib).wait_recv()
            for j in (1, 2, 3):
                blk = zone.at[_index((*_chip(j), 1 - c))]
                _remote(blk, blk, fsend, frecv, 3 * w + j - 1, sib).wait_recv()
        for w in range(k):
            zone = ins[k + w]
            mine = zone.at[_index((x, y, c))]
            _remote(ins[w], mine, send, recv, 7 * w, sib).wait_send()
            for j in (1, 2, 3):
                _remote(ins[w], mine, send, recv, 7 * w + j, (*_chip(j), c)).wait_send()
                blk = zone.at[_index((*_chip(j), c))]
                _remote(blk, blk, fsend, frecv, 3 * w + j - 1, sib).wait_send()
            pltpu.make_async_copy(ins[w], mine, local.at[w]).wait()

    _, arrays, _ = _split_call(name, body, list(shards) + list(zones), sems, [])
    return arrays[k:]


def _rs_pair_start(name, grads):
    n = len(grads)
    zones = [lax.empty((4,) + g.shape[1:], g.dtype) for g in grads]

    def body(ins, si, so, tok):
        send, recv = so
        x, y, c = _place()
        for w in range(n):
            for q in range(4):
                _remote(ins[w].at[2 * q + 1 - c], ins[n + w].at[q], send, recv, 4 * w + q, (x, y, 1 - c)).start()

    sems, arrays, token = _split_call(name, body, list(grads) + zones, [], [4 * n, 4 * n])
    return (sems, arrays[:n], arrays[n:]), token


def _rs_pair_wait(name, handle):
    sems, grads, zones = handle
    n = len(grads)

    def body(ins, si, so, tok):
        send, recv = si
        x, y, c = _place()
        for w in range(n):
            for q in range(4):
                cp = _remote(ins[w].at[2 * q + 1 - c], ins[n + w].at[q], send, recv, 4 * w + q, (x, y, 1 - c))
                cp.wait_recv()
                cp.wait_send()

    _, arrays, _ = _split_call(name, body, list(grads) + list(zones), sems, [])
    return arrays[:n], arrays[n:]


def _pair_sum(name, g, got, core):
    _, R, C = g.shape
    g4 = g.reshape(4, 2, R, C)
    tr = _tile(R, 1024, 16)

    def body(c_ref, g_ref, a_ref, *rest):
        o_ref = rest[-1]
        o_ref[...] = (g_ref[...].astype(F32) + a_ref[...].astype(F32)).astype(o_ref.dtype)

    prev = [] if _LAST[0] is None or _LAST[0] is g or _LAST[0] is got else [_LAST[0]]
    out = pl.pallas_call(
        body, name=name,
        grid_spec=pltpu.PrefetchScalarGridSpec(
            num_scalar_prefetch=1, grid=(4, R // tr),
            in_specs=[pl.BlockSpec((None, None, tr, C), lambda q, i, c_ref: (q, c_ref[0], i, 0)),
                      pl.BlockSpec((None, tr, C), lambda q, i, c_ref: (q, i, 0))] + [ANY] * len(prev),
            out_specs=pl.BlockSpec((None, tr, C), lambda q, i, c_ref: (q, i, 0))),
        out_shape=_sds((4, R, C), g.dtype),
        compiler_params=_params(),
    )(core, g4, got, *prev)
    _LAST[0] = out
    return out


def _rs_chip_start(name, sums):
    n = len(sums)
    zones = [lax.empty(s.shape, s.dtype) for s in sums]

    def body(ins, si, so, tok):
        send, recv, local = so
        x, y, c = _place()
        mine = _chip_index((x, y))
        for w in range(n):
            pltpu.make_async_copy(ins[w].at[mine], ins[n + w].at[mine], local.at[w]).start()
            for j in (1, 2, 3):
                _remote(ins[w].at[_chip_index(_chip(j))], ins[n + w].at[mine], send, recv, 3 * w + j - 1,
                        (*_chip(j), c)).start()

    sems, arrays, token = _split_call(name, body, list(sums) + zones, [], [3 * n, 3 * n, n])
    return (sems, arrays[:n], arrays[n:]), token


def _rs_chip_wait(name, handle):
    sems, sums, zones = handle
    n = len(sums)

    def body(ins, si, so, tok):
        send, recv, local = si
        x, y, c = _place()
        mine = _chip_index((x, y))
        for w in range(n):
            for j in (1, 2, 3):
                _remote(ins[w].at[mine], ins[n + w].at[_chip_index(_chip(j))], send, recv, 3 * w + j - 1,
                        (*_chip(j), c)).wait_recv()
        for w in range(n):
            for j in (1, 2, 3):
                _remote(ins[w].at[_chip_index(_chip(j))], ins[n + w].at[mine], send, recv, 3 * w + j - 1,
                        (*_chip(j), c)).wait_send()
            pltpu.make_async_copy(ins[w].at[mine], ins[n + w].at[mine], local.at[w]).wait()

    _, arrays, _ = _split_call(name, body, list(sums) + list(zones), sems, [])
    return arrays[n:]


def _chip_sum(name, slots, layer, layers, into=None):
    _, R, C = slots.shape
    tr = _tile(R, 256, 16)

    def body(*refs):
        s_ref, o_ref = refs[0], refs[-1]
        total = s_ref[0].astype(F32)
        for k in range(1, 4):
            total = total + s_ref[k].astype(F32)
        o_ref[...] = total

    extra = [] if into is None else [into]
    return _ordered_call(
        body, [slots] + extra, name=name, grid=(R // tr,),
        in_specs=[pl.BlockSpec((4, tr, C), lambda i: (0, i, 0))] + [ANY] * len(extra),
        out_specs=pl.BlockSpec((None, tr, C), lambda i: (layer, i, 0)),
        out_shape=_sds((layers, R, C), F32),
        input_output_aliases={} if into is None else {1: 0},
        compiler_params=_params(),
    )


def _pack_rows(name, parts, rows):
    C = parts[0].shape[1]

    def body(*refs):
        o_ref = refs[-1]
        off = 0
        for r in refs[:-1]:
            o_ref[off:off + r.shape[0], :] = r[...]
            off += r.shape[0]
        if off < rows:
            o_ref[off:rows, :] = jnp.zeros((rows - off, C), F32)

    vmem = pl.BlockSpec(memory_space=pltpu.VMEM)
    return _ordered_call(body, list(parts), name=name, in_specs=[vmem] * len(parts), out_specs=vmem,
                         out_shape=_sds((rows, C), F32))


def _reduce_small(rep, shd):
    R, C = rep.shape
    _, Rs, Cs = shd.shape

    def body(r_ref, s_ref, or_ref, os_ref, all_r, all_s, send, recv):
        me = _index(_place())
        all_r[me] = r_ref[...]
        all_s[me] = s_ref[me]
        copies = []
        for j in range(1, N_DEV):
            there = _index(_peer(j))
            copies.append((
                pltpu.make_async_remote_copy(src_ref=r_ref, dst_ref=all_r.at[me], send_sem=send.at[2 * j - 2],
                                             recv_sem=recv.at[2 * j - 2], device_id=_peer(j), device_id_type=MESH),
                pltpu.make_async_remote_copy(src_ref=s_ref.at[there], dst_ref=all_s.at[me],
                                             send_sem=send.at[2 * j - 1], recv_sem=recv.at[2 * j - 1],
                                             device_id=_peer(j), device_id_type=MESH)))
        for a, b in copies:
            a.start()
            b.start()
        for j in range(1, N_DEV):
            there = _index(_peer(j))
            pltpu.make_async_remote_copy(src_ref=r_ref, dst_ref=all_r.at[there], send_sem=send.at[2 * j - 2],
                                         recv_sem=recv.at[2 * j - 2], device_id=_peer(j),
                                         device_id_type=MESH).wait_recv()
            pltpu.make_async_remote_copy(src_ref=s_ref.at[me], dst_ref=all_s.at[there], send_sem=send.at[2 * j - 1],
                                         recv_sem=recv.at[2 * j - 1], device_id=_peer(j),
                                         device_id_type=MESH).wait_recv()
        for a, b in copies:
            a.wait_send()
            b.wait_send()
        tot_r, tot_s = all_r[0], all_s[0]
        for k in range(1, N_DEV):
            tot_r = tot_r + all_r[k]
            tot_s = tot_s + all_s[k]
        or_ref[...] = tot_r
        os_ref[...] = tot_s

    vmem = pl.BlockSpec(memory_space=pltpu.VMEM)
    return _ordered_call(
        body, [rep, shd], name="reduce_small",
        in_specs=[vmem, vmem], out_specs=[vmem, vmem],
        out_shape=[_sds((R, C), F32), _sds((Rs, Cs), F32)],
        scratch_shapes=[pltpu.VMEM((N_DEV, R, C), F32), pltpu.VMEM((N_DEV, Rs, Cs), F32),
                        pltpu.SemaphoreType.DMA((2 * N_DEV - 2,)), pltpu.SemaphoreType.DMA((2 * N_DEV - 2,))],
        compiler_params=_params(),
    )


def _cast_layer(name, w, layer):
    _, R, C = w.shape
    tr = _tile(R, 1024, 16)

    def body(w_ref, o_ref):
        o_ref[...] = w_ref[...].astype(BF16)

    return _ordered_call(
        body, [w], name=name, grid=(R // tr,),
        in_specs=[pl.BlockSpec((None, tr, C), lambda i: (layer, i, 0))],
        out_specs=pl.BlockSpec((tr, C), lambda i: (i, 0)),
        out_shape=_sds((R, C), BF16),
        compiler_params=_params(),
    )


def _adam_math(w, g, m, v):
    c1 = 1.0 / (1.0 - ADAM_B1 ** ADAM_STEP)
    c2 = 1.0 / (1.0 - ADAM_B2 ** ADAM_STEP)
    nm = ADAM_B1 * m + (1.0 - ADAM_B1) * g
    nv = ADAM_B2 * v + (1.0 - ADAM_B2) * (g * g)
    return -ADAM_LR * ((nm * c1) / (jnp.sqrt(nv * c2) + ADAM_EPS) + ADAM_WD * w), nm, nv


def _sum_adam(name, slots, w, m, v, layer, into=None):
    L, R, C = w.shape
    tr = _tile(R, 256, 16)

    def body(*refs):
        s_ref, w_ref, m_ref, v_ref = refs[:4]
        g_ref, d_ref, nm_ref, nv_ref = refs[-4:]
        g = s_ref[0].astype(F32)
        for k in range(1, 4):
            g = g + s_ref[k].astype(F32)
        g_ref[...] = g
        d_ref[...], nm_ref[...], nv_ref[...] = _adam_math(w_ref[...], g, m_ref[...], v_ref[...])

    spec = pl.BlockSpec((None, tr, C), lambda i: (layer, i, 0))
    extra = [] if into is None else list(into)
    return _ordered_call(
        body, [slots, w, m, v] + extra, name=name, grid=(R // tr,),
        in_specs=[pl.BlockSpec((4, tr, C), lambda i: (0, i, 0)), spec, spec, spec] + [ANY] * len(extra),
        out_specs=[spec] * 4,
        out_shape=[_sds((L, R, C), F32)] * 4,
        input_output_aliases={4 + k: k for k in range(len(extra))},
        compiler_params=_params(),
    )


def _adamw(name, w, g, m, v):
    shape = w.shape
    R, C = shape[-2], shape[-1]
    L = 1
    for s in shape[:-2]:
        L *= s
    w3, g3, m3, v3 = (a.reshape(L, R, C) for a in (w, g, m, v))
    tr = _tile(R, 256, 8)

    def body(w_ref, g_ref, m_ref, v_ref, d_ref, nm_ref, nv_ref):
        d_ref[...], nm_ref[...], nv_ref[...] = _adam_math(w_ref[...], g_ref[...], m_ref[...], v_ref[...])

    spec = pl.BlockSpec((None, tr, C), lambda l, i: (l, i, 0))
    outs = _ordered_call(
        body, [w3, g3, m3, v3], name=name, grid=(L, R // tr),
        in_specs=[spec] * 4, out_specs=[spec] * 3,
        out_shape=[_sds((L, R, C), F32)] * 3,
        compiler_params=_params(),
    )
    return tuple(o.reshape(shape) for o in outs)


def _pad_rows(a, rows):
    return jnp.pad(a, ((0, rows - a.shape[0]), (0, 0)))


def _pad_cols(a, cols):
    return jnp.pad(a, ((0, 0), (0, cols - a.shape[1])))


def _rope_tables(positions):
    q = D_ROPE // 2
    inv_freq = ROPE_THETA ** (-jnp.arange(0, D_ROPE, 2, dtype=F32) / D_ROPE)
    ang = positions.astype(F32)[:, None] * inv_freq
    cos, sin = jnp.cos(ang), jnp.sin(ang)
    z = jnp.zeros_like(cos)
    zz = jnp.zeros((cos.shape[0], 128 - 2 * q), F32)
    c = jnp.concatenate([cos, cos, zz], axis=1)
    s1 = jnp.concatenate([z, sin, zz], axis=1)
    s2 = jnp.concatenate([-sin, z, zz], axis=1)
    return c, s1, s2


def kernel(x, p, positions, ffn_a_norm, ffn_a_w_in, ffn_a_w_out, ffn_b_norm, ffn_b_w_in, ffn_b_w_out, mix_norm, mla_w_in, mla_q_lat_norm, mla_kv_lat_norm, mla_w_uq, mla_w_ukv, mla_q_gain, mla_k_gain, mla_w_o, conv_w_pw1, conv_b_pw1, conv_w_dw, conv_b_dw, conv_ln_g, conv_ln_b, conv_w_pw2, ple_w_proj, ple_norm, ple_gate_norm, ple_w_gate, loss_target, m_ffn_a_norm, m_ffn_a_w_in, m_ffn_a_w_out, m_ffn_b_norm, m_ffn_b_w_in, m_ffn_b_w_out, m_mix_norm, m_mla_w_in, m_mla_q_lat_norm, m_mla_kv_lat_norm, m_mla_w_uq, m_mla_w_ukv, m_mla_q_gain, m_mla_k_gain, m_mla_w_o, m_conv_w_pw1, m_conv_b_pw1, m_conv_w_dw, m_conv_b_dw, m_conv_ln_g, m_conv_ln_b, m_conv_w_pw2, m_ple_w_proj, m_ple_norm, m_ple_gate_norm, m_ple_w_gate, v_ffn_a_norm, v_ffn_a_w_in, v_ffn_a_w_out, v_ffn_b_norm, v_ffn_b_w_in, v_ffn_b_w_out, v_mix_norm, v_mla_w_in, v_mla_q_lat_norm, v_mla_kv_lat_norm, v_mla_w_uq, v_mla_w_ukv, v_mla_q_gain, v_mla_k_gain, v_mla_w_o, v_conv_w_pw1, v_conv_b_pw1, v_conv_w_dw, v_conv_b_dw, v_conv_ln_g, v_conv_ln_b, v_conv_w_pw2, v_ple_w_proj, v_ple_norm, v_ple_gate_norm, v_ple_w_gate):
    weights = dict(ffn_a_norm=ffn_a_norm, ffn_a_w_in=ffn_a_w_in, ffn_a_w_out=ffn_a_w_out, ffn_b_norm=ffn_b_norm,
                   ffn_b_w_in=ffn_b_w_in, ffn_b_w_out=ffn_b_w_out, mix_norm=mix_norm, mla_w_in=mla_w_in,
                   mla_q_lat_norm=mla_q_lat_norm, mla_kv_lat_norm=mla_kv_lat_norm, mla_w_uq=mla_w_uq,
                   mla_w_ukv=mla_w_ukv, mla_q_gain=mla_q_gain, mla_k_gain=mla_k_gain, mla_w_o=mla_w_o,
                   conv_w_pw1=conv_w_pw1, conv_b_pw1=conv_b_pw1, conv_w_dw=conv_w_dw, conv_b_dw=conv_b_dw,
                   conv_ln_g=conv_ln_g, conv_ln_b=conv_ln_b, conv_w_pw2=conv_w_pw2, ple_w_proj=ple_w_proj,
                   ple_norm=ple_norm, ple_gate_norm=ple_gate_norm, ple_w_gate=ple_w_gate)
    moments_m = dict(ffn_a_norm=m_ffn_a_norm, ffn_a_w_in=m_ffn_a_w_in, ffn_a_w_out=m_ffn_a_w_out,
                     ffn_b_norm=m_ffn_b_norm, ffn_b_w_in=m_ffn_b_w_in, ffn_b_w_out=m_ffn_b_w_out,
                     mix_norm=m_mix_norm, mla_w_in=m_mla_w_in, mla_q_lat_norm=m_mla_q_lat_norm,
                     mla_kv_lat_norm=m_mla_kv_lat_norm, mla_w_uq=m_mla_w_uq, mla_w_ukv=m_mla_w_ukv,
                     mla_q_gain=m_mla_q_gain, mla_k_gain=m_mla_k_gain, mla_w_o=m_mla_w_o,
                     conv_w_pw1=m_conv_w_pw1, conv_b_pw1=m_conv_b_pw1, conv_w_dw=m_conv_w_dw,
                     conv_b_dw=m_conv_b_dw, conv_ln_g=m_conv_ln_g, conv_ln_b=m_conv_ln_b, conv_w_pw2=m_conv_w_pw2,
                     ple_w_proj=m_ple_w_proj, ple_norm=m_ple_norm, ple_gate_norm=m_ple_gate_norm,
                     ple_w_gate=m_ple_w_gate)
    moments_v = dict(ffn_a_norm=v_ffn_a_norm, ffn_a_w_in=v_ffn_a_w_in, ffn_a_w_out=v_ffn_a_w_out,
                     ffn_b_norm=v_ffn_b_norm, ffn_b_w_in=v_ffn_b_w_in, ffn_b_w_out=v_ffn_b_w_out,
                     mix_norm=v_mix_norm, mla_w_in=v_mla_w_in, mla_q_lat_norm=v_mla_q_lat_norm,
                     mla_kv_lat_norm=v_mla_kv_lat_norm, mla_w_uq=v_mla_w_uq, mla_w_ukv=v_mla_w_ukv,
                     mla_q_gain=v_mla_q_gain, mla_k_gain=v_mla_k_gain, mla_w_o=v_mla_w_o,
                     conv_w_pw1=v_conv_w_pw1, conv_b_pw1=v_conv_b_pw1, conv_w_dw=v_conv_w_dw,
                     conv_b_dw=v_conv_b_dw, conv_ln_g=v_conv_ln_g, conv_ln_b=v_conv_ln_b, conv_w_pw2=v_conv_w_pw2,
                     ple_w_proj=v_ple_w_proj, ple_norm=v_ple_norm, ple_gate_norm=v_ple_gate_norm,
                     ple_w_gate=v_ple_w_gate)
    order = list(weights.keys())
    _LAST[0] = None

    T, D = x.shape[1], x.shape[2]
    me = _index(_place())
    h0 = x[0]
    target = loss_target[0]
    tabs = _rope_tables(positions[0])
    H = N_HEADS
    hps = H // N_DEV
    QL = mla_q_lat_norm.shape[1]
    Cs = conv_b_dw.shape[1]

    def cast(n, i):
        return _cast_layer(f"cast_{n}{i}", weights[n], i)

    first, tok = _ag_start("ag_start0", [[cast("ffn_a_w_in", 0)]])
    m_in_pad = _pad_cols(mla_w_in[0], mla_w_in.shape[2] - D_ROPE + 128)[None]
    uq_pad = jnp.pad(mla_w_uq[0].reshape(QL, hps, QK_DIM), ((0, 0), (0, 0), (0, HEAD_PAD - QK_DIM)))
    uq_pad = uq_pad.reshape(1, QL, hps * HEAD_PAD)
    conv_small = jnp.concatenate([
        _pad_rows(_pad_cols(conv_b_pw1, 2 * Cs), 8),
        _pad_rows(_pad_cols(conv_w_dw[0], 2 * Cs), 32),
        _pad_rows(_pad_cols(jnp.concatenate([conv_b_dw, conv_ln_g, conv_ln_b], axis=0), 2 * Cs), 8)], axis=0)
    rest, tok = _ag_start("ag_start1", [
        [cast("ffn_a_w_out", 0), _cast_layer("cast_mla_in", m_in_pad, 0), _cast_layer("cast_mla_uq", uq_pad, 0),
         _cast_layer("cast_mla_ukv", mla_w_ukv, 0), _cast_layer("cast_mla_wo", mla_w_o, 0)],
        [cast("ffn_b_w_in", 0), cast("ffn_b_w_out", 0), cast("ple_w_gate", 0), cast("ple_w_proj", 0)],
        [cast("ffn_a_w_in", 1), cast("ffn_a_w_out", 1)],
        [_cast_layer("cast_conv_pw1", conv_w_pw1, 0), _cast_layer("cast_conv_pw2", conv_w_pw2, 0), conv_small],
        [cast("ffn_b_w_in", 1), cast("ffn_b_w_out", 1), cast("ple_w_gate", 1), cast("ple_w_proj", 1)]])
    groups = [dict(handle=hd, stage=0, arrays=None) for hd in first + rest]

    def prefetch(gi):
        st = groups[gi]
        if st["stage"] == 0:
            st["handle"] = _ag_forward(f"ag{gi}_forward", st["handle"])
            st["stage"] = 1

    def fetch(gi):
        prefetch(gi)
        st = groups[gi]
        if st["stage"] == 1:
            st["arrays"] = _ag_wait(f"ag{gi}_wait", st["handle"])
            st["stage"] = 2
        return st["arrays"]

    def getter(gi, k, shape=None, ahead=None):
        def get():
            if ahead is not None:
                prefetch(ahead)
            a = fetch(gi)[k]
            return a if shape is None else a.reshape(shape)
        return get

    def conv_small_params():
        small = fetch(4)[2]
        return (small[:, 0:1, :],
                jnp.transpose(small[:, 8:40, :Cs], (1, 0, 2)).reshape(32, N_DEV * Cs),
                small[:, 40, :Cs].reshape(1, N_DEV * Cs), small[:, 41, :Cs].reshape(1, N_DEV * Cs),
                small[:, 42, :Cs].reshape(1, N_DEV * Cs))

    rows = (-1, D)
    get_ffn = [dict(a_in=getter(0, 0), a_out=getter(1, 0, rows), b_in=getter(2, 0), b_out=getter(2, 1, rows)),
               dict(a_in=getter(3, 0), a_out=getter(3, 1, rows, ahead=4), b_in=getter(5, 0),
                    b_out=getter(5, 1, rows))]
    get_ple = [dict(proj=getter(2, 3, ahead=3), gate=getter(2, 2, rows)),
               dict(proj=getter(5, 3), gate=getter(5, 2, rows))]
    get_mla = dict(m_in=getter(1, 1, (D, -1)), uq=getter(1, 2), ukv=getter(1, 3), wo=getter(1, 4, rows, ahead=2))
    get_conv = dict(pw1=getter(4, 0), pw2=getter(4, 1, rows, ahead=5), small=conv_small_params)
    gq_pad = _pad_cols(mla_q_gain, HEAD_PAD)
    gk_pad = _pad_cols(mla_k_gain, HEAD_PAD)
    prefetch(0)

    saved = []
    h = h0
    for i in range(2):
        h, s_a = _ffn_fwd(f"ffn_a{i}", h, ffn_a_norm[i:i + 1], get_ffn[i]["a_in"], get_ffn[i]["a_out"])
        if i == 0:
            h, s_m = _mla_fwd(h, mix_norm[0:1], tabs, get_mla, mla_q_lat_norm, mla_kv_lat_norm, gq_pad, gk_pad)
        else:
            h, s_m = _conv_fwd(h, mix_norm[1:2], get_conv)
        h, s_b = _ffn_fwd(f"ffn_b{i}", h, ffn_b_norm[i:i + 1], get_ffn[i]["b_in"], get_ffn[i]["b_out"])
        h, s_p = _ple_fwd(f"ple{i}", h, p[i, 0], get_ple[i]["proj"], ple_norm[i:i + 1], ple_gate_norm[i:i + 1],
                          get_ple[i]["gate"])
        saved.append((s_a, s_m, s_b, s_p))
    W = [dict(a_in=get_ffn[i]["a_in"](), a_out=get_ffn[i]["a_out"](), b_in=get_ffn[i]["b_in"](),
              b_out=get_ffn[i]["b_out"](), proj=get_ple[i]["proj"](), gate=get_ple[i]["gate"]())
         for i in range(2)]
    Wm = {n: g() for n, g in get_mla.items()}
    Wc = dict(pw1=get_conv["pw1"](), pw2=get_conv["pw2"]())
    _, w_dw_full, _, ln_g_full, ln_b_full = conv_small_params()

    dh, dhb, loss_row = _loss_head(h, target)

    G = {}
    small_g = {}
    stacked = ["ffn_a_w_in", "ffn_a_w_out", "ffn_b_w_in", "ffn_b_w_out", "ple_w_proj", "ple_w_gate"]
    row_sharded = {"ffn_a_w_out", "ffn_b_w_out", "ple_w_gate", "mla_w_in", "mla_w_o", "conv_w_pw2"}
    core = lax.axis_index("c").astype(jnp.int32).reshape(1)
    rs_groups = {
        "pb1": [("ple_w_proj", 1), ("ple_w_gate", 1), ("ffn_b_w_in", 1), ("ffn_b_w_out", 1)],
        "c1": [("conv_w_pw1", 0), ("conv_w_pw2", 0)],
        "a1": [("ffn_a_w_in", 1), ("ffn_a_w_out", 1)],
        "pb0": [("ple_w_proj", 0), ("ple_w_gate", 0), ("ffn_b_w_in", 0), ("ffn_b_w_out", 0)],
        "m0": [("mla_w_in", 0), ("mla_w_uq", 0), ("mla_w_ukv", 0), ("mla_w_o", 0)],
        "ao0": [("ffn_a_w_out", 0)],
        "ai0": [("ffn_a_w_in", 0)]}
    rs = {}

    def rs_begin(tag):
        grads = []
        for n, i in rs_groups[tag]:
            g = G[(n, i)]
            grads.append(g.reshape(N_DEV, g.shape[0] // N_DEV, g.shape[1]) if n in row_sharded else g)
        rs[tag] = _rs_pair_start(f"rs_{tag}_pair_start", grads)[0]

    def rs_pairs(tag):
        mine, got = _rs_pair_wait(f"rs_{tag}_pair_wait", rs[tag])
        rs[tag] = [_pair_sum(f"pairsum_{n}{i}", a, b, core) for (n, i), a, b in zip(rs_groups[tag], mine, got)]

    def rs_chips(tag):
        rs[tag] = _rs_chip_start(f"rs_{tag}_chip_start", rs[tag])[0]

    def rs_end(tag):
        return dict(zip(rs_groups[tag], _rs_chip_wait(f"rs_{tag}_chip_wait", rs[tag])))

    s_a, s_m, s_b, s_p = saved[1]
    dh, dhb, d_pn, d_gn, G[("ple_w_proj", 1)], G[("ple_w_gate", 1)] = _ple_bwd(
        "ple1", s_p, p[1, 0], W[1]["proj"], ple_norm[1:2], ple_gate_norm[1:2], W[1]["gate"], dh, dhb)
    small_g[("ple_norm", 1)], small_g[("ple_gate_norm", 1)] = d_pn, d_gn
    dh, dhb, small_g[("ffn_b_norm", 1)], G[("ffn_b_w_in", 1)], G[("ffn_b_w_out", 1)] = _ffn_bwd(
        "ffn_b1", s_b, ffn_b_norm[1:2], W[1]["b_in"], W[1]["b_out"], dh, dhb)
    rs_begin("pb1")
    (dh, dhb, small_g[("mix_norm", 1)], G[("conv_w_pw1", 0)], d_b_pw1, d_w_dw, d_b_dw, d_ln_g, d_ln_b,
     G[("conv_w_pw2", 0)]) = _conv_bwd(s_m, mix_norm[1:2], Wc, w_dw_full, ln_g_full, ln_b_full, dh, dhb)
    rs_pairs("pb1")
    rs_chips("pb1")
    rs_begin("c1")
    dh, dhb, small_g[("ffn_a_norm", 1)], G[("ffn_a_w_in", 1)], G[("ffn_a_w_out", 1)] = _ffn_bwd(
        "ffn_a1", s_a, ffn_a_norm[1:2], W[1]["a_in"], W[1]["a_out"], dh, dhb)
    rs_pairs("c1")
    rs_chips("c1")
    rs_begin("a1")

    s_a, s_m, s_b, s_p = saved[0]
    dh, dhb, d_pn, d_gn, G[("ple_w_proj", 0)], G[("ple_w_gate", 0)] = _ple_bwd(
        "ple0", s_p, p[0, 0], W[0]["proj"], ple_norm[0:1], ple_gate_norm[0:1], W[0]["gate"], dh, dhb)
    small_g[("ple_norm", 0)], small_g[("ple_gate_norm", 0)] = d_pn, d_gn
    rs_pairs("a1")
    rs_chips("a1")
    dh, dhb, small_g[("ffn_b_norm", 0)], G[("ffn_b_w_in", 0)], G[("ffn_b_w_out", 0)] = _ffn_bwd(
        "ffn_b0", s_b, ffn_b_norm[0:1], W[0]["b_in"], W[0]["b_out"], dh, dhb)
    rs_begin("pb0")

    def in_mla():
        rs_pairs("pb0")
        rs_chips("pb0")

    (dh, dhb, small_g[("mix_norm", 0)], d_qln, d_kvln, d_gq, d_gk,
     G[("mla_w_in", 0)], G[("mla_w_uq", 0)], G[("mla_w_ukv", 0)], G[("mla_w_o", 0)]) = _mla_bwd(
        s_m, mix_norm[0:1], tabs, Wm, mla_q_lat_norm, mla_kv_lat_norm, gq_pad, gk_pad, dh, dhb, hook=in_mla)
    rs_begin("m0")

    def with_dw_out(dw_out):
        G[("ffn_a_w_out", 0)] = dw_out
        rs_pairs("m0")
        rs_chips("m0")
        rs_begin("ao0")

    def with_dw_in(dw_in):
        G[("ffn_a_w_in", 0)] = dw_in
        rs_pairs("ao0")
        rs_chips("ao0")
        rs_begin("ai0")

    dh, dhb, small_g[("ffn_a_norm", 0)], _, _ = _ffn_bwd(
        "ffn_a0", s_a, ffn_a_norm[0:1], W[0]["a_in"], W[0]["a_out"], dh, dhb,
        hooks=(with_dw_out, with_dw_in, lambda: rs_pairs("ai0")))
    grad_x = dh[None]

    replicated = ["ffn_a_norm", "ffn_b_norm", "mix_norm", "ple_norm", "ple_gate_norm"]
    rep = _pack_rows("pack_small", [small_g[(n, i)] for n in replicated for i in (0, 1)]
                     + [_pad_cols(v, D) for v in (d_qln, d_kvln, d_gq, d_gk, loss_row)], 16)

    def dest_major(v, rows):
        r, w = v.shape[0], v.shape[1] // N_DEV
        v = jnp.transpose(v.reshape(r, N_DEV, w), (1, 0, 2))
        return jnp.pad(v, ((0, 0), (0, rows - r), (0, 2 * Cs - w)))

    shd = jnp.concatenate([dest_major(d_b_pw1, 8), dest_major(d_w_dw, 32),
                           dest_major(jnp.concatenate([d_b_dw, d_ln_g, d_ln_b], axis=0), 8)], axis=1)
    red, red_s = _reduce_small(rep, shd)
    loss = red[14, 0]
    small_grads = {n: red[2 * k:2 * k + 2] for k, n in enumerate(replicated)}
    small_grads.update(
        mla_q_lat_norm=red[10:11, :QL], mla_kv_lat_norm=red[11:12, :mla_kv_lat_norm.shape[1]],
        mla_q_gain=red[12:13, :QK_DIM], mla_k_gain=red[13:14, :QK_DIM],
        conv_b_pw1=red_s[0:1], conv_w_dw=red_s[8:8 + CONV_WIDTH, :Cs][None],
        conv_b_dw=red_s[40:41, :Cs], conv_ln_g=red_s[41:42, :Cs], conv_ln_b=red_s[42:43, :Cs])

    rs_chips("ai0")
    done = {}

    def plain_adamw(n, g):
        done[n] = (g,) + _adamw(f"adamw_{n}", weights[n], g, moments_m[n], moments_v[n])

    def slot_adamw(n, layer, slots):
        done[n] = tuple(_sum_adam(f"adamw_{n}{layer}", slots, weights[n], moments_m[n], moments_v[n], layer,
                                  into=done.get(n)))

    for n, g in small_grads.items():
        plain_adamw(n, g)
    for tag in ("pb1", "c1", "a1", "pb0"):
        slots = rs_end(tag)
        for n, i in rs_groups[tag]:
            slot_adamw(n, i, slots[(n, i)])
    slots = rs_end("m0")
    slot_adamw("mla_w_ukv", 0, slots[("mla_w_ukv", 0)])
    slot_adamw("mla_w_o", 0, slots[("mla_w_o", 0)])
    g_in = _chip_sum("chipsum_mla_w_in", slots[("mla_w_in", 0)], 0, 1)
    plain_adamw("mla_w_in", g_in[:, :, :mla_w_in.shape[2]])
    g_uq = _chip_sum("chipsum_mla_w_uq", slots[("mla_w_uq", 0)], 0, 1)
    plain_adamw("mla_w_uq", g_uq.reshape(1, QL, hps, HEAD_PAD)[..., :QK_DIM].reshape(mla_w_uq.shape))
    for tag in ("ao0", "ai0"):
        slots = rs_end(tag)
        for n, i in rs_groups[tag]:
            slot_adamw(n, i, slots[(n, i)])
    grads, deltas, new_m, new_v = ({n: done[n][k] for n in order} for k in range(4))

    return (loss, grad_x, *[grads[n] for n in order], *[deltas[n] for n in order],
            *[new_m[n] for n in order], *[new_v[n] for n in order])
```

```python
import functools

import jax
import jax.numpy as jnp
from jax import lax
from jax.experimental import pallas as pl
from jax.experimental.pallas import tpu as pltpu

F32 = jnp.float32
BF16 = jnp.bfloat16
MESH = pl.DeviceIdType.MESH
ANY = pl.BlockSpec(memory_space=pl.ANY)

N_DEV = 8
N_HEADS = 16
D_NOPE = 128
D_ROPE = 64
D_V = 128
QK_DIM = D_NOPE + D_ROPE
HEAD_PAD = 256
ROPE_THETA = 10000.0
CONV_WIDTH = 31
CONV_PAD = 32
FFN_RES = 0.5
EPS = 1e-6
ADAM_LR = 0.001
ADAM_B1 = 0.9
ADAM_B2 = 0.999
ADAM_EPS = 1e-08
ADAM_WD = 0.01
ADAM_STEP = 10
VMEM_LIMIT = 56 * 1024 * 1024


def _sds(shape, dtype):
    return jax.ShapeDtypeStruct(tuple(int(s) for s in shape), dtype)


def _tile(n, pref, mult=128):
    if n <= pref:
        return n
    t = (pref // mult) * mult
    while t >= mult:
        if n % t == 0:
            return t
        t -= mult
    return n


def _params():
    return pltpu.CompilerParams(vmem_limit_bytes=VMEM_LIMIT)


_LAST = [None]


def _ordered_call(body, operands, chain_out=0, **kw):
    operands = list(operands)
    n_in = len(operands)
    if _LAST[0] is not None and not any(op is _LAST[0] for op in operands):
        inner = body

        def body(*refs):
            inner(*refs[:n_in], *refs[n_in + 1:])

        kw = dict(kw, in_specs=list(kw["in_specs"]) + [ANY])
        operands.append(_LAST[0])
    out = pl.pallas_call(body, **kw)(*operands)
    _LAST[0] = out[chain_out] if isinstance(out, (list, tuple)) else out
    return out


def _matmul(name, grid, ops, terms, dims, acc_shapes, outs, epilogue, extras=()):
    nk = grid[2]
    n_ops, n_ex, n_out, n_acc = len(ops), len(extras), len(outs), len(acc_shapes)

    def body(*refs):
        op_refs = refs[:n_ops]
        ex_refs = refs[n_ops:n_ops + n_ex]
        out_refs = refs[n_ops + n_ex:n_ops + n_ex + n_out]
        acc_refs = refs[n_ops + n_ex + n_out:]
        vals = {}

        def opval(i):
            if i not in vals:
                v = op_refs[i][...]
                vals[i] = v if v.dtype == BF16 else v.astype(BF16)
            return vals[i]

        parts = [None] * n_acc
        for ai, li, ri in terms:
            d = lax.dot_general(opval(li), opval(ri), (dims, ((), ())), preferred_element_type=F32)
            parts[ai] = d if parts[ai] is None else parts[ai] + d
        if nk == 1:
            epilogue(parts, ex_refs, out_refs)
            return
        k = pl.program_id(2)

        @pl.when(k == 0)
        def _():
            for a_ref, p in zip(acc_refs, parts):
                a_ref[...] = p

        @pl.when(k > 0)
        def _():
            for a_ref, p in zip(acc_refs, parts):
                a_ref[...] += p

        @pl.when(k == nk - 1)
        def _():
            epilogue([a[...] for a in acc_refs], ex_refs, out_refs)

    scratch = [pltpu.VMEM(s, F32) for s in acc_shapes] if nk > 1 else []
    return _ordered_call(
        body, [a for a, _ in ops] + [a for a, _ in extras], name=name, grid=grid,
        in_specs=[s for _, s in ops] + [s for _, s in extras],
        out_specs=[s for _, s in outs],
        out_shape=[o for o, _ in outs],
        scratch_shapes=scratch,
        compiler_params=_params(),
    )


NN = ((1,), (0,))
NT = ((1,), (1,))
TN = ((0,), (0,))


def _store(i=0):
    def ep(accs, ex, outs):
        outs[0][...] = accs[0].astype(outs[0].dtype)
    return ep


def _mm_nn(name, a, b, out_dtype, tm=1024, tn=1024, tk=2048, res=None, scale=1.0):
    M, K = a.shape
    N = b.shape[1]
    tm, tn, tk = _tile(M, tm, 16), _tile(N, tn), _tile(K, tk)
    spec = pl.BlockSpec((tm, tn), lambda i, j, k: (i, j))

    def ep(accs, ex, outs):
        v = accs[0] if scale == 1.0 else accs[0] * scale
        outs[0][...] = (v if res is None else ex[0][...] + v).astype(outs[0].dtype)

    return _matmul(name, (M // tm, N // tn, K // tk),
                   [(a, pl.BlockSpec((tm, tk), lambda i, j, k: (i, k))),
                    (b, pl.BlockSpec((tk, tn), lambda i, j, k: (k, j)))],
                   [(0, 0, 1)], NN, [(tm, tn)], [(_sds((M, N), out_dtype), spec)], ep,
                   [] if res is None else [(res, spec)])


def _mm_nt(name, a, b, out_dtype, tm=1024, tn=1024, tk=2048, epilogue=None, extras=()):
    M, K = a.shape
    N = b.shape[0]
    tm, tn, tk = _tile(M, tm, 16), _tile(N, tn), _tile(K, tk)
    outs = [(_sds((M, N), out_dtype), pl.BlockSpec((tm, tn), lambda i, j, k: (i, j)))]
    return _matmul(name, (M // tm, N // tn, K // tk),
                   [(a, pl.BlockSpec((tm, tk), lambda i, j, k: (i, k))),
                    (b, pl.BlockSpec((tn, tk), lambda i, j, k: (j, k)))],
                   [(0, 0, 1)], NT, [(tm, tn)], outs, epilogue or _store(), extras)


def _mm_tn(name, a, b, out_dtype, tm=512, tn=2048, tk=2048, scale=None):
    T, M = a.shape
    N = b.shape[1]
    tm, tn, tk = _tile(M, tm), _tile(N, tn), _tile(T, tk)

    def ep(accs, ex, outs):
        v = accs[0] if scale is None else accs[0] * scale
        outs[0][...] = v.astype(outs[0].dtype)

    outs = [(_sds((M, N), out_dtype), pl.BlockSpec((tm, tn), lambda i, j, k: (i, j)))]
    return _matmul(name, (M // tm, N // tn, T // tk),
                   [(a, pl.BlockSpec((tk, tm), lambda i, j, k: (k, i))),
                    (b, pl.BlockSpec((tk, tn), lambda i, j, k: (k, j)))],
                   [(0, 0, 1)], TN, [(tm, tn)], outs, ep)[0]


def _mm_nn_sm(name, a, w, out_dtype, tm=2048, tk=2048, epilogue=None, extras=()):
    M, K = a.shape
    S, _, Ns = w.shape
    tm, tk = _tile(M, tm, 16), _tile(K, tk)
    outs = [(_sds((M, S * Ns), out_dtype), pl.BlockSpec((tm, Ns), lambda j, i, k: (i, j)))]
    return _matmul(name, (S, M // tm, K // tk),
                   [(a, pl.BlockSpec((tm, tk), lambda j, i, k: (i, k))),
                    (w, pl.BlockSpec((None, tk, Ns), lambda j, i, k: (j, k, 0)))],
                   [(0, 0, 1)], NN, [(tm, Ns)], outs, epilogue or _store(), extras)[0]


def _mm_nt_sm(name, a, w, out_dtype, tm=1024, tn=1024):
    M = a.shape[0]
    S, K, Ns = w.shape
    tm, tn = _tile(M, tm, 16), _tile(K, tn)
    outs = [(_sds((M, K), out_dtype), pl.BlockSpec((tm, tn), lambda i, n, j: (i, n)))]
    return _matmul(name, (M // tm, K // tn, S),
                   [(a, pl.BlockSpec((tm, Ns), lambda i, n, j: (i, j))),
                    (w, pl.BlockSpec((None, tn, Ns), lambda i, n, j: (j, n, 0)))],
                   [(0, 0, 1)], NT, [(tm, tn)], outs, _store())[0]


def _mm_tn_sm(name, a, b, S, out_dtype, tm=1024, tk=2048):
    T, M = a.shape
    Ns = b.shape[1] // S
    tm, tk = _tile(M, tm), _tile(T, tk)
    outs = [(_sds((S, M, Ns), out_dtype), pl.BlockSpec((None, tm, Ns), lambda j, i, k: (j, i, 0)))]
    return _matmul(name, (S, M // tm, T // tk),
                   [(a, pl.BlockSpec((tk, tm), lambda j, i, k: (k, i))),
                    (b, pl.BlockSpec((tk, Ns), lambda j, i, k: (k, j)))],
                   [(0, 0, 1)], TN, [(tm, Ns)], outs, _store())[0]


def _row_spec(shape, axis, tm):
    block = tuple(tm if d == axis else s for d, s in enumerate(shape))
    nd = len(shape)

    def imap(i):
        return tuple(i if d == axis else 0 for d in range(nd))
    return pl.BlockSpec(block, imap)


def _full_spec(shape):
    nd = len(shape)
    return pl.BlockSpec(tuple(shape), lambda i: (0,) * nd)


def _rowwise(name, fn, T, tm, rows, consts, outs, accs=()):
    tm = _tile(T, tm, 16)
    n_in = len(rows) + len(consts)
    n_out = len(outs)

    def body(*refs):
        in_refs = refs[:n_in]
        out_refs = refs[n_in:n_in + n_out]
        acc_refs = refs[n_in + n_out:]
        i = pl.program_id(0)

        def acc_add(ai, val):
            @pl.when(i == 0)
            def _():
                acc_refs[ai][...] = val

            @pl.when(i > 0)
            def _():
                acc_refs[ai][...] += val

        fn(in_refs, out_refs, acc_add)

    return _ordered_call(
        body, [a for a, _ in rows] + list(consts), name=name, grid=(T // tm,),
        in_specs=[_row_spec(a.shape, ax, tm) for a, ax in rows] + [_full_spec(c.shape) for c in consts],
        out_specs=[_row_spec(s, ax, tm) for s, _, ax in outs] + [_full_spec(s) for s in accs],
        out_shape=[_sds(s, d) for s, d, _ in outs] + [_sds(s, F32) for s in accs],
        compiler_params=_params(),
    )


def _rms(x, g, n=None):
    n = x.shape[-1] if n is None else n
    return x * lax.rsqrt(jnp.sum(x * x, axis=-1, keepdims=True) * (1.0 / n) + EPS) * g


def _norm_fwd(name, h, gain):
    T, D = h.shape

    def fn(ins, outs, acc):
        outs[0][...] = _rms(ins[0][...], ins[1][...]).astype(BF16)

    return _rowwise(name, fn, T, 512, [(h, 0)], [gain], [((T, D), BF16, 0)])[0]


def _norm_bwd(name, h, gain, dhn, dh_res):
    T, D = h.shape

    def fn(ins, outs, acc):
        _, vjp = jax.vjp(_rms, ins[0][...], ins[3][...])
        dh, dg = vjp(ins[1][...])
        dh = dh + ins[2][...]
        outs[0][...] = dh
        outs[1][...] = dh.astype(BF16)
        acc(0, dg)

    return _rowwise(name, fn, T, 256, [(h, 0), (dhn, 0), (dh_res, 0)], [gain],
                    [((T, D), F32, 0), ((T, D), BF16, 0)], [(1, D)])


def _ffn_fwd(tag, h, gain, get_in, get_out):
    T, D = h.shape
    hn = _norm_fwd(tag + "_norm", h, gain)
    w_in = get_in()
    S, _, Ns = w_in.shape
    half = S // 2
    F = half * Ns
    tm, tk = _tile(T, 256, 16), _tile(D, 2048)

    def ep(accs, ex, outs):
        g, u = accs
        sg = jax.nn.sigmoid(g)
        silu = g * sg
        outs[0][0] = (u * (sg * (1.0 + g * (1.0 - sg)))).astype(BF16)
        outs[0][1] = silu.astype(BF16)
        outs[1][...] = (silu * u).astype(BF16)

    gu, act = _matmul(
        tag + "_in", (half, T // tm, D // tk),
        [(hn, pl.BlockSpec((tm, tk), lambda j, i, k: (i, k))),
         (w_in, pl.BlockSpec((None, tk, Ns), lambda j, i, k: (j, k, 0))),
         (w_in, pl.BlockSpec((None, tk, Ns), lambda j, i, k: (j + half, k, 0)))],
        [(0, 0, 1), (1, 0, 2)], NN, [(tm, Ns), (tm, Ns)],
        [(_sds((2, T, F), BF16), pl.BlockSpec((2, tm, Ns), lambda j, i, k: (0, i, j))),
         (_sds((T, F), BF16), pl.BlockSpec((tm, Ns), lambda j, i, k: (i, j)))],
        ep)

    w_out = get_out()
    h_new = _mm_nn(tag + "_out", act, w_out, F32, tm=1024, tn=512, tk=F, res=h, scale=FFN_RES)[0]
    return h_new, (h, hn, gu, act)


def _no_hook(*_):
    return None


def _ffn_bwd(tag, saved, gain, w_in, w_out, dh, dhb, hooks=(_no_hook, _no_hook, _no_hook)):
    h, hn, gu, act = saved
    T, D = h.shape
    S, _, Ns = w_in.shape
    half = S // 2
    F = half * Ns
    tm, tk = _tile(T, 512, 16), _tile(D, 2048)

    def ep(accs, ex, outs):
        dact = FFN_RES * accs[0]
        outs[0][0] = (dact * ex[0][0].astype(F32)).astype(BF16)
        outs[0][1] = (dact * ex[0][1].astype(F32)).astype(BF16)

    gu_spec = pl.BlockSpec((2, tm, Ns), lambda j, i, k: (0, i, j))
    dgu = _matmul(
        tag + "_dact", (half, T // tm, D // tk),
        [(dhb, pl.BlockSpec((tm, tk), lambda j, i, k: (i, k))),
         (w_out, pl.BlockSpec((Ns, tk), lambda j, i, k: (j, k)))],
        [(0, 0, 1)], NT, [(tm, Ns)],
        [(_sds((2, T, F), BF16), gu_spec)], ep, extras=[(gu, gu_spec)])[0]

    dw_out = _mm_tn(tag + "_dwout", act, dhb, BF16, scale=FFN_RES)
    hooks[0](dw_out)

    tm = _tile(T, 1024, 16)
    tkd, tt = _tile(D, 512), _tile(T, 2048)
    dw_in = _matmul(
        tag + "_dwin", (S, D // tkd, T // tt),
        [(hn, pl.BlockSpec((tt, tkd), lambda j, i, k: (k, i))),
         (dgu, pl.BlockSpec((None, tt, Ns), lambda j, i, k: (j // half, k, j % half)))],
        [(0, 0, 1)], TN, [(tkd, Ns)],
        [(_sds((S, D, Ns), BF16), pl.BlockSpec((None, tkd, Ns), lambda j, i, k: (j, i, 0)))], _store())[0]
    hooks[1](dw_in)

    tn = _tile(D, 1024)
    dhn = _matmul(
        tag + "_dhn", (T // tm, D // tn, half),
        [(dgu, pl.BlockSpec((None, tm, Ns), lambda i, n, j: (0, i, j))),
         (dgu, pl.BlockSpec((None, tm, Ns), lambda i, n, j: (1, i, j))),
         (w_in, pl.BlockSpec((None, tn, Ns), lambda i, n, j: (j, n, 0))),
         (w_in, pl.BlockSpec((None, tn, Ns), lambda i, n, j: (j + half, n, 0)))],
        [(0, 0, 2), (0, 1, 3)], NT, [(tm, tn)],
        [(_sds((T, D), F32), pl.BlockSpec((tm, tn), lambda i, n, j: (i, n)))], _store())[0]
    hooks[2]()

    dh_in, dh_in_b, dgain = _norm_bwd(tag + "_dnorm", h, gain, dhn, dh)
    return dh_in, dh_in_b, dgain, dw_in, dw_out


def _ple_fwd(tag, h, p, get_proj, ple_norm, gate_norm, get_gate):
    T, D = h.shape
    e_raw = _mm_nn_sm(tag + "_proj", p, get_proj(), F32)
    hn = _norm_fwd(tag + "_norm", h, gate_norm)
    gate_raw = _mm_nn(tag + "_gate", hn, get_gate(), F32)[0]

    def fn(ins, outs, acc):
        e = _rms(ins[1][...], ins[3][...])
        outs[0][...] = ins[0][...] + e * jax.nn.sigmoid(ins[2][...])

    h_new = _rowwise(tag + "_mix", fn, T, 256, [(h, 0), (e_raw, 0), (gate_raw, 0)], [ple_norm],
                     [((T, D), F32, 0)])[0]
    return h_new, (h, hn, e_raw, gate_raw)


def _ple_bwd(tag, saved, p, w_proj, ple_norm, gate_norm, w_gate, dh, dhb):
    h, hn, e_raw, gate_raw = saved
    T, D = h.shape
    S = w_proj.shape[0]

    def fn(ins, outs, acc):
        def f(e_raw_, gate_raw_, g_):
            return _rms(e_raw_, g_) * jax.nn.sigmoid(gate_raw_)
        _, vjp = jax.vjp(f, ins[0][...], ins[1][...], ins[3][...])
        de, dgate, dg = vjp(ins[2][...])
        outs[0][...] = de.astype(BF16)
        outs[1][...] = dgate.astype(BF16)
        acc(0, dg)

    de, dgate, d_ple_norm = _rowwise(tag + "_dmix", fn, T, 256, [(e_raw, 0), (gate_raw, 0), (dh, 0)], [ple_norm],
                                     [((T, D), BF16, 0), ((T, D), BF16, 0)], [(1, D)])
    dw_proj = _mm_tn_sm(tag + "_dwproj", p, de, S, BF16)
    dw_gate = _mm_tn(tag + "_dwgate", hn, dgate, BF16)
    dhn = _mm_nt(tag + "_dhn", dgate, w_gate, F32)[0]
    dh_in, dh_in_b, d_gate_norm = _norm_bwd(tag + "_dnorm", h, gate_norm, dhn, dh)
    return dh_in, dh_in_b, d_ple_norm, d_gate_norm, dw_proj, dw_gate


def _rope(t, c, s1, s2):
    q = D_ROPE // 2
    return t * c + pltpu.roll(t, q, 1) * s1 + pltpu.roll(t, 128 - q, 1) * s2


def _rope_t(d, c, s1, s2):
    q = D_ROPE // 2
    return d * c + pltpu.roll(d * s1, 128 - q, 1) + pltpu.roll(d * s2, q, 1)


def _head_norm(lo, hi, g_lo, g_hi):
    ms = (jnp.sum(lo * lo, axis=-1, keepdims=True) + jnp.sum(hi * hi, axis=-1, keepdims=True)) * (1.0 / QK_DIM)
    inv = lax.rsqrt(ms + EPS)
    return lo * inv * g_lo, hi * inv * g_hi


def _qk_prep(qraw, kvraw, lat, tabs, gq, gk, H):
    T = qraw.shape[0]
    koff = lat.shape[1] - 128

    def fn(ins, outs, acc):
        q_ref, kv_ref, lat_ref, c_ref, s1_ref, s2_ref, gq_ref, gk_ref = ins
        c, s1, s2 = c_ref[...], s1_ref[...], s2_ref[...]
        kr = lat_ref[:, koff:koff + 128]
        for hd in range(H):
            o = hd * HEAD_PAD
            lo, hi = _head_norm(q_ref[:, o:o + 128], q_ref[:, o + 128:o + 256], gq_ref[:, 0:128], gq_ref[:, 128:256])
            outs[0][hd, :, 0:128] = lo.astype(BF16)
            outs[0][hd, :, 128:256] = _rope(hi, c, s1, s2).astype(BF16)
            lo, hi = _head_norm(kv_ref[:, o:o + 128], kr, gk_ref[:, 0:128], gk_ref[:, 128:256])
            outs[1][hd, :, 0:128] = lo.astype(BF16)
            outs[1][hd, :, 128:256] = _rope(hi, c, s1, s2).astype(BF16)
            outs[2][hd] = kv_ref[:, o + 128:o + 256].astype(BF16)

    return _rowwise("mla_qkprep", fn, T, 256, [(qraw, 0), (kvraw, 0), (lat, 0)] + [(t, 0) for t in tabs], [gq, gk],
                    [((H, T, HEAD_PAD), BF16, 1), ((H, T, HEAD_PAD), BF16, 1), ((H, T, D_V), BF16, 1)])


def _qk_prep_bwd(qraw, kvraw, lat, tabs, gq, gk, dQ, dK, dV, H):
    T = qraw.shape[0]
    koff = lat.shape[1] - 128

    def fn(ins, outs, acc):
        q_ref, kv_ref, lat_ref, c_ref, s1_ref, s2_ref, dq_ref, dk_ref, dv_ref, gq_ref, gk_ref = ins
        c, s1, s2 = c_ref[...], s1_ref[...], s2_ref[...]
        kr = lat_ref[:, koff:koff + 128]
        dkr = jnp.zeros_like(kr)
        dg = [None] * 4
        for hd in range(H):
            o = hd * HEAD_PAD
            _, vjp = jax.vjp(_head_norm, q_ref[:, o:o + 128], q_ref[:, o + 128:o + 256],
                             gq_ref[:, 0:128], gq_ref[:, 128:256])
            dlo, dhi, dg0, dg1 = vjp((dq_ref[hd, :, 0:128], _rope_t(dq_ref[hd, :, 128:256], c, s1, s2)))
            outs[0][:, o:o + 128] = dlo.astype(BF16)
            outs[0][:, o + 128:o + 256] = dhi.astype(BF16)
            _, vjp = jax.vjp(_head_norm, kv_ref[:, o:o + 128], kr, gk_ref[:, 0:128], gk_ref[:, 128:256])
            dlo, dhi, dg2, dg3 = vjp((dk_ref[hd, :, 0:128], _rope_t(dk_ref[hd, :, 128:256], c, s1, s2)))
            outs[1][:, o:o + 128] = dlo.astype(BF16)
            outs[1][:, o + 128:o + 256] = dv_ref[hd].astype(BF16)
            dkr = dkr + dhi
            for n, v in enumerate((dg0, dg1, dg2, dg3)):
                dg[n] = v if dg[n] is None else dg[n] + v
        outs[2][...] = dkr
        for n in range(4):
            acc(n, dg[n])

    W = H * HEAD_PAD
    return _rowwise("mla_dqkprep", fn, T, 128,
                    [(qraw, 0), (kvraw, 0), (lat, 0)] + [(t, 0) for t in tabs] + [(dQ, 1), (dK, 1), (dV, 1)], [gq, gk],
                    [((T, W), BF16, 0), ((T, W), BF16, 0), ((T, 128), F32, 0)], [(1, 128)] * 4)


def _attn_probs(q, k, c, tq):
    nk = k.shape[0]
    s = lax.dot_general(q, k, (NT, ((), ())), preferred_element_type=F32) * (QK_DIM ** -0.5)
    row = c * tq + lax.broadcasted_iota(jnp.int32, (tq, nk), 0)
    col = lax.broadcasted_iota(jnp.int32, (tq, nk), 1)
    s = jnp.where(col <= row, s, -jnp.inf)
    p = jnp.exp(s - jnp.max(s, axis=-1, keepdims=True))
    return p / jnp.sum(p, axis=-1, keepdims=True)


def _per_query_block(nq, fn):
    i = pl.program_id(1)
    for c in range(nq):
        pl.when(i == c)(functools.partial(fn, c))


def _attn_fwd(Q, K, V):
    H, T, _ = Q.shape
    tq = _tile(T, 256)

    def body(q_ref, k_ref, v_ref, o_ref):
        def block(c):
            nk = (c + 1) * tq
            p = _attn_probs(q_ref[...], k_ref[0:nk, :], c, tq)
            o_ref[...] = jnp.dot(p.astype(BF16), v_ref[0:nk, :], preferred_element_type=F32).astype(BF16)

        _per_query_block(T // tq, block)

    return _ordered_call(
        body, [Q, K, V], name="mla_attn", grid=(H, T // tq),
        in_specs=[pl.BlockSpec((None, tq, HEAD_PAD), lambda h, i: (h, i, 0)),
                  pl.BlockSpec((None, T, HEAD_PAD), lambda h, i: (h, 0, 0)),
                  pl.BlockSpec((None, T, D_V), lambda h, i: (h, 0, 0))],
        out_specs=pl.BlockSpec((tq, D_V), lambda h, i: (i, h)),
        out_shape=_sds((T, H * D_V), BF16),
        compiler_params=_params(),
    )


def _attn_bwd(Q, K, V, dO):
    H, T, _ = Q.shape
    tq = _tile(T, 256)

    def body(q_ref, k_ref, v_ref, do_ref, dq_ref, dk_ref, dv_ref):
        @pl.when(pl.program_id(1) == 0)
        def _():
            dk_ref[...] = jnp.zeros_like(dk_ref)
            dv_ref[...] = jnp.zeros_like(dv_ref)

        def block(c):
            nk = (c + 1) * tq
            q, k, do = q_ref[...], k_ref[0:nk, :], do_ref[...]
            p = _attn_probs(q, k, c, tq)
            dv_ref[0:nk, :] += lax.dot_general(p.astype(BF16), do, (TN, ((), ())), preferred_element_type=F32)
            dp = lax.dot_general(do, v_ref[0:nk, :], (NT, ((), ())), preferred_element_type=F32)
            ds = p * (dp - jnp.sum(p * dp, axis=-1, keepdims=True)) * (QK_DIM ** -0.5)
            dsb = ds.astype(BF16)
            dq_ref[...] = jnp.dot(dsb, k, preferred_element_type=F32)
            dk_ref[0:nk, :] += lax.dot_general(dsb, q, (TN, ((), ())), preferred_element_type=F32)

        _per_query_block(T // tq, block)

    return _ordered_call(
        body, [Q, K, V, dO], name="mla_dattn", grid=(H, T // tq),
        in_specs=[pl.BlockSpec((None, tq, HEAD_PAD), lambda h, i: (h, i, 0)),
                  pl.BlockSpec((None, T, HEAD_PAD), lambda h, i: (h, 0, 0)),
                  pl.BlockSpec((None, T, D_V), lambda h, i: (h, 0, 0)),
                  pl.BlockSpec((tq, D_V), lambda h, i: (i, h))],
        out_specs=[pl.BlockSpec((None, tq, HEAD_PAD), lambda h, i: (h, i, 0)),
                   pl.BlockSpec((None, T, HEAD_PAD), lambda h, i: (h, 0, 0)),
                   pl.BlockSpec((None, T, D_V), lambda h, i: (h, 0, 0))],
        out_shape=[_sds((H, T, HEAD_PAD), F32), _sds((H, T, HEAD_PAD), F32), _sds((H, T, D_V), F32)],
        compiler_params=_params(),
    )


def _mla_fwd(h, gain, tabs, get, q_lat_norm, kv_lat_norm, gq, gk):
    T, D = h.shape
    QL, KL = q_lat_norm.shape[1], kv_lat_norm.shape[1]
    hn = _norm_fwd("mla_norm", h, gain)
    w = dict(m_in=get["m_in"]())
    lat = _mm_nn("mla_lat", hn, w["m_in"], F32, tn=w["m_in"].shape[1])[0]

    def fn(ins, outs, acc):
        outs[0][...] = _rms(ins[0][:, 0:QL], ins[1][...]).astype(BF16)
        outs[1][...] = _rms(ins[0][:, QL:QL + KL], ins[2][...]).astype(BF16)

    cq, ckv = _rowwise("mla_latnorm", fn, T, 256, [(lat, 0)], [q_lat_norm, kv_lat_norm],
                       [((T, QL), BF16, 0), ((T, KL), BF16, 0)])
    w["uq"], w["ukv"] = get["uq"](), get["ukv"]()
    H = w["uq"].shape[0] * w["uq"].shape[2] // HEAD_PAD
    qraw = _mm_nn_sm("mla_uq", cq, w["uq"], F32)
    kvraw = _mm_nn_sm("mla_ukv", ckv, w["ukv"], F32)
    Q, K, V = _qk_prep(qraw, kvraw, lat, tabs, gq, gk, H)
    O = _attn_fwd(Q, K, V)
    w["wo"] = get["wo"]()

    h_new = _mm_nn("mla_out", O, w["wo"], F32, res=h)[0]
    return h_new, (h, hn, lat, cq, ckv, qraw, kvraw, Q, K, V, O)


def _mla_bwd(saved, gain, tabs, w, q_lat_norm, kv_lat_norm, gq, gk, dh, dhb, hook=_no_hook):
    h, hn, lat, cq, ckv, qraw, kvraw, Q, K, V, O = saved
    T, D = h.shape
    H = Q.shape[0]
    S = w["uq"].shape[0]
    QL, KL = q_lat_norm.shape[1], kv_lat_norm.shape[1]
    dO = _mm_nt("mla_dO", dhb, w["wo"], BF16)[0]
    dwo = _mm_tn("mla_dwo", O, dhb, BF16)
    hook()
    dQ, dK, dV = _attn_bwd(Q, K, V, dO)
    dqraw, dkvraw, dkr, dgq0, dgq1, dgk0, dgk1 = _qk_prep_bwd(qraw, kvraw, lat, tabs, gq, gk, dQ, dK, dV, H)
    dcq = _mm_nt_sm("mla_dcq", dqraw, w["uq"], F32)
    dckv = _mm_nt_sm("mla_dckv", dkvraw, w["ukv"], F32)
    dwuq = _mm_tn_sm("mla_dwuq", cq, dqraw, S, BF16)
    dwukv = _mm_tn_sm("mla_dwukv", ckv, dkvraw, S, BF16)

    def fn(ins, outs, acc):
        _, vjp = jax.vjp(_rms, ins[0][:, 0:QL], ins[4][...])
        d, dgq_ = vjp(ins[1][...])
        outs[0][:, 0:QL] = d.astype(BF16)
        _, vjp = jax.vjp(_rms, ins[0][:, QL:QL + KL], ins[5][...])
        d, dgkv_ = vjp(ins[2][...])
        outs[0][:, QL:QL + KL] = d.astype(BF16)
        outs[0][:, QL + KL:QL + KL + 128] = ins[3][...].astype(BF16)
        acc(0, dgq_)
        acc(1, dgkv_)

    dlat, d_qln, d_kvln = _rowwise("mla_dlatnorm", fn, T, 256, [(lat, 0), (dcq, 0), (dckv, 0), (dkr, 0)],
                                   [q_lat_norm, kv_lat_norm], [(lat.shape, BF16, 0)], [(1, QL), (1, KL)])
    dhn = _mm_nt("mla_dhn", dlat, w["m_in"], F32, tk=lat.shape[1])[0]
    dw_min = _mm_tn("mla_dwin", hn, dlat, BF16, tn=lat.shape[1])
    dh_in, dh_in_b, dgain = _norm_bwd("mla_dnorm", h, gain, dhn, dh)
    d_gq = jnp.concatenate([dgq0, dgq1], axis=1)[:, :QK_DIM]
    d_gk = jnp.concatenate([dgk0, dgk1], axis=1)[:, :QK_DIM]
    return dh_in, dh_in_b, dgain, d_qln, d_kvln, d_gq, d_gk, dw_min, dwuq, dwukv, dwo


def _conv_rows(T):
    return _tile(T, 128, 8)


def _dwconv_fwd(u, w_dw, b_dw):
    T, C = u.shape
    tc, R = _tile(C, 256), _conv_rows(T)
    off = CONV_PAD - (CONV_WIDTH - 1)

    def body(u_ref, w_ref, b_ref, y_ref, pad_ref):
        pad_ref[0:CONV_PAD, :] = jnp.zeros((CONV_PAD, tc), F32)
        pad_ref[CONV_PAD:CONV_PAD + T, :] = u_ref[...]
        for r in range(T // R):
            acc = jnp.broadcast_to(b_ref[...], (R, tc))
            for j in range(CONV_WIDTH):
                acc = acc + w_ref[j:j + 1, :] * pad_ref[r * R + off + j:r * R + off + j + R, :]
            y_ref[r * R:(r + 1) * R, :] = acc

    return _ordered_call(
        body, [u, w_dw, b_dw], name="conv_dw", grid=(C // tc,),
        in_specs=[pl.BlockSpec((T, tc), lambda c: (0, c)), pl.BlockSpec((32, tc), lambda c: (0, c)),
                  pl.BlockSpec((1, tc), lambda c: (0, c))],
        out_specs=pl.BlockSpec((T, tc), lambda c: (0, c)),
        out_shape=_sds((T, C), F32),
        scratch_shapes=[pltpu.VMEM((T + CONV_PAD, tc), F32)],
        compiler_params=_params(),
    )


def _dwconv_bwd(u, w_dw, dy):
    T, C = u.shape
    tc, R = _tile(C, 256), _conv_rows(T)
    off = CONV_PAD - (CONV_WIDTH - 1)

    def body(u_ref, w_ref, dy_ref, du_ref, dw_ref, db_ref, upad_ref, dpad_ref):
        upad_ref[0:CONV_PAD, :] = jnp.zeros((CONV_PAD, tc), F32)
        upad_ref[CONV_PAD:CONV_PAD + T, :] = u_ref[...]
        dpad_ref[0:T, :] = dy_ref[...]
        dpad_ref[T:T + CONV_PAD, :] = jnp.zeros((CONV_PAD, tc), F32)
        for r in range(T // R):
            acc = jnp.zeros((R, tc), F32)
            for j in range(CONV_WIDTH):
                s = r * R + (CONV_WIDTH - 1) - j
                acc = acc + w_ref[j:j + 1, :] * dpad_ref[s:s + R, :]
            du_ref[r * R:(r + 1) * R, :] = acc
        for j in range(CONV_WIDTH):
            acc = jnp.zeros((R, tc), F32)
            for r in range(T // R):
                acc = acc + dy_ref[r * R:(r + 1) * R, :] * upad_ref[r * R + off + j:r * R + off + j + R, :]
            dw_ref[j:j + 1, :] = jnp.sum(acc, axis=0, keepdims=True)
        dw_ref[CONV_WIDTH:32, :] = jnp.zeros((32 - CONV_WIDTH, tc), F32)
        db_ref[...] = jnp.sum(dy_ref[...], axis=0, keepdims=True)

    return _ordered_call(
        body, [u, w_dw, dy], name="conv_ddw", grid=(C // tc,),
        in_specs=[pl.BlockSpec((T, tc), lambda c: (0, c)), pl.BlockSpec((32, tc), lambda c: (0, c)),
                  pl.BlockSpec((T, tc), lambda c: (0, c))],
        out_specs=[pl.BlockSpec((T, tc), lambda c: (0, c)), pl.BlockSpec((32, tc), lambda c: (0, c)),
                   pl.BlockSpec((1, tc), lambda c: (0, c))],
        out_shape=[_sds((T, C), F32), _sds((32, C), F32), _sds((1, C), F32)],
        scratch_shapes=[pltpu.VMEM((T + CONV_PAD, tc), F32), pltpu.VMEM((T + CONV_PAD, tc), F32)],
        compiler_params=_params(),
    )


def _ln_silu(y, g, b):
    mu = jnp.mean(y, axis=-1, keepdims=True)
    yc = y - mu
    z = yc * lax.rsqrt(jnp.mean(yc * yc, axis=-1, keepdims=True) + EPS) * g + b
    return z * jax.nn.sigmoid(z)


def _conv_fwd(h, gain, get):
    T, D = h.shape
    hn = _norm_fwd("conv_norm", h, gain)
    w = dict(pw1=get["pw1"]())
    b_pw1, w_dw, b_dw, ln_g, ln_b = get["small"]()
    S, _, Ns = w["pw1"].shape
    half = S // 2
    C = half * Ns
    tm, tk = _tile(T, 512, 16), _tile(D, 2048)

    def ep(accs, ex, outs):
        a = accs[0] + ex[0][...]
        g = accs[1] + ex[1][...]
        outs[0][0] = a.astype(BF16)
        outs[0][1] = g.astype(BF16)
        outs[1][...] = a * jax.nn.sigmoid(g)

    ag, u = _matmul(
        "conv_pw1", (half, T // tm, D // tk),
        [(hn, pl.BlockSpec((tm, tk), lambda j, i, k: (i, k))),
         (w["pw1"], pl.BlockSpec((None, tk, Ns), lambda j, i, k: (j, k, 0))),
         (w["pw1"], pl.BlockSpec((None, tk, Ns), lambda j, i, k: (j + half, k, 0)))],
        [(0, 0, 1), (1, 0, 2)], NN, [(tm, Ns), (tm, Ns)],
        [(_sds((2, T, C), BF16), pl.BlockSpec((2, tm, Ns), lambda j, i, k: (0, i, j))),
         (_sds((T, C), F32), pl.BlockSpec((tm, Ns), lambda j, i, k: (i, j)))],
        ep,
        extras=[(b_pw1, pl.BlockSpec((None, 1, Ns), lambda j, i, k: (j, 0, 0))),
                (b_pw1, pl.BlockSpec((None, 1, Ns), lambda j, i, k: (j + half, 0, 0)))])
    y = _dwconv_fwd(u, w_dw, b_dw)

    def fn(ins, outs, acc):
        outs[0][...] = _ln_silu(ins[0][...], ins[1][...], ins[2][...]).astype(BF16)

    s = _rowwise("conv_ln", fn, T, 256, [(y, 0)], [ln_g, ln_b], [((T, C), BF16, 0)])[0]

    w["pw2"] = get["pw2"]()
    h_new = _mm_nn("conv_pw2", s, w["pw2"], F32, res=h)[0]
    return h_new, (h, hn, ag, u, y, s)


def _conv_bwd(saved, gain, w, w_dw, ln_g, ln_b, dh, dhb):
    h, hn, ag, u, y, s = saved
    T, D = h.shape
    S, _, Ns = w["pw1"].shape
    half = S // 2
    C = half * Ns
    ds = _mm_nt("conv_ds", dhb, w["pw2"], F32)[0]
    dw_pw2 = _mm_tn("conv_dwpw2", s, dhb, BF16)

    def fn(ins, outs, acc):
        _, vjp = jax.vjp(_ln_silu, ins[0][...], ins[2][...], ins[3][...])
        dy, dg, db = vjp(ins[1][...])
        outs[0][...] = dy
        acc(0, dg)
        acc(1, db)

    dy, d_ln_g, d_ln_b = _rowwise("conv_dln", fn, T, 256, [(y, 0), (ds, 0)], [ln_g, ln_b],
                                  [((T, C), F32, 0)], [(1, C), (1, C)])
    du, d_w_dw, d_b_dw = _dwconv_bwd(u, w_dw, dy)

    def fn2(ins, outs, acc):
        a = ins[0][0].astype(F32)
        g = ins[0][1].astype(F32)
        du_ = ins[1][...]
        sg = jax.nn.sigmoid(g)
        da = du_ * sg
        dg = du_ * a * sg * (1.0 - sg)
        outs[0][0] = da.astype(BF16)
        outs[0][1] = dg.astype(BF16)
        acc(0, jnp.sum(da, axis=0, keepdims=True))
        acc(1, jnp.sum(dg, axis=0, keepdims=True))

    dag, d_b_a, d_b_g = _rowwise("conv_dglu", fn2, T, 256, [(ag, 1), (du, 0)], [],
                                 [((2, T, C), BF16, 1)], [(1, C), (1, C)])
    tm, tn = _tile(T, 1024, 16), _tile(D, 1024)
    dhn = _matmul(
        "conv_dhn", (T // tm, D // tn, half),
        [(dag, pl.BlockSpec((None, tm, Ns), lambda i, n, j: (0, i, j))),
         (dag, pl.BlockSpec((None, tm, Ns), lambda i, n, j: (1, i, j))),
         (w["pw1"], pl.BlockSpec((None, tn, Ns), lambda i, n, j: (j, n, 0))),
         (w["pw1"], pl.BlockSpec((None, tn, Ns), lambda i, n, j: (j + half, n, 0)))],
        [(0, 0, 2), (0, 1, 3)], NT, [(tm, tn)],
        [(_sds((T, D), F32), pl.BlockSpec((tm, tn), lambda i, n, j: (i, n)))], _store())[0]
    tkd, tt = _tile(D, 1024), _tile(T, 2048)
    dw_pw1 = _matmul(
        "conv_dwpw1", (S, D // tkd, T // tt),
        [(hn, pl.BlockSpec((tt, tkd), lambda j, i, k: (k, i))),
         (dag, pl.BlockSpec((None, tt, Ns), lambda j, i, k: (j // half, k, j % half)))],
        [(0, 0, 1)], TN, [(tkd, Ns)],
        [(_sds((S, D, Ns), BF16), pl.BlockSpec((None, tkd, Ns), lambda j, i, k: (j, i, 0)))], _store())[0]
    dh_in, dh_in_b, dgain = _norm_bwd("conv_dnorm", h, gain, dhn, dh)
    d_b_pw1 = jnp.concatenate([d_b_a, d_b_g], axis=1)
    return dh_in, dh_in_b, dgain, dw_pw1, d_b_pw1, d_w_dw, d_b_dw, d_ln_g, d_ln_b, dw_pw2


def _loss_head(y, target):
    T, D = y.shape

    def fn(ins, outs, acc):
        e = ins[0][...] - ins[1][...]
        d = e * (1.0 / D)
        outs[0][...] = d
        outs[1][...] = d.astype(BF16)
        part = jnp.sum(jnp.sum(e * e, axis=-1, keepdims=True), axis=0, keepdims=True) * (0.5 / D)
        acc(0, jnp.broadcast_to(part, (1, 128)))

    return _rowwise("loss_head", fn, T, 256, [(y, 0), (target, 0)], [], [((T, D), F32, 0), ((T, D), BF16, 0)],
                    [(1, 128)])


def _place():
    return lax.axis_index("x"), lax.axis_index("y"), lax.axis_index("c")


def _peer(j):
    x, y, c = _place()
    return (1 - x if j & 4 else x, 1 - y if j & 2 else y, 1 - c if j & 1 else c)


def _index(place):
    return 4 * place[0] + 2 * place[1] + place[2]


HBM = pl.BlockSpec(memory_space=pltpu.HBM)
SEM = pl.BlockSpec(memory_space=pltpu.SEMAPHORE)
EFFECT = pltpu.SideEffectType.DATAFLOW_SIDE_EFFECTING


def _chip(j):
    x, y, _ = _place()
    return (1 - x if j & 2 else x, 1 - y if j & 1 else y)


def _chip_index(chip):
    return 2 * chip[0] + chip[1]


def _remote(src, dst, send, recv, k, device):
    return pltpu.make_async_remote_copy(src_ref=src, dst_ref=dst, send_sem=send.at[k], recv_sem=recv.at[k],
                                        device_id=device, device_id_type=MESH)


def _hbm(arrays):
    return [pltpu.with_memory_space_constraint(a, pltpu.HBM) for a in arrays]


def _split_call(name, body, ins, sems_in, sems_out):
    n, ns_in, ns_out = len(ins), len(sems_in), len(sems_out)
    if any(a is _LAST[0] for a in ins):
        _LAST[0] = None

    def kernel_body(*refs):
        in_refs = refs[:n]
        si = refs[n:n + ns_in]
        so = refs[n + ns_in:n + ns_in + ns_out]
        tok = refs[-1]
        body(in_refs, si, so, tok)
        tok[...] = jnp.zeros_like(tok)

    res = _ordered_call(
        kernel_body, _hbm(ins) + list(sems_in), chain_out=ns_out + n, name=name,
        in_specs=[HBM] * n + [SEM] * ns_in,
        out_specs=[SEM] * ns_out + [HBM] * n + [pl.BlockSpec(memory_space=pltpu.VMEM)],
        out_shape=[pltpu.SemaphoreType.DMA((s,)) for s in sems_out] + [pltpu.HBM(a.shape, a.dtype) for a in ins]
        + [_sds((8, 128), F32)],
        input_output_aliases={i: ns_out + i for i in range(n)},
        compiler_params=pltpu.CompilerParams(has_side_effects=EFFECT),
    )
    sems = list(res[:ns_out])
    arrays = list(res[ns_out:ns_out + n])
    return sems, arrays, res[-1]


def _ag_start(name, groups):
    flat = [s for grp in groups for s in grp]
    zones = [lax.empty((N_DEV,) + s.shape, s.dtype) for s in flat]
    n = len(flat)
    sizes = []
    for grp in groups:
        sizes += [7 * len(grp), 7 * len(grp), len(grp)]

    def body(ins, si, so, tok):
        x, y, c = _place()
        me = _index((x, y, c))
        base = 0
        for gi, grp in enumerate(groups):
            send, recv, local = so[3 * gi:3 * gi + 3]
            for w in range(len(grp)):
                src, dst = ins[base + w], ins[n + base + w].at[me]
                pltpu.make_async_copy(src, dst, local.at[w]).start()
                _remote(src, dst, send, recv, 7 * w, (x, y, 1 - c)).start()
                for j in (1, 2, 3):
                    _remote(src, dst, send, recv, 7 * w + j, (*_chip(j), c)).start()
            base += len(grp)

    sems, arrays, token = _split_call(name, body, flat + zones, [], sizes)
    out, base = [], 0
    for gi, grp in enumerate(groups):
        k = len(grp)
        out.append((sems[3 * gi:3 * gi + 3], arrays[base:base + k], arrays[n + base:n + base + k]))
        base += k
    return out, token


def _ag_forward(name, handle):
    sems, shards, zones = handle
    k = len(shards)

    def arrive(ins, si, so, tok):
        send, recv, _ = si
        _, _, c = _place()
        for j in (1, 2, 3):
            for w in range(k):
                blk = ins[k + w].at[_index((*_chip(j), c))]
                _remote(ins[w], blk, send, recv, 7 * w + j, (*_chip(j), c)).wait_recv()

    _, arrays, _ = _split_call(name + "_arrive", arrive, list(shards) + list(zones), sems, [])

    def pass_on(ins, si, so, tok):
        fsend, frecv = so
        x, y, c = _place()
        for j in (1, 2, 3):
            for w in range(k):
                blk = ins[w].at[_index((*_chip(j), c))]
                _remote(blk, blk, fsend, frecv, 3 * w + j - 1, (x, y, 1 - c)).start()

    fsems, zones2, _ = _split_call(name + "_pass", pass_on, arrays[k:], [], [3 * k, 3 * k])
    return (list(sems) + fsems, arrays[:k], zones2)


def _ag_wait(name, handle):
    sems, shards, zones = handle
    k = len(shards)

    def body(ins, si, so, tok):
        send, recv, local, fsend, frecv = si
        x, y, c = _place()
        sib = (x, y, 1 - c)
        for w in range(k):
            zone = ins[k + w]
            _remote(ins[w], zone.at[_index(sib)], send, recv, 7 * w, sib).wait_recv()
            for j in (1, 2, 3):
                blk = zone.at[_index((*_chip(j), 1 - c))]
                _remote(blk, blk, fsend, frecv, 3 * w + j - 1, sib).wait_recv()
        for w in range(k):
            zone = ins[k + w]
            mine = zone.at[_index((x, y, c))]
            _remote(ins[w], mine, send, recv, 7 * w, sib).wait_send()
            for j in (1, 2, 3):
                _remote(ins[w], mine, send, recv, 7 * w + j, (*_chip(j), c)).wait_send()
                blk = zone.at[_index((*_chip(j), c))]
                _remote(blk, blk, fsend, frecv, 3 * w + j - 1, sib).wait_send()
            pltpu.make_async_copy(ins[w], mine, local.at[w]).wait()

    _, arrays, _ = _split_call(name, body, list(shards) + list(zones), sems, [])
    return arrays[k:]


def _rs_pair_start(name, grads):
    n = len(grads)
    zones = [lax.empty((4,) + g.shape[1:], g.dtype) for g in grads]

    def body(ins, si, so, tok):
        send, recv = so
        x, y, c = _place()
        for w in range(n):
            for q in range(4):
                _remote(ins[w].at[2 * q + 1 - c], ins[n + w].at[q], send, recv, 4 * w + q, (x, y, 1 - c)).start()

    sems, arrays, token = _split_call(name, body, list(grads) + zones, [], [4 * n, 4 * n])
    return (sems, arrays[:n], arrays[n:]), token


def _rs_pair_wait(name, handle):
    sems, grads, zones = handle
    n = len(grads)

    def body(ins, si, so, tok):
        send, recv = si
        x, y, c = _place()
        for w in range(n):
            for q in range(4):
                cp = _remote(ins[w].at[2 * q + 1 - c], ins[n + w].at[q], send, recv, 4 * w + q, (x, y, 1 - c))
                cp.wait_recv()
                cp.wait_send()

    _, arrays, _ = _split_call(name, body, list(grads) + list(zones), sems, [])
    return arrays[:n], arrays[n:]


def _pair_sum(name, g, got, core):
    _, R, C = g.shape
    g4 = g.reshape(4, 2, R, C)
    tr = _tile(R, 1024, 16)

    def body(c_ref, g_ref, a_ref, *rest):
        o_ref = rest[-1]
        o_ref[...] = (g_ref[...].astype(F32) + a_ref[...].astype(F32)).astype(o_ref.dtype)

    prev = [] if _LAST[0] is None or _LAST[0] is g or _LAST[0] is got else [_LAST[0]]
    out = pl.pallas_call(
        body, name=name,
        grid_spec=pltpu.PrefetchScalarGridSpec(
            num_scalar_prefetch=1, grid=(4, R // tr),
            in_specs=[pl.BlockSpec((None, None, tr, C), lambda q, i, c_ref: (q, c_ref[0], i, 0)),
                      pl.BlockSpec((None, tr, C), lambda q, i, c_ref: (q, i, 0))] + [ANY] * len(prev),
            out_specs=pl.BlockSpec((None, tr, C), lambda q, i, c_ref: (q, i, 0))),
        out_shape=_sds((4, R, C), g.dtype),
        compiler_params=_params(),
    )(core, g4, got, *prev)
    _LAST[0] = out
    return out


def _rs_chip_start(name, sums):
    n = len(sums)
    zones = [lax.empty(s.shape, s.dtype) for s in sums]

    def body(ins, si, so, tok):
        send, recv, local = so
        x, y, c = _place()
        mine = _chip_index((x, y))
        for w in range(n):
            pltpu.make_async_copy(ins[w].at[mine], ins[n + w].at[mine], local.at[w]).start()
            for j in (1, 2, 3):
                _remote(ins[w].at[_chip_index(_chip(j))], ins[n + w].at[mine], send, recv, 3 * w + j - 1,
                        (*_chip(j), c)).start()

    sems, arrays, token = _split_call(name, body, list(sums) + zones, [], [3 * n, 3 * n, n])
    return (sems, arrays[:n], arrays[n:]), token


def _rs_chip_wait(name, handle):
    sems, sums, zones = handle
    n = len(sums)

    def body(ins, si, so, tok):
        send, recv, local = si
        x, y, c = _place()
        mine = _chip_index((x, y))
        for w in range(n):
            for j in (1, 2, 3):
                _remote(ins[w].at[mine], ins[n + w].at[_chip_index(_chip(j))], send, recv, 3 * w + j - 1,
                        (*_chip(j), c)).wait_recv()
        for w in range(n):
            for j in (1, 2, 3):
                _remote(ins[w].at[_chip_index(_chip(j))], ins[n + w].at[mine], send, recv, 3 * w + j - 1,
                        (*_chip(j), c)).wait_send()
            pltpu.make_async_copy(ins[w].at[mine], ins[n + w].at[mine], local.at[w]).wait()

    _, arrays, _ = _split_call(name, body, list(sums) + list(zones), sems, [])
    return arrays[n:]


def _chip_sum(name, slots, layer, layers, into=None):
    _, R, C = slots.shape
    tr = _tile(R, 256, 16)

    def body(*refs):
        s_ref, o_ref = refs[0], refs[-1]
        total = s_ref[0].astype(F32)
        for k in range(1, 4):
            total = total + s_ref[k].astype(F32)
        o_ref[...] = total

    extra = [] if into is None else [into]
    return _ordered_call(
        body, [slots] + extra, name=name, grid=(R // tr,),
        in_specs=[pl.BlockSpec((4, tr, C), lambda i: (0, i, 0))] + [ANY] * len(extra),
        out_specs=pl.BlockSpec((None, tr, C), lambda i: (layer, i, 0)),
        out_shape=_sds((layers, R, C), F32),
        input_output_aliases={} if into is None else {1: 0},
        compiler_params=_params(),
    )


def _pack_rows(name, parts, rows):
    C = parts[0].shape[1]

    def body(*refs):
        o_ref = refs[-1]
        off = 0
        for r in refs[:-1]:
            o_ref[off:off + r.shape[0], :] = r[...]
            off += r.shape[0]
        if off < rows:
            o_ref[off:rows, :] = jnp.zeros((rows - off, C), F32)

    vmem = pl.BlockSpec(memory_space=pltpu.VMEM)
    return _ordered_call(body, list(parts), name=name, in_specs=[vmem] * len(parts), out_specs=vmem,
                         out_shape=_sds((rows, C), F32))


def _reduce_small(rep, shd):
    R, C = rep.shape
    _, Rs, Cs = shd.shape

    def body(r_ref, s_ref, or_ref, os_ref, all_r, all_s, send, recv):
        me = _index(_place())
        all_r[me] = r_ref[...]
        all_s[me] = s_ref[me]
        copies = []
        for j in range(1, N_DEV):
            there = _index(_peer(j))
            copies.append((
                pltpu.make_async_remote_copy(src_ref=r_ref, dst_ref=all_r.at[me], send_sem=send.at[2 * j - 2],
                                             recv_sem=recv.at[2 * j - 2], device_id=_peer(j), device_id_type=MESH),
                pltpu.make_async_remote_copy(src_ref=s_ref.at[there], dst_ref=all_s.at[me],
                                             send_sem=send.at[2 * j - 1], recv_sem=recv.at[2 * j - 1],
                                             device_id=_peer(j), device_id_type=MESH)))
        for a, b in copies:
            a.start()
            b.start()
        for j in range(1, N_DEV):
            there = _index(_peer(j))
            pltpu.make_async_remote_copy(src_ref=r_ref, dst_ref=all_r.at[there], send_sem=send.at[2 * j - 2],
                                         recv_sem=recv.at[2 * j - 2], device_id=_peer(j),
                                         device_id_type=MESH).wait_recv()
            pltpu.make_async_remote_copy(src_ref=s_ref.at[me], dst_ref=all_s.at[there], send_sem=send.at[2 * j - 1],
                                         recv_sem=recv.at[2 * j - 1], device_id=_peer(j),
                                         device_id_type=MESH).wait_recv()
        for a, b in copies:
            a.wait_send()
            b.wait_send()
        tot_r, tot_s = all_r[0], all_s[0]
        for k in range(1, N_DEV):
            tot_r = tot_r + all_r[k]
            tot_s = tot_s + all_s[k]
        or_ref[...] = tot_r
        os_ref[...] = tot_s

    vmem = pl.BlockSpec(memory_space=pltpu.VMEM)
    return _ordered_call(
        body, [rep, shd], name="reduce_small",
        in_specs=[vmem, vmem], out_specs=[vmem, vmem],
        out_shape=[_sds((R, C), F32), _sds((Rs, Cs), F32)],
        scratch_shapes=[pltpu.VMEM((N_DEV, R, C), F32), pltpu.VMEM((N_DEV, Rs, Cs), F32),
                        pltpu.SemaphoreType.DMA((2 * N_DEV - 2,)), pltpu.SemaphoreType.DMA((2 * N_DEV - 2,))],
        compiler_params=_params(),
    )


def _cast_layer(name, w, layer):
    _, R, C = w.shape
    tr = _tile(R, 1024, 16)

    def body(w_ref, o_ref):
        o_ref[...] = w_ref[...].astype(BF16)

    return _ordered_call(
        body, [w], name=name, grid=(R // tr,),
        in_specs=[pl.BlockSpec((None, tr, C), lambda i: (layer, i, 0))],
        out_specs=pl.BlockSpec((tr, C), lambda i: (i, 0)),
        out_shape=_sds((R, C), BF16),
        compiler_params=_params(),
    )


def _adam_math(w, g, m, v):
    c1 = 1.0 / (1.0 - ADAM_B1 ** ADAM_STEP)
    c2 = 1.0 / (1.0 - ADAM_B2 ** ADAM_STEP)
    nm = ADAM_B1 * m + (1.0 - ADAM_B1) * g
    nv = ADAM_B2 * v + (1.0 - ADAM_B2) * (g * g)
    return -ADAM_LR * ((nm * c1) / (jnp.sqrt(nv * c2) + ADAM_EPS) + ADAM_WD * w), nm, nv


def _sum_adam(name, slots, w, m, v, layer, into=None):
    L, R, C = w.shape
    tr = _tile(R, 256, 16)

    def body(*refs):
        s_ref, w_ref, m_ref, v_ref = refs[:4]
        g_ref, d_ref, nm_ref, nv_ref = refs[-4:]
        g = s_ref[0].astype(F32)
        for k in range(1, 4):
            g = g + s_ref[k].astype(F32)
        g_ref[...] = g
        d_ref[...], nm_ref[...], nv_ref[...] = _adam_math(w_ref[...], g, m_ref[...], v_ref[...])

    spec = pl.BlockSpec((None, tr, C), lambda i: (layer, i, 0))
    extra = [] if into is None else list(into)
    return _ordered_call(
        body, [slots, w, m, v] + extra, name=name, grid=(R // tr,),
        in_specs=[pl.BlockSpec((4, tr, C), lambda i: (0, i, 0)), spec, spec, spec] + [ANY] * len(extra),
        out_specs=[spec] * 4,
        out_shape=[_sds((L, R, C), F32)] * 4,
        input_output_aliases={4 + k: k for k in range(len(extra))},
        compiler_params=_params(),
    )


def _adamw(name, w, g, m, v):
    shape = w.shape
    R, C = shape[-2], shape[-1]
    L = 1
    for s in shape[:-2]:
        L *= s
    w3, g3, m3, v3 = (a.reshape(L, R, C) for a in (w, g, m, v))
    tr = _tile(R, 256, 8)

    def body(w_ref, g_ref, m_ref, v_ref, d_ref, nm_ref, nv_ref):
        d_ref[...], nm_ref[...], nv_ref[...] = _adam_math(w_ref[...], g_ref[...], m_ref[...], v_ref[...])

    spec = pl.BlockSpec((None, tr, C), lambda l, i: (l, i, 0))
    outs = _ordered_call(
        body, [w3, g3, m3, v3], name=name, grid=(L, R // tr),
        in_specs=[spec] * 4, out_specs=[spec] * 3,
        out_shape=[_sds((L, R, C), F32)] * 3,
        compiler_params=_params(),
    )
    return tuple(o.reshape(shape) for o in outs)


def _pad_rows(a, rows):
    return jnp.pad(a, ((0, rows - a.shape[0]), (0, 0)))


def _pad_cols(a, cols):
    return jnp.pad(a, ((0, 0), (0, cols - a.shape[1])))


def _rope_tables(positions):
    q = D_ROPE // 2
    inv_freq = ROPE_THETA ** (-jnp.arange(0, D_ROPE, 2, dtype=F32) / D_ROPE)
    ang = positions.astype(F32)[:, None] * inv_freq
    cos, sin = jnp.cos(ang), jnp.sin(ang)
    z = jnp.zeros_like(cos)
    zz = jnp.zeros((cos.shape[0], 128 - 2 * q), F32)
    c = jnp.concatenate([cos, cos, zz], axis=1)
    s1 = jnp.concatenate([z, sin, zz], axis=1)
    s2 = jnp.concatenate([-sin, z, zz], axis=1)
    return c, s1, s2


def kernel(x, p, positions, ffn_a_norm, ffn_a_w_in, ffn_a_w_out, ffn_b_norm, ffn_b_w_in, ffn_b_w_out, mix_norm, mla_w_in, mla_q_lat_norm, mla_kv_lat_norm, mla_w_uq, mla_w_ukv, mla_q_gain, mla_k_gain, mla_w_o, conv_w_pw1, conv_b_pw1, conv_w_dw, conv_b_dw, conv_ln_g, conv_ln_b, conv_w_pw2, ple_w_proj, ple_norm, ple_gate_norm, ple_w_gate, loss_target, m_ffn_a_norm, m_ffn_a_w_in, m_ffn_a_w_out, m_ffn_b_norm, m_ffn_b_w_in, m_ffn_b_w_out, m_mix_norm, m_mla_w_in, m_mla_q_lat_norm, m_mla_kv_lat_norm, m_mla_w_uq, m_mla_w_ukv, m_mla_q_gain, m_mla_k_gain, m_mla_w_o, m_conv_w_pw1, m_conv_b_pw1, m_conv_w_dw, m_conv_b_dw, m_conv_ln_g, m_conv_ln_b, m_conv_w_pw2, m_ple_w_proj, m_ple_norm, m_ple_gate_norm, m_ple_w_gate, v_ffn_a_norm, v_ffn_a_w_in, v_ffn_a_w_out, v_ffn_b_norm, v_ffn_b_w_in, v_ffn_b_w_out, v_mix_norm, v_mla_w_in, v_mla_q_lat_norm, v_mla_kv_lat_norm, v_mla_w_uq, v_mla_w_ukv, v_mla_q_gain, v_mla_k_gain, v_mla_w_o, v_conv_w_pw1, v_conv_b_pw1, v_conv_w_dw, v_conv_b_dw, v_conv_ln_g, v_conv_ln_b, v_conv_w_pw2, v_ple_w_proj, v_ple_norm, v_ple_gate_norm, v_ple_w_gate):
    weights = dict(ffn_a_norm=ffn_a_norm, ffn_a_w_in=ffn_a_w_in, ffn_a_w_out=ffn_a_w_out, ffn_b_norm=ffn_b_norm,
                   ffn_b_w_in=ffn_b_w_in, ffn_b_w_out=ffn_b_w_out, mix_norm=mix_norm, mla_w_in=mla_w_in,
                   mla_q_lat_norm=mla_q_lat_norm, mla_kv_lat_norm=mla_kv_lat_norm, mla_w_uq=mla_w_uq,
                   mla_w_ukv=mla_w_ukv, mla_q_gain=mla_q_gain, mla_k_gain=mla_k_gain, mla_w_o=mla_w_o,
                   conv_w_pw1=conv_w_pw1, conv_b_pw1=conv_b_pw1, conv_w_dw=conv_w_dw, conv_b_dw=conv_b_dw,
                   conv_ln_g=conv_ln_g, conv_ln_b=conv_ln_b, conv_w_pw2=conv_w_pw2, ple_w_proj=ple_w_proj,
                   ple_norm=ple_norm, ple_gate_norm=ple_gate_norm, ple_w_gate=ple_w_gate)
    moments_m = dict(ffn_a_norm=m_ffn_a_norm, ffn_a_w_in=m_ffn_a_w_in, ffn_a_w_out=m_ffn_a_w_out,
                     ffn_b_norm=m_ffn_b_norm, ffn_b_w_in=m_ffn_b_w_in, ffn_b_w_out=m_ffn_b_w_out,
                     mix_norm=m_mix_norm, mla_w_in=m_mla_w_in, mla_q_lat_norm=m_mla_q_lat_norm,
                     mla_kv_lat_norm=m_mla_kv_lat_norm, mla_w_uq=m_mla_w_uq, mla_w_ukv=m_mla_w_ukv,
                     mla_q_gain=m_mla_q_gain, mla_k_gain=m_mla_k_gain, mla_w_o=m_mla_w_o,
                     conv_w_pw1=m_conv_w_pw1, conv_b_pw1=m_conv_b_pw1, conv_w_dw=m_conv_w_dw,
                     conv_b_dw=m_conv_b_dw, conv_ln_g=m_conv_ln_g, conv_ln_b=m_conv_ln_b, conv_w_pw2=m_conv_w_pw2,
                     ple_w_proj=m_ple_w_proj, ple_norm=m_ple_norm, ple_gate_norm=m_ple_gate_norm,
                     ple_w_gate=m_ple_w_gate)
    moments_v = dict(ffn_a_norm=v_ffn_a_norm, ffn_a_w_in=v_ffn_a_w_in, ffn_a_w_out=v_ffn_a_w_out,
                     ffn_b_norm=v_ffn_b_norm, ffn_b_w_in=v_ffn_b_w_in, ffn_b_w_out=v_ffn_b_w_out,
                     mix_norm=v_mix_norm, mla_w_in=v_mla_w_in, mla_q_lat_norm=v_mla_q_lat_norm,
                     mla_kv_lat_norm=v_mla_kv_lat_norm, mla_w_uq=v_mla_w_uq, mla_w_ukv=v_mla_w_ukv,
                     mla_q_gain=v_mla_q_gain, mla_k_gain=v_mla_k_gain, mla_w_o=v_mla_w_o,
                     conv_w_pw1=v_conv_w_pw1, conv_b_pw1=v_conv_b_pw1, conv_w_dw=v_conv_w_dw,
                     conv_b_dw=v_conv_b_dw, conv_ln_g=v_conv_ln_g, conv_ln_b=v_conv_ln_b, conv_w_pw2=v_conv_w_pw2,
                     ple_w_proj=v_ple_w_proj, ple_norm=v_ple_norm, ple_gate_norm=v_ple_gate_norm,
                     ple_w_gate=v_ple_w_gate)
    order = list(weights.keys())
    _LAST[0] = None

    T, D = x.shape[1], x.shape[2]
    me = _index(_place())
    h0 = x[0]
    target = loss_target[0]
    tabs = _rope_tables(positions[0])
    H = N_HEADS
    hps = H // N_DEV
    QL = mla_q_lat_norm.shape[1]
    Cs = conv_b_dw.shape[1]

    def cast(n, i):
        return _cast_layer(f"cast_{n}{i}", weights[n], i)

    first, tok = _ag_start("ag_start0", [[cast("ffn_a_w_in", 0)]])
    m_in_pad = _pad_cols(mla_w_in[0], mla_w_in.shape[2] - D_ROPE + 128)[None]
    uq_pad = jnp.pad(mla_w_uq[0].reshape(QL, hps, QK_DIM), ((0, 0), (0, 0), (0, HEAD_PAD - QK_DIM)))
    uq_pad = uq_pad.reshape(1, QL, hps * HEAD_PAD)
    conv_small = jnp.concatenate([
        _pad_rows(_pad_cols(conv_b_pw1, 2 * Cs), 8),
        _pad_rows(_pad_cols(conv_w_dw[0], 2 * Cs), 32),
        _pad_rows(_pad_cols(jnp.concatenate([conv_b_dw, conv_ln_g, conv_ln_b], axis=0), 2 * Cs), 8)], axis=0)
    rest, tok = _ag_start("ag_start1", [
        [cast("ffn_a_w_out", 0), _cast_layer("cast_mla_in", m_in_pad, 0), _cast_layer("cast_mla_uq", uq_pad, 0),
         _cast_layer("cast_mla_ukv", mla_w_ukv, 0), _cast_layer("cast_mla_wo", mla_w_o, 0)],
        [cast("ffn_b_w_in", 0), cast("ffn_b_w_out", 0), cast("ple_w_gate", 0), cast("ple_w_proj", 0)],
        [cast("ffn_a_w_in", 1), cast("ffn_a_w_out", 1)],
        [_cast_layer("cast_conv_pw1", conv_w_pw1, 0), _cast_layer("cast_conv_pw2", conv_w_pw2, 0), conv_small],
        [cast("ffn_b_w_in", 1), cast("ffn_b_w_out", 1), cast("ple_w_gate", 1), cast("ple_w_proj", 1)]])
    groups = [dict(handle=hd, stage=0, arrays=None) for hd in first + rest]

    def prefetch(gi):
        st = groups[gi]
        if st["stage"] == 0:
            st["handle"] = _ag_forward(f"ag{gi}_forward", st["handle"])
            st["stage"] = 1

    def fetch(gi):
        prefetch(gi)
        st = groups[gi]
        if st["stage"] == 1:
            st["arrays"] = _ag_wait(f"ag{gi}_wait", st["handle"])
            st["stage"] = 2
        return st["arrays"]

    def getter(gi, k, shape=None, ahead=None):
        def get():
            if ahead is not None:
                prefetch(ahead)
            a = fetch(gi)[k]
            return a if shape is None else a.reshape(shape)
        return get

    def conv_small_params():
        small = fetch(4)[2]
        return (small[:, 0:1, :],
                jnp.transpose(small[:, 8:40, :Cs], (1, 0, 2)).reshape(32, N_DEV * Cs),
                small[:, 40, :Cs].reshape(1, N_DEV * Cs), small[:, 41, :Cs].reshape(1, N_DEV * Cs),
                small[:, 42, :Cs].reshape(1, N_DEV * Cs))

    rows = (-1, D)
    get_ffn = [dict(a_in=getter(0, 0), a_out=getter(1, 0, rows), b_in=getter(2, 0), b_out=getter(2, 1, rows)),
               dict(a_in=getter(3, 0), a_out=getter(3, 1, rows, ahead=4), b_in=getter(5, 0),
                    b_out=getter(5, 1, rows))]
    get_ple = [dict(proj=getter(2, 3, ahead=3), gate=getter(2, 2, rows)),
               dict(proj=getter(5, 3), gate=getter(5, 2, rows))]
    get_mla = dict(m_in=getter(1, 1, (D, -1)), uq=getter(1, 2), ukv=getter(1, 3), wo=getter(1, 4, rows, ahead=2))
    get_conv = dict(pw1=getter(4, 0), pw2=getter(4, 1, rows, ahead=5), small=conv_small_params)
    gq_pad = _pad_cols(mla_q_gain, HEAD_PAD)
    gk_pad = _pad_cols(mla_k_gain, HEAD_PAD)
    prefetch(0)

    saved = []
    h = h0
    for i in range(2):
        h, s_a = _ffn_fwd(f"ffn_a{i}", h, ffn_a_norm[i:i + 1], get_ffn[i]["a_in"], get_ffn[i]["a_out"])
        if i == 0:
            h, s_m = _mla_fwd(h, mix_norm[0:1], tabs, get_mla, mla_q_lat_norm, mla_kv_lat_norm, gq_pad, gk_pad)
        else:
            h, s_m = _conv_fwd(h, mix_norm[1:2], get_conv)
        h, s_b = _ffn_fwd(f"ffn_b{i}", h, ffn_b_norm[i:i + 1], get_ffn[i]["b_in"], get_ffn[i]["b_out"])
        h, s_p = _ple_fwd(f"ple{i}", h, p[i, 0], get_ple[i]["proj"], ple_norm[i:i + 1], ple_gate_norm[i:i + 1],
                          get_ple[i]["gate"])
        saved.append((s_a, s_m, s_b, s_p))
    W = [dict(a_in=get_ffn[i]["a_in"](), a_out=get_ffn[i]["a_out"](), b_in=get_ffn[i]["b_in"](),
              b_out=get_ffn[i]["b_out"](), proj=get_ple[i]["proj"](), gate=get_ple[i]["gate"]())
         for i in range(2)]
    Wm = {n: g() for n, g in get_mla.items()}
    Wc = dict(pw1=get_conv["pw1"](), pw2=get_conv["pw2"]())
    _, w_dw_full, _, ln_g_full, ln_b_full = conv_small_params()

    dh, dhb, loss_row = _loss_head(h, target)

    G = {}
    small_g = {}
    stacked = ["ffn_a_w_in", "ffn_a_w_out", "ffn_b_w_in", "ffn_b_w_out", "ple_w_proj", "ple_w_gate"]
    row_sharded = {"ffn_a_w_out", "ffn_b_w_out", "ple_w_gate", "mla_w_in", "mla_w_o", "conv_w_pw2"}
    core = lax.axis_index("c").astype(jnp.int32).reshape(1)
    rs_groups = {
        "pb1": [("ple_w_proj", 1), ("ple_w_gate", 1), ("ffn_b_w_in", 1), ("ffn_b_w_out", 1)],
        "c1": [("conv_w_pw1", 0), ("conv_w_pw2", 0)],
        "a1": [("ffn_a_w_in", 1), ("ffn_a_w_out", 1)],
        "pb0": [("ple_w_proj", 0), ("ple_w_gate", 0), ("ffn_b_w_in", 0), ("ffn_b_w_out", 0)],
        "m0": [("mla_w_in", 0), ("mla_w_uq", 0), ("mla_w_ukv", 0), ("mla_w_o", 0)],
        "ao0": [("ffn_a_w_out", 0)],
        "ai0": [("ffn_a_w_in", 0)]}
    rs = {}

    def rs_begin(tag):
        grads = []
        for n, i in rs_groups[tag]:
            g = G[(n, i)]
            grads.append(g.reshape(N_DEV, g.shape[0] // N_DEV, g.shape[1]) if n in row_sharded else g)
        rs[tag] = _rs_pair_start(f"rs_{tag}_pair_start", grads)[0]

    def rs_pairs(tag):
        mine, got = _rs_pair_wait(f"rs_{tag}_pair_wait", rs[tag])
        rs[tag] = [_pair_sum(f"pairsum_{n}{i}", a, b, core) for (n, i), a, b in zip(rs_groups[tag], mine, got)]

    def rs_chips(tag):
        rs[tag] = _rs_chip_start(f"rs_{tag}_chip_start", rs[tag])[0]

    def rs_end(tag):
        return dict(zip(rs_groups[tag], _rs_chip_wait(f"rs_{tag}_chip_wait", rs[tag])))

    s_a, s_m, s_b, s_p = saved[1]
    dh, dhb, d_pn, d_gn, G[("ple_w_proj", 1)], G[("ple_w_gate", 1)] = _ple_bwd(
        "ple1", s_p, p[1, 0], W[1]["proj"], ple_norm[1:2], ple_gate_norm[1:2], W[1]["gate"], dh, dhb)
    small_g[("ple_norm", 1)], small_g[("ple_gate_norm", 1)] = d_pn, d_gn
    dh, dhb, small_g[("ffn_b_norm", 1)], G[("ffn_b_w_in", 1)], G[("ffn_b_w_out", 1)] = _ffn_bwd(
        "ffn_b1", s_b, ffn_b_norm[1:2], W[1]["b_in"], W[1]["b_out"], dh, dhb)
    rs_begin("pb1")
    (dh, dhb, small_g[("mix_norm", 1)], G[("conv_w_pw1", 0)], d_b_pw1, d_w_dw, d_b_dw, d_ln_g, d_ln_b,
     G[("conv_w_pw2", 0)]) = _conv_bwd(s_m, mix_norm[1:2], Wc, w_dw_full, ln_g_full, ln_b_full, dh, dhb)
    rs_pairs("pb1")
    rs_chips("pb1")
    rs_begin("c1")
    dh, dhb, small_g[("ffn_a_norm", 1)], G[("ffn_a_w_in", 1)], G[("ffn_a_w_out", 1)] = _ffn_bwd(
        "ffn_a1", s_a, ffn_a_norm[1:2], W[1]["a_in"], W[1]["a_out"], dh, dhb)
    rs_pairs("c1")
    rs_chips("c1")
    rs_begin("a1")

    s_a, s_m, s_b, s_p = saved[0]
    dh, dhb, d_pn, d_gn, G[("ple_w_proj", 0)], G[("ple_w_gate", 0)] = _ple_bwd(
        "ple0", s_p, p[0, 0], W[0]["proj"], ple_norm[0:1], ple_gate_norm[0:1], W[0]["gate"], dh, dhb)
    small_g[("ple_norm", 0)], small_g[("ple_gate_norm", 0)] = d_pn, d_gn
    rs_pairs("a1")
    rs_chips("a1")
    dh, dhb, small_g[("ffn_b_norm", 0)], G[("ffn_b_w_in", 0)], G[("ffn_b_w_out", 0)] = _ffn_bwd(
        "ffn_b0", s_b, ffn_b_norm[0:1], W[0]["b_in"], W[0]["b_out"], dh, dhb)
    rs_begin("pb0")

    def in_mla():
        rs_pairs("pb0")
        rs_chips("pb0")

    (dh, dhb, small_g[("mix_norm", 0)], d_qln, d_kvln, d_gq, d_gk,
     G[("mla_w_in", 0)], G[("mla_w_uq", 0)], G[("mla_w_ukv", 0)], G[("mla_w_o", 0)]) = _mla_bwd(
        s_m, mix_norm[0:1], tabs, Wm, mla_q_lat_norm, mla_kv_lat_norm, gq_pad, gk_pad, dh, dhb, hook=in_mla)
    rs_begin("m0")

    def with_dw_out(dw_out):
        G[("ffn_a_w_out", 0)] = dw_out
        rs_pairs("m0")
        rs_chips("m0")
        rs_begin("ao0")

    def with_dw_in(dw_in):
        G[("ffn_a_w_in", 0)] = dw_in
        rs_pairs("ao0")
        rs_chips("ao0")
        rs_begin("ai0")

    dh, dhb, small_g[("ffn_a_norm", 0)], _, _ = _ffn_bwd(
        "ffn_a0", s_a, ffn_a_norm[0:1], W[0]["a_in"], W[0]["a_out"], dh, dhb,
        hooks=(with_dw_out, with_dw_in, lambda: rs_pairs("ai0")))
    grad_x = dh[None]

    replicated = ["ffn_a_norm", "ffn_b_norm", "mix_norm", "ple_norm", "ple_gate_norm"]
    rep = _pack_rows("pack_small", [small_g[(n, i)] for n in replicated for i in (0, 1)]
                     + [_pad_cols(v, D) for v in (d_qln, d_kvln, d_gq, d_gk, loss_row)], 16)

    def dest_major(v, rows):
        r, w = v.shape[0], v.shape[1] // N_DEV
        v = jnp.transpose(v.reshape(r, N_DEV, w), (1, 0, 2))
        return jnp.pad(v, ((0, 0), (0, rows - r), (0, 2 * Cs - w)))

    shd = jnp.concatenate([dest_major(d_b_pw1, 8), dest_major(d_w_dw, 32),
                           dest_major(jnp.concatenate([d_b_dw, d_ln_g, d_ln_b], axis=0), 8)], axis=1)
    red, red_s = _reduce_small(rep, shd)
    loss = red[14, 0]
    small_grads = {n: red[2 * k:2 * k + 2] for k, n in enumerate(replicated)}
    small_grads.update(
        mla_q_lat_norm=red[10:11, :QL], mla_kv_lat_norm=red[11:12, :mla_kv_lat_norm.shape[1]],
        mla_q_gain=red[12:13, :QK_DIM], mla_k_gain=red[13:14, :QK_DIM],
        conv_b_pw1=red_s[0:1], conv_w_dw=red_s[8:8 + CONV_WIDTH, :Cs][None],
        conv_b_dw=red_s[40:41, :Cs], conv_ln_g=red_s[41:42, :Cs], conv_ln_b=red_s[42:43, :Cs])

    rs_chips("ai0")
    done = {}

    def plain_adamw(n, g):
        done[n] = (g,) + _adamw(f"adamw_{n}", weights[n], g, moments_m[n], moments_v[n])

    def slot_adamw(n, layer, slots):
        done[n] = tuple(_sum_adam(f"adamw_{n}{layer}", slots, weights[n], moments_m[n], moments_v[n], layer,
                                  into=done.get(n)))

    for n, g in small_grads.items():
        plain_adamw(n, g)
    for tag in ("pb1", "c1", "a1", "pb0"):
        slots = rs_end(tag)
        for n, i in rs_groups[tag]:
            slot_adamw(n, i, slots[(n, i)])
    slots = rs_end("m0")
    slot_adamw("mla_w_ukv", 0, slots[("mla_w_ukv", 0)])
    slot_adamw("mla_w_o", 0, slots[("mla_w_o", 0)])
    g_in = _chip_sum("chipsum_mla_w_in", slots[("mla_w_in", 0)], 0, 1)
    plain_adamw("mla_w_in", g_in[:, :, :mla_w_in.shape[2]])
    g_uq = _chip_sum("chipsum_mla_w_uq", slots[("mla_w_uq", 0)], 0, 1)
    plain_adamw("mla_w_uq", g_uq.reshape(1, QL, hps, HEAD_PAD)[..., :QK_DIM].reshape(mla_w_uq.shape))
    for tag in ("ao0", "ai0"):
        slots = rs_end(tag)
        for n, i in rs_groups[tag]:
            slot_adamw(n, i, slots[(n, i)])
    grads, deltas, new_m, new_v = ({n: done[n][k] for n in order} for k in range(4))

    return (loss, grad_x, *[grads[n] for n in order], *[deltas[n] for n in order],
            *[new_m[n] for n in order], *[new_v[n] for n in order])
```

```python
import functools

import jax
import jax.numpy as jnp
from jax import lax
from jax.experimental import pallas as pl
from jax.experimental.pallas import tpu as pltpu

F32 = jnp.float32
BF16 = jnp.bfloat16
MESH = pl.DeviceIdType.MESH
ANY = pl.BlockSpec(memory_space=pl.ANY)

N_DEV = 8
N_HEADS = 16
D_NOPE = 128
D_ROPE = 64
D_V = 128
QK_DIM = D_NOPE + D_ROPE
HEAD_PAD = 256
ROPE_THETA = 10000.0
CONV_WIDTH = 31
CONV_PAD = 32
FFN_RES = 0.5
EPS = 1e-6
ADAM_LR = 0.001
ADAM_B1 = 0.9
ADAM_B2 = 0.999
ADAM_EPS = 1e-08
ADAM_WD = 0.01
ADAM_STEP = 10
VMEM_LIMIT = 56 * 1024 * 1024


def _sds(shape, dtype):
    return jax.ShapeDtypeStruct(tuple(int(s) for s in shape), dtype)


def _tile(n, pref, mult=128):
    if n <= pref:
        return n
    t = (pref // mult) * mult
    while t >= mult:
        if n % t == 0:
            return t
        t -= mult
    return n


def _params():
    return pltpu.CompilerParams(vmem_limit_bytes=VMEM_LIMIT)


_LAST = [None]


def _ordered_call(body, operands, chain_out=0, **kw):
    operands = list(operands)
    n_in = len(operands)
    if _LAST[0] is not None and not any(op is _LAST[0] for op in operands):
        inner = body

        def body(*refs):
            inner(*refs[:n_in], *refs[n_in + 1:])

        kw = dict(kw, in_specs=list(kw["in_specs"]) + [ANY])
        operands.append(_LAST[0])
    out = pl.pallas_call(body, **kw)(*operands)
    _LAST[0] = out[chain_out] if isinstance(out, (list, tuple)) else out
    return out


def _matmul(name, grid, ops, terms, dims, acc_shapes, outs, epilogue, extras=()):
    nk = grid[2]
    n_ops, n_ex, n_out, n_acc = len(ops), len(extras), len(outs), len(acc_shapes)

    def body(*refs):
        op_refs = refs[:n_ops]
        ex_refs = refs[n_ops:n_ops + n_ex]
        out_refs = refs[n_ops + n_ex:n_ops + n_ex + n_out]
        acc_refs = refs[n_ops + n_ex + n_out:]
        vals = {}

        def opval(i):
            if i not in vals:
                v = op_refs[i][...]
                vals[i] = v if v.dtype == BF16 else v.astype(BF16)
            return vals[i]

        parts = [None] * n_acc
        for ai, li, ri in terms:
            d = lax.dot_general(opval(li), opval(ri), (dims, ((), ())), preferred_element_type=F32)
            parts[ai] = d if parts[ai] is None else parts[ai] + d
        if nk == 1:
            epilogue(parts, ex_refs, out_refs)
            return
        k = pl.program_id(2)

        @pl.when(k == 0)
        def _():
            for a_ref, p in zip(acc_refs, parts):
                a_ref[...] = p

        @pl.when(k > 0)
        def _():
            for a_ref, p in zip(acc_refs, parts):
                a_ref[...] += p

        @pl.when(k == nk - 1)
        def _():
            epilogue([a[...] for a in acc_refs], ex_refs, out_refs)

    scratch = [pltpu.VMEM(s, F32) for s in acc_shapes] if nk > 1 else []
    return _ordered_call(
        body, [a for a, _ in ops] + [a for a, _ in extras], name=name, grid=grid,
        in_specs=[s for _, s in ops] + [s for _, s in extras],
        out_specs=[s for _, s in outs],
        out_shape=[o for o, _ in outs],
        scratch_shapes=scratch,
        compiler_params=_params(),
    )


NN = ((1,), (0,))
NT = ((1,), (1,))
TN = ((0,), (0,))


def _store(i=0):
    def ep(accs, ex, outs):
        outs[0][...] = accs[0].astype(outs[0].dtype)
    return ep


def _mm_nn(name, a, b, out_dtype, tm=1024, tn=1024, tk=2048, res=None, scale=1.0):
    M, K = a.shape
    N = b.shape[1]
    tm, tn, tk = _tile(M, tm, 16), _tile(N, tn), _tile(K, tk)
    spec = pl.BlockSpec((tm, tn), lambda i, j, k: (i, j))

    def ep(accs, ex, outs):
        v = accs[0] if scale == 1.0 else accs[0] * scale
        outs[0][...] = (v if res is None else ex[0][...] + v).astype(outs[0].dtype)

    return _matmul(name, (M // tm, N // tn, K // tk),
                   [(a, pl.BlockSpec((tm, tk), lambda i, j, k: (i, k))),
                    (b, pl.BlockSpec((tk, tn), lambda i, j, k: (k, j)))],
                   [(0, 0, 1)], NN, [(tm, tn)], [(_sds((M, N), out_dtype), spec)], ep,
                   [] if res is None else [(res, spec)])


def _mm_nt(name, a, b, out_dtype, tm=1024, tn=1024, tk=2048, epilogue=None, extras=()):
    M, K = a.shape
    N = b.shape[0]
    tm, tn, tk = _tile(M, tm, 16), _tile(N, tn), _tile(K, tk)
    outs = [(_sds((M, N), out_dtype), pl.BlockSpec((tm, tn), lambda i, j, k: (i, j)))]
    return _matmul(name, (M // tm, N // tn, K // tk),
                   [(a, pl.BlockSpec((tm, tk), lambda i, j, k: (i, k))),
                    (b, pl.BlockSpec((tn, tk), lambda i, j, k: (j, k)))],
                   [(0, 0, 1)], NT, [(tm, tn)], outs, epilogue or _store(), extras)


def _mm_tn(name, a, b, out_dtype, tm=512, tn=2048, tk=2048, scale=None):
    T, M = a.shape
    N = b.shape[1]
    tm, tn, tk = _tile(M, tm), _tile(N, tn), _tile(T, tk)

    def ep(accs, ex, outs):
        v = accs[0] if scale is None else accs[0] * scale
        outs[0][...] = v.astype(outs[0].dtype)

    outs = [(_sds((M, N), out_dtype), pl.BlockSpec((tm, tn), lambda i, j, k: (i, j)))]
    return _matmul(name, (M // tm, N // tn, T // tk),
                   [(a, pl.BlockSpec((tk, tm), lambda i, j, k: (k, i))),
                    (b, pl.BlockSpec((tk, tn), lambda i, j, k: (k, j)))],
                   [(0, 0, 1)], TN, [(tm, tn)], outs, ep)[0]


def _mm_nn_sm(name, a, w, out_dtype, tm=2048, tk=2048, epilogue=None, extras=()):
    M, K = a.shape
    S, _, Ns = w.shape
    tm, tk = _tile(M, tm, 16), _tile(K, tk)
    outs = [(_sds((M, S * Ns), out_dtype), pl.BlockSpec((tm, Ns), lambda j, i, k: (i, j)))]
    return _matmul(name, (S, M // tm, K // tk),
                   [(a, pl.BlockSpec((tm, tk), lambda j, i, k: (i, k))),
                    (w, pl.BlockSpec((None, tk, Ns), lambda j, i, k: (j, k, 0)))],
                   [(0, 0, 1)], NN, [(tm, Ns)], outs, epilogue or _store(), extras)[0]


def _mm_nt_sm(name, a, w, out_dtype, tm=1024, tn=1024):
    M = a.shape[0]
    S, K, Ns = w.shape
    tm, tn = _tile(M, tm, 16), _tile(K, tn)
    outs = [(_sds((M, K), out_dtype), pl.BlockSpec((tm, tn), lambda i, n, j: (i, n)))]
    return _matmul(name, (M // tm, K // tn, S),
                   [(a, pl.BlockSpec((tm, Ns), lambda i, n, j: (i, j))),
                    (w, pl.BlockSpec((None, tn, Ns), lambda i, n, j: (j, n, 0)))],
                   [(0, 0, 1)], NT, [(tm, tn)], outs, _store())[0]


def _mm_tn_sm(name, a, b, S, out_dtype, tm=1024, tk=2048):
    T, M = a.shape
    Ns = b.shape[1] // S
    tm, tk = _tile(M, tm), _tile(T, tk)
    outs = [(_sds((S, M, Ns), out_dtype), pl.BlockSpec((None, tm, Ns), lambda j, i, k: (j, i, 0)))]
    return _matmul(name, (S, M // tm, T // tk),
                   [(a, pl.BlockSpec((tk, tm), lambda j, i, k: (k, i))),
                    (b, pl.BlockSpec((tk, Ns), lambda j, i, k: (k, j)))],
                   [(0, 0, 1)], TN, [(tm, Ns)], outs, _store())[0]


def _row_spec(shape, axis, tm):
    block = tuple(tm if d == axis else s for d, s in enumerate(shape))
    nd = len(shape)

    def imap(i):
        return tuple(i if d == axis else 0 for d in range(nd))
    return pl.BlockSpec(block, imap)


def _full_spec(shape):
    nd = len(shape)
    return pl.BlockSpec(tuple(shape), lambda i: (0,) * nd)


def _rowwise(name, fn, T, tm, rows, consts, outs, accs=()):
    tm = _tile(T, tm, 16)
    n_in = len(rows) + len(consts)
    n_out = len(outs)

    def body(*refs):
        in_refs = refs[:n_in]
        out_refs = refs[n_in:n_in + n_out]
        acc_refs = refs[n_in + n_out:]
        i = pl.program_id(0)

        def acc_add(ai, val):
            @pl.when(i == 0)
            def _():
                acc_refs[ai][...] = val

            @pl.when(i > 0)
            def _():
                acc_refs[ai][...] += val

        fn(in_refs, out_refs, acc_add)

    return _ordered_call(
        body, [a for a, _ in rows] + list(consts), name=name, grid=(T // tm,),
        in_specs=[_row_spec(a.shape, ax, tm) for a, ax in rows] + [_full_spec(c.shape) for c in consts],
        out_specs=[_row_spec(s, ax, tm) for s, _, ax in outs] + [_full_spec(s) for s in accs],
        out_shape=[_sds(s, d) for s, d, _ in outs] + [_sds(s, F32) for s in accs],
        compiler_params=_params(),
    )


def _rms(x, g, n=None):
    n = x.shape[-1] if n is None else n
    return x * lax.rsqrt(jnp.sum(x * x, axis=-1, keepdims=True) * (1.0 / n) + EPS) * g


def _norm_fwd(name, h, gain):
    T, D = h.shape

    def fn(ins, outs, acc):
        outs[0][...] = _rms(ins[0][...], ins[1][...]).astype(BF16)

    return _rowwise(name, fn, T, 512, [(h, 0)], [gain], [((T, D), BF16, 0)])[0]


def _norm_bwd(name, h, gain, dhn, dh_res):
    T, D = h.shape

    def fn(ins, outs, acc):
        _, vjp = jax.vjp(_rms, ins[0][...], ins[3][...])
        dh, dg = vjp(ins[1][...])
        dh = dh + ins[2][...]
        outs[0][...] = dh
        outs[1][...] = dh.astype(BF16)
        acc(0, dg)

    return _rowwise(name, fn, T, 256, [(h, 0), (dhn, 0), (dh_res, 0)], [gain],
                    [((T, D), F32, 0), ((T, D), BF16, 0)], [(1, D)])


def _ffn_fwd(tag, h, gain, get_in, get_out):
    T, D = h.shape
    hn = _norm_fwd(tag + "_norm", h, gain)
    w_in = get_in()
    S, _, Ns = w_in.shape
    half = S // 2
    F = half * Ns
    tm, tk = _tile(T, 256, 16), _tile(D, 2048)

    def ep(accs, ex, outs):
        g, u = accs
        sg = jax.nn.sigmoid(g)
        silu = g * sg
        outs[0][0] = (u * (sg * (1.0 + g * (1.0 - sg)))).astype(BF16)
        outs[0][1] = silu.astype(BF16)
        outs[1][...] = (silu * u).astype(BF16)

    gu, act = _matmul(
        tag + "_in", (half, T // tm, D // tk),
        [(hn, pl.BlockSpec((tm, tk), lambda j, i, k: (i, k))),
         (w_in, pl.BlockSpec((None, tk, Ns), lambda j, i, k: (j, k, 0))),
         (w_in, pl.BlockSpec((None, tk, Ns), lambda j, i, k: (j + half, k, 0)))],
        [(0, 0, 1), (1, 0, 2)], NN, [(tm, Ns), (tm, Ns)],
        [(_sds((2, T, F), BF16), pl.BlockSpec((2, tm, Ns), lambda j, i, k: (0, i, j))),
         (_sds((T, F), BF16), pl.BlockSpec((tm, Ns), lambda j, i, k: (i, j)))],
        ep)

    w_out = get_out()
    h_new = _mm_nn(tag + "_out", act, w_out, F32, tm=1024, tn=512, tk=F, res=h, scale=FFN_RES)[0]
    return h_new, (h, hn, gu, act)


def _no_hook(*_):
    return None


def _ffn_bwd(tag, saved, gain, w_in, w_out, dh, dhb, hooks=(_no_hook, _no_hook, _no_hook)):
    h, hn, gu, act = saved
    T, D = h.shape
    S, _, Ns = w_in.shape
    half = S // 2
    F = half * Ns
    tm, tk = _tile(T, 512, 16), _tile(D, 2048)

    def ep(accs, ex, outs):
        dact = FFN_RES * accs[0]
        outs[0][0] = (dact * ex[0][0].astype(F32)).astype(BF16)
        outs[0][1] = (dact * ex[0][1].astype(F32)).astype(BF16)

    gu_spec = pl.BlockSpec((2, tm, Ns), lambda j, i, k: (0, i, j))
    dgu = _matmul(
        tag + "_dact", (half, T // tm, D // tk),
        [(dhb, pl.BlockSpec((tm, tk), lambda j, i, k: (i, k))),
         (w_out, pl.BlockSpec((Ns, tk), lambda j, i, k: (j, k)))],
        [(0, 0, 1)], NT, [(tm, Ns)],
        [(_sds((2, T, F), BF16), gu_spec)], ep, extras=[(gu, gu_spec)])[0]

    dw_out = _mm_tn(tag + "_dwout", act, dhb, BF16, scale=FFN_RES)
    hooks[0](dw_out)

    tm = _tile(T, 1024, 16)
    tkd, tt = _tile(D, 512), _tile(T, 2048)
    dw_in = _matmul(
        tag + "_dwin", (S, D // tkd, T // tt),
        [(hn, pl.BlockSpec((tt, tkd), lambda j, i, k: (k, i))),
         (dgu, pl.BlockSpec((None, tt, Ns), lambda j, i, k: (j // half, k, j % half)))],
        [(0, 0, 1)], TN, [(tkd, Ns)],
        [(_sds((S, D, Ns), BF16), pl.BlockSpec((None, tkd, Ns), lambda j, i, k: (j, i, 0)))], _store())[0]
    hooks[1](dw_in)

    tn = _tile(D, 1024)
    dhn = _matmul(
        tag + "_dhn", (T // tm, D // tn, half),
        [(dgu, pl.BlockSpec((None, tm, Ns), lambda i, n, j: (0, i, j))),
         (dgu, pl.BlockSpec((None, tm, Ns), lambda i, n, j: (1, i, j))),
         (w_in, pl.BlockSpec((None, tn, Ns), lambda i, n, j: (j, n, 0))),
         (w_in, pl.BlockSpec((None, tn, Ns), lambda i, n, j: (j + half, n, 0)))],
        [(0, 0, 2), (0, 1, 3)], NT, [(tm, tn)],
        [(_sds((T, D), F32), pl.BlockSpec((tm, tn), lambda i, n, j: (i, n)))], _store())[0]
    hooks[2]()

    dh_in, dh_in_b, dgain = _norm_bwd(tag + "_dnorm", h, gain, dhn, dh)
    return dh_in, dh_in_b, dgain, dw_in, dw_out


def _ple_fwd(tag, h, p, get_proj, ple_norm, gate_norm, get_gate):
    T, D = h.shape
    e_raw = _mm_nn_sm(tag + "_proj", p, get_proj(), F32)
    hn = _norm_fwd(tag + "_norm", h, gate_norm)
    gate_raw = _mm_nn(tag + "_gate", hn, get_gate(), F32)[0]

    def fn(ins, outs, acc):
        e = _rms(ins[1][...], ins[3][...])
        outs[0][...] = ins[0][...] + e * jax.nn.sigmoid(ins[2][...])

    h_new = _rowwise(tag + "_mix", fn, T, 256, [(h, 0), (e_raw, 0), (gate_raw, 0)], [ple_norm],
                     [((T, D), F32, 0)])[0]
    return h_new, (h, hn, e_raw, gate_raw)


def _ple_bwd(tag, saved, p, w_proj, ple_norm, gate_norm, w_gate, dh, dhb):
    h, hn, e_raw, gate_raw = saved
    T, D = h.shape
    S = w_proj.shape[0]

    def fn(ins, outs, acc):
        def f(e_raw_, gate_raw_, g_):
            return _rms(e_raw_, g_) * jax.nn.sigmoid(gate_raw_)
        _, vjp = jax.vjp(f, ins[0][...], ins[1][...], ins[3][...])
        de, dgate, dg = vjp(ins[2][...])
        outs[0][...] = de.astype(BF16)
        outs[1][...] = dgate.astype(BF16)
        acc(0, dg)

    de, dgate, d_ple_norm = _rowwise(tag + "_dmix", fn, T, 256, [(e_raw, 0), (gate_raw, 0), (dh, 0)], [ple_norm],
                                     [((T, D), BF16, 0), ((T, D), BF16, 0)], [(1, D)])
    dw_proj = _mm_tn_sm(tag + "_dwproj", p, de, S, BF16)
    dw_gate = _mm_tn(tag + "_dwgate", hn, dgate, BF16)
    dhn = _mm_nt(tag + "_dhn", dgate, w_gate, F32)[0]
    dh_in, dh_in_b, d_gate_norm = _norm_bwd(tag + "_dnorm", h, gate_norm, dhn, dh)
    return dh_in, dh_in_b, d_ple_norm, d_gate_norm, dw_proj, dw_gate


def _rope(t, c, s1, s2):
    q = D_ROPE // 2
    return t * c + pltpu.roll(t, q, 1) * s1 + pltpu.roll(t, 128 - q, 1) * s2


def _rope_t(d, c, s1, s2):
    q = D_ROPE // 2
    return d * c + pltpu.roll(d * s1, 128 - q, 1) + pltpu.roll(d * s2, q, 1)


def _head_norm(lo, hi, g_lo, g_hi):
    ms = (jnp.sum(lo * lo, axis=-1, keepdims=True) + jnp.sum(hi * hi, axis=-1, keepdims=True)) * (1.0 / QK_DIM)
    inv = lax.rsqrt(ms + EPS)
    return lo * inv * g_lo, hi * inv * g_hi


def _qk_prep(qraw, kvraw, lat, tabs, gq, gk, H):
    T = qraw.shape[0]
    koff = lat.shape[1] - 128

    def fn(ins, outs, acc):
        q_ref, kv_ref, lat_ref, c_ref, s1_ref, s2_ref, gq_ref, gk_ref = ins
        c, s1, s2 = c_ref[...], s1_ref[...], s2_ref[...]
        kr = lat_ref[:, koff:koff + 128]
        for hd in range(H):
            o = hd * HEAD_PAD
            lo, hi = _head_norm(q_ref[:, o:o + 128], q_ref[:, o + 128:o + 256], gq_ref[:, 0:128], gq_ref[:, 128:256])
            outs[0][hd, :, 0:128] = lo.astype(BF16)
            outs[0][hd, :, 128:256] = _rope(hi, c, s1, s2).astype(BF16)
            lo, hi = _head_norm(kv_ref[:, o:o + 128], kr, gk_ref[:, 0:128], gk_ref[:, 128:256])
            outs[1][hd, :, 0:128] = lo.astype(BF16)
            outs[1][hd, :, 128:256] = _rope(hi, c, s1, s2).astype(BF16)
            outs[2][hd] = kv_ref[:, o + 128:o + 256].astype(BF16)

    return _rowwise("mla_qkprep", fn, T, 256, [(qraw, 0), (kvraw, 0), (lat, 0)] + [(t, 0) for t in tabs], [gq, gk],
                    [((H, T, HEAD_PAD), BF16, 1), ((H, T, HEAD_PAD), BF16, 1), ((H, T, D_V), BF16, 1)])


def _qk_prep_bwd(qraw, kvraw, lat, tabs, gq, gk, dQ, dK, dV, H):
    T = qraw.shape[0]
    koff = lat.shape[1] - 128

    def fn(ins, outs, acc):
        q_ref, kv_ref, lat_ref, c_ref, s1_ref, s2_ref, dq_ref, dk_ref, dv_ref, gq_ref, gk_ref = ins
        c, s1, s2 = c_ref[...], s1_ref[...], s2_ref[...]
        kr = lat_ref[:, koff:koff + 128]
        dkr = jnp.zeros_like(kr)
        dg = [None] * 4
        for hd in range(H):
            o = hd * HEAD_PAD
            _, vjp = jax.vjp(_head_norm, q_ref[:, o:o + 128], q_ref[:, o + 128:o + 256],
                             gq_ref[:, 0:128], gq_ref[:, 128:256])
            dlo, dhi, dg0, dg1 = vjp((dq_ref[hd, :, 0:128], _rope_t(dq_ref[hd, :, 128:256], c, s1, s2)))
            outs[0][:, o:o + 128] = dlo.astype(BF16)
            outs[0][:, o + 128:o + 256] = dhi.astype(BF16)
            _, vjp = jax.vjp(_head_norm, kv_ref[:, o:o + 128], kr, gk_ref[:, 0:128], gk_ref[:, 128:256])
            dlo, dhi, dg2, dg3 = vjp((dk_ref[hd, :, 0:128], _rope_t(dk_ref[hd, :, 128:256], c, s1, s2)))
            outs[1][:, o:o + 128] = dlo.astype(BF16)
            outs[1][:, o + 128:o + 256] = dv_ref[hd].astype(BF16)
            dkr = dkr + dhi
            for n, v in enumerate((dg0, dg1, dg2, dg3)):
                dg[n] = v if dg[n] is None else dg[n] + v
        outs[2][...] = dkr
        for n in range(4):
            acc(n, dg[n])

    W = H * HEAD_PAD
    return _rowwise("mla_dqkprep", fn, T, 128,
                    [(qraw, 0), (kvraw, 0), (lat, 0)] + [(t, 0) for t in tabs] + [(dQ, 1), (dK, 1), (dV, 1)], [gq, gk],
                    [((T, W), BF16, 0), ((T, W), BF16, 0), ((T, 128), F32, 0)], [(1, 128)] * 4)


def _attn_probs(q, k, c, tq):
    nk = k.shape[0]
    s = lax.dot_general(q, k, (NT, ((), ())), preferred_element_type=F32) * (QK_DIM ** -0.5)
    row = c * tq + lax.broadcasted_iota(jnp.int32, (tq, nk), 0)
    col = lax.broadcasted_iota(jnp.int32, (tq, nk), 1)
    s = jnp.where(col <= row, s, -jnp.inf)
    p = jnp.exp(s - jnp.max(s, axis=-1, keepdims=True))
    return p / jnp.sum(p, axis=-1, keepdims=True)


def _per_query_block(nq, fn):
    i = pl.program_id(1)
    for c in range(nq):
        pl.when(i == c)(functools.partial(fn, c))


def _attn_fwd(Q, K, V):
    H, T, _ = Q.shape
    tq = _tile(T, 256)

    def body(q_ref, k_ref, v_ref, o_ref):
        def block(c):
            nk = (c + 1) * tq
            p = _attn_probs(q_ref[...], k_ref[0:nk, :], c, tq)
            o_ref[...] = jnp.dot(p.astype(BF16), v_ref[0:nk, :], preferred_element_type=F32).astype(BF16)

        _per_query_block(T // tq, block)

    return _ordered_call(
        body, [Q, K, V], name="mla_attn", grid=(H, T // tq),
        in_specs=[pl.BlockSpec((None, tq, HEAD_PAD), lambda h, i: (h, i, 0)),
                  pl.BlockSpec((None, T, HEAD_PAD), lambda h, i: (h, 0, 0)),
                  pl.BlockSpec((None, T, D_V), lambda h, i: (h, 0, 0))],
        out_specs=pl.BlockSpec((tq, D_V), lambda h, i: (i, h)),
        out_shape=_sds((T, H * D_V), BF16),
        compiler_params=_params(),
    )


def _attn_bwd(Q, K, V, dO):
    H, T, _ = Q.shape
    tq = _tile(T, 256)

    def body(q_ref, k_ref, v_ref, do_ref, dq_ref, dk_ref, dv_ref):
        @pl.when(pl.program_id(1) == 0)
        def _():
            dk_ref[...] = jnp.zeros_like(dk_ref)
            dv_ref[...] = jnp.zeros_like(dv_ref)

        def block(c):
            nk = (c + 1) * tq
            q, k, do = q_ref[...], k_ref[0:nk, :], do_ref[...]
            p = _attn_probs(q, k, c, tq)
            dv_ref[0:nk, :] += lax.dot_general(p.astype(BF16), do, (TN, ((), ())), preferred_element_type=F32)
            dp = lax.dot_general(do, v_ref[0:nk, :], (NT, ((), ())), preferred_element_type=F32)
            ds = p * (dp - jnp.sum(p * dp, axis=-1, keepdims=True)) * (QK_DIM ** -0.5)
            dsb = ds.astype(BF16)
            dq_ref[...] = jnp.dot(dsb, k, preferred_element_type=F32)
            dk_ref[0:nk, :] += lax.dot_general(dsb, q, (TN, ((), ())), preferred_element_type=F32)

        _per_query_block(T // tq, block)

    return _ordered_call(
        body, [Q, K, V, dO], name="mla_dattn", grid=(H, T // tq),
        in_specs=[pl.BlockSpec((None, tq, HEAD_PAD), lambda h, i: (h, i, 0)),
                  pl.BlockSpec((None, T, HEAD_PAD), lambda h, i: (h, 0, 0)),
                  pl.BlockSpec((None, T, D_V), lambda h, i: (h, 0, 0)),
                  pl.BlockSpec((tq, D_V), lambda h, i: (i, h))],
        out_specs=[pl.BlockSpec((None, tq, HEAD_PAD), lambda h, i: (h, i, 0)),
                   pl.BlockSpec((None, T, HEAD_PAD), lambda h, i: (h, 0, 0)),
                   pl.BlockSpec((None, T, D_V), lambda h, i: (h, 0, 0))],
        out_shape=[_sds((H, T, HEAD_PAD), F32), _sds((H, T, HEAD_PAD), F32), _sds((H, T, D_V), F32)],
        compiler_params=_params(),
    )


def _mla_fwd(h, gain, tabs, get, q_lat_norm, kv_lat_norm, gq, gk):
    T, D = h.shape
    QL, KL = q_lat_norm.shape[1], kv_lat_norm.shape[1]
    hn = _norm_fwd("mla_norm", h, gain)
    w = dict(m_in=get["m_in"]())
    lat = _mm_nn("mla_lat", hn, w["m_in"], F32, tn=w["m_in"].shape[1])[0]

    def fn(ins, outs, acc):
        outs[0][...] = _rms(ins[0][:, 0:QL], ins[1][...]).astype(BF16)
        outs[1][...] = _rms(ins[0][:, QL:QL + KL], ins[2][...]).astype(BF16)

    cq, ckv = _rowwise("mla_latnorm", fn, T, 256, [(lat, 0)], [q_lat_norm, kv_lat_norm],
                       [((T, QL), BF16, 0), ((T, KL), BF16, 0)])
    w["uq"], w["ukv"] = get["uq"](), get["ukv"]()
    H = w["uq"].shape[0] * w["uq"].shape[2] // HEAD_PAD
    qraw = _mm_nn_sm("mla_uq", cq, w["uq"], F32)
    kvraw = _mm_nn_sm("mla_ukv", ckv, w["ukv"], F32)
    Q, K, V = _qk_prep(qraw, kvraw, lat, tabs, gq, gk, H)
    O = _attn_fwd(Q, K, V)
    w["wo"] = get["wo"]()

    h_new = _mm_nn("mla_out", O, w["wo"], F32, res=h)[0]
    return h_new, (h, hn, lat, cq, ckv, qraw, kvraw, Q, K, V, O)


def _mla_bwd(saved, gain, tabs, w, q_lat_norm, kv_lat_norm, gq, gk, dh, dhb, hook=_no_hook):
    h, hn, lat, cq, ckv, qraw, kvraw, Q, K, V, O = saved
    T, D = h.shape
    H = Q.shape[0]
    S = w["uq"].shape[0]
    QL, KL = q_lat_norm.shape[1], kv_lat_norm.shape[1]
    dO = _mm_nt("mla_dO", dhb, w["wo"], BF16)[0]
    dwo = _mm_tn("mla_dwo", O, dhb, BF16)
    hook()
    dQ, dK, dV = _attn_bwd(Q, K, V, dO)
    dqraw, dkvraw, dkr, dgq0, dgq1, dgk0, dgk1 = _qk_prep_bwd(qraw, kvraw, lat, tabs, gq, gk, dQ, dK, dV, H)
    dcq = _mm_nt_sm("mla_dcq", dqraw, w["uq"], F32)
    dckv = _mm_nt_sm("mla_dckv", dkvraw, w["ukv"], F32)
    dwuq = _mm_tn_sm("mla_dwuq", cq, dqraw, S, BF16)
    dwukv = _mm_tn_sm("mla_dwukv", ckv, dkvraw, S, BF16)

    def fn(ins, outs, acc):
        _, vjp = jax.vjp(_rms, ins[0][:, 0:QL], ins[4][...])
        d, dgq_ = vjp(ins[1][...])
        outs[0][:, 0:QL] = d.astype(BF16)
        _, vjp = jax.vjp(_rms, ins[0][:, QL:QL + KL], ins[5][...])
        d, dgkv_ = vjp(ins[2][...])
        outs[0][:, QL:QL + KL] = d.astype(BF16)
        outs[0][:, QL + KL:QL + KL + 128] = ins[3][...].astype(BF16)
        acc(0, dgq_)
        acc(1, dgkv_)

    dlat, d_qln, d_kvln = _rowwise("mla_dlatnorm", fn, T, 256, [(lat, 0), (dcq, 0), (dckv, 0), (dkr, 0)],
                                   [q_lat_norm, kv_lat_norm], [(lat.shape, BF16, 0)], [(1, QL), (1, KL)])
    dhn = _mm_nt("mla_dhn", dlat, w["m_in"], F32, tk=lat.shape[1])[0]
    dw_min = _mm_tn("mla_dwin", hn, dlat, BF16, tn=lat.shape[1])
    dh_in, dh_in_b, dgain = _norm_bwd("mla_dnorm", h, gain, dhn, dh)
    d_gq = jnp.concatenate([dgq0, dgq1], axis=1)[:, :QK_DIM]
    d_gk = jnp.concatenate([dgk0, dgk1], axis=1)[:, :QK_DIM]
    return dh_in, dh_in_b, dgain, d_qln, d_kvln, d_gq, d_gk, dw_min, dwuq, dwukv, dwo


def _conv_rows(T):
    return _tile(T, 128, 8)


def _dwconv_fwd(u, w_dw, b_dw):
    T, C = u.shape
    tc, R = _tile(C, 256), _conv_rows(T)
    off = CONV_PAD - (CONV_WIDTH - 1)

    def body(u_ref, w_ref, b_ref, y_ref, pad_ref):
        pad_ref[0:CONV_PAD, :] = jnp.zeros((CONV_PAD, tc), F32)
        pad_ref[CONV_PAD:CONV_PAD + T, :] = u_ref[...]
        for r in range(T // R):
            acc = jnp.broadcast_to(b_ref[...], (R, tc))
            for j in range(CONV_WIDTH):
                acc = acc + w_ref[j:j + 1, :] * pad_ref[r * R + off + j:r * R + off + j + R, :]
            y_ref[r * R:(r + 1) * R, :] = acc

    return _ordered_call(
        body, [u, w_dw, b_dw], name="conv_dw", grid=(C // tc,),
        in_specs=[pl.BlockSpec((T, tc), lambda c: (0, c)), pl.BlockSpec((32, tc), lambda c: (0, c)),
                  pl.BlockSpec((1, tc), lambda c: (0, c))],
        out_specs=pl.BlockSpec((T, tc), lambda c: (0, c)),
        out_shape=_sds((T, C), F32),
        scratch_shapes=[pltpu.VMEM((T + CONV_PAD, tc), F32)],
        compiler_params=_params(),
    )


def _dwconv_bwd(u, w_dw, dy):
    T, C = u.shape
    tc, R = _tile(C, 256), _conv_rows(T)
    off = CONV_PAD - (CONV_WIDTH - 1)

    def body(u_ref, w_ref, dy_ref, du_ref, dw_ref, db_ref, upad_ref, dpad_ref):
        upad_ref[0:CONV_PAD, :] = jnp.zeros((CONV_PAD, tc), F32)
        upad_ref[CONV_PAD:CONV_PAD + T, :] = u_ref[...]
        dpad_ref[0:T, :] = dy_ref[...]
        dpad_ref[T:T + CONV_PAD, :] = jnp.zeros((CONV_PAD, tc), F32)
        for r in range(T // R):
            acc = jnp.zeros((R, tc), F32)
            for j in range(CONV_WIDTH):
                s = r * R + (CONV_WIDTH - 1) - j
                acc = acc + w_ref[j:j + 1, :] * dpad_ref[s:s + R, :]
            du_ref[r * R:(r + 1) * R, :] = acc
        for j in range(CONV_WIDTH):
            acc = jnp.zeros((R, tc), F32)
            for r in range(T // R):
                acc = acc + dy_ref[r * R:(r + 1) * R, :] * upad_ref[r * R + off + j:r * R + off + j + R, :]
            dw_ref[j:j + 1, :] = jnp.sum(acc, axis=0, keepdims=True)
        dw_ref[CONV_WIDTH:32, :] = jnp.zeros((32 - CONV_WIDTH, tc), F32)
        db_ref[...] = jnp.sum(dy_ref[...], axis=0, keepdims=True)

    return _ordered_call(
        body, [u, w_dw, dy], name="conv_ddw", grid=(C // tc,),
        in_specs=[pl.BlockSpec((T, tc), lambda c: (0, c)), pl.BlockSpec((32, tc), lambda c: (0, c)),
                  pl.BlockSpec((T, tc), lambda c: (0, c))],
        out_specs=[pl.BlockSpec((T, tc), lambda c: (0, c)), pl.BlockSpec((32, tc), lambda c: (0, c)),
                   pl.BlockSpec((1, tc), lambda c: (0, c))],
        out_shape=[_sds((T, C), F32), _sds((32, C), F32), _sds((1, C), F32)],
        scratch_shapes=[pltpu.VMEM((T + CONV_PAD, tc), F32), pltpu.VMEM((T + CONV_PAD, tc), F32)],
        compiler_params=_params(),
    )


def _ln_silu(y, g, b):
    mu = jnp.mean(y, axis=-1, keepdims=True)
    yc = y - mu
    z = yc * lax.rsqrt(jnp.mean(yc * yc, axis=-1, keepdims=True) + EPS) * g + b
    return z * jax.nn.sigmoid(z)


def _conv_fwd(h, gain, get):
    T, D = h.shape
    hn = _norm_fwd("conv_norm", h, gain)
    w = dict(pw1=get["pw1"]())
    b_pw1, w_dw, b_dw, ln_g, ln_b = get["small"]()
    S, _, Ns = w["pw1"].shape
    half = S // 2
    C = half * Ns
    tm, tk = _tile(T, 512, 16), _tile(D, 2048)

    def ep(accs, ex, outs):
        a = accs[0] + ex[0][...]
        g = accs[1] + ex[1][...]
        outs[0][0] = a.astype(BF16)
        outs[0][1] = g.astype(BF16)
        outs[1][...] = a * jax.nn.sigmoid(g)

    ag, u = _matmul(
        "conv_pw1", (half, T // tm, D // tk),
        [(hn, pl.BlockSpec((tm, tk), lambda j, i, k: (i, k))),
         (w["pw1"], pl.BlockSpec((None, tk, Ns), lambda j, i, k: (j, k, 0))),
         (w["pw1"], pl.BlockSpec((None, tk, Ns), lambda j, i, k: (j + half, k, 0)))],
        [(0, 0, 1), (1, 0, 2)], NN, [(tm, Ns), (tm, Ns)],
        [(_sds((2, T, C), BF16), pl.BlockSpec((2, tm, Ns), lambda j, i, k: (0, i, j))),
         (_sds((T, C), F32), pl.BlockSpec((tm, Ns), lambda j, i, k: (i, j)))],
        ep,
        extras=[(b_pw1, pl.BlockSpec((None, 1, Ns), lambda j, i, k: (j, 0, 0))),
                (b_pw1, pl.BlockSpec((None, 1, Ns), lambda j, i, k: (j + half, 0, 0)))])
    y = _dwconv_fwd(u, w_dw, b_dw)

    def fn(ins, outs, acc):
        outs[0][...] = _ln_silu(ins[0][...], ins[1][...], ins[2][...]).astype(BF16)

    s = _rowwise("conv_ln", fn, T, 256, [(y, 0)], [ln_g, ln_b], [((T, C), BF16, 0)])[0]

    w["pw2"] = get["pw2"]()
    h_new = _mm_nn("conv_pw2", s, w["pw2"], F32, res=h)[0]
    return h_new, (h, hn, ag, u, y, s)


def _conv_bwd(saved, gain, w, w_dw, ln_g, ln_b, dh, dhb):
    h, hn, ag, u, y, s = saved
    T, D = h.shape
    S, _, Ns = w["pw1"].shape
    half = S // 2
    C = half * Ns
    ds = _mm_nt("conv_ds", dhb, w["pw2"], F32)[0]
    dw_pw2 = _mm_tn("conv_dwpw2", s, dhb, BF16)

    def fn(ins, outs, acc):
        _, vjp = jax.vjp(_ln_silu, ins[0][...], ins[2][...], ins[3][...])
        dy, dg, db = vjp(ins[1][...])
        outs[0][...] = dy
        acc(0, dg)
        acc(1, db)

    dy, d_ln_g, d_ln_b = _rowwise("conv_dln", fn, T, 256, [(y, 0), (ds, 0)], [ln_g, ln_b],
                                  [((T, C), F32, 0)], [(1, C), (1, C)])
    du, d_w_dw, d_b_dw = _dwconv_bwd(u, w_dw, dy)

    def fn2(ins, outs, acc):
        a = ins[0][0].astype(F32)
        g = ins[0][1].astype(F32)
        du_ = ins[1][...]
        sg = jax.nn.sigmoid(g)
        da = du_ * sg
        dg = du_ * a * sg * (1.0 - sg)
        outs[0][0] = da.astype(BF16)
        outs[0][1] = dg.astype(BF16)
        acc(0, jnp.sum(da, axis=0, keepdims=True))
        acc(1, jnp.sum(dg, axis=0, keepdims=True))

    dag, d_b_a, d_b_g = _rowwise("conv_dglu", fn2, T, 256, [(ag, 1), (du, 0)], [],
                                 [((2, T, C), BF16, 1)], [(1, C), (1, C)])
    tm, tn = _tile(T, 1024, 16), _tile(D, 1024)
    dhn = _matmul(
        "conv_dhn", (T // tm, D // tn, half),
        [(dag, pl.BlockSpec((None, tm, Ns), lambda i, n, j: (0, i, j))),
         (dag, pl.BlockSpec((None, tm, Ns), lambda i, n, j: (1, i, j))),
         (w["pw1"], pl.BlockSpec((None, tn, Ns), lambda i, n, j: (j, n, 0))),
         (w["pw1"], pl.BlockSpec((None, tn, Ns), lambda i, n, j: (j + half, n, 0)))],
        [(0, 0, 2), (0, 1, 3)], NT, [(tm, tn)],
        [(_sds((T, D), F32), pl.BlockSpec((tm, tn), lambda i, n, j: (i, n)))], _store())[0]
    tkd, tt = _tile(D, 1024), _tile(T, 2048)
    dw_pw1 = _matmul(
        "conv_dwpw1", (S, D // tkd, T // tt),
        [(hn, pl.BlockSpec((tt, tkd), lambda j, i, k: (k, i))),
         (dag, pl.BlockSpec((None, tt, Ns), lambda j, i, k: (j // half, k, j % half)))],
        [(0, 0, 1)], TN, [(tkd, Ns)],
        [(_sds((S, D, Ns), BF16), pl.BlockSpec((None, tkd, Ns), lambda j, i, k: (j, i, 0)))], _store())[0]
    dh_in, dh_in_b, dgain = _norm_bwd("conv_dnorm", h, gain, dhn, dh)
    d_b_pw1 = jnp.concatenate([d_b_a, d_b_g], axis=1)
    return dh_in, dh_in_b, dgain, dw_pw1, d_b_pw1, d_w_dw, d_b_dw, d_ln_g, d_ln_b, dw_pw2


def _loss_head(y, target):
    T, D = y.shape

    def fn(ins, outs, acc):
        e = ins[0][...] - ins[1][...]
        d = e * (1.0 / D)
        outs[0][...] = d
        outs[1][...] = d.astype(BF16)
        part = jnp.sum(jnp.sum(e * e, axis=-1, keepdims=True), axis=0, keepdims=True) * (0.5 / D)
        acc(0, jnp.broadcast_to(part, (1, 128)))

    return _rowwise("loss_head", fn, T, 256, [(y, 0), (target, 0)], [], [((T, D), F32, 0), ((T, D), BF16, 0)],
                    [(1, 128)])


def _place():
    return lax.axis_index("x"), lax.axis_index("y"), lax.axis_index("c")


def _peer(j):
    x, y, c = _place()
    return (1 - x if j & 4 else x, 1 - y if j & 2 else y, 1 - c if j & 1 else c)


def _index(place):
    return 4 * place[0] + 2 * place[1] + place[2]


HBM = pl.BlockSpec(memory_space=pltpu.HBM)
SEM = pl.BlockSpec(memory_space=pltpu.SEMAPHORE)
EFFECT = pltpu.SideEffectType.DATAFLOW_SIDE_EFFECTING


def _chip(j):
    x, y, _ = _place()
    return (1 - x if j & 2 else x, 1 - y if j & 1 else y)


def _chip_index(chip):
    return 2 * chip[0] + chip[1]


def _remote(src, dst, send, recv, k, device):
    return pltpu.make_async_remote_copy(src_ref=src, dst_ref=dst, send_sem=send.at[k], recv_sem=recv.at[k],
                                        device_id=device, device_id_type=MESH)


def _hbm(arrays):
    return [pltpu.with_memory_space_constraint(a, pltpu.HBM) for a in arrays]


def _split_call(name, body, ins, sems_in, sems_out):
    n, ns_in, ns_out = len(ins), len(sems_in), len(sems_out)
    if any(a is _LAST[0] for a in ins):
        _LAST[0] = None

    def kernel_body(*refs):
        in_refs = refs[:n]
        si = refs[n:n + ns_in]
        so = refs[n + ns_in:n + ns_in + ns_out]
        tok = refs[-1]
        body(in_refs, si, so, tok)
        tok[...] = jnp.zeros_like(tok)

    res = _ordered_call(
        kernel_body, _hbm(ins) + list(sems_in), chain_out=ns_out + n, name=name,
        in_specs=[HBM] * n + [SEM] * ns_in,
        out_specs=[SEM] * ns_out + [HBM] * n + [pl.BlockSpec(memory_space=pltpu.VMEM)],
        out_shape=[pltpu.SemaphoreType.DMA((s,)) for s in sems_out] + [pltpu.HBM(a.shape, a.dtype) for a in ins]
        + [_sds((8, 128), F32)],
        input_output_aliases={i: ns_out + i for i in range(n)},
        compiler_params=pltpu.CompilerParams(has_side_effects=EFFECT),
    )
    sems = list(res[:ns_out])
    arrays = list(res[ns_out:ns_out + n])
    return sems, arrays, res[-1]


def _ag_start(name, groups):
    flat = [s for grp in groups for s in grp]
    zones = [lax.empty((N_DEV,) + s.shape, s.dtype) for s in flat]
    n = len(flat)
    sizes = []
    for grp in groups:
        sizes += [7 * len(grp), 7 * len(grp), len(grp)]

    def body(ins, si, so, tok):
        x, y, c = _place()
        me = _index((x, y, c))
        base = 0
        for gi, grp in enumerate(groups):
            send, recv, local = so[3 * gi:3 * gi + 3]
            for w in range(len(grp)):
                src, dst = ins[base + w], ins[n + base + w].at[me]
                pltpu.make_async_copy(src, dst, local.at[w]).start()
                _remote(src, dst, send, recv, 7 * w, (x, y, 1 - c)).start()
                for j in (1, 2, 3):
                    _remote(src, dst, send, recv, 7 * w + j, (*_chip(j), c)).start()
            base += len(grp)

    sems, arrays, token = _split_call(name, body, flat + zones, [], sizes)
    out, base = [], 0
    for gi, grp in enumerate(groups):
        k = len(grp)
        out.append((sems[3 * gi:3 * gi + 3], arrays[base:base + k], arrays[n + base:n + base + k]))
        base += k
    return out, token


def _ag_forward(name, handle):
    sems, shards, zones = handle
    k = len(shards)

    def arrive(ins, si, so, tok):
        send, recv, _ = si
        _, _, c = _place()
        for j in (1, 2, 3):
            for w in range(k):
                blk = ins[k + w].at[_index((*_chip(j), c))]
                _remote(ins[w], blk, send, recv, 7 * w + j, (*_chip(j), c)).wait_recv()

    _, arrays, _ = _split_call(name + "_arrive", arrive, list(shards) + list(zones), sems, [])

    def pass_on(ins, si, so, tok):
        fsend, frecv = so
        x, y, c = _place()
        for j in (1, 2, 3):
            for w in range(k):
                blk = ins[w].at[_index((*_chip(j), c))]
                _remote(blk, blk, fsend, frecv, 3 * w + j - 1, (x, y, 1 - c)).start()

    fsems, zones2, _ = _split_call(name + "_pass", pass_on, arrays[k:], [], [3 * k, 3 * k])
    return (list(sems) + fsems, arrays[:k], zones2)


def _ag_wait(name, handle):
    sems, shards, zones = handle
    k = len(shards)

    def body(ins, si, so, tok):
        send, recv, local, fsend, frecv = si
        x, y, c = _place()
        sib = (x, y, 1 - c)
        for w in range(k):
            zone = ins[k + w]
            _remote(ins[w], zone.at[_index(sib)], send, recv, 7 * w, sib).wait_recv()
            for j in (1, 2, 3):
                blk = zone.at[_index((*_chip(j), 1 - c))]
                _remote(blk, blk, fsend, frecv, 3 * w + j - 1, sib).wait_recv()
        for w in range(k):
            zone = ins[k + w]
            mine = zone.at[_index((x, y, c))]
            _remote(ins[w], mine, send, recv, 7 * w, sib).wait_send()
            for j in (1, 2, 3):
                _remote(ins[w], mine, send, recv, 7 * w + j, (*_chip(j), c)).wait_send()
                blk = zone.at[_index((*_chip(j), c))]
                _remote(blk, blk, fsend, frecv, 3 * w + j - 1, sib).wait_send()
            pltpu.make_async_copy(ins[w], mine, local.at[w]).wait()

    _, arrays, _ = _split_call(name, body, list(shards) + list(zones), sems, [])
    return arrays[k:]


def _rs_pair_start(name, grads):
    n = len(grads)
    zones = [lax.empty((4,) + g.shape[1:], g.dtype) for g in grads]

    def body(ins, si, so, tok):
        send, recv = so
        x, y, c = _place()
        for w in range(n):
            for q in range(4):
                _remote(ins[w].at[2 * q + 1 - c], ins[n + w].at[q], send, recv, 4 * w + q, (x, y, 1 - c)).start()

    sems, arrays, token = _split_call(name, body, list(grads) + zones, [], [4 * n, 4 * n])
    return (sems, arrays[:n], arrays[n:]), token


def _rs_pair_wait(name, handle):
    sems, grads, zones = handle
    n = len(grads)

    def body(ins, si, so, tok):
        send, recv = si
        x, y, c = _place()
        for w in range(n):
            for q in range(4):
                cp = _remote(ins[w].at[2 * q + 1 - c], ins[n + w].at[q], send, recv, 4 * w + q, (x, y, 1 - c))
                cp.wait_recv()
                cp.wait_send()

    _, arrays, _ = _split_call(name, body, list(grads) + list(zones), sems, [])
    return arrays[:n], arrays[n:]


def _pair_sum(name, g, got, core):
    _, R, C = g.shape
    g4 = g.reshape(4, 2, R, C)
    tr = _tile(R, 1024, 16)

    def body(c_ref, g_ref, a_ref, *rest):
        o_ref = rest[-1]
        o_ref[...] = (g_ref[...].astype(F32) + a_ref[...].astype(F32)).astype(o_ref.dtype)

    prev = [] if _LAST[0] is None or _LAST[0] is g or _LAST[0] is got else [_LAST[0]]
    out = pl.pallas_call(
        body, name=name,
        grid_spec=pltpu.PrefetchScalarGridSpec(
            num_scalar_prefetch=1, grid=(4, R // tr),
            in_specs=[pl.BlockSpec((None, None, tr, C), lambda q, i, c_ref: (q, c_ref[0], i, 0)),
                      pl.BlockSpec((None, tr, C), lambda q, i, c_ref: (q, i, 0))] + [ANY] * len(prev),
            out_specs=pl.BlockSpec((None, tr, C), lambda q, i, c_ref: (q, i, 0))),
        out_shape=_sds((4, R, C), g.dtype),
        compiler_params=_params(),
    )(core, g4, got, *prev)
    _LAST[0] = out
    return out


def _rs_chip_start(name, sums):
    n = len(sums)
    zones = [lax.empty(s.shape, s.dtype) for s in sums]

    def body(ins, si, so, tok):
        send, recv, local = so
        x, y, c = _place()
        mine = _chip_index((x, y))
        for w in range(n):
            pltpu.make_async_copy(ins[w].at[mine], ins[n + w].at[mine], local.at[w]).start()
            for j in (1, 2, 3):
                _remote(ins[w].at[_chip_index(_chip(j))], ins[n + w].at[mine], send, recv, 3 * w + j - 1,
                        (*_chip(j), c)).start()

    sems, arrays, token = _split_call(name, body, list(sums) + zones, [], [3 * n, 3 * n, n])
    return (sems, arrays[:n], arrays[n:]), token


def _rs_chip_wait(name, handle):
    sems, sums, zones = handle
    n = len(sums)

    def body(ins, si, so, tok):
        send, recv, local = si
        x, y, c = _place()
        mine = _chip_index((x, y))
        for w in range(n):
            for j in (1, 2, 3):
                _remote(ins[w].at[mine], ins[n + w].at[_chip_index(_chip(j))], send, recv, 3 * w + j - 1,
                        (*_chip(j), c)).wait_recv()
        for w in range(n):
            for j in (1, 2, 3):
                _remote(ins[w].at[_chip_index(_chip(j))], ins[n + w].at[mine], send, recv, 3 * w + j - 1,
                        (*_chip(j), c)).wait_send()
            pltpu.make_async_copy(ins[w].at[mine], ins[n + w].at[mine], local.at[w]).wait()

    _, arrays, _ = _split_call(name, body, list(sums) + list(zones), sems, [])
    return arrays[n:]


def _chip_sum(name, slots, layer, layers, into=None):
    _, R, C = slots.shape
    tr = _tile(R, 256, 16)

    def body(*refs):
        s_ref, o_ref = refs[0], refs[-1]
        total = s_ref[0].astype(F32)
        for k in range(1, 4):
            total = total + s_ref[k].astype(F32)
        o_ref[...] = total

    extra = [] if into is None else [into]
    return _ordered_call(
        body, [slots] + extra, name=name, grid=(R // tr,),
        in_specs=[pl.BlockSpec((4, tr, C), lambda i: (0, i, 0))] + [ANY] * len(extra),
        out_specs=pl.BlockSpec((None, tr, C), lambda i: (layer, i, 0)),
        out_shape=_sds((layers, R, C), F32),
        input_output_aliases={} if into is None else {1: 0},
        compiler_params=_params(),
    )


def _pack_rows(name, parts, rows):
    C = parts[0].shape[1]

    def body(*refs):
        o_ref = refs[-1]
        off = 0
        for r in refs[:-1]:
            o_ref[off:off + r.shape[0], :] = r[...]
            off += r.shape[0]
        if off < rows:
            o_ref[off:rows, :] = jnp.zeros((rows - off, C), F32)

    vmem = pl.BlockSpec(memory_space=pltpu.VMEM)
    return _ordered_call(body, list(parts), name=name, in_specs=[vmem] * len(parts), out_specs=vmem,
                         out_shape=_sds((rows, C), F32))


def _reduce_small(rep, shd):
    R, C = rep.shape
    _, Rs, Cs = shd.shape

    def body(r_ref, s_ref, or_ref, os_ref, all_r, all_s, send, recv):
        me = _index(_place())
        all_r[me] = r_ref[...]
        all_s[me] = s_ref[me]
        copies = []
        for j in range(1, N_DEV):
            there = _index(_peer(j))
            copies.append((
                pltpu.make_async_remote_copy(src_ref=r_ref, dst_ref=all_r.at[me], send_sem=send.at[2 * j - 2],
                                             recv_sem=recv.at[2 * j - 2], device_id=_peer(j), device_id_type=MESH),
                pltpu.make_async_remote_copy(src_ref=s_ref.at[there], dst_ref=all_s.at[me],
                                             send_sem=send.at[2 * j - 1], recv_sem=recv.at[2 * j - 1],
                                             device_id=_peer(j), device_id_type=MESH)))
        for a, b in copies:
            a.start()
            b.start()
        for j in range(1, N_DEV):
            there = _index(_peer(j))
            pltpu.make_async_remote_copy(src_ref=r_ref, dst_ref=all_r.at[there], send_sem=send.at[2 * j - 2],
                                         recv_sem=recv.at[2 * j - 2], device_id=_peer(j),
                                         device_id_type=MESH).wait_recv()
            pltpu.make_async_remote_copy(src_ref=s_ref.at[me], dst_ref=all_s.at[there], send_sem=send.at[2 * j - 1],
                                         recv_sem=recv.at[2 * j - 1], device_id=_peer(j),
                                         device_id_type=MESH).wait_recv()
        for a, b in copies:
            a.wait_send()
            b.wait_send()
        tot_r, tot_s = all_r[0], all_s[0]
        for k in range(1, N_DEV):
            tot_r = tot_r + all_r[k]
            tot_s = tot_s + all_s[k]
        or_ref[...] = tot_r
        os_ref[...] = tot_s

    vmem = pl.BlockSpec(memory_space=pltpu.VMEM)
    return _ordered_call(
        body, [rep, shd], name="reduce_small",
        in_specs=[vmem, vmem], out_specs=[vmem, vmem],
        out_shape=[_sds((R, C), F32), _sds((Rs, Cs), F32)],
        scratch_shapes=[pltpu.VMEM((N_DEV, R, C), F32), pltpu.VMEM((N_DEV, Rs, Cs), F32),
                        pltpu.SemaphoreType.DMA((2 * N_DEV - 2,)), pltpu.SemaphoreType.DMA((2 * N_DEV - 2,))],
        compiler_params=_params(),
    )


def _cast_layer(name, w, layer):
    _, R, C = w.shape
    tr = _tile(R, 1024, 16)

    def body(w_ref, o_ref):
        o_ref[...] = w_ref[...].astype(BF16)

    return _ordered_call(
        body, [w], name=name, grid=(R // tr,),
        in_specs=[pl.BlockSpec((None, tr, C), lambda i: (layer, i, 0))],
        out_specs=pl.BlockSpec((tr, C), lambda i: (i, 0)),
        out_shape=_sds((R, C), BF16),
        compiler_params=_params(),
    )


def _adam_math(w, g, m, v):
    c1 = 1.0 / (1.0 - ADAM_B1 ** ADAM_STEP)
    c2 = 1.0 / (1.0 - ADAM_B2 ** ADAM_STEP)
    nm = ADAM_B1 * m + (1.0 - ADAM_B1) * g
    nv = ADAM_B2 * v + (1.0 - ADAM_B2) * (g * g)
    return -ADAM_LR * ((nm * c1) / (jnp.sqrt(nv * c2) + ADAM_EPS) + ADAM_WD * w), nm, nv


def _sum_adam(name, slots, w, m, v, layer, into=None):
    L, R, C = w.shape
    tr = _tile(R, 256, 16)

    def body(*refs):
        s_ref, w_ref, m_ref, v_ref = refs[:4]
        g_ref, d_ref, nm_ref, nv_ref = refs[-4:]
        g = s_ref[0].astype(F32)
        for k in range(1, 4):
            g = g + s_ref[k].astype(F32)
        g_ref[...] = g
        d_ref[...], nm_ref[...], nv_ref[...] = _adam_math(w_ref[...], g, m_ref[...], v_ref[...])

    spec = pl.BlockSpec((None, tr, C), lambda i: (layer, i, 0))
    extra = [] if into is None else list(into)
    return _ordered_call(
        body, [slots, w, m, v] + extra, name=name, grid=(R // tr,),
        in_specs=[pl.BlockSpec((4, tr, C), lambda i: (0, i, 0)), spec, spec, spec] + [ANY] * len(extra),
        out_specs=[spec] * 4,
        out_shape=[_sds((L, R, C), F32)] * 4,
        input_output_aliases={4 + k: k for k in range(len(extra))},
        compiler_params=_params(),
    )


def _adamw(name, w, g, m, v):
    shape = w.shape
    R, C = shape[-2], shape[-1]
    L = 1
    for s in shape[:-2]:
        L *= s
    w3, g3, m3, v3 = (a.reshape(L, R, C) for a in (w, g, m, v))
    tr = _tile(R, 256, 8)

    def body(w_ref, g_ref, m_ref, v_ref, d_ref, nm_ref, nv_ref):
        d_ref[...], nm_ref[...], nv_ref[...] = _adam_math(w_ref[...], g_ref[...], m_ref[...], v_ref[...])

    spec = pl.BlockSpec((None, tr, C), lambda l, i: (l, i, 0))
    outs = _ordered_call(
        body, [w3, g3, m3, v3], name=name, grid=(L, R // tr),
        in_specs=[spec] * 4, out_specs=[spec] * 3,
        out_shape=[_sds((L, R, C), F32)] * 3,
        compiler_params=_params(),
    )
    return tuple(o.reshape(shape) for o in outs)


def _pad_rows(a, rows):
    return jnp.pad(a, ((0, rows - a.shape[0]), (0, 0)))


def _pad_cols(a, cols):
    return jnp.pad(a, ((0, 0), (0, cols - a.shape[1])))


def _rope_tables(positions):
    q = D_ROPE // 2
    inv_freq = ROPE_THETA ** (-jnp.arange(0, D_ROPE, 2, dtype=F32) / D_ROPE)
    ang = positions.astype(F32)[:, None] * inv_freq
    cos, sin = jnp.cos(ang), jnp.sin(ang)
    z = jnp.zeros_like(cos)
    zz = jnp.zeros((cos.shape[0], 128 - 2 * q), F32)
    c = jnp.concatenate([cos, cos, zz], axis=1)
    s1 = jnp.concatenate([z, sin, zz], axis=1)
    s2 = jnp.concatenate([-sin, z, zz], axis=1)
    return c, s1, s2


def kernel(x, p, positions, ffn_a_norm, ffn_a_w_in, ffn_a_w_out, ffn_b_norm, ffn_b_w_in, ffn_b_w_out, mix_norm, mla_w_in, mla_q_lat_norm, mla_kv_lat_norm, mla_w_uq, mla_w_ukv, mla_q_gain, mla_k_gain, mla_w_o, conv_w_pw1, conv_b_pw1, conv_w_dw, conv_b_dw, conv_ln_g, conv_ln_b, conv_w_pw2, ple_w_proj, ple_norm, ple_gate_norm, ple_w_gate, loss_target, m_ffn_a_norm, m_ffn_a_w_in, m_ffn_a_w_out, m_ffn_b_norm, m_ffn_b_w_in, m_ffn_b_w_out, m_mix_norm, m_mla_w_in, m_mla_q_lat_norm, m_mla_kv_lat_norm, m_mla_w_uq, m_mla_w_ukv, m_mla_q_gain, m_mla_k_gain, m_mla_w_o, m_conv_w_pw1, m_conv_b_pw1, m_conv_w_dw, m_conv_b_dw, m_conv_ln_g, m_conv_ln_b, m_conv_w_pw2, m_ple_w_proj, m_ple_norm, m_ple_gate_norm, m_ple_w_gate, v_ffn_a_norm, v_ffn_a_w_in, v_ffn_a_w_out, v_ffn_b_norm, v_ffn_b_w_in, v_ffn_b_w_out, v_mix_norm, v_mla_w_in, v_mla_q_lat_norm, v_mla_kv_lat_norm, v_mla_w_uq, v_mla_w_ukv, v_mla_q_gain, v_mla_k_gain, v_mla_w_o, v_conv_w_pw1, v_conv_b_pw1, v_conv_w_dw, v_conv_b_dw, v_conv_ln_g, v_conv_ln_b, v_conv_w_pw2, v_ple_w_proj, v_ple_norm, v_ple_gate_norm, v_ple_w_gate):
    weights = dict(ffn_a_norm=ffn_a_norm, ffn_a_w_in=ffn_a_w_in, ffn_a_w_out=ffn_a_w_out, ffn_b_norm=ffn_b_norm,
                   ffn_b_w_in=ffn_b_w_in, ffn_b_w_out=ffn_b_w_out, mix_norm=mix_norm, mla_w_in=mla_w_in,
                   mla_q_lat_norm=mla_q_lat_norm, mla_kv_lat_norm=mla_kv_lat_norm, mla_w_uq=mla_w_uq,
                   mla_w_ukv=mla_w_ukv, mla_q_gain=mla_q_gain, mla_k_gain=mla_k_gain, mla_w_o=mla_w_o,
                   conv_w_pw1=conv_w_pw1, conv_b_pw1=conv_b_pw1, conv_w_dw=conv_w_dw, conv_b_dw=conv_b_dw,
                   conv_ln_g=conv_ln_g, conv_ln_b=conv_ln_b, conv_w_pw2=conv_w_pw2, ple_w_proj=ple_w_proj,
                   ple_norm=ple_norm, ple_gate_norm=ple_gate_norm, ple_w_gate=ple_w_gate)
    moments_m = dict(ffn_a_norm=m_ffn_a_norm, ffn_a_w_in=m_ffn_a_w_in, ffn_a_w_out=m_ffn_a_w_out,
                     ffn_b_norm=m_ffn_b_norm, ffn_b_w_in=m_ffn_b_w_in, ffn_b_w_out=m_ffn_b_w_out,
                     mix_norm=m_mix_norm, mla_w_in=m_mla_w_in, mla_q_lat_norm=m_mla_q_lat_norm,
                     mla_kv_lat_norm=m_mla_kv_lat_norm, mla_w_uq=m_mla_w_uq, mla_w_ukv=m_mla_w_ukv,
                     mla_q_gain=m_mla_q_gain, mla_k_gain=m_mla_k_gain, mla_w_o=m_mla_w_o,
                     conv_w_pw1=m_conv_w_pw1, conv_b_pw1=m_conv_b_pw1, conv_w_dw=m_conv_w_dw,
                     conv_b_dw=m_conv_b_dw, conv_ln_g=m_conv_ln_g, conv_ln_b=m_conv_ln_b, conv_w_pw2=m_conv_w_pw2,
                     ple_w_proj=m_ple_w_proj, ple_norm=m_ple_norm, ple_gate_norm=m_ple_gate_norm,
                     ple_w_gate=m_ple_w_gate)
    moments_v = dict(ffn_a_norm=v_ffn_a_norm, ffn_a_w_in=v_ffn_a_w_in, ffn_a_w_out=v_ffn_a_w_out,
                     ffn_b_norm=v_ffn_b_norm, ffn_b_w_in=v_ffn_b_w_in, ffn_b_w_out=v_ffn_b_w_out,
                     mix_norm=v_mix_norm, mla_w_in=v_mla_w_in, mla_q_lat_norm=v_mla_q_lat_norm,
                     mla_kv_lat_norm=v_mla_kv_lat_norm, mla_w_uq=v_mla_w_uq, mla_w_ukv=v_mla_w_ukv,
                     mla_q_gain=v_mla_q_gain, mla_k_gain=v_mla_k_gain, mla_w_o=v_mla_w_o,
                     conv_w_pw1=v_conv_w_pw1, conv_b_pw1=v_conv_b_pw1, conv_w_dw=v_conv_w_dw,
                     conv_b_dw=v_conv_b_dw, conv_ln_g=v_conv_ln_g, conv_ln_b=v_conv_ln_b, conv_w_pw2=v_conv_w_pw2,
                     ple_w_proj=v_ple_w_proj, ple_norm=v_ple_norm, ple_gate_norm=v_ple_gate_norm,
                     ple_w_gate=v_ple_w_gate)
    order = list(weights.keys())
    _LAST[0] = None

    T, D = x.shape[1], x.shape[2]
    me = _index(_place())
    h0 = x[0]
    target = loss_target[0]
    tabs = _rope_tables(positions[0])
    H = N_HEADS
    hps = H // N_DEV
    QL = mla_q_lat_norm.shape[1]
    Cs = conv_b_dw.shape[1]

    def cast(n, i):
        return _cast_layer(f"cast_{n}{i}", weights[n], i)

    first, tok = _ag_start("ag_start0", [[cast("ffn_a_w_in", 0)]])
    m_in_pad = _pad_cols(mla_w_in[0], mla_w_in.shape[2] - D_ROPE + 128)[None]
    uq_pad = jnp.pad(mla_w_uq[0].reshape(QL, hps, QK_DIM), ((0, 0), (0, 0), (0, HEAD_PAD - QK_DIM)))
    uq_pad = uq_pad.reshape(1, QL, hps * HEAD_PAD)
    conv_small = jnp.concatenate([
        _pad_rows(_pad_cols(conv_b_pw1, 2 * Cs), 8),
        _pad_rows(_pad_cols(conv_w_dw[0], 2 * Cs), 32),
        _pad_rows(_pad_cols(jnp.concatenate([conv_b_dw, conv_ln_g, conv_ln_b], axis=0), 2 * Cs), 8)], axis=0)
    rest, tok = _ag_start("ag_start1", [
        [cast("ffn_a_w_out", 0), _cast_layer("cast_mla_in", m_in_pad, 0), _cast_layer("cast_mla_uq", uq_pad, 0),
         _cast_layer("cast_mla_ukv", mla_w_ukv, 0), _cast_layer("cast_mla_wo", mla_w_o, 0)],
        [cast("ffn_b_w_in", 0), cast("ffn_b_w_out", 0), cast("ple_w_gate", 0), cast("ple_w_proj", 0)],
        [cast("ffn_a_w_in", 1), cast("ffn_a_w_out", 1)],
        [_cast_layer("cast_conv_pw1", conv_w_pw1, 0), _cast_layer("cast_conv_pw2", conv_w_pw2, 0), conv_small],
        [cast("ffn_b_w_in", 1)], [cast("ffn_b_w_out", 1)], [cast("ple_w_gate", 1), cast("ple_w_proj", 1)]])
    groups = [dict(handle=hd, stage=0, arrays=None) for hd in first + rest]

    def prefetch(gi):
        st = groups[gi]
        if st["stage"] == 0:
            st["handle"] = _ag_forward(f"ag{gi}_forward", st["handle"])
            st["stage"] = 1

    def fetch(gi):
        prefetch(gi)
        st = groups[gi]
        if st["stage"] == 1:
            st["arrays"] = _ag_wait(f"ag{gi}_wait", st["handle"])
            st["stage"] = 2
        return st["arrays"]

    def getter(gi, k, shape=None, ahead=None):
        def get():
            if ahead is not None:
                prefetch(ahead)
            a = fetch(gi)[k]
            return a if shape is None else a.reshape(shape)
        return get

    def conv_small_params():
        small = fetch(4)[2]
        return (small[:, 0:1, :],
                jnp.transpose(small[:, 8:40, :Cs], (1, 0, 2)).reshape(32, N_DEV * Cs),
                small[:, 40, :Cs].reshape(1, N_DEV * Cs), small[:, 41, :Cs].reshape(1, N_DEV * Cs),
                small[:, 42, :Cs].reshape(1, N_DEV * Cs))

    rows = (-1, D)
    get_ffn = [dict(a_in=getter(0, 0), a_out=getter(1, 0, rows), b_in=getter(2, 0), b_out=getter(2, 1, rows)),
               dict(a_in=getter(3, 0), a_out=getter(3, 1, rows, ahead=4), b_in=getter(5, 0),
                    b_out=getter(6, 0, rows))]
    get_ple = [dict(proj=getter(2, 3, ahead=3), gate=getter(2, 2, rows)),
               dict(proj=getter(7, 1), gate=getter(7, 0, rows))]
    get_mla = dict(m_in=getter(1, 1, (D, -1)), uq=getter(1, 2), ukv=getter(1, 3), wo=getter(1, 4, rows, ahead=2))
    get_conv = dict(pw1=getter(4, 0), pw2=getter(4, 1, rows, ahead=5), small=conv_small_params)
    gq_pad = _pad_cols(mla_q_gain, HEAD_PAD)
    gk_pad = _pad_cols(mla_k_gain, HEAD_PAD)
    prefetch(0)

    saved = []
    h = h0
    for i in range(2):
        h, s_a = _ffn_fwd(f"ffn_a{i}", h, ffn_a_norm[i:i + 1], get_ffn[i]["a_in"], get_ffn[i]["a_out"])
        if i == 0:
            h, s_m = _mla_fwd(h, mix_norm[0:1], tabs, get_mla, mla_q_lat_norm, mla_kv_lat_norm, gq_pad, gk_pad)
        else:
            h, s_m = _conv_fwd(h, mix_norm[1:2], get_conv)
        h, s_b = _ffn_fwd(f"ffn_b{i}", h, ffn_b_norm[i:i + 1], get_ffn[i]["b_in"], get_ffn[i]["b_out"])
        h, s_p = _ple_fwd(f"ple{i}", h, p[i, 0], get_ple[i]["proj"], ple_norm[i:i + 1], ple_gate_norm[i:i + 1],
                          get_ple[i]["gate"])
        saved.append((s_a, s_m, s_b, s_p))
    W = [dict(a_in=get_ffn[i]["a_in"](), a_out=get_ffn[i]["a_out"](), b_in=get_ffn[i]["b_in"](),
              b_out=get_ffn[i]["b_out"](), proj=get_ple[i]["proj"](), gate=get_ple[i]["gate"]())
         for i in range(2)]
    Wm = {n: g() for n, g in get_mla.items()}
    Wc = dict(pw1=get_conv["pw1"](), pw2=get_conv["pw2"]())
    _, w_dw_full, _, ln_g_full, ln_b_full = conv_small_params()

    dh, dhb, loss_row = _loss_head(h, target)

    G = {}
    small_g = {}
    stacked = ["ffn_a_w_in", "ffn_a_w_out", "ffn_b_w_in", "ffn_b_w_out", "ple_w_proj", "ple_w_gate"]
    row_sharded = {"ffn_a_w_out", "ffn_b_w_out", "ple_w_gate", "mla_w_in", "mla_w_o", "conv_w_pw2"}
    core = lax.axis_index("c").astype(jnp.int32).reshape(1)
    rs_groups = {
        "pb1": [("ple_w_proj", 1), ("ple_w_gate", 1), ("ffn_b_w_in", 1), ("ffn_b_w_out", 1)],
        "c1": [("conv_w_pw1", 0), ("conv_w_pw2", 0)],
        "a1": [("ffn_a_w_in", 1), ("ffn_a_w_out", 1)],
        "pb0": [("ple_w_proj", 0), ("ple_w_gate", 0), ("ffn_b_w_in", 0), ("ffn_b_w_out", 0)],
        "m0": [("mla_w_in", 0), ("mla_w_uq", 0), ("mla_w_ukv", 0), ("mla_w_o", 0)],
        "ao0": [("ffn_a_w_out", 0)],
        "ai0": [("ffn_a_w_in", 0)]}
    rs = {}

    def rs_begin(tag):
        grads = []
        for n, i in rs_groups[tag]:
            g = G[(n, i)]
            grads.append(g.reshape(N_DEV, g.shape[0] // N_DEV, g.shape[1]) if n in row_sharded else g)
        rs[tag] = _rs_pair_start(f"rs_{tag}_pair_start", grads)[0]

    def rs_pairs(tag):
        mine, got = _rs_pair_wait(f"rs_{tag}_pair_wait", rs[tag])
        rs[tag] = [_pair_sum(f"pairsum_{n}{i}", a, b, core) for (n, i), a, b in zip(rs_groups[tag], mine, got)]

    def rs_chips(tag):
        rs[tag] = _rs_chip_start(f"rs_{tag}_chip_start", rs[tag])[0]

    def rs_end(tag):
        return dict(zip(rs_groups[tag], _rs_chip_wait(f"rs_{tag}_chip_wait", rs[tag])))

    s_a, s_m, s_b, s_p = saved[1]
    dh, dhb, d_pn, d_gn, G[("ple_w_proj", 1)], G[("ple_w_gate", 1)] = _ple_bwd(
        "ple1", s_p, p[1, 0], W[1]["proj"], ple_norm[1:2], ple_gate_norm[1:2], W[1]["gate"], dh, dhb)
    small_g[("ple_norm", 1)], small_g[("ple_gate_norm", 1)] = d_pn, d_gn
    dh, dhb, small_g[("ffn_b_norm", 1)], G[("ffn_b_w_in", 1)], G[("ffn_b_w_out", 1)] = _ffn_bwd(
        "ffn_b1", s_b, ffn_b_norm[1:2], W[1]["b_in"], W[1]["b_out"], dh, dhb)
    rs_begin("pb1")
    (dh, dhb, small_g[("mix_norm", 1)], G[("conv_w_pw1", 0)], d_b_pw1, d_w_dw, d_b_dw, d_ln_g, d_ln_b,
     G[("conv_w_pw2", 0)]) = _conv_bwd(s_m, mix_norm[1:2], Wc, w_dw_full, ln_g_full, ln_b_full, dh, dhb)
    rs_pairs("pb1")
    rs_chips("pb1")
    rs_begin("c1")
    dh, dhb, small_g[("ffn_a_norm", 1)], G[("ffn_a_w_in", 1)], G[("ffn_a_w_out", 1)] = _ffn_bwd(
        "ffn_a1", s_a, ffn_a_norm[1:2], W[1]["a_in"], W[1]["a_out"], dh, dhb)
    rs_pairs("c1")
    rs_chips("c1")
    rs_begin("a1")

    s_a, s_m, s_b, s_p = saved[0]
    dh, dhb, d_pn, d_gn, G[("ple_w_proj", 0)], G[("ple_w_gate", 0)] = _ple_bwd(
        "ple0", s_p, p[0, 0], W[0]["proj"], ple_norm[0:1], ple_gate_norm[0:1], W[0]["gate"], dh, dhb)
    small_g[("ple_norm", 0)], small_g[("ple_gate_norm", 0)] = d_pn, d_gn
    rs_pairs("a1")
    rs_chips("a1")
    dh, dhb, small_g[("ffn_b_norm", 0)], G[("ffn_b_w_in", 0)], G[("ffn_b_w_out", 0)] = _ffn_bwd(
        "ffn_b0", s_b, ffn_b_norm[0:1], W[0]["b_in"], W[0]["b_out"], dh, dhb)
    rs_begin("pb0")

    def in_mla():
        rs_pairs("pb0")
        rs_chips("pb0")

    (dh, dhb, small_g[("mix_norm", 0)], d_qln, d_kvln, d_gq, d_gk,
     G[("mla_w_in", 0)], G[("mla_w_uq", 0)], G[("mla_w_ukv", 0)], G[("mla_w_o", 0)]) = _mla_bwd(
        s_m, mix_norm[0:1], tabs, Wm, mla_q_lat_norm, mla_kv_lat_norm, gq_pad, gk_pad, dh, dhb, hook=in_mla)
    rs_begin("m0")

    def with_dw_out(dw_out):
        G[("ffn_a_w_out", 0)] = dw_out
        rs_pairs("m0")
        rs_chips("m0")
        rs_begin("ao0")

    def with_dw_in(dw_in):
        G[("ffn_a_w_in", 0)] = dw_in
        rs_pairs("ao0")
        rs_chips("ao0")
        rs_begin("ai0")

    dh, dhb, small_g[("ffn_a_norm", 0)], _, _ = _ffn_bwd(
        "ffn_a0", s_a, ffn_a_norm[0:1], W[0]["a_in"], W[0]["a_out"], dh, dhb,
        hooks=(with_dw_out, with_dw_in, lambda: rs_pairs("ai0")))
    grad_x = dh[None]

    replicated = ["ffn_a_norm", "ffn_b_norm", "mix_norm", "ple_norm", "ple_gate_norm"]
    rep = _pack_rows("pack_small", [small_g[(n, i)] for n in replicated for i in (0, 1)]
                     + [_pad_cols(v, D) for v in (d_qln, d_kvln, d_gq, d_gk, loss_row)], 16)

    def dest_major(v, rows):
        r, w = v.shape[0], v.shape[1] // N_DEV
        v = jnp.transpose(v.reshape(r, N_DEV, w), (1, 0, 2))
        return jnp.pad(v, ((0, 0), (0, rows - r), (0, 2 * Cs - w)))

    shd = jnp.concatenate([dest_major(d_b_pw1, 8), dest_major(d_w_dw, 32),
                           dest_major(jnp.concatenate([d_b_dw, d_ln_g, d_ln_b], axis=0), 8)], axis=1)
    red, red_s = _reduce_small(rep, shd)
    loss = red[14, 0]
    small_grads = {n: red[2 * k:2 * k + 2] for k, n in enumerate(replicated)}
    small_grads.update(
        mla_q_lat_norm=red[10:11, :QL], mla_kv_lat_norm=red[11:12, :mla_kv_lat_norm.shape[1]],
        mla_q_gain=red[12:13, :QK_DIM], mla_k_gain=red[13:14, :QK_DIM],
        conv_b_pw1=red_s[0:1], conv_w_dw=red_s[8:8 + CONV_WIDTH, :Cs][None],
        conv_b_dw=red_s[40:41, :Cs], conv_ln_g=red_s[41:42, :Cs], conv_ln_b=red_s[42:43, :Cs])

    rs_chips("ai0")
    done = {}

    def plain_adamw(n, g):
        done[n] = (g,) + _adamw(f"adamw_{n}", weights[n], g, moments_m[n], moments_v[n])

    def slot_adamw(n, layer, slots):
        done[n] = tuple(_sum_adam(f"adamw_{n}{layer}", slots, weights[n], moments_m[n], moments_v[n], layer,
                                  into=done.get(n)))

    for n, g in small_grads.items():
        plain_adamw(n, g)
    for tag in ("pb1", "c1", "a1", "pb0"):
        slots = rs_end(tag)
        for n, i in rs_groups[tag]:
            slot_adamw(n, i, slots[(n, i)])
    slots = rs_end("m0")
    slot_adamw("mla_w_ukv", 0, slots[("mla_w_ukv", 0)])
    slot_adamw("mla_w_o", 0, slots[("mla_w_o", 0)])
    g_in = _chip_sum("chipsum_mla_w_in", slots[("mla_w_in", 0)], 0, 1)
    plain_adamw("mla_w_in", g_in[:, :, :mla_w_in.shape[2]])
    g_uq = _chip_sum("chipsum_mla_w_uq", slots[("mla_w_uq", 0)], 0, 1)
    plain_adamw("mla_w_uq", g_uq.reshape(1, QL, hps, HEAD_PAD)[..., :QK_DIM].reshape(mla_w_uq.shape))
    for tag in ("ao0", "ai0"):
        slots = rs_end(tag)
        for n, i in rs_groups[tag]:
            slot_adamw(n, i, slots[(n, i)])
    grads, deltas, new_m, new_v = ({n: done[n][k] for n in order} for k in range(4))

    return (loss, grad_x, *[grads[n] for n in order], *[deltas[n] for n in order],
            *[new_m[n] for n in order], *[new_v[n] for n in order])
```

```python
import functools

import jax
import jax.numpy as jnp
from jax import lax
from jax.experimental import pallas as pl
from jax.experimental.pallas import tpu as pltpu

F32 = jnp.float32
BF16 = jnp.bfloat16
MESH = pl.DeviceIdType.MESH
ANY = pl.BlockSpec(memory_space=pl.ANY)

N_DEV = 8
N_HEADS = 16
D_NOPE = 128
D_ROPE = 64
D_V = 128
QK_DIM = D_NOPE + D_ROPE
HEAD_PAD = 256
ROPE_THETA = 10000.0
CONV_WIDTH = 31
CONV_PAD = 32
FFN_RES = 0.5
EPS = 1e-6
ADAM_LR = 0.001
ADAM_B1 = 0.9
ADAM_B2 = 0.999
ADAM_EPS = 1e-08
ADAM_WD = 0.01
ADAM_STEP = 10
VMEM_LIMIT = 56 * 1024 * 1024


def _sds(shape, dtype):
    return jax.ShapeDtypeStruct(tuple(int(s) for s in shape), dtype)


def _tile(n, pref, mult=128):
    if n <= pref:
        return n
    t = (pref // mult) * mult
    while t >= mult:
        if n % t == 0:
            return t
        t -= mult
    return n


def _params():
    return pltpu.CompilerParams(vmem_limit_bytes=VMEM_LIMIT)


_LAST = [None]


def _ordered_call(body, operands, chain_out=0, **kw):
    operands = list(operands)
    n_in = len(operands)
    if _LAST[0] is not None and not any(op is _LAST[0] for op in operands):
        inner = body

        def body(*refs):
            inner(*refs[:n_in], *refs[n_in + 1:])

        kw = dict(kw, in_specs=list(kw["in_specs"]) + [ANY])
        operands.append(_LAST[0])
    out = pl.pallas_call(body, **kw)(*operands)
    _LAST[0] = out[chain_out] if isinstance(out, (list, tuple)) else out
    return out


def _matmul(name, grid, ops, terms, dims, acc_shapes, outs, epilogue, extras=()):
    nk = grid[2]
    n_ops, n_ex, n_out, n_acc = len(ops), len(extras), len(outs), len(acc_shapes)

    def body(*refs):
        op_refs = refs[:n_ops]
        ex_refs = refs[n_ops:n_ops + n_ex]
        out_refs = refs[n_ops + n_ex:n_ops + n_ex + n_out]
        acc_refs = refs[n_ops + n_ex + n_out:]
        vals = {}

        def opval(i):
            if i not in vals:
                v = op_refs[i][...]
                vals[i] = v if v.dtype == BF16 else v.astype(BF16)
            return vals[i]

        parts = [None] * n_acc
        for ai, li, ri in terms:
            d = lax.dot_general(opval(li), opval(ri), (dims, ((), ())), preferred_element_type=F32)
            parts[ai] = d if parts[ai] is None else parts[ai] + d
        if nk == 1:
            epilogue(parts, ex_refs, out_refs)
            return
        k = pl.program_id(2)

        @pl.when(k == 0)
        def _():
            for a_ref, p in zip(acc_refs, parts):
                a_ref[...] = p

        @pl.when(k > 0)
        def _():
            for a_ref, p in zip(acc_refs, parts):
                a_ref[...] += p

        @pl.when(k == nk - 1)
        def _():
            epilogue([a[...] for a in acc_refs], ex_refs, out_refs)

    scratch = [pltpu.VMEM(s, F32) for s in acc_shapes] if nk > 1 else []
    return _ordered_call(
        body, [a for a, _ in ops] + [a for a, _ in extras], name=name, grid=grid,
        in_specs=[s for _, s in ops] + [s for _, s in extras],
        out_specs=[s for _, s in outs],
        out_shape=[o for o, _ in outs],
        scratch_shapes=scratch,
        compiler_params=_params(),
    )


NN = ((1,), (0,))
NT = ((1,), (1,))
TN = ((0,), (0,))


def _store(i=0):
    def ep(accs, ex, outs):
        outs[0][...] = accs[0].astype(outs[0].dtype)
    return ep


def _mm_nn(name, a, b, out_dtype, tm=1024, tn=1024, tk=2048, res=None, scale=1.0):
    M, K = a.shape
    N = b.shape[1]
    tm, tn, tk = _tile(M, tm, 16), _tile(N, tn), _tile(K, tk)
    spec = pl.BlockSpec((tm, tn), lambda i, j, k: (i, j))

    def ep(accs, ex, outs):
        v = accs[0] if scale == 1.0 else accs[0] * scale
        outs[0][...] = (v if res is None else ex[0][...] + v).astype(outs[0].dtype)

    return _matmul(name, (M // tm, N // tn, K // tk),
                   [(a, pl.BlockSpec((tm, tk), lambda i, j, k: (i, k))),
                    (b, pl.BlockSpec((tk, tn), lambda i, j, k: (k, j)))],
                   [(0, 0, 1)], NN, [(tm, tn)], [(_sds((M, N), out_dtype), spec)], ep,
                   [] if res is None else [(res, spec)])


def _mm_nt(name, a, b, out_dtype, tm=1024, tn=1024, tk=2048, epilogue=None, extras=()):
    M, K = a.shape
    N = b.shape[0]
    tm, tn, tk = _tile(M, tm, 16), _tile(N, tn), _tile(K, tk)
    outs = [(_sds((M, N), out_dtype), pl.BlockSpec((tm, tn), lambda i, j, k: (i, j)))]
    return _matmul(name, (M // tm, N // tn, K // tk),
                   [(a, pl.BlockSpec((tm, tk), lambda i, j, k: (i, k))),
                    (b, pl.BlockSpec((tn, tk), lambda i, j, k: (j, k)))],
                   [(0, 0, 1)], NT, [(tm, tn)], outs, epilogue or _store(), extras)


def _mm_tn(name, a, b, out_dtype, tm=512, tn=2048, tk=2048, scale=None):
    T, M = a.shape
    N = b.shape[1]
    tm, tn, tk = _tile(M, tm), _tile(N, tn), _tile(T, tk)

    def ep(accs, ex, outs):
        v = accs[0] if scale is None else accs[0] * scale
        outs[0][...] = v.astype(outs[0].dtype)

    outs = [(_sds((M, N), out_dtype), pl.BlockSpec((tm, tn), lambda i, j, k: (i, j)))]
    return _matmul(name, (M // tm, N // tn, T // tk),
                   [(a, pl.BlockSpec((tk, tm), lambda i, j, k: (k, i))),
                    (b, pl.BlockSpec((tk, tn), lambda i, j, k: (k, j)))],
                   [(0, 0, 1)], TN, [(tm, tn)], outs, ep)[0]


def _mm_nn_sm(name, a, w, out_dtype, tm=2048, tk=2048, epilogue=None, extras=()):
    M, K = a.shape
    S, _, Ns = w.shape
    tm, tk = _tile(M, tm, 16), _tile(K, tk)
    outs = [(_sds((M, S * Ns), out_dtype), pl.BlockSpec((tm, Ns), lambda j, i, k: (i, j)))]
    return _matmul(name, (S, M // tm, K // tk),
                   [(a, pl.BlockSpec((tm, tk), lambda j, i, k: (i, k))),
                    (w, pl.BlockSpec((None, tk, Ns), lambda j, i, k: (j, k, 0)))],
                   [(0, 0, 1)], NN, [(tm, Ns)], outs, epilogue or _store(), extras)[0]


def _mm_nt_sm(name, a, w, out_dtype, tm=1024, tn=1024):
    M = a.shape[0]
    S, K, Ns = w.shape
    tm, tn = _tile(M, tm, 16), _tile(K, tn)
    outs = [(_sds((M, K), out_dtype), pl.BlockSpec((tm, tn), lambda i, n, j: (i, n)))]
    return _matmul(name, (M // tm, K // tn, S),
                   [(a, pl.BlockSpec((tm, Ns), lambda i, n, j: (i, j))),
                    (w, pl.BlockSpec((None, tn, Ns), lambda i, n, j: (j, n, 0)))],
                   [(0, 0, 1)], NT, [(tm, tn)], outs, _store())[0]


def _mm_tn_sm(name, a, b, S, out_dtype, tm=1024, tk=2048):
    T, M = a.shape
    Ns = b.shape[1] // S
    tm, tk = _tile(M, tm), _tile(T, tk)
    outs = [(_sds((S, M, Ns), out_dtype), pl.BlockSpec((None, tm, Ns), lambda j, i, k: (j, i, 0)))]
    return _matmul(name, (S, M // tm, T // tk),
                   [(a, pl.BlockSpec((tk, tm), lambda j, i, k: (k, i))),
                    (b, pl.BlockSpec((tk, Ns), lambda j, i, k: (k, j)))],
                   [(0, 0, 1)], TN, [(tm, Ns)], outs, _store())[0]


def _row_spec(shape, axis, tm):
    block = tuple(tm if d == axis else s for d, s in enumerate(shape))
    nd = len(shape)

    def imap(i):
        return tuple(i if d == axis else 0 for d in range(nd))
    return pl.BlockSpec(block, imap)


def _full_spec(shape):
    nd = len(shape)
    return pl.BlockSpec(tuple(shape), lambda i: (0,) * nd)


def _rowwise(name, fn, T, tm, rows, consts, outs, accs=()):
    tm = _tile(T, tm, 16)
    n_in = len(rows) + len(consts)
    n_out = len(outs)

    def body(*refs):
        in_refs = refs[:n_in]
        out_refs = refs[n_in:n_in + n_out]
        acc_refs = refs[n_in + n_out:]
        i = pl.program_id(0)

        def acc_add(ai, val):
            @pl.when(i == 0)
            def _():
                acc_refs[ai][...] = val

            @pl.when(i > 0)
            def _():
                acc_refs[ai][...] += val

        fn(in_refs, out_refs, acc_add)

    return _ordered_call(
        body, [a for a, _ in rows] + list(consts), name=name, grid=(T // tm,),
        in_specs=[_row_spec(a.shape, ax, tm) for a, ax in rows] + [_full_spec(c.shape) for c in consts],
        out_specs=[_row_spec(s, ax, tm) for s, _, ax in outs] + [_full_spec(s) for s in accs],
        out_shape=[_sds(s, d) for s, d, _ in outs] + [_sds(s, F32) for s in accs],
        compiler_params=_params(),
    )


def _rms(x, g, n=None):
    n = x.shape[-1] if n is None else n
    return x * lax.rsqrt(jnp.sum(x * x, axis=-1, keepdims=True) * (1.0 / n) + EPS) * g


def _norm_fwd(name, h, gain):
    T, D = h.shape

    def fn(ins, outs, acc):
        outs[0][...] = _rms(ins[0][...], ins[1][...]).astype(BF16)

    return _rowwise(name, fn, T, 512, [(h, 0)], [gain], [((T, D), BF16, 0)])[0]


def _norm_bwd(name, h, gain, dhn, dh_res):
    T, D = h.shape

    def fn(ins, outs, acc):
        _, vjp = jax.vjp(_rms, ins[0][...], ins[3][...])
        dh, dg = vjp(ins[1][...])
        dh = dh + ins[2][...]
        outs[0][...] = dh
        outs[1][...] = dh.astype(BF16)
        acc(0, dg)

    return _rowwise(name, fn, T, 256, [(h, 0), (dhn, 0), (dh_res, 0)], [gain],
                    [((T, D), F32, 0), ((T, D), BF16, 0)], [(1, D)])


def _ffn_fwd(tag, h, gain, get_in, get_out):
    T, D = h.shape
    hn = _norm_fwd(tag + "_norm", h, gain)
    w_in = get_in()
    S, _, Ns = w_in.shape
    half = S // 2
    F = half * Ns
    tm, tk = _tile(T, 256, 16), _tile(D, 2048)

    def ep(accs, ex, outs):
        g, u = accs
        sg = jax.nn.sigmoid(g)
        silu = g * sg
        outs[0][0] = (u * (sg * (1.0 + g * (1.0 - sg)))).astype(BF16)
        outs[0][1] = silu.astype(BF16)
        outs[1][...] = (silu * u).astype(BF16)

    gu, act = _matmul(
        tag + "_in", (half, T // tm, D // tk),
        [(hn, pl.BlockSpec((tm, tk), lambda j, i, k: (i, k))),
         (w_in, pl.BlockSpec((None, tk, Ns), lambda j, i, k: (j, k, 0))),
         (w_in, pl.BlockSpec((None, tk, Ns), lambda j, i, k: (j + half, k, 0)))],
        [(0, 0, 1), (1, 0, 2)], NN, [(tm, Ns), (tm, Ns)],
        [(_sds((2, T, F), BF16), pl.BlockSpec((2, tm, Ns), lambda j, i, k: (0, i, j))),
         (_sds((T, F), BF16), pl.BlockSpec((tm, Ns), lambda j, i, k: (i, j)))],
        ep)

    w_out = get_out()
    h_new = _mm_nn(tag + "_out", act, w_out, F32, tm=1024, tn=512, tk=F, res=h, scale=FFN_RES)[0]
    return h_new, (h, hn, gu, act)


def _no_hook(*_):
    return None


def _ffn_bwd(tag, saved, gain, w_in, w_out, dh, dhb, hooks=(_no_hook, _no_hook, _no_hook)):
    h, hn, gu, act = saved
    T, D = h.shape
    S, _, Ns = w_in.shape
    half = S // 2
    F = half * Ns
    tm, tk = _tile(T, 512, 16), _tile(D, 2048)

    def ep(accs, ex, outs):
        dact = FFN_RES * accs[0]
        outs[0][0] = (dact * ex[0][0].astype(F32)).astype(BF16)
        outs[0][1] = (dact * ex[0][1].astype(F32)).astype(BF16)

    gu_spec = pl.BlockSpec((2, tm, Ns), lambda j, i, k: (0, i, j))
    dgu = _matmul(
        tag + "_dact", (half, T // tm, D // tk),
        [(dhb, pl.BlockSpec((tm, tk), lambda j, i, k: (i, k))),
         (w_out, pl.BlockSpec((Ns, tk), lambda j, i, k: (j, k)))],
        [(0, 0, 1)], NT, [(tm, Ns)],
        [(_sds((2, T, F), BF16), gu_spec)], ep, extras=[(gu, gu_spec)])[0]

    dw_out = _mm_tn(tag + "_dwout", act, dhb, BF16, scale=FFN_RES)
    hooks[0](dw_out)

    tm = _tile(T, 1024, 16)
    tkd, tt = _tile(D, 512), _tile(T, 2048)
    dw_in = _matmul(
        tag + "_dwin", (S, D // tkd, T // tt),
        [(hn, pl.BlockSpec((tt, tkd), lambda j, i, k: (k, i))),
         (dgu, pl.BlockSpec((None, tt, Ns), lambda j, i, k: (j // half, k, j % half)))],
        [(0, 0, 1)], TN, [(tkd, Ns)],
        [(_sds((S, D, Ns), BF16), pl.BlockSpec((None, tkd, Ns), lambda j, i, k: (j, i, 0)))], _store())[0]
    hooks[1](dw_in)

    tn = _tile(D, 1024)
    dhn = _matmul(
        tag + "_dhn", (T // tm, D // tn, half),
        [(dgu, pl.BlockSpec((None, tm, Ns), lambda i, n, j: (0, i, j))),
         (dgu, pl.BlockSpec((None, tm, Ns), lambda i, n, j: (1, i, j))),
         (w_in, pl.BlockSpec((None, tn, Ns), lambda i, n, j: (j, n, 0))),
         (w_in, pl.BlockSpec((None, tn, Ns), lambda i, n, j: (j + half, n, 0)))],
        [(0, 0, 2), (0, 1, 3)], NT, [(tm, tn)],
        [(_sds((T, D), F32), pl.BlockSpec((tm, tn), lambda i, n, j: (i, n)))], _store())[0]
    hooks[2]()

    dh_in, dh_in_b, dgain = _norm_bwd(tag + "_dnorm", h, gain, dhn, dh)
    return dh_in, dh_in_b, dgain, dw_in, dw_out


def _ple_fwd(tag, h, p, get_proj, ple_norm, gate_norm, get_gate):
    T, D = h.shape
    e_raw = _mm_nn_sm(tag + "_proj", p, get_proj(), F32)
    hn = _norm_fwd(tag + "_norm", h, gate_norm)
    gate_raw = _mm_nn(tag + "_gate", hn, get_gate(), F32)[0]

    def fn(ins, outs, acc):
        e = _rms(ins[1][...], ins[3][...])
        outs[0][...] = ins[0][...] + e * jax.nn.sigmoid(ins[2][...])

    h_new = _rowwise(tag + "_mix", fn, T, 256, [(h, 0), (e_raw, 0), (gate_raw, 0)], [ple_norm],
                     [((T, D), F32, 0)])[0]
    return h_new, (h, hn, e_raw, gate_raw)


def _ple_bwd(tag, saved, p, w_proj, ple_norm, gate_norm, w_gate, dh, dhb):
    h, hn, e_raw, gate_raw = saved
    T, D = h.shape
    S = w_proj.shape[0]

    def fn(ins, outs, acc):
        def f(e_raw_, gate_raw_, g_):
            return _rms(e_raw_, g_) * jax.nn.sigmoid(gate_raw_)
        _, vjp = jax.vjp(f, ins[0][...], ins[1][...], ins[3][...])
        de, dgate, dg = vjp(ins[2][...])
        outs[0][...] = de.astype(BF16)
        outs[1][...] = dgate.astype(BF16)
        acc(0, dg)

    de, dgate, d_ple_norm = _rowwise(tag + "_dmix", fn, T, 256, [(e_raw, 0), (gate_raw, 0), (dh, 0)], [ple_norm],
                                     [((T, D), BF16, 0), ((T, D), BF16, 0)], [(1, D)])
    dw_proj = _mm_tn_sm(tag + "_dwproj", p, de, S, BF16)
    dw_gate = _mm_tn(tag + "_dwgate", hn, dgate, BF16)
    dhn = _mm_nt(tag + "_dhn", dgate, w_gate, F32)[0]
    dh_in, dh_in_b, d_gate_norm = _norm_bwd(tag + "_dnorm", h, gate_norm, dhn, dh)
    return dh_in, dh_in_b, d_ple_norm, d_gate_norm, dw_proj, dw_gate


def _rope(t, c, s1, s2):
    q = D_ROPE // 2
    return t * c + pltpu.roll(t, q, 1) * s1 + pltpu.roll(t, 128 - q, 1) * s2


def _rope_t(d, c, s1, s2):
    q = D_ROPE // 2
    return d * c + pltpu.roll(d * s1, 128 - q, 1) + pltpu.roll(d * s2, q, 1)


def _head_norm(lo, hi, g_lo, g_hi):
    ms = (jnp.sum(lo * lo, axis=-1, keepdims=True) + jnp.sum(hi * hi, axis=-1, keepdims=True)) * (1.0 / QK_DIM)
    inv = lax.rsqrt(ms + EPS)
    return lo * inv * g_lo, hi * inv * g_hi


def _qk_prep(qraw, kvraw, lat, tabs, gq, gk, H):
    T = qraw.shape[0]
    koff = lat.shape[1] - 128

    def fn(ins, outs, acc):
        q_ref, kv_ref, lat_ref, c_ref, s1_ref, s2_ref, gq_ref, gk_ref = ins
        c, s1, s2 = c_ref[...], s1_ref[...], s2_ref[...]
        kr = lat_ref[:, koff:koff + 128]
        for hd in range(H):
            o = hd * HEAD_PAD
            lo, hi = _head_norm(q_ref[:, o:o + 128], q_ref[:, o + 128:o + 256], gq_ref[:, 0:128], gq_ref[:, 128:256])
            outs[0][hd, :, 0:128] = lo.astype(BF16)
            outs[0][hd, :, 128:256] = _rope(hi, c, s1, s2).astype(BF16)
            lo, hi = _head_norm(kv_ref[:, o:o + 128], kr, gk_ref[:, 0:128], gk_ref[:, 128:256])
            outs[1][hd, :, 0:128] = lo.astype(BF16)
            outs[1][hd, :, 128:256] = _rope(hi, c, s1, s2).astype(BF16)
            outs[2][hd] = kv_ref[:, o + 128:o + 256].astype(BF16)

    return _rowwise("mla_qkprep", fn, T, 256, [(qraw, 0), (kvraw, 0), (lat, 0)] + [(t, 0) for t in tabs], [gq, gk],
                    [((H, T, HEAD_PAD), BF16, 1), ((H, T, HEAD_PAD), BF16, 1), ((H, T, D_V), BF16, 1)])


def _qk_prep_bwd(qraw, kvraw, lat, tabs, gq, gk, dQ, dK, dV, H):
    T = qraw.shape[0]
    koff = lat.shape[1] - 128

    def fn(ins, outs, acc):
        q_ref, kv_ref, lat_ref, c_ref, s1_ref, s2_ref, dq_ref, dk_ref, dv_ref, gq_ref, gk_ref = ins
        c, s1, s2 = c_ref[...], s1_ref[...], s2_ref[...]
        kr = lat_ref[:, koff:koff + 128]
        dkr = jnp.zeros_like(kr)
        dg = [None] * 4
        for hd in range(H):
            o = hd * HEAD_PAD
            _, vjp = jax.vjp(_head_norm, q_ref[:, o:o + 128], q_ref[:, o + 128:o + 256],
                             gq_ref[:, 0:128], gq_ref[:, 128:256])
            dlo, dhi, dg0, dg1 = vjp((dq_ref[hd, :, 0:128], _rope_t(dq_ref[hd, :, 128:256], c, s1, s2)))
            outs[0][:, o:o + 128] = dlo.astype(BF16)
            outs[0][:, o + 128:o + 256] = dhi.astype(BF16)
            _, vjp = jax.vjp(_head_norm, kv_ref[:, o:o + 128], kr, gk_ref[:, 0:128], gk_ref[:, 128:256])
            dlo, dhi, dg2, dg3 = vjp((dk_ref[hd, :, 0:128], _rope_t(dk_ref[hd, :, 128:256], c, s1, s2)))
            outs[1][:, o:o + 128] = dlo.astype(BF16)
            outs[1][:, o + 128:o + 256] = dv_ref[hd].astype(BF16)
            dkr = dkr + dhi
            for n, v in enumerate((dg0, dg1, dg2, dg3)):
                dg[n] = v if dg[n] is None else dg[n] + v
        outs[2][...] = dkr
        for n in range(4):
            acc(n, dg[n])

    W = H * HEAD_PAD
    return _rowwise("mla_dqkprep", fn, T, 128,
                    [(qraw, 0), (kvraw, 0), (lat, 0)] + [(t, 0) for t in tabs] + [(dQ, 1), (dK, 1), (dV, 1)], [gq, gk],
                    [((T, W), BF16, 0), ((T, W), BF16, 0), ((T, 128), F32, 0)], [(1, 128)] * 4)


def _attn_probs(q, k, c, tq):
    nk = k.shape[0]
    s = lax.dot_general(q, k, (NT, ((), ())), preferred_element_type=F32) * (QK_DIM ** -0.5)
    row = c * tq + lax.broadcasted_iota(jnp.int32, (tq, nk), 0)
    col = lax.broadcasted_iota(jnp.int32, (tq, nk), 1)
    s = jnp.where(col <= row, s, -jnp.inf)
    p = jnp.exp(s - jnp.max(s, axis=-1, keepdims=True))
    return p / jnp.sum(p, axis=-1, keepdims=True)


def _per_query_block(nq, fn):
    i = pl.program_id(1)
    for c in range(nq):
        pl.when(i == c)(functools.partial(fn, c))


def _attn_fwd(Q, K, V):
    H, T, _ = Q.shape
    tq = _tile(T, 512)

    def body(q_ref, k_ref, v_ref, o_ref):
        def block(c):
            nk = (c + 1) * tq
            p = _attn_probs(q_ref[...], k_ref[0:nk, :], c, tq)
            o_ref[...] = jnp.dot(p.astype(BF16), v_ref[0:nk, :], preferred_element_type=F32).astype(BF16)

        _per_query_block(T // tq, block)

    return _ordered_call(
        body, [Q, K, V], name="mla_attn", grid=(H, T // tq),
        in_specs=[pl.BlockSpec((None, tq, HEAD_PAD), lambda h, i: (h, i, 0)),
                  pl.BlockSpec((None, T, HEAD_PAD), lambda h, i: (h, 0, 0)),
                  pl.BlockSpec((None, T, D_V), lambda h, i: (h, 0, 0))],
        out_specs=pl.BlockSpec((tq, D_V), lambda h, i: (i, h)),
        out_shape=_sds((T, H * D_V), BF16),
        compiler_params=_params(),
    )


def _attn_bwd(Q, K, V, dO):
    H, T, _ = Q.shape
    tq = _tile(T, 512)

    def body(q_ref, k_ref, v_ref, do_ref, dq_ref, dk_ref, dv_ref):
        @pl.when(pl.program_id(1) == 0)
        def _():
            dk_ref[...] = jnp.zeros_like(dk_ref)
            dv_ref[...] = jnp.zeros_like(dv_ref)

        def block(c):
            nk = (c + 1) * tq
            q, k, do = q_ref[...], k_ref[0:nk, :], do_ref[...]
            p = _attn_probs(q, k, c, tq)
            dv_ref[0:nk, :] += lax.dot_general(p.astype(BF16), do, (TN, ((), ())), preferred_element_type=F32)
            dp = lax.dot_general(do, v_ref[0:nk, :], (NT, ((), ())), preferred_element_type=F32)
            ds = p * (dp - jnp.sum(p * dp, axis=-1, keepdims=True)) * (QK_DIM ** -0.5)
            dsb = ds.astype(BF16)
            dq_ref[...] = jnp.dot(dsb, k, preferred_element_type=F32)
            dk_ref[0:nk, :] += lax.dot_general(dsb, q, (TN, ((), ())), preferred_element_type=F32)

        _per_query_block(T // tq, block)

    return _ordered_call(
        body, [Q, K, V, dO], name="mla_dattn", grid=(H, T // tq),
        in_specs=[pl.BlockSpec((None, tq, HEAD_PAD), lambda h, i: (h, i, 0)),
                  pl.BlockSpec((None, T, HEAD_PAD), lambda h, i: (h, 0, 0)),
                  pl.BlockSpec((None, T, D_V), lambda h, i: (h, 0, 0)),
                  pl.BlockSpec((tq, D_V), lambda h, i: (i, h))],
        out_specs=[pl.BlockSpec((None, tq, HEAD_PAD), lambda h, i: (h, i, 0)),
                   pl.BlockSpec((None, T, HEAD_PAD), lambda h, i: (h, 0, 0)),
                   pl.BlockSpec((None, T, D_V), lambda h, i: (h, 0, 0))],
        out_shape=[_sds((H, T, HEAD_PAD), F32), _sds((H, T, HEAD_PAD), F32), _sds((H, T, D_V), F32)],
        compiler_params=_params(),
    )


def _mla_fwd(h, gain, tabs, get, q_lat_norm, kv_lat_norm, gq, gk):
    T, D = h.shape
    QL, KL = q_lat_norm.shape[1], kv_lat_norm.shape[1]
    hn = _norm_fwd("mla_norm", h, gain)
    w = dict(m_in=get["m_in"]())
    lat = _mm_nn("mla_lat", hn, w["m_in"], F32, tn=w["m_in"].shape[1])[0]

    def fn(ins, outs, acc):
        outs[0][...] = _rms(ins[0][:, 0:QL], ins[1][...]).astype(BF16)
        outs[1][...] = _rms(ins[0][:, QL:QL + KL], ins[2][...]).astype(BF16)

    cq, ckv = _rowwise("mla_latnorm", fn, T, 256, [(lat, 0)], [q_lat_norm, kv_lat_norm],
                       [((T, QL), BF16, 0), ((T, KL), BF16, 0)])
    w["uq"], w["ukv"] = get["uq"](), get["ukv"]()
    H = w["uq"].shape[0] * w["uq"].shape[2] // HEAD_PAD
    qraw = _mm_nn_sm("mla_uq", cq, w["uq"], F32)
    kvraw = _mm_nn_sm("mla_ukv", ckv, w["ukv"], F32)
    Q, K, V = _qk_prep(qraw, kvraw, lat, tabs, gq, gk, H)
    O = _attn_fwd(Q, K, V)
    w["wo"] = get["wo"]()

    h_new = _mm_nn("mla_out", O, w["wo"], F32, res=h)[0]
    return h_new, (h, hn, lat, cq, ckv, qraw, kvraw, Q, K, V, O)


def _mla_bwd(saved, gain, tabs, w, q_lat_norm, kv_lat_norm, gq, gk, dh, dhb, hook=_no_hook):
    h, hn, lat, cq, ckv, qraw, kvraw, Q, K, V, O = saved
    T, D = h.shape
    H = Q.shape[0]
    S = w["uq"].shape[0]
    QL, KL = q_lat_norm.shape[1], kv_lat_norm.shape[1]
    dO = _mm_nt("mla_dO", dhb, w["wo"], BF16)[0]
    dwo = _mm_tn("mla_dwo", O, dhb, BF16)
    hook()
    dQ, dK, dV = _attn_bwd(Q, K, V, dO)
    dqraw, dkvraw, dkr, dgq0, dgq1, dgk0, dgk1 = _qk_prep_bwd(qraw, kvraw, lat, tabs, gq, gk, dQ, dK, dV, H)
    dcq = _mm_nt_sm("mla_dcq", dqraw, w["uq"], F32)
    dckv = _mm_nt_sm("mla_dckv", dkvraw, w["ukv"], F32)
    dwuq = _mm_tn_sm("mla_dwuq", cq, dqraw, S, BF16)
    dwukv = _mm_tn_sm("mla_dwukv", ckv, dkvraw, S, BF16)

    def fn(ins, outs, acc):
        _, vjp = jax.vjp(_rms, ins[0][:, 0:QL], ins[4][...])
        d, dgq_ = vjp(ins[1][...])
        outs[0][:, 0:QL] = d.astype(BF16)
        _, vjp = jax.vjp(_rms, ins[0][:, QL:QL + KL], ins[5][...])
        d, dgkv_ = vjp(ins[2][...])
        outs[0][:, QL:QL + KL] = d.astype(BF16)
        outs[0][:, QL + KL:QL + KL + 128] = ins[3][...].astype(BF16)
        acc(0, dgq_)
        acc(1, dgkv_)

    dlat, d_qln, d_kvln = _rowwise("mla_dlatnorm", fn, T, 256, [(lat, 0), (dcq, 0), (dckv, 0), (dkr, 0)],
                                   [q_lat_norm, kv_lat_norm], [(lat.shape, BF16, 0)], [(1, QL), (1, KL)])
    dhn = _mm_nt("mla_dhn", dlat, w["m_in"], F32, tk=lat.shape[1])[0]
    dw_min = _mm_tn("mla_dwin", hn, dlat, BF16, tn=lat.shape[1])
    dh_in, dh_in_b, dgain = _norm_bwd("mla_dnorm", h, gain, dhn, dh)
    d_gq = jnp.concatenate([dgq0, dgq1], axis=1)[:, :QK_DIM]
    d_gk = jnp.concatenate([dgk0, dgk1], axis=1)[:, :QK_DIM]
    return dh_in, dh_in_b, dgain, d_qln, d_kvln, d_gq, d_gk, dw_min, dwuq, dwukv, dwo


def _conv_rows(T):
    return _tile(T, 128, 8)


def _dwconv_fwd(u, w_dw, b_dw):
    T, C = u.shape
    tc, R = _tile(C, 256), _conv_rows(T)
    off = CONV_PAD - (CONV_WIDTH - 1)

    def body(u_ref, w_ref, b_ref, y_ref, pad_ref):
        pad_ref[0:CONV_PAD, :] = jnp.zeros((CONV_PAD, tc), F32)
        pad_ref[CONV_PAD:CONV_PAD + T, :] = u_ref[...]
        for r in range(T // R):
            acc = jnp.broadcast_to(b_ref[...], (R, tc))
            for j in range(CONV_WIDTH):
                acc = acc + w_ref[j:j + 1, :] * pad_ref[r * R + off + j:r * R + off + j + R, :]
            y_ref[r * R:(r + 1) * R, :] = acc

    return _ordered_call(
        body, [u, w_dw, b_dw], name="conv_dw", grid=(C // tc,),
        in_specs=[pl.BlockSpec((T, tc), lambda c: (0, c)), pl.BlockSpec((32, tc), lambda c: (0, c)),
                  pl.BlockSpec((1, tc), lambda c: (0, c))],
        out_specs=pl.BlockSpec((T, tc), lambda c: (0, c)),
        out_shape=_sds((T, C), F32),
        scratch_shapes=[pltpu.VMEM((T + CONV_PAD, tc), F32)],
        compiler_params=_params(),
    )


def _dwconv_bwd(u, w_dw, dy):
    T, C = u.shape
    tc, R = _tile(C, 256), _conv_rows(T)
    off = CONV_PAD - (CONV_WIDTH - 1)

    def body(u_ref, w_ref, dy_ref, du_ref, dw_ref, db_ref, upad_ref, dpad_ref):
        upad_ref[0:CONV_PAD, :] = jnp.zeros((CONV_PAD, tc), F32)
        upad_ref[CONV_PAD:CONV_PAD + T, :] = u_ref[...]
        dpad_ref[0:T, :] = dy_ref[...]
        dpad_ref[T:T + CONV_PAD, :] = jnp.zeros((CONV_PAD, tc), F32)
        for r in range(T // R):
            acc = jnp.zeros((R, tc), F32)
            for j in range(CONV_WIDTH):
                s = r * R + (CONV_WIDTH - 1) - j
                acc = acc + w_ref[j:j + 1, :] * dpad_ref[s:s + R, :]
            du_ref[r * R:(r + 1) * R, :] = acc
        for j in range(CONV_WIDTH):
            acc = jnp.zeros((R, tc), F32)
            for r in range(T // R):
                acc = acc + dy_ref[r * R:(r + 1) * R, :] * upad_ref[r * R + off + j:r * R + off + j + R, :]
            dw_ref[j:j + 1, :] = jnp.sum(acc, axis=0, keepdims=True)
        dw_ref[CONV_WIDTH:32, :] = jnp.zeros((32 - CONV_WIDTH, tc), F32)
        db_ref[...] = jnp.sum(dy_ref[...], axis=0, keepdims=True)

    return _ordered_call(
        body, [u, w_dw, dy], name="conv_ddw", grid=(C // tc,),
        in_specs=[pl.BlockSpec((T, tc), lambda c: (0, c)), pl.BlockSpec((32, tc), lambda c: (0, c)),
                  pl.BlockSpec((T, tc), lambda c: (0, c))],
        out_specs=[pl.BlockSpec((T, tc), lambda c: (0, c)), pl.BlockSpec((32, tc), lambda c: (0, c)),
                   pl.BlockSpec((1, tc), lambda c: (0, c))],
        out_shape=[_sds((T, C), F32), _sds((32, C), F32), _sds((1, C), F32)],
        scratch_shapes=[pltpu.VMEM((T + CONV_PAD, tc), F32), pltpu.VMEM((T + CONV_PAD, tc), F32)],
        compiler_params=_params(),
    )


def _ln_silu(y, g, b):
    mu = jnp.mean(y, axis=-1, keepdims=True)
    yc = y - mu
    z = yc * lax.rsqrt(jnp.mean(yc * yc, axis=-1, keepdims=True) + EPS) * g + b
    return z * jax.nn.sigmoid(z)


def _conv_fwd(h, gain, get):
    T, D = h.shape
    hn = _norm_fwd("conv_norm", h, gain)
    w = dict(pw1=get["pw1"]())
    b_pw1, w_dw, b_dw, ln_g, ln_b = get["small"]()
    S, _, Ns = w["pw1"].shape
    half = S // 2
    C = half * Ns
    tm, tk = _tile(T, 512, 16), _tile(D, 2048)

    def ep(accs, ex, outs):
        a = accs[0] + ex[0][...]
        g = accs[1] + ex[1][...]
        outs[0][0] = a.astype(BF16)
        outs[0][1] = g.astype(BF16)
        outs[1][...] = a * jax.nn.sigmoid(g)

    ag, u = _matmul(
        "conv_pw1", (half, T // tm, D // tk),
        [(hn, pl.BlockSpec((tm, tk), lambda j, i, k: (i, k))),
         (w["pw1"], pl.BlockSpec((None, tk, Ns), lambda j, i, k: (j, k, 0))),
         (w["pw1"], pl.BlockSpec((None, tk, Ns), lambda j, i, k: (j + half, k, 0)))],
        [(0, 0, 1), (1, 0, 2)], NN, [(tm, Ns), (tm, Ns)],
        [(_sds((2, T, C), BF16), pl.BlockSpec((2, tm, Ns), lambda j, i, k: (0, i, j))),
         (_sds((T, C), F32), pl.BlockSpec((tm, Ns), lambda j, i, k: (i, j)))],
        ep,
        extras=[(b_pw1, pl.BlockSpec((None, 1, Ns), lambda j, i, k: (j, 0, 0))),
                (b_pw1, pl.BlockSpec((None, 1, Ns), lambda j, i, k: (j + half, 0, 0)))])
    y = _dwconv_fwd(u, w_dw, b_dw)

    def fn(ins, outs, acc):
        outs[0][...] = _ln_silu(ins[0][...], ins[1][...], ins[2][...]).astype(BF16)

    s = _rowwise("conv_ln", fn, T, 256, [(y, 0)], [ln_g, ln_b], [((T, C), BF16, 0)])[0]

    w["pw2"] = get["pw2"]()
    h_new = _mm_nn("conv_pw2", s, w["pw2"], F32, res=h)[0]
    return h_new, (h, hn, ag, u, y, s)


def _conv_bwd(saved, gain, w, w_dw, ln_g, ln_b, dh, dhb):
    h, hn, ag, u, y, s = saved
    T, D = h.shape
    S, _, Ns = w["pw1"].shape
    half = S // 2
    C = half * Ns
    ds = _mm_nt("conv_ds", dhb, w["pw2"], F32)[0]
    dw_pw2 = _mm_tn("conv_dwpw2", s, dhb, BF16)

    def fn(ins, outs, acc):
        _, vjp = jax.vjp(_ln_silu, ins[0][...], ins[2][...], ins[3][...])
        dy, dg, db = vjp(ins[1][...])
        outs[0][...] = dy
        acc(0, dg)
        acc(1, db)

    dy, d_ln_g, d_ln_b = _rowwise("conv_dln", fn, T, 256, [(y, 0), (ds, 0)], [ln_g, ln_b],
                                  [((T, C), F32, 0)], [(1, C), (1, C)])
    du, d_w_dw, d_b_dw = _dwconv_bwd(u, w_dw, dy)

    def fn2(ins, outs, acc):
        a = ins[0][0].astype(F32)
        g = ins[0][1].astype(F32)
        du_ = ins[1][...]
        sg = jax.nn.sigmoid(g)
        da = du_ * sg
        dg = du_ * a * sg * (1.0 - sg)
        outs[0][0] = da.astype(BF16)
        outs[0][1] = dg.astype(BF16)
        acc(0, jnp.sum(da, axis=0, keepdims=True))
        acc(1, jnp.sum(dg, axis=0, keepdims=True))

    dag, d_b_a, d_b_g = _rowwise("conv_dglu", fn2, T, 256, [(ag, 1), (du, 0)], [],
                                 [((2, T, C), BF16, 1)], [(1, C), (1, C)])
    tm, tn = _tile(T, 1024, 16), _tile(D, 1024)
    dhn = _matmul(
        "conv_dhn", (T // tm, D // tn, half),
        [(dag, pl.BlockSpec((None, tm, Ns), lambda i, n, j: (0, i, j))),
         (dag, pl.BlockSpec((None, tm, Ns), lambda i, n, j: (1, i, j))),
         (w["pw1"], pl.BlockSpec((None, tn, Ns), lambda i, n, j: (j, n, 0))),
         (w["pw1"], pl.BlockSpec((None, tn, Ns), lambda i, n, j: (j + half, n, 0)))],
        [(0, 0, 2), (0, 1, 3)], NT, [(tm, tn)],
        [(_sds((T, D), F32), pl.BlockSpec((tm, tn), lambda i, n, j: (i, n)))], _store())[0]
    tkd, tt = _tile(D, 1024), _tile(T, 2048)
    dw_pw1 = _matmul(
        "conv_dwpw1", (S, D // tkd, T // tt),
        [(hn, pl.BlockSpec((tt, tkd), lambda j, i, k: (k, i))),
         (dag, pl.BlockSpec((None, tt, Ns), lambda j, i, k: (j // half, k, j % half)))],
        [(0, 0, 1)], TN, [(tkd, Ns)],
        [(_sds((S, D, Ns), BF16), pl.BlockSpec((None, tkd, Ns), lambda j, i, k: (j, i, 0)))], _store())[0]
    dh_in, dh_in_b, dgain = _norm_bwd("conv_dnorm", h, gain, dhn, dh)
    d_b_pw1 = jnp.concatenate([d_b_a, d_b_g], axis=1)
    return dh_in, dh_in_b, dgain, dw_pw1, d_b_pw1, d_w_dw, d_b_dw, d_ln_g, d_ln_b, dw_pw2


def _loss_head(y, target):
    T, D = y.shape

    def fn(ins, outs, acc):
        e = ins[0][...] - ins[1][...]
        d = e * (1.0 / D)
        outs[0][...] = d
        outs[1][...] = d.astype(BF16)
        part = jnp.sum(jnp.sum(e * e, axis=-1, keepdims=True), axis=0, keepdims=True) * (0.5 / D)
        acc(0, jnp.broadcast_to(part, (1, 128)))

    return _rowwise("loss_head", fn, T, 256, [(y, 0), (target, 0)], [], [((T, D), F32, 0), ((T, D), BF16, 0)],
                    [(1, 128)])


def _place():
    return lax.axis_index("x"), lax.axis_index("y"), lax.axis_index("c")


def _peer(j):
    x, y, c = _place()
    return (1 - x if j & 4 else x, 1 - y if j & 2 else y, 1 - c if j & 1 else c)


def _index(place):
    return 4 * place[0] + 2 * place[1] + place[2]


HBM = pl.BlockSpec(memory_space=pltpu.HBM)
SEM = pl.BlockSpec(memory_space=pltpu.SEMAPHORE)
EFFECT = pltpu.SideEffectType.DATAFLOW_SIDE_EFFECTING


def _chip(j):
    x, y, _ = _place()
    return (1 - x if j & 2 else x, 1 - y if j & 1 else y)


def _chip_index(chip):
    return 2 * chip[0] + chip[1]


def _remote(src, dst, send, recv, k, device):
    return pltpu.make_async_remote_copy(src_ref=src, dst_ref=dst, send_sem=send.at[k], recv_sem=recv.at[k],
                                        device_id=device, device_id_type=MESH)


def _hbm(arrays):
    return [pltpu.with_memory_space_constraint(a, pltpu.HBM) for a in arrays]


def _split_call(name, body, ins, sems_in, sems_out):
    n, ns_in, ns_out = len(ins), len(sems_in), len(sems_out)
    if any(a is _LAST[0] for a in ins):
        _LAST[0] = None

    def kernel_body(*refs):
        in_refs = refs[:n]
        si = refs[n:n + ns_in]
        so = refs[n + ns_in:n + ns_in + ns_out]
        tok = refs[-1]
        body(in_refs, si, so, tok)
        tok[...] = jnp.zeros_like(tok)

    res = _ordered_call(
        kernel_body, _hbm(ins) + list(sems_in), chain_out=ns_out + n, name=name,
        in_specs=[HBM] * n + [SEM] * ns_in,
        out_specs=[SEM] * ns_out + [HBM] * n + [pl.BlockSpec(memory_space=pltpu.VMEM)],
        out_shape=[pltpu.SemaphoreType.DMA((s,)) for s in sems_out] + [pltpu.HBM(a.shape, a.dtype) for a in ins]
        + [_sds((8, 128), F32)],
        input_output_aliases={i: ns_out + i for i in range(n)},
        compiler_params=pltpu.CompilerParams(has_side_effects=EFFECT),
    )
    sems = list(res[:ns_out])
    arrays = list(res[ns_out:ns_out + n])
    return sems, arrays, res[-1]


def _ag_start(name, groups):
    flat = [s for grp in groups for s in grp]
    zones = [lax.empty((N_DEV,) + s.shape, s.dtype) for s in flat]
    n = len(flat)
    sizes = []
    for grp in groups:
        sizes += [7 * len(grp), 7 * len(grp), len(grp)]

    def body(ins, si, so, tok):
        x, y, c = _place()
        me = _index((x, y, c))
        base = 0
        for gi, grp in enumerate(groups):
            send, recv, local = so[3 * gi:3 * gi + 3]
            for w in range(len(grp)):
                src, dst = ins[base + w], ins[n + base + w].at[me]
                pltpu.make_async_copy(src, dst, local.at[w]).start()
                _remote(src, dst, send, recv, 7 * w, (x, y, 1 - c)).start()
                for j in (1, 2, 3):
                    _remote(src, dst, send, recv, 7 * w + j, (*_chip(j), c)).start()
            base += len(grp)

    sems, arrays, token = _split_call(name, body, flat + zones, [], sizes)
    out, base = [], 0
    for gi, grp in enumerate(groups):
        k = len(grp)
        out.append((sems[3 * gi:3 * gi + 3], arrays[base:base + k], arrays[n + base:n + base + k]))
        base += k
    return out, token


def _ag_forward(name, handle):
    sems, shards, zones = handle
    k = len(shards)

    def arrive(ins, si, so, tok):
        send, recv, _ = si
        _, _, c = _place()
        for j in (1, 2, 3):
            for w in range(k):
                blk = ins[k + w].at[_index((*_chip(j), c))]
                _remote(ins[w], blk, send, recv, 7 * w + j, (*_chip(j), c)).wait_recv()

    _, arrays, _ = _split_call(name + "_arrive", arrive, list(shards) + list(zones), sems, [])

    def pass_on(ins, si, so, tok):
        fsend, frecv = so
        x, y, c = _place()
        for j in (1, 2, 3):
            for w in range(k):
                blk = ins[w].at[_index((*_chip(j), c))]
                _remote(blk, blk, fsend, frecv, 3 * w + j - 1, (x, y, 1 - c)).start()

    fsems, zones2, _ = _split_call(name + "_pass", pass_on, arrays[k:], [], [3 * k, 3 * k])
    return (list(sems) + fsems, arrays[:k], zones2)


def _ag_wait(name, handle):
    sems, shards, zones = handle
    k = len(shards)

    def body(ins, si, so, tok):
        send, recv, local, fsend, frecv = si
        x, y, c = _place()
        sib = (x, y, 1 - c)
        for w in range(k):
            zone = ins[k + w]
            _remote(ins[w], zone.at[_index(sib)], send, recv, 7 * w, sib).wait_recv()
            for j in (1, 2, 3):
                blk = zone.at[_index((*_chip(j), 1 - c))]
                _remote(blk, blk, fsend, frecv, 3 * w + j - 1, sib).wait_recv()
        for w in range(k):
            zone = ins[k + w]
            mine = zone.at[_index((x, y, c))]
            _remote(ins[w], mine, send, recv, 7 * w, sib).wait_send()
            for j in (1, 2, 3):
                _remote(ins[w], mine, send, recv, 7 * w + j, (*_chip(j), c)).wait_send()
                blk = zone.at[_index((*_chip(j), c))]
                _remote(blk, blk, fsend, frecv, 3 * w + j - 1, sib).wait_send()
            pltpu.make_async_copy(ins[w], mine, local.at[w]).wait()

    _, arrays, _ = _split_call(name, body, list(shards) + list(zones), sems, [])
    return arrays[k:]


def _rs_pair_start(name, grads):
    n = len(grads)
    zones = [lax.empty((4,) + g.shape[1:], g.dtype) for g in grads]

    def body(ins, si, so, tok):
        send, recv = so
        x, y, c = _place()
        for w in range(n):
            for q in range(4):
                _remote(ins[w].at[2 * q + 1 - c], ins[n + w].at[q], send, recv, 4 * w + q, (x, y, 1 - c)).start()

    sems, arrays, token = _split_call(name, body, list(grads) + zones, [], [4 * n, 4 * n])
    return (sems, arrays[:n], arrays[n:]), token


def _rs_pair_wait(name, handle):
    sems, grads, zones = handle
    n = len(grads)

    def body(ins, si, so, tok):
        send, recv = si
        x, y, c = _place()
        for w in range(n):
            for q in range(4):
                cp = _remote(ins[w].at[2 * q + 1 - c], ins[n + w].at[q], send, recv, 4 * w + q, (x, y, 1 - c))
                cp.wait_recv()
                cp.wait_send()

    _, arrays, _ = _split_call(name, body, list(grads) + list(zones), sems, [])
    return arrays[:n], arrays[n:]


def _pair_sum(name, g, got, core):
    _, R, C = g.shape
    g4 = g.reshape(4, 2, R, C)
    tr = _tile(R, 1024, 16)

    def body(c_ref, g_ref, a_ref, *rest):
        o_ref = rest[-1]
        o_ref[...] = (g_ref[...].astype(F32) + a_ref[...].astype(F32)).astype(o_ref.dtype)

    prev = [] if _LAST[0] is None or _LAST[0] is g or _LAST[0] is got else [_LAST[0]]
    out = pl.pallas_call(
        body, name=name,
        grid_spec=pltpu.PrefetchScalarGridSpec(
            num_scalar_prefetch=1, grid=(4, R // tr),
            in_specs=[pl.BlockSpec((None, None, tr, C), lambda q, i, c_ref: (q, c_ref[0], i, 0)),
                      pl.BlockSpec((None, tr, C), lambda q, i, c_ref: (q, i, 0))] + [ANY] * len(prev),
            out_specs=pl.BlockSpec((None, tr, C), lambda q, i, c_ref: (q, i, 0))),
        out_shape=_sds((4, R, C), g.dtype),
        compiler_params=_params(),
    )(core, g4, got, *prev)
    _LAST[0] = out
    return out


def _rs_chip_start(name, sums):
    n = len(sums)
    zones = [lax.empty(s.shape, s.dtype) for s in sums]

    def body(ins, si, so, tok):
        send, recv, local = so
        x, y, c = _place()
        mine = _chip_index((x, y))
        for w in range(n):
            pltpu.make_async_copy(ins[w].at[mine], ins[n + w].at[mine], local.at[w]).start()
            for j in (1, 2, 3):
                _remote(ins[w].at[_chip_index(_chip(j))], ins[n + w].at[mine], send, recv, 3 * w + j - 1,
                        (*_chip(j), c)).start()

    sems, arrays, token = _split_call(name, body, list(sums) + zones, [], [3 * n, 3 * n, n])
    return (sems, arrays[:n], arrays[n:]), token


def _rs_chip_wait(name, handle):
    sems, sums, zones = handle
    n = len(sums)

    def body(ins, si, so, tok):
        send, recv, local = si
        x, y, c = _place()
        mine = _chip_index((x, y))
        for w in range(n):
            for j in (1, 2, 3):
                _remote(ins[w].at[mine], ins[n + w].at[_chip_index(_chip(j))], send, recv, 3 * w + j - 1,
                        (*_chip(j), c)).wait_recv()
        for w in range(n):
            for j in (1, 2, 3):
                _remote(ins[w].at[_chip_index(_chip(j))], ins[n + w].at[mine], send, recv, 3 * w + j - 1,
                        (*_chip(j), c)).wait_send()
            pltpu.make_async_copy(ins[w].at[mine], ins[n + w].at[mine], local.at[w]).wait()

    _, arrays, _ = _split_call(name, body, list(sums) + list(zones), sems, [])
    return arrays[n:]


def _chip_sum(name, slots, layer, layers, into=None):
    _, R, C = slots.shape
    tr = _tile(R, 256, 16)

    def body(*refs):
        s_ref, o_ref = refs[0], refs[-1]
        total = s_ref[0].astype(F32)
        for k in range(1, 4):
            total = total + s_ref[k].astype(F32)
        o_ref[...] = total

    extra = [] if into is None else [into]
    return _ordered_call(
        body, [slots] + extra, name=name, grid=(R // tr,),
        in_specs=[pl.BlockSpec((4, tr, C), lambda i: (0, i, 0))] + [ANY] * len(extra),
        out_specs=pl.BlockSpec((None, tr, C), lambda i: (layer, i, 0)),
        out_shape=_sds((layers, R, C), F32),
        input_output_aliases={} if into is None else {1: 0},
        compiler_params=_params(),
    )


def _pack_rows(name, parts, rows):
    C = parts[0].shape[1]

    def body(*refs):
        o_ref = refs[-1]
        off = 0
        for r in refs[:-1]:
            o_ref[off:off + r.shape[0], :] = r[...]
            off += r.shape[0]
        if off < rows:
            o_ref[off:rows, :] = jnp.zeros((rows - off, C), F32)

    vmem = pl.BlockSpec(memory_space=pltpu.VMEM)
    return _ordered_call(body, list(parts), name=name, in_specs=[vmem] * len(parts), out_specs=vmem,
                         out_shape=_sds((rows, C), F32))


def _reduce_small(rep, shd):
    R, C = rep.shape
    _, Rs, Cs = shd.shape

    def body(r_ref, s_ref, or_ref, os_ref, all_r, all_s, send, recv):
        me = _index(_place())
        all_r[me] = r_ref[...]
        all_s[me] = s_ref[me]
        copies = []
        for j in range(1, N_DEV):
            there = _index(_peer(j))
            copies.append((
                pltpu.make_async_remote_copy(src_ref=r_ref, dst_ref=all_r.at[me], send_sem=send.at[2 * j - 2],
                                             recv_sem=recv.at[2 * j - 2], device_id=_peer(j), device_id_type=MESH),
                pltpu.make_async_remote_copy(src_ref=s_ref.at[there], dst_ref=all_s.at[me],
                                             send_sem=send.at[2 * j - 1], recv_sem=recv.at[2 * j - 1],
                                             device_id=_peer(j), device_id_type=MESH)))
        for a, b in copies:
            a.start()
            b.start()
        for j in range(1, N_DEV):
            there = _index(_peer(j))
            pltpu.make_async_remote_copy(src_ref=r_ref, dst_ref=all_r.at[there], send_sem=send.at[2 * j - 2],
                                         recv_sem=recv.at[2 * j - 2], device_id=_peer(j),
                                         device_id_type=MESH).wait_recv()
            pltpu.make_async_remote_copy(src_ref=s_ref.at[me], dst_ref=all_s.at[there], send_sem=send.at[2 * j - 1],
                                         recv_sem=recv.at[2 * j - 1], device_id=_peer(j),
                                         device_id_type=MESH).wait_recv()
        for a, b in copies:
            a.wait_send()
            b.wait_send()
        tot_r, tot_s = all_r[0], all_s[0]
        for k in range(1, N_DEV):
            tot_r = tot_r + all_r[k]
            tot_s = tot_s + all_s[k]
        or_ref[...] = tot_r
        os_ref[...] = tot_s

    vmem = pl.BlockSpec(memory_space=pltpu.VMEM)
    return _ordered_call(
        body, [rep, shd], name="reduce_small",
        in_specs=[vmem, vmem], out_specs=[vmem, vmem],
        out_shape=[_sds((R, C), F32), _sds((Rs, Cs), F32)],
        scratch_shapes=[pltpu.VMEM((N_DEV, R, C), F32), pltpu.VMEM((N_DEV, Rs, Cs), F32),
                        pltpu.SemaphoreType.DMA((2 * N_DEV - 2,)), pltpu.SemaphoreType.DMA((2 * N_DEV - 2,))],
        compiler_params=_params(),
    )


def _cast_layer(name, w, layer):
    _, R, C = w.shape
    tr = _tile(R, 1024, 16)

    def body(w_ref, o_ref):
        o_ref[...] = w_ref[...].astype(BF16)

    return _ordered_call(
        body, [w], name=name, grid=(R // tr,),
        in_specs=[pl.BlockSpec((None, tr, C), lambda i: (layer, i, 0))],
        out_specs=pl.BlockSpec((tr, C), lambda i: (i, 0)),
        out_shape=_sds((R, C), BF16),
        compiler_params=_params(),
    )


def _adam_math(w, g, m, v):
    c1 = 1.0 / (1.0 - ADAM_B1 ** ADAM_STEP)
    c2 = 1.0 / (1.0 - ADAM_B2 ** ADAM_STEP)
    nm = ADAM_B1 * m + (1.0 - ADAM_B1) * g
    nv = ADAM_B2 * v + (1.0 - ADAM_B2) * (g * g)
    return -ADAM_LR * ((nm * c1) / (jnp.sqrt(nv * c2) + ADAM_EPS) + ADAM_WD * w), nm, nv


def _sum_adam(name, slots, w, m, v, layer, into=None):
    L, R, C = w.shape
    tr = _tile(R, 256, 16)

    def body(*refs):
        s_ref, w_ref, m_ref, v_ref = refs[:4]
        g_ref, d_ref, nm_ref, nv_ref = refs[-4:]
        g = s_ref[0].astype(F32)
        for k in range(1, 4):
            g = g + s_ref[k].astype(F32)
        g_ref[...] = g
        d_ref[...], nm_ref[...], nv_ref[...] = _adam_math(w_ref[...], g, m_ref[...], v_ref[...])

    spec = pl.BlockSpec((None, tr, C), lambda i: (layer, i, 0))
    extra = [] if into is None else list(into)
    return _ordered_call(
        body, [slots, w, m, v] + extra, name=name, grid=(R // tr,),
        in_specs=[pl.BlockSpec((4, tr, C), lambda i: (0, i, 0)), spec, spec, spec] + [ANY] * len(extra),
        out_specs=[spec] * 4,
        out_shape=[_sds((L, R, C), F32)] * 4,
        input_output_aliases={4 + k: k for k in range(len(extra))},
        compiler_params=_params(),
    )


def _adamw(name, w, g, m, v):
    shape = w.shape
    R, C = shape[-2], shape[-1]
    L = 1
    for s in shape[:-2]:
        L *= s
    w3, g3, m3, v3 = (a.reshape(L, R, C) for a in (w, g, m, v))
    tr = _tile(R, 256, 8)

    def body(w_ref, g_ref, m_ref, v_ref, d_ref, nm_ref, nv_ref):
        d_ref[...], nm_ref[...], nv_ref[...] = _adam_math(w_ref[...], g_ref[...], m_ref[...], v_ref[...])

    spec = pl.BlockSpec((None, tr, C), lambda l, i: (l, i, 0))
    outs = _ordered_call(
        body, [w3, g3, m3, v3], name=name, grid=(L, R // tr),
        in_specs=[spec] * 4, out_specs=[spec] * 3,
        out_shape=[_sds((L, R, C), F32)] * 3,
        compiler_params=_params(),
    )
    return tuple(o.reshape(shape) for o in outs)


def _pad_rows(a, rows):
    return jnp.pad(a, ((0, rows - a.shape[0]), (0, 0)))


def _pad_cols(a, cols):
    return jnp.pad(a, ((0, 0), (0, cols - a.shape[1])))


def _rope_tables(positions):
    q = D_ROPE // 2
    inv_freq = ROPE_THETA ** (-jnp.arange(0, D_ROPE, 2, dtype=F32) / D_ROPE)
    ang = positions.astype(F32)[:, None] * inv_freq
    cos, sin = jnp.cos(ang), jnp.sin(ang)
    z = jnp.zeros_like(cos)
    zz = jnp.zeros((cos.shape[0], 128 - 2 * q), F32)
    c = jnp.concatenate([cos, cos, zz], axis=1)
    s1 = jnp.concatenate([z, sin, zz], axis=1)
    s2 = jnp.concatenate([-sin, z, zz], axis=1)
    return c, s1, s2


def kernel(x, p, positions, ffn_a_norm, ffn_a_w_in, ffn_a_w_out, ffn_b_norm, ffn_b_w_in, ffn_b_w_out, mix_norm, mla_w_in, mla_q_lat_norm, mla_kv_lat_norm, mla_w_uq, mla_w_ukv, mla_q_gain, mla_k_gain, mla_w_o, conv_w_pw1, conv_b_pw1, conv_w_dw, conv_b_dw, conv_ln_g, conv_ln_b, conv_w_pw2, ple_w_proj, ple_norm, ple_gate_norm, ple_w_gate, loss_target, m_ffn_a_norm, m_ffn_a_w_in, m_ffn_a_w_out, m_ffn_b_norm, m_ffn_b_w_in, m_ffn_b_w_out, m_mix_norm, m_mla_w_in, m_mla_q_lat_norm, m_mla_kv_lat_norm, m_mla_w_uq, m_mla_w_ukv, m_mla_q_gain, m_mla_k_gain, m_mla_w_o, m_conv_w_pw1, m_conv_b_pw1, m_conv_w_dw, m_conv_b_dw, m_conv_ln_g, m_conv_ln_b, m_conv_w_pw2, m_ple_w_proj, m_ple_norm, m_ple_gate_norm, m_ple_w_gate, v_ffn_a_norm, v_ffn_a_w_in, v_ffn_a_w_out, v_ffn_b_norm, v_ffn_b_w_in, v_ffn_b_w_out, v_mix_norm, v_mla_w_in, v_mla_q_lat_norm, v_mla_kv_lat_norm, v_mla_w_uq, v_mla_w_ukv, v_mla_q_gain, v_mla_k_gain, v_mla_w_o, v_conv_w_pw1, v_conv_b_pw1, v_conv_w_dw, v_conv_b_dw, v_conv_ln_g, v_conv_ln_b, v_conv_w_pw2, v_ple_w_proj, v_ple_norm, v_ple_gate_norm, v_ple_w_gate):
    weights = dict(ffn_a_norm=ffn_a_norm, ffn_a_w_in=ffn_a_w_in, ffn_a_w_out=ffn_a_w_out, ffn_b_norm=ffn_b_norm,
                   ffn_b_w_in=ffn_b_w_in, ffn_b_w_out=ffn_b_w_out, mix_norm=mix_norm, mla_w_in=mla_w_in,
                   mla_q_lat_norm=mla_q_lat_norm, mla_kv_lat_norm=mla_kv_lat_norm, mla_w_uq=mla_w_uq,
                   mla_w_ukv=mla_w_ukv, mla_q_gain=mla_q_gain, mla_k_gain=mla_k_gain, mla_w_o=mla_w_o,
                   conv_w_pw1=conv_w_pw1, conv_b_pw1=conv_b_pw1, conv_w_dw=conv_w_dw, conv_b_dw=conv_b_dw,
                   conv_ln_g=conv_ln_g, conv_ln_b=conv_ln_b, conv_w_pw2=conv_w_pw2, ple_w_proj=ple_w_proj,
                   ple_norm=ple_norm, ple_gate_norm=ple_gate_norm, ple_w_gate=ple_w_gate)
    moments_m = dict(ffn_a_norm=m_ffn_a_norm, ffn_a_w_in=m_ffn_a_w_in, ffn_a_w_out=m_ffn_a_w_out,
                     ffn_b_norm=m_ffn_b_norm, ffn_b_w_in=m_ffn_b_w_in, ffn_b_w_out=m_ffn_b_w_out,
                     mix_norm=m_mix_norm, mla_w_in=m_mla_w_in, mla_q_lat_norm=m_mla_q_lat_norm,
                     mla_kv_lat_norm=m_mla_kv_lat_norm, mla_w_uq=m_mla_w_uq, mla_w_ukv=m_mla_w_ukv,
                     mla_q_gain=m_mla_q_gain, mla_k_gain=m_mla_k_gain, mla_w_o=m_mla_w_o,
                     conv_w_pw1=m_conv_w_pw1, conv_b_pw1=m_conv_b_pw1, conv_w_dw=m_conv_w_dw,
                     conv_b_dw=m_conv_b_dw, conv_ln_g=m_conv_ln_g, conv_ln_b=m_conv_ln_b, conv_w_pw2=m_conv_w_pw2,
                     ple_w_proj=m_ple_w_proj, ple_norm=m_ple_norm, ple_gate_norm=m_ple_gate_norm,
                     ple_w_gate=m_ple_w_gate)
    moments_v = dict(ffn_a_norm=v_ffn_a_norm, ffn_a_w_in=v_ffn_a_w_in, ffn_a_w_out=v_ffn_a_w_out,
                     ffn_b_norm=v_ffn_b_norm, ffn_b_w_in=v_ffn_b_w_in, ffn_b_w_out=v_ffn_b_w_out,
                     mix_norm=v_mix_norm, mla_w_in=v_mla_w_in, mla_q_lat_norm=v_mla_q_lat_norm,
                     mla_kv_lat_norm=v_mla_kv_lat_norm, mla_w_uq=v_mla_w_uq, mla_w_ukv=v_mla_w_ukv,
                     mla_q_gain=v_mla_q_gain, mla_k_gain=v_mla_k_gain, mla_w_o=v_mla_w_o,
                     conv_w_pw1=v_conv_w_pw1, conv_b_pw1=v_conv_b_pw1, conv_w_dw=v_conv_w_dw,
                     conv_b_dw=v_conv_b_dw, conv_ln_g=v_conv_ln_g, conv_ln_b=v_conv_ln_b, conv_w_pw2=v_conv_w_pw2,
                     ple_w_proj=v_ple_w_proj, ple_norm=v_ple_norm, ple_gate_norm=v_ple_gate_norm,
                     ple_w_gate=v_ple_w_gate)
    order = list(weights.keys())
    _LAST[0] = None

    T, D = x.shape[1], x.shape[2]
    me = _index(_place())
    h0 = x[0]
    target = loss_target[0]
    tabs = _rope_tables(positions[0])
    H = N_HEADS
    hps = H // N_DEV
    QL = mla_q_lat_norm.shape[1]
    Cs = conv_b_dw.shape[1]

    def cast(n, i):
        return _cast_layer(f"cast_{n}{i}", weights[n], i)

    first, tok = _ag_start("ag_start0", [[cast("ffn_a_w_in", 0)]])
    m_in_pad = _pad_cols(mla_w_in[0], mla_w_in.shape[2] - D_ROPE + 128)[None]
    uq_pad = jnp.pad(mla_w_uq[0].reshape(QL, hps, QK_DIM), ((0, 0), (0, 0), (0, HEAD_PAD - QK_DIM)))
    uq_pad = uq_pad.reshape(1, QL, hps * HEAD_PAD)
    conv_small = jnp.concatenate([
        _pad_rows(_pad_cols(conv_b_pw1, 2 * Cs), 8),
        _pad_rows(_pad_cols(conv_w_dw[0], 2 * Cs), 32),
        _pad_rows(_pad_cols(jnp.concatenate([conv_b_dw, conv_ln_g, conv_ln_b], axis=0), 2 * Cs), 8)], axis=0)
    rest, tok = _ag_start("ag_start1", [
        [cast("ffn_a_w_out", 0), _cast_layer("cast_mla_in", m_in_pad, 0), _cast_layer("cast_mla_uq", uq_pad, 0),
         _cast_layer("cast_mla_ukv", mla_w_ukv, 0), _cast_layer("cast_mla_wo", mla_w_o, 0)],
        [cast("ffn_b_w_in", 0), cast("ffn_b_w_out", 0), cast("ple_w_gate", 0), cast("ple_w_proj", 0)],
        [cast("ffn_a_w_in", 1), cast("ffn_a_w_out", 1)],
        [_cast_layer("cast_conv_pw1", conv_w_pw1, 0), _cast_layer("cast_conv_pw2", conv_w_pw2, 0), conv_small],
        [cast("ffn_b_w_in", 1)], [cast("ffn_b_w_out", 1)], [cast("ple_w_gate", 1), cast("ple_w_proj", 1)]])
    groups = [dict(handle=hd, stage=0, arrays=None) for hd in first + rest]

    def prefetch(gi):
        st = groups[gi]
        if st["stage"] == 0:
            st["handle"] = _ag_forward(f"ag{gi}_forward", st["handle"])
            st["stage"] = 1

    def fetch(gi):
        prefetch(gi)
        st = groups[gi]
        if st["stage"] == 1:
            st["arrays"] = _ag_wait(f"ag{gi}_wait", st["handle"])
            st["stage"] = 2
        return st["arrays"]

    def getter(gi, k, shape=None, ahead=None):
        def get():
            if ahead is not None:
                prefetch(ahead)
            a = fetch(gi)[k]
            return a if shape is None else a.reshape(shape)
        return get

    def conv_small_params():
        small = fetch(4)[2]
        return (small[:, 0:1, :],
                jnp.transpose(small[:, 8:40, :Cs], (1, 0, 2)).reshape(32, N_DEV * Cs),
                small[:, 40, :Cs].reshape(1, N_DEV * Cs), small[:, 41, :Cs].reshape(1, N_DEV * Cs),
                small[:, 42, :Cs].reshape(1, N_DEV * Cs))

    rows = (-1, D)
    get_ffn = [dict(a_in=getter(0, 0), a_out=getter(1, 0, rows), b_in=getter(2, 0), b_out=getter(2, 1, rows)),
               dict(a_in=getter(3, 0), a_out=getter(3, 1, rows, ahead=4), b_in=getter(5, 0),
                    b_out=getter(6, 0, rows))]
    get_ple = [dict(proj=getter(2, 3, ahead=3), gate=getter(2, 2, rows)),
               dict(proj=getter(7, 1), gate=getter(7, 0, rows))]
    get_mla = dict(m_in=getter(1, 1, (D, -1)), uq=getter(1, 2), ukv=getter(1, 3), wo=getter(1, 4, rows, ahead=2))
    get_conv = dict(pw1=getter(4, 0), pw2=getter(4, 1, rows, ahead=5), small=conv_small_params)
    gq_pad = _pad_cols(mla_q_gain, HEAD_PAD)
    gk_pad = _pad_cols(mla_k_gain, HEAD_PAD)
    prefetch(0)

    saved = []
    h = h0
    for i in range(2):
        h, s_a = _ffn_fwd(f"ffn_a{i}", h, ffn_a_norm[i:i + 1], get_ffn[i]["a_in"], get_ffn[i]["a_out"])
        if i == 0:
            h, s_m = _mla_fwd(h, mix_norm[0:1], tabs, get_mla, mla_q_lat_norm, mla_kv_lat_norm, gq_pad, gk_pad)
        else:
            h, s_m = _conv_fwd(h, mix_norm[1:2], get_conv)
        h, s_b = _ffn_fwd(f"ffn_b{i}", h, ffn_b_norm[i:i + 1], get_ffn[i]["b_in"], get_ffn[i]["b_out"])
        h, s_p = _ple_fwd(f"ple{i}", h, p[i, 0], get_ple[i]["proj"], ple_norm[i:i + 1], ple_gate_norm[i:i + 1],
                          get_ple[i]["gate"])
        saved.append((s_a, s_m, s_b, s_p))
    W = [dict(a_in=get_ffn[i]["a_in"](), a_out=get_ffn[i]["a_out"](), b_in=get_ffn[i]["b_in"](),
              b_out=get_ffn[i]["b_out"](), proj=get_ple[i]["proj"](), gate=get_ple[i]["gate"]())
         for i in range(2)]
    Wm = {n: g() for n, g in get_mla.items()}
    Wc = dict(pw1=get_conv["pw1"](), pw2=get_conv["pw2"]())
    _, w_dw_full, _, ln_g_full, ln_b_full = conv_small_params()

    dh, dhb, loss_row = _loss_head(h, target)

    G = {}
    small_g = {}
    stacked = ["ffn_a_w_in", "ffn_a_w_out", "ffn_b_w_in", "ffn_b_w_out", "ple_w_proj", "ple_w_gate"]
    row_sharded = {"ffn_a_w_out", "ffn_b_w_out", "ple_w_gate", "mla_w_in", "mla_w_o", "conv_w_pw2"}
    core = lax.axis_index("c").astype(jnp.int32).reshape(1)
    rs_groups = {
        "pb1": [("ple_w_proj", 1), ("ple_w_gate", 1), ("ffn_b_w_in", 1), ("ffn_b_w_out", 1)],
        "c1": [("conv_w_pw1", 0), ("conv_w_pw2", 0)],
        "a1": [("ffn_a_w_in", 1), ("ffn_a_w_out", 1)],
        "pb0": [("ple_w_proj", 0), ("ple_w_gate", 0), ("ffn_b_w_in", 0), ("ffn_b_w_out", 0)],
        "m0": [("mla_w_in", 0), ("mla_w_uq", 0), ("mla_w_ukv", 0), ("mla_w_o", 0)],
        "ao0": [("ffn_a_w_out", 0)],
        "ai0": [("ffn_a_w_in", 0)]}
    rs = {}

    def rs_begin(tag):
        grads = []
        for n, i in rs_groups[tag]:
            g = G[(n, i)]
            grads.append(g.reshape(N_DEV, g.shape[0] // N_DEV, g.shape[1]) if n in row_sharded else g)
        rs[tag] = _rs_pair_start(f"rs_{tag}_pair_start", grads)[0]

    def rs_pairs(tag):
        mine, got = _rs_pair_wait(f"rs_{tag}_pair_wait", rs[tag])
        rs[tag] = [_pair_sum(f"pairsum_{n}{i}", a, b, core) for (n, i), a, b in zip(rs_groups[tag], mine, got)]

    def rs_chips(tag):
        rs[tag] = _rs_chip_start(f"rs_{tag}_chip_start", rs[tag])[0]

    def rs_end(tag):
        return dict(zip(rs_groups[tag], _rs_chip_wait(f"rs_{tag}_chip_wait", rs[tag])))

    s_a, s_m, s_b, s_p = saved[1]
    dh, dhb, d_pn, d_gn, G[("ple_w_proj", 1)], G[("ple_w_gate", 1)] = _ple_bwd(
        "ple1", s_p, p[1, 0], W[1]["proj"], ple_norm[1:2], ple_gate_norm[1:2], W[1]["gate"], dh, dhb)
    small_g[("ple_norm", 1)], small_g[("ple_gate_norm", 1)] = d_pn, d_gn
    dh, dhb, small_g[("ffn_b_norm", 1)], G[("ffn_b_w_in", 1)], G[("ffn_b_w_out", 1)] = _ffn_bwd(
        "ffn_b1", s_b, ffn_b_norm[1:2], W[1]["b_in"], W[1]["b_out"], dh, dhb)
    rs_begin("pb1")
    (dh, dhb, small_g[("mix_norm", 1)], G[("conv_w_pw1", 0)], d_b_pw1, d_w_dw, d_b_dw, d_ln_g, d_ln_b,
     G[("conv_w_pw2", 0)]) = _conv_bwd(s_m, mix_norm[1:2], Wc, w_dw_full, ln_g_full, ln_b_full, dh, dhb)
    rs_pairs("pb1")
    rs_chips("pb1")
    rs_begin("c1")
    dh, dhb, small_g[("ffn_a_norm", 1)], G[("ffn_a_w_in", 1)], G[("ffn_a_w_out", 1)] = _ffn_bwd(
        "ffn_a1", s_a, ffn_a_norm[1:2], W[1]["a_in"], W[1]["a_out"], dh, dhb)
    rs_pairs("c1")
    rs_chips("c1")
    rs_begin("a1")

    s_a, s_m, s_b, s_p = saved[0]
    dh, dhb, d_pn, d_gn, G[("ple_w_proj", 0)], G[("ple_w_gate", 0)] = _ple_bwd(
        "ple0", s_p, p[0, 0], W[0]["proj"], ple_norm[0:1], ple_gate_norm[0:1], W[0]["gate"], dh, dhb)
    small_g[("ple_norm", 0)], small_g[("ple_gate_norm", 0)] = d_pn, d_gn
    rs_pairs("a1")
    rs_chips("a1")
    dh, dhb, small_g[("ffn_b_norm", 0)], G[("ffn_b_w_in", 0)], G[("ffn_b_w_out", 0)] = _ffn_bwd(
        "ffn_b0", s_b, ffn_b_norm[0:1], W[0]["b_in"], W[0]["b_out"], dh, dhb)
    rs_begin("pb0")

    def in_mla():
        rs_pairs("pb0")
        rs_chips("pb0")

    (dh, dhb, small_g[("mix_norm", 0)], d_qln, d_kvln, d_gq, d_gk,
     G[("mla_w_in", 0)], G[("mla_w_uq", 0)], G[("mla_w_ukv", 0)], G[("mla_w_o", 0)]) = _mla_bwd(
        s_m, mix_norm[0:1], tabs, Wm, mla_q_lat_norm, mla_kv_lat_norm, gq_pad, gk_pad, dh, dhb, hook=in_mla)
    rs_begin("m0")

    def with_dw_out(dw_out):
        G[("ffn_a_w_out", 0)] = dw_out
        rs_pairs("m0")
        rs_chips("m0")
        rs_begin("ao0")

    def with_dw_in(dw_in):
        G[("ffn_a_w_in", 0)] = dw_in
        rs_pairs("ao0")
        rs_chips("ao0")
        rs_begin("ai0")

    dh, dhb, small_g[("ffn_a_norm", 0)], _, _ = _ffn_bwd(
        "ffn_a0", s_a, ffn_a_norm[0:1], W[0]["a_in"], W[0]["a_out"], dh, dhb,
        hooks=(with_dw_out, with_dw_in, lambda: rs_pairs("ai0")))
    grad_x = dh[None]

    replicated = ["ffn_a_norm", "ffn_b_norm", "mix_norm", "ple_norm", "ple_gate_norm"]
    rep = _pack_rows("pack_small", [small_g[(n, i)] for n in replicated for i in (0, 1)]
                     + [_pad_cols(v, D) for v in (d_qln, d_kvln, d_gq, d_gk, loss_row)], 16)

    def dest_major(v, rows):
        r, w = v.shape[0], v.shape[1] // N_DEV
        v = jnp.transpose(v.reshape(r, N_DEV, w), (1, 0, 2))
        return jnp.pad(v, ((0, 0), (0, rows - r), (0, 2 * Cs - w)))

    shd = jnp.concatenate([dest_major(d_b_pw1, 8), dest_major(d_w_dw, 32),
                           dest_major(jnp.concatenate([d_b_dw, d_ln_g, d_ln_b], axis=0), 8)], axis=1)
    red, red_s = _reduce_small(rep, shd)
    loss = red[14, 0]
    small_grads = {n: red[2 * k:2 * k + 2] for k, n in enumerate(replicated)}
    small_grads.update(
        mla_q_lat_norm=red[10:11, :QL], mla_kv_lat_norm=red[11:12, :mla_kv_lat_norm.shape[1]],
        mla_q_gain=red[12:13, :QK_DIM], mla_k_gain=red[13:14, :QK_DIM],
        conv_b_pw1=red_s[0:1], conv_w_dw=red_s[8:8 + CONV_WIDTH, :Cs][None],
        conv_b_dw=red_s[40:41, :Cs], conv_ln_g=red_s[41:42, :Cs], conv_ln_b=red_s[42:43, :Cs])

    rs_chips("ai0")
    done = {}

    def plain_adamw(n, g):
        done[n] = (g,) + _adamw(f"adamw_{n}", weights[n], g, moments_m[n], moments_v[n])

    def slot_adamw(n, layer, slots):
        done[n] = tuple(_sum_adam(f"adamw_{n}{layer}", slots, weights[n], moments_m[n], moments_v[n], layer,
                                  into=done.get(n)))

    for n, g in small_grads.items():
        plain_adamw(n, g)
    for tag in ("pb1", "c1", "a1", "pb0"):
        slots = rs_end(tag)
        for n, i in rs_groups[tag]:
            slot_adamw(n, i, slots[(n, i)])
    slots = rs_end("m0")
    slot_adamw("mla_w_ukv", 0, slots[("mla_w_ukv", 0)])
    slot_adamw("mla_w_o", 0, slots[("mla_w_o", 0)])
    g_in = _chip_sum("chipsum_mla_w_in", slots[("mla_w_in", 0)], 0, 1)
    plain_adamw("mla_w_in", g_in[:, :, :mla_w_in.shape[2]])
    g_uq = _chip_sum("chipsum_mla_w_uq", slots[("mla_w_uq", 0)], 0, 1)
    plain_adamw("mla_w_uq", g_uq.reshape(1, QL, hps, HEAD_PAD)[..., :QK_DIM].reshape(mla_w_uq.shape))
    for tag in ("ao0", "ai0"):
        slots = rs_end(tag)
        for n, i in rs_groups[tag]:
            slot_adamw(n, i, slots[(n, i)])
    grads, deltas, new_m, new_v = ({n: done[n][k] for n in order} for k in range(4))

    return (loss, grad_x, *[grads[n] for n in order], *[deltas[n] for n in order],
            *[new_m[n] for n in order], *[new_v[n] for n in order])
```
